```python
import jax, jax.numpy as jnp
from jax import lax
import numpy as np

D_MODEL = 1024
BATCH = 4
SEQ = 8192
DEPTH = 1

D_MIX = D_MODEL
D_ATTN = D_MIX // 2
N_HEADS = 8
HEAD_DIM = D_ATTN // N_HEADS
N_IDX_HEADS = 8
IDX_DIM = 64
TOPK_MAX = 256
Q_BLOCK = 128
ROPE_THETA = 10000.0
D_POOL = D_MIX - D_ATTN
POOL_WINDOWS = (2, 4, 8, 16)
N_POOL_GROUPS = 4
POOL_GROUP_DIM = D_POOL // N_POOL_GROUPS
SPLIT_POINTS = (D_ATTN,
                D_ATTN + HEAD_DIM,
                D_ATTN + 2 * HEAD_DIM,
                D_ATTN + 2 * HEAD_DIM + N_IDX_HEADS * IDX_DIM,
                D_ATTN + 2 * HEAD_DIM + N_IDX_HEADS * IDX_DIM + IDX_DIM,
                D_ATTN + 2 * HEAD_DIM + N_IDX_HEADS * IDX_DIM + IDX_DIM + N_IDX_HEADS)
D_IN = SPLIT_POINTS[-1] + D_POOL
N_GROUPS = 4
EXPERTS_PER_GROUP = 8
N_EXPERTS = N_GROUPS * EXPERTS_PER_GROUP
TOP_K = 2
D_EXPERT = 256
EPS = 1e-6

kernel_name = "hymba_dsa_pool_hmoe_adaln"


def rms_norm(x, g):
    xf = x.astype(jnp.float32)
    y = xf * lax.rsqrt(jnp.mean(xf * xf, axis=-1, keepdims=True) + EPS)
    return (y * g.astype(jnp.float32)).astype(x.dtype)


def rope_tables(positions, dim):
    inv_freq = ROPE_THETA ** (-jnp.arange(0, dim, 2, dtype=jnp.float32) / dim)
    ang = positions.astype(jnp.float32)[..., None] * inv_freq
    return jnp.cos(ang), jnp.sin(ang)


def apply_rope(x, cos, sin):
    x1, x2 = jnp.split(x.astype(jnp.float32), 2, axis=-1)
    return jnp.concatenate([x1 * cos - x2 * sin, x1 * sin + x2 * cos], axis=-1).astype(x.dtype)


def dsa_attention(q, kv, q_idx, k_idx, w_idx):
    B, S = q.shape[0], q.shape[1]
    topk = min(TOPK_MAX, S // 4)
    n_blocks = S // Q_BLOCK
    key_pos = jnp.arange(S)

    def block(i):
        start = i * Q_BLOCK
        qb = lax.dynamic_slice_in_dim(q, start, Q_BLOCK, axis=1)
        qib = lax.dynamic_slice_in_dim(q_idx, start, Q_BLOCK, axis=1)
        wb = lax.dynamic_slice_in_dim(w_idx, start, Q_BLOCK, axis=1)
        s_idx = jnp.einsum('bqhd,bsd->bqhs', qib, k_idx, preferred_element_type=jnp.float32)
        score = jnp.einsum('bqh,bqhs->bqs', wb, jax.nn.relu(s_idx))
        q_pos = start + jnp.arange(Q_BLOCK)
        causal = key_pos[None, :] <= q_pos[:, None]
        score = jnp.where(causal[None], score, -jnp.inf)
        top_val, top_ind = lax.top_k(score, topk)
        valid = top_val > -jnp.inf
        kv_sel = jax.vmap(lambda kvb, ib: kvb[ib])(kv, top_ind)
        k_sel, v_sel = jnp.split(kv_sel, 2, axis=-1)
        logits = jnp.einsum('bqhd,bqkd->bqhk', qb, k_sel,
                            preferred_element_type=jnp.float32) * (HEAD_DIM ** -0.5)
        logits = jnp.where(valid[:, :, None, :], logits, -jnp.inf)
        p = jax.nn.softmax(logits, axis=-1).astype(v_sel.dtype)
        return jnp.einsum('bqhk,bqkd->bqhd', p, v_sel)

    out = lax.map(block, jnp.arange(n_blocks))
    return jnp.moveaxis(out, 0, 1).reshape(B, S, N_HEADS * HEAD_DIM)


def pool_mixer(u, w_pool, pool_scale):
    B, S, _ = u.shape
    ug = u.reshape(B, S, N_POOL_GROUPS, POOL_GROUP_DIM)
    cs = jnp.cumsum(ug.astype(jnp.float32), axis=1)
    t = jnp.arange(S)
    outs = []
    for g, win in enumerate(POOL_WINDOWS):
        cg = cs[:, :, g]
        c_shift = jnp.pad(cg, ((0, 0), (win, 0), (0, 0)))[:, :S]
        cnt = jnp.minimum(t + 1, win).astype(jnp.float32)[None, :, None]
        outs.append((cg - c_shift) / cnt - ug[:, :, g].astype(jnp.float32))
    pooled = jnp.stack(outs, axis=2).astype(u.dtype)
    mixed = jnp.einsum('bsgc,gcd->bsgd', pooled, w_pool)
    return mixed.reshape(B, S, D_POOL) * pool_scale


def hierarchical_moe(h, w_rg, b_rg, w_re, b_re, w_gate, w_up, w_down):
    B, S, D = h.shape
    ht = h.reshape(B * S, D)
    g_logits = (ht @ w_rg + b_rg).astype(jnp.float32)
    p_group = jax.nn.softmax(g_logits, axis=-1)
    g_sel = jnp.argmax(g_logits, axis=-1)
    p_g = jnp.take_along_axis(p_group, g_sel[:, None], axis=-1)
    e_logits = (ht @ w_re + b_re).astype(jnp.float32).reshape(-1, N_GROUPS, EXPERTS_PER_GROUP)
    e_logits = jnp.take_along_axis(e_logits, g_sel[:, None, None], axis=1)[:, 0]
    p_e = jax.nn.softmax(e_logits, axis=-1)
    top_p, top_e = lax.top_k(p_e, TOP_K)
    top_p = top_p / jnp.sum(top_p, axis=-1, keepdims=True)
    flat = g_sel[:, None] * EXPERTS_PER_GROUP + top_e
    gates = jnp.sum(jax.nn.one_hot(flat, N_EXPERTS, dtype=jnp.float32)
                    * (p_g * top_p)[..., None], axis=1).astype(h.dtype)
    y = jnp.zeros_like(ht)
    for e in range(N_EXPERTS):
        a = jax.nn.silu(ht @ w_gate[e]) * (ht @ w_up[e])
        y = y + gates[:, e:e + 1] * (a @ w_down[e])
    return y.reshape(B, S, D)


def setup_inputs(seed: int = 0) -> dict:
    key = jax.random.key(seed)
    ks = jax.random.split(key, 22)
    f32 = jnp.float32
    nrm = lambda k, shape, s: jax.random.normal(k, shape, f32) * s
    x = jax.random.normal(ks[0], (BATCH, SEQ, D_MODEL), f32)
    c = jax.random.normal(ks[1], (BATCH, D_MODEL), f32)
    offset = jax.random.randint(ks[2], (BATCH, 1), 0, 4096, dtype=jnp.int32)
    positions = (offset + jnp.arange(SEQ, dtype=jnp.int32)[None, :]).astype(jnp.int32)
    return {
        "x": x,
        "c": c,
        "positions": positions,
        "w_ada": nrm(ks[3], (DEPTH, D_MODEL, 6 * D_MODEL), 0.5 * D_MODEL ** -0.5),
        "b_ada": nrm(ks[4], (DEPTH, 6 * D_MODEL), 0.02),
        "g_norm_mix": 1.0 + nrm(ks[5], (DEPTH, D_MODEL), 0.1),
        "g_norm_ffn": 1.0 + nrm(ks[6], (DEPTH, D_MODEL), 0.1),
        "w_in": nrm(ks[7], (DEPTH, D_MODEL, D_IN), D_MODEL ** -0.5),
        "g_q": 1.0 + nrm(ks[8], (DEPTH, HEAD_DIM), 0.1),
        "g_k": 1.0 + nrm(ks[9], (DEPTH, HEAD_DIM), 0.1),
        "g_kidx": 1.0 + nrm(ks[10], (DEPTH, IDX_DIM), 0.1),
        "w_pool": nrm(ks[11], (DEPTH, N_POOL_GROUPS, POOL_GROUP_DIM, POOL_GROUP_DIM), POOL_GROUP_DIM ** -0.5),
        "pool_scale": 0.5 + nrm(ks[12], (DEPTH, D_POOL), 0.1),
        "w_out": nrm(ks[13], (DEPTH, D_MIX, D_MODEL), D_MIX ** -0.5),
        "w_router_group": nrm(ks[14], (DEPTH, D_MODEL, N_GROUPS), D_MODEL ** -0.5),
        "b_router_group": nrm(ks[15], (DEPTH, N_GROUPS), 0.01),
        "w_router_expert": nrm(ks[16], (DEPTH, D_MODEL, N_EXPERTS), D_MODEL ** -0.5),
        "b_router_expert": nrm(ks[17], (DEPTH, N_EXPERTS), 0.01),
        "w_gate": nrm(ks[18], (DEPTH, N_EXPERTS, D_MODEL, D_EXPERT), D_MODEL ** -0.5),
        "w_up": nrm(ks[19], (DEPTH, N_EXPERTS, D_MODEL, D_EXPERT), D_MODEL ** -0.5),
        "w_down": nrm(ks[20], (DEPTH, N_EXPERTS, D_EXPERT, D_MODEL), D_EXPERT ** -0.5),
    }


def reference(x, c, positions, w_ada, b_ada, g_norm_mix, g_norm_ffn, w_in, g_q, g_k, g_kidx,
              w_pool, pool_scale, w_out, w_router_group, b_router_group, w_router_expert,
              b_router_expert, w_gate, w_up, w_down):
    B, S, D = x.shape
    cos, sin = rope_tables(positions, HEAD_DIM)
    cos_h, sin_h = cos[:, :, None, :], sin[:, :, None, :]
    c_act = jax.nn.silu(c)
    for l in range(DEPTH):
        mod = c_act @ w_ada[l] + b_ada[l]
        shift1, scale1, gate1, shift2, scale2, gate2 = [m[:, None, :] for m in jnp.split(mod, 6, axis=-1)]

        h = rms_norm(x, g_norm_mix[l]) * (1.0 + scale1) + shift1
        proj = h @ w_in[l]
        q, k, v, q_idx, k_idx, w_idx, u = jnp.split(proj, SPLIT_POINTS, axis=-1)
        q = apply_rope(rms_norm(q.reshape(B, S, N_HEADS, HEAD_DIM), g_q[l]), cos_h, sin_h)
        k = apply_rope(rms_norm(k, g_k[l]), cos, sin)
        kv = jnp.concatenate([k, v], axis=-1)
        q_idx = apply_rope(q_idx.reshape(B, S, N_IDX_HEADS, IDX_DIM), cos_h, sin_h)
        k_idx = apply_rope(rms_norm(k_idx, g_kidx[l]), cos, sin)
        w_idx = w_idx.astype(jnp.float32) * (N_IDX_HEADS ** -0.5 * IDX_DIM ** -0.5)
        attn_out = dsa_attention(q, kv, q_idx, k_idx, w_idx)
        pool_out = pool_mixer(u, w_pool[l], pool_scale[l])
        mix = jnp.concatenate([attn_out, pool_out], axis=-1) @ w_out[l]
        x = x + gate1 * mix

        h2 = rms_norm(x, g_norm_ffn[l]) * (1.0 + scale2) + shift2
        x = x + gate2 * hierarchical_moe(h2, w_router_group[l], b_router_group[l],
                                         w_router_expert[l], b_router_expert[l],
                                         w_gate[l], w_up[l], w_down[l])
    return x
```

```python
import functools

import jax
import jax.numpy as jnp
from jax import lax
from jax.experimental import pallas as pl
from jax.experimental.pallas import tpu as pltpu

N_HEADS = 8
HEAD_DIM = 64
N_IDX_HEADS = 8
IDX_DIM = 64
TOPK_MAX = 256
ROPE_THETA = 10000.0
POOL_WINDOWS = (2, 4, 8, 16)
N_GROUPS = 4
EXPERTS_PER_GROUP = 8
N_EXPERTS = N_GROUPS * EXPERTS_PER_GROUP
EPS = 1e-6

LANES = 128
SUBLANES = 8
VMEM_LIMIT_BYTES = 56 * 1024 * 1024

Q_BLK = 128
K_BLK = 256
TM_PROJ = 512
TM_MOE = 1024
MAX_WIN = max(POOL_WINDOWS)
NEG_BIG = -1e30
F32_LOWEST = -3.0e38

BF16 = jnp.bfloat16
F32 = jnp.float32


def _cparams(sem):
    return pltpu.CompilerParams(dimension_semantics=sem, vmem_limit_bytes=VMEM_LIMIT_BYTES)


def _adaln_kernel(c_ref, w_ref, b_ref, o_ref):
    c = c_ref[...]
    c_act = c * jax.nn.sigmoid(c)
    o_ref[...] = jnp.dot(c_act, w_ref[...], preferred_element_type=F32) + b_ref[...]


def _adaln(c_pad, w_ada, b_ada):
    rows, d = c_pad.shape
    n = w_ada.shape[1]
    tn = n // 6
    return pl.pallas_call(
        _adaln_kernel,
        out_shape=jax.ShapeDtypeStruct((rows, n), F32),
        grid=(n // tn,),
        in_specs=[pl.BlockSpec((rows, d), lambda j: (0, 0)),
                  pl.BlockSpec((d, tn), lambda j: (0, j)),
                  pl.BlockSpec((1, tn), lambda j: (0, j))],
        out_specs=pl.BlockSpec((rows, tn), lambda j: (0, j)),
        compiler_params=_cparams(("arbitrary",)),
        name="adaln",
    )(c_pad, w_ada, b_ada)


def _rope_chunk(y, cos, sin_signed, first_half):
    from_hi = pltpu.roll(y, LANES - HEAD_DIM // 2, 1)
    from_lo = pltpu.roll(y, HEAD_DIM // 2, 1)
    return y * cos + jnp.where(first_half, from_hi, from_lo) * sin_signed


def _inproj_kernel(pos_ref, x_ref, scale_ref, shift_ref, gmix_ref, win_ref, segsum_ref,
                   gq_ref, gk_ref, gkidx_ref, invf_ref, wpool_ref, pscale_ref,
                   q_ref, kv_ref, qi_ref, ki_ref, wi_ref, pool_ref, ubuf_ref):
    tm = x_ref.shape[1]
    d_attn = N_HEADS * HEAD_DIM
    d_qidx = N_IDX_HEADS * IDX_DIM
    s_tile = pl.program_id(1)

    x = x_ref[0]
    ms = jnp.mean(x * x, axis=-1, keepdims=True)
    h = (x * lax.rsqrt(ms + EPS) * gmix_ref[...]) * (1.0 + scale_ref[0]) + shift_ref[0]
    proj = jnp.dot(h.astype(BF16), win_ref[...], preferred_element_type=F32)

    lane = lax.broadcasted_iota(jnp.int32, (tm, LANES), 1)
    first_half = (lane & (HEAD_DIM - 1)) < (HEAD_DIM // 2)
    ang = pos_ref[0].astype(F32) * invf_ref[...]
    cos = jnp.cos(ang)
    sin = jnp.sin(ang)
    sin_signed = jnp.where(first_half, -sin, sin)
    rope = functools.partial(_rope_chunk, cos=cos, sin_signed=sin_signed, first_half=first_half)

    qf = proj[:, :d_attn]
    qsq = qf * qf
    qsq_hi = qsq.astype(BF16)
    qsq_lo = (qsq - qsq_hi.astype(F32)).astype(BF16)
    seg = segsum_ref[...]
    ssq = (jnp.dot(qsq_hi, seg, preferred_element_type=F32)
           + jnp.dot(qsq_lo, seg, preferred_element_type=F32))
    qn = qf * lax.rsqrt(ssq * (1.0 / HEAD_DIM) + EPS) * gq_ref[...]
    for j in range(d_attn // LANES):
        sl = slice(j * LANES, (j + 1) * LANES)
        q_ref[0, :, sl] = (rope(qn[:, sl]) * (HEAD_DIM ** -0.5)).astype(BF16)

    kvc = proj[:, d_attn:d_attn + LANES]
    is_k = lane < HEAD_DIM
    ksq = jnp.sum(jnp.where(is_k, kvc * kvc, 0.0), axis=-1, keepdims=True)
    kn = kvc * lax.rsqrt(ksq * (1.0 / HEAD_DIM) + EPS) * gk_ref[...]
    kv_ref[0] = jnp.where(is_k, rope(kn), kvc).astype(BF16)

    o_qi = d_attn + LANES
    for j in range(d_qidx // LANES):
        qi_ref[0, :, j * LANES:(j + 1) * LANES] = rope(
            proj[:, o_qi + j * LANES:o_qi + (j + 1) * LANES]).astype(BF16)

    o_ki = o_qi + d_qidx
    kic = proj[:, o_ki:o_ki + LANES]
    kisq = jnp.sum(jnp.where(is_k, kic * kic, 0.0), axis=-1, keepdims=True)
    kin = kic * lax.rsqrt(kisq * (1.0 / IDX_DIM) + EPS) * gkidx_ref[...]
    ki_ref[0] = jnp.where(is_k, rope(kin), 0.0).astype(BF16)
    wi_ref[0] = kic[:, IDX_DIM:IDX_DIM + N_IDX_HEADS] * (N_IDX_HEADS ** -0.5 * IDX_DIM ** -0.5)

    o_u = o_ki + LANES
    u = proj[:, o_u:o_u + LANES * len(POOL_WINDOWS)]

    @pl.when(s_tile == 0)
    def _():
        ubuf_ref[0:MAX_WIN, :] = jnp.zeros((MAX_WIN, u.shape[1]), F32)

    @pl.when(s_tile != 0)
    def _():
        ubuf_ref[0:MAX_WIN, :] = ubuf_ref[tm:tm + MAX_WIN, :]

    ubuf_ref[MAX_WIN:MAX_WIN + tm, :] = u
    t_idx = s_tile * tm + lax.broadcasted_iota(jnp.int32, (tm, 1), 0)
    for g, win in enumerate(POOL_WINDOWS):
        sl = slice(g * LANES, (g + 1) * LANES)
        wsum = u[:, sl]
        for j in range(1, win):
            wsum = wsum + ubuf_ref[MAX_WIN - j:MAX_WIN - j + tm, sl]
        cnt = jnp.minimum(t_idx + 1, win).astype(F32)
        pooled = wsum / cnt - u[:, sl]
        mixed = jnp.dot(pooled.astype(BF16), wpool_ref[g], preferred_element_type=F32)
        pool_ref[0, :, sl] = (mixed * pscale_ref[:, sl]).astype(BF16)


def _inproj(pos3, x, scale1, shift1, g_mix, w_in_p, segsum, gq_t, gk_e, gkidx_e, invf, w_pool, pscale):
    b, s, d = x.shape
    tm = TM_PROJ
    d_attn = N_HEADS * HEAD_DIM
    d_qidx = N_IDX_HEADS * IDX_DIM
    d_pool = LANES * len(POOL_WINDOWS)
    tok = lambda w: pl.BlockSpec((1, tm, w), lambda bi, si: (bi, si, 0))
    per_b = pl.BlockSpec((1, 1, d), lambda bi, si: (bi, 0, 0))
    full = lambda a: pl.BlockSpec(a.shape, lambda bi, si: (0,) * a.ndim)
    return pl.pallas_call(
        _inproj_kernel,
        out_shape=(jax.ShapeDtypeStruct((b, s, d_attn), BF16),
                   jax.ShapeDtypeStruct((b, s, LANES), BF16),
                   jax.ShapeDtypeStruct((b, s, d_qidx), BF16),
                   jax.ShapeDtypeStruct((b, s, LANES), BF16),
                   jax.ShapeDtypeStruct((b, s, N_IDX_HEADS), F32),
                   jax.ShapeDtypeStruct((b, s, d_pool), BF16)),
        grid=(b, s // tm),
        in_specs=[tok(1), tok(d), per_b, per_b, full(g_mix), full(w_in_p), full(segsum),
                  full(gq_t), full(gk_e), full(gkidx_e), full(invf), full(w_pool), full(pscale)],
        out_specs=(tok(d_attn), tok(LANES), tok(d_qidx), tok(LANES), tok(N_IDX_HEADS), tok(d_pool)),
        scratch_shapes=[pltpu.VMEM((tm + 2 * MAX_WIN, d_pool), F32)],
        compiler_params=_cparams(("arbitrary", "arbitrary")),
        name="inproj",
    )(pos3, x, scale1, shift1, g_mix, w_in_p, segsum, gq_t, gk_e, gkidx_e, invf, w_pool, pscale)


def _dsa_kernel(qt_ref, qit_ref, w_ref, kv_ref, ki_ref, vt_ref, o_ref,
                sc_ref, qe_ref, qie_ref, m_ref, l_ref, acc_ref, p_ref):
    topk = float(min(TOPK_MAX, (sc_ref.shape[0] * K_BLK) // 4))
    qb = pl.program_id(1)
    n_heads = qt_ref.shape[3] // Q_BLK
    n_cols = qt_ref.shape[3]
    nkb = ((qb + 1) * Q_BLK + K_BLK - 1) // K_BLK
    kgrp = K_BLK // SUBLANES

    zeros_half = jnp.zeros((LANES - HEAD_DIM, n_cols), BF16)
    qe_ref[0:HEAD_DIM, :] = qt_ref[0, 0]
    qe_ref[HEAD_DIM:LANES, :] = zeros_half
    qie_ref[0:IDX_DIM, :] = qit_ref[0, 0]
    qie_ref[IDX_DIM:LANES, :] = zeros_half

    q_pos = qb * Q_BLK + lax.broadcasted_iota(jnp.int32, (K_BLK, Q_BLK), 1)
    key_off = lax.broadcasted_iota(jnp.int32, (K_BLK, Q_BLK), 0)
    w_row = w_ref[0, 0]

    def score_body(kb, carry):
        rmax, rmin = carry
        s_h = jnp.dot(ki_ref[0, kb], qie_ref[...], preferred_element_type=F32)
        s_h = jnp.maximum(s_h, 0.0) * w_row
        score = s_h[:, 0:Q_BLK]
        for hh in range(1, n_heads):
            score = score + s_h[:, hh * Q_BLK:(hh + 1) * Q_BLK]
        causal = (kb * K_BLK + key_off) <= q_pos
        sc_ref[kb] = jnp.where(causal, score, -jnp.inf)
        hi_part = jnp.where(causal, score, -jnp.inf).reshape(kgrp, SUBLANES, Q_BLK).max(axis=0)
        lo_part = jnp.where(causal, score, jnp.inf).reshape(kgrp, SUBLANES, Q_BLK).min(axis=0)
        return jnp.maximum(rmax, hi_part), jnp.minimum(rmin, lo_part)

    rmax8, rmin8 = lax.fori_loop(
        0, nkb, score_body,
        (jnp.full((SUBLANES, Q_BLK), -jnp.inf, F32), jnp.full((SUBLANES, Q_BLK), jnp.inf, F32)))
    rowmax = jnp.max(rmax8, axis=0, keepdims=True)
    rowmin = jnp.min(rmin8, axis=0, keepdims=True)

    n_causal = (qb * Q_BLK + 1 + lax.broadcasted_iota(jnp.int32, (1, Q_BLK), 1)).astype(F32)
    kt = jnp.minimum(n_causal, topk)

    def count_ge(t):
        def body(kb, acc):
            hit = jnp.where(sc_ref[kb] >= t, 1.0, 0.0)
            return acc + hit.reshape(kgrp, SUBLANES, Q_BLK).sum(axis=0)
        acc = lax.fori_loop(0, nkb, body, jnp.zeros((SUBLANES, Q_BLK), F32))
        return jnp.sum(acc, axis=0, keepdims=True)

    def bisect_pass(state):
        lo, hi, c_lo, c_hi, thr, done = state
        mid = jnp.where(hi == jnp.inf, rowmax, lo + 0.5 * (hi - lo))
        c = count_ge(mid)
        hit = jnp.logical_and(done == 0.0, c == kt)
        thr = jnp.where(hit, mid, thr)
        done = jnp.where(hit, 1.0, done)
        go_up = c >= kt
        active = done == 0.0
        lo_n = jnp.where(jnp.logical_and(active, go_up), mid, lo)
        c_lo_n = jnp.where(jnp.logical_and(active, go_up), c, c_lo)
        hi_n = jnp.where(jnp.logical_and(active, jnp.logical_not(go_up)), mid, hi)
        c_hi_n = jnp.where(jnp.logical_and(active, jnp.logical_not(go_up)), c, c_hi)
        return lo_n, hi_n, c_lo_n, c_hi_n, thr, done

    def snap_pass(state):
        lo, hi, c_lo, c_hi, thr, done = state

        def body(kb, carry):
            a8, b8 = carry
            s = sc_ref[kb]
            a = jnp.where(s >= lo, s, jnp.inf).reshape(kgrp, SUBLANES, Q_BLK).min(axis=0)
            b = jnp.where(s < hi, s, -jnp.inf).reshape(kgrp, SUBLANES, Q_BLK).max(axis=0)
            return jnp.minimum(a8, a), jnp.maximum(b8, b)

        a8, b8 = lax.fori_loop(
            0, nkb, body,
            (jnp.full((SUBLANES, Q_BLK), jnp.inf, F32), jnp.full((SUBLANES, Q_BLK), -jnp.inf, F32)))
        a = jnp.min(a8, axis=0, keepdims=True)
        b = jnp.max(b8, axis=0, keepdims=True)
        hit = jnp.logical_and(done == 0.0, a == b)
        thr = jnp.where(hit, a, thr)
        done = jnp.where(hit, 2.0, done)
        return lo, hi, c_lo, c_hi, thr, done

    few = n_causal <= topk
    state0 = (rowmin, jnp.full((1, Q_BLK), jnp.inf, F32), n_causal, jnp.zeros((1, Q_BLK), F32),
              jnp.where(few, F32_LOWEST, 0.0), jnp.where(few, 1.0, 0.0))

    def outer_cond(carry):
        return carry[1] > 0.0

    def outer_body(carry):
        state, _ = carry
        state = lax.fori_loop(0, 4, lambda i, st: bisect_pass(st), state)
        state = snap_pass(state)
        pending = jnp.max(jnp.where(state[5] == 0.0, 1.0, 0.0))
        return state, pending

    state1 = lax.fori_loop(0, 12, lambda i, st: bisect_pass(st), state0)
    state1 = snap_pass(state1)
    pending1 = jnp.max(jnp.where(state1[5] == 0.0, 1.0, 0.0))
    (lo, hi, c_lo, c_hi, thr, done), _ = lax.while_loop(outer_cond, outer_body, (state1, pending1))

    excess = jnp.where(done == 2.0, c_lo - kt, 0.0)
    need = kt - c_hi

    @pl.when(jnp.max(excess) > 0.0)
    def _():
        def count_tied_upto(m):
            def body(kb, acc):
                tied = jnp.logical_and(sc_ref[kb] == thr, (kb * K_BLK + key_off) <= m)
                return acc + jnp.where(tied, 1.0, 0.0).reshape(kgrp, SUBLANES, Q_BLK).sum(axis=0)
            acc = lax.fori_loop(0, nkb, body, jnp.zeros((SUBLANES, Q_BLK), F32))
            return jnp.sum(acc, axis=0, keepdims=True)

        def idx_body(i, carry):
            lo_i, hi_i = carry
            m = jnp.right_shift(lo_i + hi_i, 1)
            ok = count_tied_upto(m) >= need
            return jnp.where(ok, lo_i, m), jnp.where(ok, m, hi_i)

        n_keys = sc_ref.shape[0] * K_BLK
        _, cut = lax.fori_loop(
            0, max(1, (n_keys - 1).bit_length() + 1), idx_body,
            (jnp.full((1, Q_BLK), -1, jnp.int32), jnp.full((1, Q_BLK), n_keys - 1, jnp.int32)))

        def drop_body(kb, _):
            s = sc_ref[kb]
            drop = jnp.logical_and(jnp.logical_and(s == thr, (kb * K_BLK + key_off) > cut), excess > 0.0)
            sc_ref[kb] = jnp.where(drop, -jnp.inf, s)
            return 0

        lax.fori_loop(0, nkb, drop_body, 0)

    m_ref[...] = jnp.full(m_ref.shape, NEG_BIG, F32)
    l_ref[...] = jnp.zeros(l_ref.shape, F32)
    acc_ref[...] = jnp.zeros(acc_ref.shape, F32)

    def attn_body(kb, _):
        logits = jnp.dot(kv_ref[0, kb], qe_ref[...], preferred_element_type=F32)
        sel = sc_ref[kb] >= thr
        for hh in range(n_heads):
            cs = slice(hh * Q_BLK, (hh + 1) * Q_BLK)
            lg = jnp.where(sel, logits[:, cs], NEG_BIG)
            m_old = m_ref[:, cs]
            m_new = jnp.maximum(m_old, jnp.max(lg, axis=0, keepdims=True))
            alpha = jnp.exp(m_old - m_new)
            p = jnp.where(sel, jnp.exp(lg - m_new), 0.0)
            l_ref[:, cs] = alpha * l_ref[:, cs] + jnp.sum(p, axis=0, keepdims=True)
            m_ref[:, cs] = m_new
            acc_ref[:, cs] = acc_ref[:, cs] * alpha
            p_ref[:, cs] = p.astype(BF16)
        acc_ref[...] += jnp.dot(vt_ref[0, kb], p_ref[...], preferred_element_type=F32)
        return 0

    lax.fori_loop(0, nkb, attn_body, 0)
    o_ref[0, 0] = (acc_ref[...] / l_ref[...]).astype(o_ref.dtype)


def _dsa(qt, qit, w_t, kv4, ki4, vt4):
    b, nqb, dh, n_cols = qt.shape
    nkb_all = kv4.shape[1]
    per_q = lambda a: pl.BlockSpec((1, 1) + a.shape[2:], lambda bi, qi: (bi, qi, 0, 0))
    per_b = lambda a: pl.BlockSpec((1,) + a.shape[1:], lambda bi, qi: (bi, 0, 0, 0))
    return pl.pallas_call(
        _dsa_kernel,
        out_shape=jax.ShapeDtypeStruct((b, nqb, dh, n_cols), BF16),
        grid=(b, nqb),
        in_specs=[per_q(qt), per_q(qit), per_q(w_t), per_b(kv4), per_b(ki4), per_b(vt4)],
        out_specs=pl.BlockSpec((1, 1, dh, n_cols), lambda bi, qi: (bi, qi, 0, 0)),
        scratch_shapes=[pltpu.VMEM((nkb_all, K_BLK, Q_BLK), F32),
                        pltpu.VMEM((LANES, n_cols), BF16),
                        pltpu.VMEM((LANES, n_cols), BF16),
                        pltpu.VMEM((1, n_cols), F32),
                        pltpu.VMEM((1, n_cols), F32),
                        pltpu.VMEM((dh, n_cols), F32),
                        pltpu.VMEM((K_BLK, n_cols), BF16)],
        compiler_params=_cparams(("arbitrary", "arbitrary")),
        name="dsa",
    )(qt, qit, w_t, kv4, ki4, vt4)


def _outproj_kernel(x_ref, attn_ref, pool_ref, woa_ref, wop_ref, gate1_ref, gffn_ref,
                    scale2_ref, shift2_ref, wr_ref, br_ref, x1_ref, h2_ref, gates_ref):
    tm = x_ref.shape[1]
    mix = (jnp.dot(attn_ref[0], woa_ref[...], preferred_element_type=F32)
           + jnp.dot(pool_ref[0], wop_ref[...], preferred_element_type=F32))
    x1 = x_ref[0] + gate1_ref[0] * mix
    x1_ref[0] = x1
    ms = jnp.mean(x1 * x1, axis=-1, keepdims=True)
    h2 = (x1 * lax.rsqrt(ms + EPS) * gffn_ref[...]) * (1.0 + scale2_ref[0]) + shift2_ref[0]
    h2_hi = h2.astype(BF16)
    h2_ref[0] = h2_hi

    h2_lo = (h2 - h2_hi.astype(F32)).astype(BF16)
    wr = wr_ref[...]
    wr_hi = wr.astype(BF16)
    wr_lo = (wr - wr_hi.astype(F32)).astype(BF16)
    logits = (jnp.dot(h2_hi, wr_hi, preferred_element_type=F32)
              + jnp.dot(h2_lo, wr_hi, preferred_element_type=F32)
              + jnp.dot(h2_hi, wr_lo, preferred_element_type=F32)) + br_ref[...]

    lane = lax.broadcasted_iota(jnp.int32, (tm, LANES), 1)
    big = jnp.int32(LANES)
    is_g = jnp.logical_and(lane >= N_EXPERTS, lane < N_EXPERTS + N_GROUPS)
    glog = jnp.where(is_g, logits, -jnp.inf)
    gmax = jnp.max(glog, axis=-1, keepdims=True)
    gsum = jnp.sum(jnp.exp(glog - gmax), axis=-1, keepdims=True)
    p_g = 1.0 / gsum
    g_sel = jnp.min(jnp.where(glog == gmax, lane, big), axis=-1, keepdims=True) - N_EXPERTS
    in_grp = jnp.logical_and(lane < N_EXPERTS, jnp.right_shift(lane, 3) == g_sel)
    elog = jnp.where(in_grp, logits, -jnp.inf)
    emax = jnp.max(elog, axis=-1, keepdims=True)
    eexp = jnp.exp(elog - emax)
    esum = jnp.sum(eexp, axis=-1, keepdims=True)
    p_e = jnp.where(in_grp, eexp / esum, -1.0)
    p1 = jnp.max(p_e, axis=-1, keepdims=True)
    i1 = jnp.min(jnp.where(p_e == p1, lane, big), axis=-1, keepdims=True)
    p_e2 = jnp.where(lane == i1, -1.0, p_e)
    p2 = jnp.max(p_e2, axis=-1, keepdims=True)
    i2 = jnp.min(jnp.where(p_e2 == p2, lane, big), axis=-1, keepdims=True)
    tot = p1 + p2
    gates_ref[0] = (jnp.where(lane == i1, p_g * (p1 / tot), 0.0)
                    + jnp.where(lane == i2, p_g * (p2 / tot), 0.0))


def _outproj(x, attn, pool, wo_a, wo_p, gate1, g_ffn, scale2, shift2, w_r, b_r):
    b, s, d = x.shape
    tm = TM_PROJ
    tok = lambda w: pl.BlockSpec((1, tm, w), lambda bi, si: (bi, si, 0))
    per_b = pl.BlockSpec((1, 1, d), lambda bi, si: (bi, 0, 0))
    full = lambda a: pl.BlockSpec(a.shape, lambda bi, si: (0,) * a.ndim)
    return pl.pallas_call(
        _outproj_kernel,
        out_shape=(jax.ShapeDtypeStruct((b, s, d), F32),
                   jax.ShapeDtypeStruct((b, s, d), BF16),
                   jax.ShapeDtypeStruct((b, s, LANES), F32)),
        grid=(b, s // tm),
        in_specs=[tok(d), tok(attn.shape[2]), tok(pool.shape[2]), full(wo_a), full(wo_p), per_b,
                  full(g_ffn), per_b, per_b, full(w_r), full(b_r)],
        out_specs=(tok(d), tok(d), tok(LANES)),
        compiler_params=_cparams(("arbitrary", "arbitrary")),
        name="outproj",
    )(x, attn, pool, wo_a, wo_p, gate1, g_ffn, scale2, shift2, w_r, b_r)


def _moe_kernel(x1_ref, h2_ref, gates_ref, gate2_ref, wgu_ref, wd_ref, o_ref, acc_ref):
    e = pl.program_id(2)
    d_exp = wd_ref.shape[1]
    tm = h2_ref.shape[1]

    @pl.when(e == 0)
    def _():
        acc_ref[...] = jnp.zeros(acc_ref.shape, F32)

    gu = jnp.dot(h2_ref[0], wgu_ref[0], preferred_element_type=F32)
    g = gu[:, :d_exp]
    a = (g * jax.nn.sigmoid(g)) * gu[:, d_exp:]
    lane = lax.broadcasted_iota(jnp.int32, (tm, LANES), 1)
    gate_e = jnp.sum(jnp.where(lane == e, gates_ref[0], 0.0), axis=-1, keepdims=True)
    acc_ref[...] += jnp.dot((a * gate_e).astype(BF16), wd_ref[0], preferred_element_type=F32)

    @pl.when(e == pl.num_programs(2) - 1)
    def _():
        o_ref[0] = x1_ref[0] + gate2_ref[0] * acc_ref[...]


def _moe(x1, h2, gates, gate2, w_gu, w_d):
    b, s, d = x1.shape
    tm = TM_MOE
    n_exp = w_gu.shape[0]
    tok = lambda w: pl.BlockSpec((1, tm, w), lambda bi, si, e: (bi, si, 0))
    return pl.pallas_call(
        _moe_kernel,
        out_shape=jax.ShapeDtypeStruct((b, s, d), F32),
        grid=(b, s // tm, n_exp),
        in_specs=[tok(d), tok(d), tok(LANES),
                  pl.BlockSpec((1, 1, d), lambda bi, si, e: (bi, 0, 0)),
                  pl.BlockSpec((1,) + w_gu.shape[1:], lambda bi, si, e: (e, 0, 0)),
                  pl.BlockSpec((1,) + w_d.shape[1:], lambda bi, si, e: (e, 0, 0))],
        out_specs=tok(d),
        scratch_shapes=[pltpu.VMEM((tm, d), F32)],
        compiler_params=_cparams(("arbitrary", "arbitrary", "arbitrary")),
        name="moe",
    )(x1, h2, gates, gate2, w_gu, w_d)


def _layer(x, mod, pos3, g_mix, g_ffn, w_in, g_q, g_k, g_kidx, w_pool, pool_scale, w_out,
           w_rg, b_rg, w_re, b_re, w_gate, w_up, w_down):
    b, s, d = x.shape
    d_attn = N_HEADS * HEAD_DIM
    nqb = s // Q_BLK
    nkb = s // K_BLK
    shift1, scale1, gate1, shift2, scale2, gate2 = [m[:, None, :] for m in jnp.split(mod, 6, axis=-1)]

    n_front = d_attn + 2 * HEAD_DIM + N_IDX_HEADS * IDX_DIM + IDX_DIM + N_IDX_HEADS
    pad = (-n_front) % LANES
    w_in_p = jnp.concatenate([w_in[:, :n_front], jnp.zeros((d, pad), w_in.dtype), w_in[:, n_front:]],
                             axis=1).astype(BF16)
    seg_id = jnp.arange(d_attn) // HEAD_DIM
    segsum = (seg_id[:, None] == seg_id[None, :]).astype(BF16)
    ones_half = jnp.ones((LANES - HEAD_DIM,), F32)
    gq_t = jnp.tile(g_q, N_HEADS)[None, :]
    gk_e = jnp.concatenate([g_k, ones_half])[None, :]
    gkidx_e = jnp.concatenate([g_kidx, ones_half])[None, :]
    half = HEAD_DIM // 2
    inv_freq = ROPE_THETA ** (-jnp.arange(0, HEAD_DIM, 2, dtype=F32) / HEAD_DIM)
    invf = jnp.tile(inv_freq, LANES // half)[None, :]

    q, kv, qi, ki, wi, pool = _inproj(pos3, x, scale1, shift1, g_mix[None, :], w_in_p, segsum, gq_t,
                                      gk_e, gkidx_e, invf, w_pool.astype(BF16), pool_scale[None, :])

    def to_cols(a, width):
        n_h = a.shape[2] // width
        a = a.reshape(b, nqb, Q_BLK, n_h, width)
        return jnp.transpose(a, (0, 1, 4, 3, 2)).reshape(b, nqb, width, n_h * Q_BLK)

    qt = to_cols(q, HEAD_DIM)
    qit = to_cols(qi, IDX_DIM)
    w_t = to_cols(wi, 1)
    kv4 = kv.reshape(b, nkb, K_BLK, LANES)
    ki4 = ki.reshape(b, nkb, K_BLK, LANES)
    vt4 = jnp.transpose(kv[:, :, HEAD_DIM:].reshape(b, nkb, K_BLK, HEAD_DIM), (0, 1, 3, 2))

    attn_t = _dsa(qt, qit, w_t, kv4, ki4, vt4)
    attn = jnp.transpose(attn_t.reshape(b, nqb, HEAD_DIM, N_HEADS, Q_BLK), (0, 1, 4, 3, 2))
    attn = attn.reshape(b, s, d_attn)

    w_out_b = w_out.astype(BF16)
    w_r = jnp.concatenate([w_re, w_rg, jnp.zeros((d, LANES - N_EXPERTS - N_GROUPS), F32)], axis=1)
    b_r = jnp.concatenate([b_re, b_rg, jnp.zeros((LANES - N_EXPERTS - N_GROUPS,), F32)])[None, :]
    x1, h2, gates = _outproj(x, attn, pool, w_out_b[:d_attn], w_out_b[d_attn:], gate1,
                             g_ffn[None, :], scale2, shift2, w_r, b_r)

    w_gu = jnp.concatenate([w_gate, w_up], axis=-1).astype(BF16)
    return _moe(x1, h2, gates, gate2, w_gu, w_down.astype(BF16))


def kernel(x, c, positions, w_ada, b_ada, g_norm_mix, g_norm_ffn, w_in, g_q, g_k, g_kidx, w_pool,
           pool_scale, w_out, w_router_group, b_router_group, w_router_expert, b_router_expert,
           w_gate, w_up, w_down):
    b, s, d = x.shape
    depth = w_ada.shape[0]
    assert s % TM_MOE == 0 and s % K_BLK == 0 and d % LANES == 0
    pos3 = positions[:, :, None]
    c_pad = jnp.concatenate([c, jnp.zeros((-b % SUBLANES, d), c.dtype)], axis=0)
    for l in range(depth):
        mod = _adaln(c_pad, w_ada[l], b_ada[l][None, :])[:b]
        x = _layer(x, mod, pos3, g_norm_mix[l], g_norm_ffn[l], w_in[l], g_q[l], g_k[l], g_kidx[l],
                   w_pool[l], pool_scale[l], w_out[l], w_router_group[l], b_router_group[l],
                   w_router_expert[l], b_router_expert[l], w_gate[l], w_up[l], w_down[l])
    return x
```

```python
import functools

import jax
import jax.numpy as jnp
from jax import lax
from jax.experimental import pallas as pl
from jax.experimental.pallas import tpu as pltpu

N_HEADS = 8
HEAD_DIM = 64
N_IDX_HEADS = 8
IDX_DIM = 64
TOPK_MAX = 256
ROPE_THETA = 10000.0
POOL_WINDOWS = (2, 4, 8, 16)
N_GROUPS = 4
EXPERTS_PER_GROUP = 8
N_EXPERTS = N_GROUPS * EXPERTS_PER_GROUP
EPS = 1e-6

LANES = 128
SUBLANES = 8
VMEM_LIMIT_BYTES = 56 * 1024 * 1024

Q_BLK = 128
K_BLK = 256
COL_BLK = 256
CNT_BLK = 512
CNT_ROWS = 32
TM_PROJ = 512
TM_MOE = 1024
MAX_WIN = max(POOL_WINDOWS)
M_INIT = -1e29
MASKED = -1e30
F32_LOWEST = -3.0e38
LOG2_E = 1.4426950408889634

BF16 = jnp.bfloat16
F32 = jnp.float32


def _cparams(sem):
    return pltpu.CompilerParams(dimension_semantics=sem, vmem_limit_bytes=VMEM_LIMIT_BYTES)


def _adaln_kernel(c_ref, w_ref, b_ref, o_ref):
    c = c_ref[...]
    c_act = c * jax.nn.sigmoid(c)
    o_ref[...] = jnp.dot(c_act, w_ref[...], preferred_element_type=F32) + b_ref[...]


def _adaln(c_pad, w_ada, b_ada):
    rows, d = c_pad.shape
    n = w_ada.shape[1]
    tn = n // 6
    return pl.pallas_call(
        _adaln_kernel,
        out_shape=jax.ShapeDtypeStruct((rows, n), F32),
        grid=(n // tn,),
        in_specs=[pl.BlockSpec((rows, d), lambda j: (0, 0)),
                  pl.BlockSpec((d, tn), lambda j: (0, j)),
                  pl.BlockSpec((1, tn), lambda j: (0, j))],
        out_specs=pl.BlockSpec((rows, tn), lambda j: (0, j)),
        compiler_params=_cparams(("arbitrary",)),
        name="adaln",
    )(c_pad, w_ada, b_ada)


def _rope_chunk(y, cos, sin_signed, first_half):
    from_hi = pltpu.roll(y, LANES - HEAD_DIM // 2, 1)
    from_lo = pltpu.roll(y, HEAD_DIM // 2, 1)
    return y * cos + jnp.where(first_half, from_hi, from_lo) * sin_signed


def _inproj_kernel(pos_ref, x_ref, scale_ref, shift_ref, gmix_ref, win_ref, segsum_ref,
                   gq_ref, gk_ref, gkidx_ref, invf_ref, wpool_ref, pscale_ref,
                   q_ref, kv_ref, qi_ref, ki_ref, wi_ref, pool_ref, ubuf_ref):
    tm = x_ref.shape[1]
    d_attn = N_HEADS * HEAD_DIM
    d_qidx = N_IDX_HEADS * IDX_DIM
    s_tile = pl.program_id(1)

    x = x_ref[0]
    ms = jnp.mean(x * x, axis=-1, keepdims=True)
    h = (x * lax.rsqrt(ms + EPS) * gmix_ref[...]) * (1.0 + scale_ref[0]) + shift_ref[0]
    proj = jnp.dot(h.astype(BF16), win_ref[...], preferred_element_type=F32)

    lane = lax.broadcasted_iota(jnp.int32, (tm, LANES), 1)
    first_half = (lane & (HEAD_DIM - 1)) < (HEAD_DIM // 2)
    ang = pos_ref[0].astype(F32) * invf_ref[...]
    cos = jnp.cos(ang)
    sin = jnp.sin(ang)
    sin_signed = jnp.where(first_half, -sin, sin)
    rope = functools.partial(_rope_chunk, cos=cos, sin_signed=sin_signed, first_half=first_half)

    qf = proj[:, :d_attn]
    qsq = qf * qf
    qsq_hi = qsq.astype(BF16)
    qsq_lo = (qsq - qsq_hi.astype(F32)).astype(BF16)
    seg = segsum_ref[...]
    ssq = (jnp.dot(qsq_hi, seg, preferred_element_type=F32)
           + jnp.dot(qsq_lo, seg, preferred_element_type=F32))
    qn = qf * lax.rsqrt(ssq * (1.0 / HEAD_DIM) + EPS) * gq_ref[...]
    for j in range(d_attn // LANES):
        sl = slice(j * LANES, (j + 1) * LANES)
        q_ref[0, :, sl] = (rope(qn[:, sl]) * (LOG2_E * HEAD_DIM ** -0.5)).astype(BF16)

    kvc = proj[:, d_attn:d_attn + LANES]
    is_k = lane < HEAD_DIM
    ksq = jnp.sum(jnp.where(is_k, kvc * kvc, 0.0), axis=-1, keepdims=True)
    kn = kvc * lax.rsqrt(ksq * (1.0 / HEAD_DIM) + EPS) * gk_ref[...]
    kv_ref[0] = jnp.where(is_k, rope(kn), kvc).astype(BF16)

    o_qi = d_attn + LANES
    for j in range(d_qidx // LANES):
        qi_ref[0, :, j * LANES:(j + 1) * LANES] = rope(
            proj[:, o_qi + j * LANES:o_qi + (j + 1) * LANES]).astype(BF16)

    o_ki = o_qi + d_qidx
    kic = proj[:, o_ki:o_ki + LANES]
    kisq = jnp.sum(jnp.where(is_k, kic * kic, 0.0), axis=-1, keepdims=True)
    kin = kic * lax.rsqrt(kisq * (1.0 / IDX_DIM) + EPS) * gkidx_ref[...]
    ki_ref[0] = jnp.where(is_k, rope(kin), 0.0).astype(BF16)
    wi_ref[0] = kic[:, IDX_DIM:IDX_DIM + N_IDX_HEADS] * (N_IDX_HEADS ** -0.5 * IDX_DIM ** -0.5)

    o_u = o_ki + LANES
    u = proj[:, o_u:o_u + LANES * len(POOL_WINDOWS)]

    @pl.when(s_tile == 0)
    def _():
        ubuf_ref[0:MAX_WIN, :] = jnp.zeros((MAX_WIN, u.shape[1]), F32)

    @pl.when(s_tile != 0)
    def _():
        ubuf_ref[0:MAX_WIN, :] = ubuf_ref[tm:tm + MAX_WIN, :]

    ubuf_ref[MAX_WIN:MAX_WIN + tm, :] = u
    t_idx = s_tile * tm + lax.broadcasted_iota(jnp.int32, (tm, 1), 0)
    for g, win in enumerate(POOL_WINDOWS):
        sl = slice(g * LANES, (g + 1) * LANES)
        wsum = u[:, sl]
        for j in range(1, win):
            wsum = wsum + ubuf_ref[MAX_WIN - j:MAX_WIN - j + tm, sl]
        cnt = jnp.minimum(t_idx + 1, win).astype(F32)
        pooled = wsum / cnt - u[:, sl]
        mixed = jnp.dot(pooled.astype(BF16), wpool_ref[g], preferred_element_type=F32)
        pool_ref[0, :, sl] = (mixed * pscale_ref[:, sl]).astype(BF16)


def _inproj(pos3, x, scale1, shift1, g_mix, w_in_p, segsum, gq_t, gk_e, gkidx_e, invf, w_pool, pscale):
    b, s, d = x.shape
    tm = TM_PROJ
    d_attn = N_HEADS * HEAD_DIM
    d_qidx = N_IDX_HEADS * IDX_DIM
    d_pool = LANES * len(POOL_WINDOWS)
    tok = lambda w: pl.BlockSpec((1, tm, w), lambda bi, si: (bi, si, 0))
    per_b = pl.BlockSpec((1, 1, d), lambda bi, si: (bi, 0, 0))
    full = lambda a: pl.BlockSpec(a.shape, lambda bi, si: (0,) * a.ndim)
    return pl.pallas_call(
        _inproj_kernel,
        out_shape=(jax.ShapeDtypeStruct((b, s, d_attn), BF16),
                   jax.ShapeDtypeStruct((b, s, LANES), BF16),
                   jax.ShapeDtypeStruct((b, s, d_qidx), BF16),
                   jax.ShapeDtypeStruct((b, s, LANES), BF16),
                   jax.ShapeDtypeStruct((b, s, N_IDX_HEADS), F32),
                   jax.ShapeDtypeStruct((b, s, d_pool), BF16)),
        grid=(b, s // tm),
        in_specs=[tok(1), tok(d), per_b, per_b, full(g_mix), full(w_in_p), full(segsum),
                  full(gq_t), full(gk_e), full(gkidx_e), full(invf), full(w_pool), full(pscale)],
        out_specs=(tok(d_attn), tok(LANES), tok(d_qidx), tok(LANES), tok(N_IDX_HEADS), tok(d_pool)),
        scratch_shapes=[pltpu.VMEM((tm + 2 * MAX_WIN, d_pool), F32)],
        compiler_params=_cparams(("arbitrary", "arbitrary")),
        name="inproj",
    )(pos3, x, scale1, shift1, g_mix, w_in_p, segsum, gq_t, gk_e, gkidx_e, invf, w_pool, pscale)


def _dsa_kernel(qt_ref, qit_ref, w_ref, kv_ref, ki_ref, vt_ref, o_ref,
                sc_ref, qe_ref, qie_ref, m_ref, alpha_ref, acc_ref, p_ref):
    topk = float(min(TOPK_MAX, (sc_ref.shape[0] * CNT_BLK) // 4))
    qb = pl.program_id(1)
    n_cols = qt_ref.shape[3]
    n_chunks = n_cols // COL_BLK
    sub = CNT_BLK // K_BLK
    nch = ((qb + 1) * Q_BLK + CNT_BLK - 1) // CNT_BLK
    kgrp = K_BLK // SUBLANES
    sub_rows = [slice(j * K_BLK, (j + 1) * K_BLK) for j in range(sub)]

    zeros_half = jnp.zeros((LANES - HEAD_DIM, n_cols), BF16)
    qe_ref[0:HEAD_DIM, :] = qt_ref[0, 0]
    qe_ref[HEAD_DIM:LANES, :] = zeros_half
    qie_ref[0:IDX_DIM, :] = qit_ref[0, 0]
    qie_ref[IDX_DIM:LANES, :] = zeros_half

    q_pos = qb * Q_BLK + lax.broadcasted_iota(jnp.int32, (K_BLK, Q_BLK), 1)
    key_off = lax.broadcasted_iota(jnp.int32, (K_BLK, Q_BLK), 0)

    def score_body(ch, carry):
        rmax, rmin = carry
        for j in range(sub):
            ki_blk = ki_ref[0, ch, sub_rows[j], :]
            score = None
            for cc in range(n_chunks):
                cs = slice(cc * COL_BLK, (cc + 1) * COL_BLK)
                s_h = jnp.dot(ki_blk, qie_ref[:, cs], preferred_element_type=F32)
                s_h = jnp.maximum(s_h, 0.0) * w_ref[0, 0, :, cs]
                part = s_h[:, :Q_BLK] + s_h[:, Q_BLK:]
                score = part if score is None else score + part
            causal = (ch * CNT_BLK + j * K_BLK + key_off) <= q_pos
            masked = jnp.where(causal, score, -jnp.inf)
            sc_ref[ch, sub_rows[j], :] = masked
            hi_part = masked.reshape(kgrp, SUBLANES, Q_BLK).max(axis=0)
            lo_part = jnp.where(causal, score, jnp.inf).reshape(kgrp, SUBLANES, Q_BLK).min(axis=0)
            rmax, rmin = jnp.maximum(rmax, hi_part), jnp.minimum(rmin, lo_part)
        return rmax, rmin

    rmax8, rmin8 = lax.fori_loop(
        0, nch, score_body,
        (jnp.full((SUBLANES, Q_BLK), -jnp.inf, F32), jnp.full((SUBLANES, Q_BLK), jnp.inf, F32)))
    rowmax = jnp.max(rmax8, axis=0, keepdims=True)
    rowmin = jnp.min(rmin8, axis=0, keepdims=True)

    n_causal = (qb * Q_BLK + 1 + lax.broadcasted_iota(jnp.int32, (1, Q_BLK), 1)).astype(F32)
    kt = jnp.minimum(n_causal, topk)

    cgrp = CNT_BLK // CNT_ROWS

    def count_ge(t):
        def body(ch, acc):
            hit = jnp.where(sc_ref[ch] >= t, 1.0, 0.0)
            return acc + hit.reshape(cgrp, CNT_ROWS, Q_BLK).sum(axis=0)
        acc = lax.fori_loop(0, nch, body, jnp.zeros((CNT_ROWS, Q_BLK), F32))
        return jnp.sum(acc, axis=0, keepdims=True)

    def bisect_pass(state):
        lo, hi, c_lo, c_hi, thr, done = state
        mid = jnp.where(hi == jnp.inf, rowmax, lo + 0.5 * (hi - lo))
        c = count_ge(mid)
        hit = jnp.logical_and(done == 0.0, c == kt)
        thr = jnp.where(hit, mid, thr)
        done = jnp.where(hit, 1.0, done)
        go_up = c >= kt
        active = done == 0.0
        lo_n = jnp.where(jnp.logical_and(active, go_up), mid, lo)
        c_lo_n = jnp.where(jnp.logical_and(active, go_up), c, c_lo)
        hi_n = jnp.where(jnp.logical_and(active, jnp.logical_not(go_up)), mid, hi)
        c_hi_n = jnp.where(jnp.logical_and(active, jnp.logical_not(go_up)), c, c_hi)
        return lo_n, hi_n, c_lo_n, c_hi_n, thr, done

    def snap_pass(state):
        lo, hi, c_lo, c_hi, thr, done = state

        def body(ch, carry):
            a8, b8 = carry
            s = sc_ref[ch]
            a = jnp.where(s >= lo, s, jnp.inf).reshape(cgrp, CNT_ROWS, Q_BLK).min(axis=0)
            b = jnp.where(s < hi, s, -jnp.inf).reshape(cgrp, CNT_ROWS, Q_BLK).max(axis=0)
            return jnp.minimum(a8, a), jnp.maximum(b8, b)

        a8, b8 = lax.fori_loop(
            0, nch, body,
            (jnp.full((CNT_ROWS, Q_BLK), jnp.inf, F32), jnp.full((CNT_ROWS, Q_BLK), -jnp.inf, F32)))
        a = jnp.min(a8, axis=0, keepdims=True)
        b = jnp.max(b8, axis=0, keepdims=True)
        hit = jnp.logical_and(done == 0.0, a == b)
        thr = jnp.where(hit, a, thr)
        done = jnp.where(hit, 2.0, done)
        return lo, hi, c_lo, c_hi, thr, done

    few = n_causal <= topk
    state0 = (rowmin, jnp.full((1, Q_BLK), jnp.inf, F32), n_causal, jnp.zeros((1, Q_BLK), F32),
              jnp.where(few, F32_LOWEST, 0.0), jnp.where(few, 1.0, 0.0))

    def outer_cond(carry):
        return carry[1] > 0.0

    def outer_body(carry):
        state, _ = carry
        state = lax.fori_loop(0, 3, lambda i, st: bisect_pass(st), state)
        state = snap_pass(state)
        pending = jnp.max(jnp.where(state[5] == 0.0, 1.0, 0.0))
        return state, pending

    state1 = lax.fori_loop(0, 14, lambda i, st: bisect_pass(st), state0)
    state1 = snap_pass(state1)
    pending1 = jnp.max(jnp.where(state1[5] == 0.0, 1.0, 0.0))
    (lo, hi, c_lo, c_hi, thr, done), _ = lax.while_loop(outer_cond, outer_body, (state1, pending1))

    excess = jnp.where(done == 2.0, c_lo - kt, 0.0)
    need = kt - c_hi

    @pl.when(jnp.max(excess) > 0.0)
    def _():
        tri = (lax.broadcasted_iota(jnp.int32, (K_BLK, K_BLK), 0)
               >= lax.broadcasted_iota(jnp.int32, (K_BLK, K_BLK), 1)).astype(BF16)
        has_excess = excess > 0.0

        def drop_body(ch, run):
            for j in range(sub):
                s = sc_ref[ch, sub_rows[j], :]
                tied = jnp.logical_and(s == thr, has_excess)
                prefix = jnp.dot(tri, jnp.where(tied, 1.0, 0.0).astype(BF16), preferred_element_type=F32)
                drop = jnp.logical_and(tied, run + prefix > need)
                sc_ref[ch, sub_rows[j], :] = jnp.where(drop, -jnp.inf, s)
                run = run + jnp.max(prefix, axis=0, keepdims=True)
            return run

        lax.fori_loop(0, nch, drop_body, jnp.zeros((1, Q_BLK), F32))

    m_ref[...] = jnp.full(m_ref.shape, M_INIT, F32)
    acc_ref[...] = jnp.zeros(acc_ref.shape, F32)

    def attn_body(ch, _):
        for j in range(sub):
            kv_blk = kv_ref[0, ch, sub_rows[j], :]
            bias = jnp.where(sc_ref[ch, sub_rows[j], :] >= thr, 0.0, MASKED)
            for cc in range(n_chunks):
                logits = jnp.dot(kv_blk, qe_ref[:, cc * COL_BLK:(cc + 1) * COL_BLK],
                                 preferred_element_type=F32)
                for hh in range(COL_BLK // Q_BLK):
                    cs = slice(cc * COL_BLK + hh * Q_BLK, cc * COL_BLK + (hh + 1) * Q_BLK)
                    lg = logits[:, hh * Q_BLK:(hh + 1) * Q_BLK] + bias
                    m_old = m_ref[:, cs]
                    m_new = jnp.maximum(m_old, jnp.max(lg, axis=0, keepdims=True))
                    alpha_ref[:, cs] = jnp.exp2(m_old - m_new)
                    m_ref[:, cs] = m_new
                    p_ref[j, :, cs] = jnp.exp2(lg - m_new).astype(BF16)
            acc_ref[...] = acc_ref[...] * alpha_ref[...] + jnp.dot(
                vt_ref[0, ch * sub + j], p_ref[j], preferred_element_type=F32)
        return 0

    lax.fori_loop(0, nch, attn_body, 0)
    dh = o_ref.shape[2]
    o_ref[0, 0] = (acc_ref[0:dh, :] / acc_ref[dh:dh + 1, :]).astype(o_ref.dtype)


def _dsa(qt, qit, w_t, kv4, ki4, vt4):
    b, nqb, dh, n_cols = qt.shape
    n_steps = kv4.shape[1]
    assert kv4.shape[2] == CNT_BLK and CNT_BLK % K_BLK == 0 and n_cols % COL_BLK == 0
    assert vt4.shape[1] * K_BLK == n_steps * CNT_BLK and vt4.shape[2] == dh + SUBLANES
    per_q = lambda a: pl.BlockSpec((1, 1) + a.shape[2:], lambda bi, qi: (bi, qi, 0, 0))
    per_b = lambda a: pl.BlockSpec((1,) + a.shape[1:], lambda bi, qi: (bi, 0, 0, 0))
    return pl.pallas_call(
        _dsa_kernel,
        out_shape=jax.ShapeDtypeStruct((b, nqb, dh, n_cols), BF16),
        grid=(b, nqb),
        in_specs=[per_q(qt), per_q(qit), per_q(w_t), per_b(kv4), per_b(ki4), per_b(vt4)],
        out_specs=pl.BlockSpec((1, 1, dh, n_cols), lambda bi, qi: (bi, qi, 0, 0)),
        scratch_shapes=[pltpu.VMEM((n_steps, CNT_BLK, Q_BLK), F32),
                        pltpu.VMEM((LANES, n_cols), BF16),
                        pltpu.VMEM((LANES, n_cols), BF16),
                        pltpu.VMEM((1, n_cols), F32),
                        pltpu.VMEM((1, n_cols), F32),
                        pltpu.VMEM((dh + SUBLANES, n_cols), F32),
                        pltpu.VMEM((CNT_BLK // K_BLK, K_BLK, n_cols), BF16)],
        compiler_params=_cparams(("arbitrary", "arbitrary")),
        name="dsa",
    )(qt, qit, w_t, kv4, ki4, vt4)


def _outproj_kernel(x_ref, attn_ref, pool_ref, woa_ref, wop_ref, gate1_ref, gffn_ref,
                    scale2_ref, shift2_ref, wr_ref, br_ref, x1_ref, h2_ref, gates_ref):
    tm = x_ref.shape[1]
    mix = (jnp.dot(attn_ref[0], woa_ref[...], preferred_element_type=F32)
           + jnp.dot(pool_ref[0], wop_ref[...], preferred_element_type=F32))
    x1 = x_ref[0] + gate1_ref[0] * mix
    x1_ref[0] = x1
    ms = jnp.mean(x1 * x1, axis=-1, keepdims=True)
    h2 = (x1 * lax.rsqrt(ms + EPS) * gffn_ref[...]) * (1.0 + scale2_ref[0]) + shift2_ref[0]
    h2_hi = h2.astype(BF16)
    h2_ref[0] = h2_hi

    h2_lo = (h2 - h2_hi.astype(F32)).astype(BF16)
    wr = wr_ref[...]
    wr_hi = wr.astype(BF16)
    wr_lo = (wr - wr_hi.astype(F32)).astype(BF16)
    logits = (jnp.dot(h2_hi, wr_hi, preferred_element_type=F32)
              + jnp.dot(h2_lo, wr_hi, preferred_element_type=F32)
              + jnp.dot(h2_hi, wr_lo, preferred_element_type=F32)) + br_ref[...]

    lane = lax.broadcasted_iota(jnp.int32, (tm, LANES), 1)
    big = jnp.int32(LANES)
    is_g = jnp.logical_and(lane >= N_EXPERTS, lane < N_EXPERTS + N_GROUPS)
    glog = jnp.where(is_g, logits, -jnp.inf)
    gmax = jnp.max(glog, axis=-1, keepdims=True)
    gsum = jnp.sum(jnp.exp(glog - gmax), axis=-1, keepdims=True)
    p_g = 1.0 / gsum
    g_sel = jnp.min(jnp.where(glog == gmax, lane, big), axis=-1, keepdims=True) - N_EXPERTS
    in_grp = jnp.logical_and(lane < N_EXPERTS, jnp.right_shift(lane, 3) == g_sel)
    elog = jnp.where(in_grp, logits, -jnp.inf)
    emax = jnp.max(elog, axis=-1, keepdims=True)
    eexp = jnp.exp(elog - emax)
    esum = jnp.sum(eexp, axis=-1, keepdims=True)
    p_e = jnp.where(in_grp, eexp / esum, -1.0)
    p1 = jnp.max(p_e, axis=-1, keepdims=True)
    i1 = jnp.min(jnp.where(p_e == p1, lane, big), axis=-1, keepdims=True)
    p_e2 = jnp.where(lane == i1, -1.0, p_e)
    p2 = jnp.max(p_e2, axis=-1, keepdims=True)
    i2 = jnp.min(jnp.where(p_e2 == p2, lane, big), axis=-1, keepdims=True)
    tot = p1 + p2
    gates_ref[0] = (jnp.where(lane == i1, p_g * (p1 / tot), 0.0)
                    + jnp.where(lane == i2, p_g * (p2 / tot), 0.0))


def _outproj(x, attn, pool, wo_a, wo_p, gate1, g_ffn, scale2, shift2, w_r, b_r):
    b, s, d = x.shape
    tm = TM_PROJ
    tok = lambda w: pl.BlockSpec((1, tm, w), lambda bi, si: (bi, si, 0))
    per_b = pl.BlockSpec((1, 1, d), lambda bi, si: (bi, 0, 0))
    full = lambda a: pl.BlockSpec(a.shape, lambda bi, si: (0,) * a.ndim)
    return pl.pallas_call(
        _outproj_kernel,
        out_shape=(jax.ShapeDtypeStruct((b, s, d), F32),
                   jax.ShapeDtypeStruct((b, s, d), BF16),
                   jax.ShapeDtypeStruct((b, s, LANES), F32)),
        grid=(b, s // tm),
        in_specs=[tok(d), tok(attn.shape[2]), tok(pool.shape[2]), full(wo_a), full(wo_p), per_b,
                  full(g_ffn), per_b, per_b, full(w_r), full(b_r)],
        out_specs=(tok(d), tok(d), tok(LANES)),
        compiler_params=_cparams(("arbitrary", "arbitrary")),
        name="outproj",
    )(x, attn, pool, wo_a, wo_p, gate1, g_ffn, scale2, shift2, w_r, b_r)


def _moe_kernel(x1_ref, h2_ref, gates_ref, gate2_ref, wgu_ref, wd_ref, o_ref, acc_ref):
    e = pl.program_id(2)
    d_exp = wd_ref.shape[1]
    tm = h2_ref.shape[1]

    @pl.when(e == 0)
    def _():
        acc_ref[...] = jnp.zeros(acc_ref.shape, F32)

    gu = jnp.dot(h2_ref[0], wgu_ref[0], preferred_element_type=F32)
    g = gu[:, :d_exp]
    a = (g * jax.nn.sigmoid(g)) * gu[:, d_exp:]
    lane = lax.broadcasted_iota(jnp.int32, (tm, LANES), 1)
    gate_e = jnp.sum(jnp.where(lane == e, gates_ref[0], 0.0), axis=-1, keepdims=True)
    acc_ref[...] += jnp.dot((a * gate_e).astype(BF16), wd_ref[0], preferred_element_type=F32)

    @pl.when(e == pl.num_programs(2) - 1)
    def _():
        o_ref[0] = x1_ref[0] + gate2_ref[0] * acc_ref[...]


def _moe(x1, h2, gates, gate2, w_gu, w_d):
    b, s, d = x1.shape
    tm = TM_MOE
    n_exp = w_gu.shape[0]
    tok = lambda w: pl.BlockSpec((1, tm, w), lambda bi, si, e: (bi, si, 0))
    return pl.pallas_call(
        _moe_kernel,
        out_shape=jax.ShapeDtypeStruct((b, s, d), F32),
        grid=(b, s // tm, n_exp),
        in_specs=[tok(d), tok(d), tok(LANES),
                  pl.BlockSpec((1, 1, d), lambda bi, si, e: (bi, 0, 0)),
                  pl.BlockSpec((1,) + w_gu.shape[1:], lambda bi, si, e: (e, 0, 0)),
                  pl.BlockSpec((1,) + w_d.shape[1:], lambda bi, si, e: (e, 0, 0))],
        out_specs=tok(d),
        scratch_shapes=[pltpu.VMEM((tm, d), F32)],
        compiler_params=_cparams(("arbitrary", "arbitrary", "arbitrary")),
        name="moe",
    )(x1, h2, gates, gate2, w_gu, w_d)


def _layer(x, mod, pos3, g_mix, g_ffn, w_in, g_q, g_k, g_kidx, w_pool, pool_scale, w_out,
           w_rg, b_rg, w_re, b_re, w_gate, w_up, w_down):
    b, s, d = x.shape
    d_attn = N_HEADS * HEAD_DIM
    nqb = s // Q_BLK
    nkb = s // K_BLK
    shift1, scale1, gate1, shift2, scale2, gate2 = [m[:, None, :] for m in jnp.split(mod, 6, axis=-1)]

    n_front = d_attn + 2 * HEAD_DIM + N_IDX_HEADS * IDX_DIM + IDX_DIM + N_IDX_HEADS
    pad = (-n_front) % LANES
    w_in_p = jnp.concatenate([w_in[:, :n_front], jnp.zeros((d, pad), w_in.dtype), w_in[:, n_front:]],
                             axis=1).astype(BF16)
    seg_id = jnp.arange(d_attn) // HEAD_DIM
    segsum = (seg_id[:, None] == seg_id[None, :]).astype(BF16)
    ones_half = jnp.ones((LANES - HEAD_DIM,), F32)
    gq_t = jnp.tile(g_q, N_HEADS)[None, :]
    gk_e = jnp.concatenate([g_k, ones_half])[None, :]
    gkidx_e = jnp.concatenate([g_kidx, ones_half])[None, :]
    half = HEAD_DIM // 2
    inv_freq = ROPE_THETA ** (-jnp.arange(0, HEAD_DIM, 2, dtype=F32) / HEAD_DIM)
    invf = jnp.tile(inv_freq, LANES // half)[None, :]

    q, kv, qi, ki, wi, pool = _inproj(pos3, x, scale1, shift1, g_mix[None, :], w_in_p, segsum, gq_t,
                                      gk_e, gkidx_e, invf, w_pool.astype(BF16), pool_scale[None, :])

    def to_cols(a, width):
        n_h = a.shape[2] // width
        a = a.reshape(b, nqb, Q_BLK, n_h, width)
        return jnp.transpose(a, (0, 1, 4, 3, 2)).reshape(b, nqb, width, n_h * Q_BLK)

    qt = to_cols(q, HEAD_DIM)
    qit = to_cols(qi, IDX_DIM)
    w_t = to_cols(wi, 1)
    kv4 = kv.reshape(b, s // CNT_BLK, CNT_BLK, LANES)
    ki4 = ki.reshape(b, s // CNT_BLK, CNT_BLK, LANES)
    vt4 = jnp.transpose(kv[:, :, HEAD_DIM:].reshape(b, nkb, K_BLK, HEAD_DIM), (0, 1, 3, 2))
    ones_rows = jnp.concatenate([jnp.ones((b, nkb, 1, K_BLK), BF16),
                                 jnp.zeros((b, nkb, SUBLANES - 1, K_BLK), BF16)], axis=2)
    vt4 = jnp.concatenate([vt4, ones_rows], axis=2)

    attn_t = _dsa(qt, qit, w_t, kv4, ki4, vt4)
    attn = jnp.transpose(attn_t.reshape(b, nqb, HEAD_DIM, N_HEADS, Q_BLK), (0, 1, 4, 3, 2))
    attn = attn.reshape(b, s, d_attn)

    w_out_b = w_out.astype(BF16)
    w_r = jnp.concatenate([w_re, w_rg, jnp.zeros((d, LANES - N_EXPERTS - N_GROUPS), F32)], axis=1)
    b_r = jnp.concatenate([b_re, b_rg, jnp.zeros((LANES - N_EXPERTS - N_GROUPS,), F32)])[None, :]
    x1, h2, gates = _outproj(x, attn, pool, w_out_b[:d_attn], w_out_b[d_attn:], gate1,
                             g_ffn[None, :], scale2, shift2, w_r, b_r)

    w_gu = jnp.concatenate([w_gate, w_up], axis=-1).astype(BF16)
    return _moe(x1, h2, gates, gate2, w_gu, w_down.astype(BF16))


def kernel(x, c, positions, w_ada, b_ada, g_norm_mix, g_norm_ffn, w_in, g_q, g_k, g_kidx, w_pool,
           pool_scale, w_out, w_router_group, b_router_group, w_router_expert, b_router_expert,
           w_gate, w_up, w_down):
    b, s, d = x.shape
    depth = w_ada.shape[0]
    assert s % TM_MOE == 0 and s % K_BLK == 0 and d % LANES == 0
    pos3 = positions[:, :, None]
    c_pad = jnp.concatenate([c, jnp.zeros((-b % SUBLANES, d), c.dtype)], axis=0)
    for l in range(depth):
        mod = _adaln(c_pad, w_ada[l], b_ada[l][None, :])[:b]
        x = _layer(x, mod, pos3, g_norm_mix[l], g_norm_ffn[l], w_in[l], g_q[l], g_k[l], g_kidx[l],
                   w_pool[l], pool_scale[l], w_out[l], w_router_group[l], b_router_group[l],
                   w_router_expert[l], b_router_expert[l], w_gate[l], w_up[l], w_down[l])
    return x
```

```python
import functools

import jax
import jax.numpy as jnp
from jax import lax
from jax.experimental import pallas as pl
from jax.experimental.pallas import tpu as pltpu

N_HEADS = 8
HEAD_DIM = 64
N_IDX_HEADS = 8
IDX_DIM = 64
TOPK_MAX = 256
ROPE_THETA = 10000.0
POOL_WINDOWS = (2, 4, 8, 16)
N_GROUPS = 4
EXPERTS_PER_GROUP = 8
N_EXPERTS = N_GROUPS * EXPERTS_PER_GROUP
EPS = 1e-6

LANES = 128
SUBLANES = 8
VMEM_LIMIT_BYTES = 56 * 1024 * 1024

Q_BLK = 128
K_BLK = 256
COL_BLK = 256
CNT_BLK = 512
CNT_ROWS = 32
TM_PROJ = 512
TM_MOE = 1024
MAX_WIN = max(POOL_WINDOWS)
M_INIT = -1e29
MASKED = -1e30
F32_LOWEST = -3.0e38
LOG2_E = 1.4426950408889634

BF16 = jnp.bfloat16
F32 = jnp.float32


def _cparams(sem):
    return pltpu.CompilerParams(dimension_semantics=sem, vmem_limit_bytes=VMEM_LIMIT_BYTES)


def _adaln_kernel(c_ref, w_ref, b_ref, o_ref):
    c = c_ref[...]
    c_act = c * jax.nn.sigmoid(c)
    o_ref[...] = jnp.dot(c_act, w_ref[...], preferred_element_type=F32) + b_ref[...]


def _adaln(c_pad, w_ada, b_ada):
    rows, d = c_pad.shape
    n = w_ada.shape[1]
    tn = n // 6
    return pl.pallas_call(
        _adaln_kernel,
        out_shape=jax.ShapeDtypeStruct((rows, n), F32),
        grid=(n // tn,),
        in_specs=[pl.BlockSpec((rows, d), lambda j: (0, 0)),
                  pl.BlockSpec((d, tn), lambda j: (0, j)),
                  pl.BlockSpec((1, tn), lambda j: (0, j))],
        out_specs=pl.BlockSpec((rows, tn), lambda j: (0, j)),
        compiler_params=_cparams(("arbitrary",)),
        name="adaln",
    )(c_pad, w_ada, b_ada)


def _rope_chunk(y, cos, sin_signed, first_half):
    from_hi = pltpu.roll(y, LANES - HEAD_DIM // 2, 1)
    from_lo = pltpu.roll(y, HEAD_DIM // 2, 1)
    return y * cos + jnp.where(first_half, from_hi, from_lo) * sin_signed


def _inproj_kernel(pos_ref, x_ref, scale_ref, shift_ref, gmix_ref, win_ref, segsum_ref,
                   gq_ref, gk_ref, gkidx_ref, invf_ref, wpool_ref, pscale_ref,
                   q_ref, kv_ref, qi_ref, ki_ref, wi_ref, pool_ref, ubuf_ref):
    tm = x_ref.shape[1]
    d_attn = N_HEADS * HEAD_DIM
    d_qidx = N_IDX_HEADS * IDX_DIM
    s_tile = pl.program_id(1)

    x = x_ref[0]
    ms = jnp.mean(x * x, axis=-1, keepdims=True)
    h = (x * lax.rsqrt(ms + EPS) * gmix_ref[...]) * (1.0 + scale_ref[0]) + shift_ref[0]
    proj = jnp.dot(h.astype(BF16), win_ref[...], preferred_element_type=F32)

    lane = lax.broadcasted_iota(jnp.int32, (tm, LANES), 1)
    first_half = (lane & (HEAD_DIM - 1)) < (HEAD_DIM // 2)
    ang = pos_ref[0].astype(F32) * invf_ref[...]
    cos = jnp.cos(ang)
    sin = jnp.sin(ang)
    sin_signed = jnp.where(first_half, -sin, sin)
    rope = functools.partial(_rope_chunk, cos=cos, sin_signed=sin_signed, first_half=first_half)

    qf = proj[:, :d_attn]
    qsq = qf * qf
    qsq_hi = qsq.astype(BF16)
    qsq_lo = (qsq - qsq_hi.astype(F32)).astype(BF16)
    seg = segsum_ref[...]
    ssq = (jnp.dot(qsq_hi, seg, preferred_element_type=F32)
           + jnp.dot(qsq_lo, seg, preferred_element_type=F32))
    qn = qf * lax.rsqrt(ssq * (1.0 / HEAD_DIM) + EPS) * gq_ref[...]
    for j in range(d_attn // LANES):
        sl = slice(j * LANES, (j + 1) * LANES)
        q_ref[0, :, sl] = (rope(qn[:, sl]) * (LOG2_E * HEAD_DIM ** -0.5)).astype(BF16)

    kvc = proj[:, d_attn:d_attn + LANES]
    is_k = lane < HEAD_DIM
    ksq = jnp.sum(jnp.where(is_k, kvc * kvc, 0.0), axis=-1, keepdims=True)
    kn = kvc * lax.rsqrt(ksq * (1.0 / HEAD_DIM) + EPS) * gk_ref[...]
    kv_ref[0] = jnp.where(is_k, rope(kn), kvc).astype(BF16)

    o_qi = d_attn + LANES
    for j in range(d_qidx // LANES):
        qi_ref[0, :, j * LANES:(j + 1) * LANES] = rope(
            proj[:, o_qi + j * LANES:o_qi + (j + 1) * LANES]).astype(BF16)

    o_ki = o_qi + d_qidx
    kic = proj[:, o_ki:o_ki + LANES]
    kisq = jnp.sum(jnp.where(is_k, kic * kic, 0.0), axis=-1, keepdims=True)
    kin = kic * lax.rsqrt(kisq * (1.0 / IDX_DIM) + EPS) * gkidx_ref[...]
    ki_ref[0] = jnp.where(is_k, rope(kin), 0.0).astype(BF16)
    wi_ref[0] = kic[:, IDX_DIM:IDX_DIM + N_IDX_HEADS] * (N_IDX_HEADS ** -0.5 * IDX_DIM ** -0.5)

    o_u = o_ki + LANES
    u = proj[:, o_u:o_u + LANES * len(POOL_WINDOWS)]

    @pl.when(s_tile == 0)
    def _():
        ubuf_ref[0:MAX_WIN, :] = jnp.zeros((MAX_WIN, u.shape[1]), F32)

    @pl.when(s_tile != 0)
    def _():
        ubuf_ref[0:MAX_WIN, :] = ubuf_ref[tm:tm + MAX_WIN, :]

    ubuf_ref[MAX_WIN:MAX_WIN + tm, :] = u
    t_idx = s_tile * tm + lax.broadcasted_iota(jnp.int32, (tm, 1), 0)
    for g, win in enumerate(POOL_WINDOWS):
        sl = slice(g * LANES, (g + 1) * LANES)
        wsum = u[:, sl]
        for j in range(1, win):
            wsum = wsum + ubuf_ref[MAX_WIN - j:MAX_WIN - j + tm, sl]
        cnt = jnp.minimum(t_idx + 1, win).astype(F32)
        pooled = wsum / cnt - u[:, sl]
        mixed = jnp.dot(pooled.astype(BF16), wpool_ref[g], preferred_element_type=F32)
        pool_ref[0, :, sl] = (mixed * pscale_ref[:, sl]).astype(BF16)


def _inproj(pos3, x, scale1, shift1, g_mix, w_in_p, segsum, gq_t, gk_e, gkidx_e, invf, w_pool, pscale):
    b, s, d = x.shape
    tm = TM_PROJ
    d_attn = N_HEADS * HEAD_DIM
    d_qidx = N_IDX_HEADS * IDX_DIM
    d_pool = LANES * len(POOL_WINDOWS)
    tok = lambda w: pl.BlockSpec((1, tm, w), lambda bi, si: (bi, si, 0))
    per_b = pl.BlockSpec((1, 1, d), lambda bi, si: (bi, 0, 0))
    full = lambda a: pl.BlockSpec(a.shape, lambda bi, si: (0,) * a.ndim)
    return pl.pallas_call(
        _inproj_kernel,
        out_shape=(jax.ShapeDtypeStruct((b, s, d_attn), BF16),
                   jax.ShapeDtypeStruct((b, s, LANES), BF16),
                   jax.ShapeDtypeStruct((b, s, d_qidx), BF16),
                   jax.ShapeDtypeStruct((b, s, LANES), BF16),
                   jax.ShapeDtypeStruct((b, s, N_IDX_HEADS), F32),
                   jax.ShapeDtypeStruct((b, s, d_pool), BF16)),
        grid=(b, s // tm),
        in_specs=[tok(1), tok(d), per_b, per_b, full(g_mix), full(w_in_p), full(segsum),
                  full(gq_t), full(gk_e), full(gkidx_e), full(invf), full(w_pool), full(pscale)],
        out_specs=(tok(d_attn), tok(LANES), tok(d_qidx), tok(LANES), tok(N_IDX_HEADS), tok(d_pool)),
        scratch_shapes=[pltpu.VMEM((tm + 2 * MAX_WIN, d_pool), F32)],
        compiler_params=_cparams(("arbitrary", "arbitrary")),
        name="inproj",
    )(pos3, x, scale1, shift1, g_mix, w_in_p, segsum, gq_t, gk_e, gkidx_e, invf, w_pool, pscale)


def _dsa_kernel(qt_ref, qit_ref, w_ref, kv_ref, ki_ref, vt_ref, o_ref,
                sc_ref, qe_ref, qie_ref, m_ref, mx_ref, st_ref, acc_ref, lg_ref, p_ref):
    topk = float(min(TOPK_MAX, (sc_ref.shape[0] * CNT_BLK) // 4))
    qb = pl.program_id(1)
    n_cols = qt_ref.shape[3]
    n_chunks = n_cols // COL_BLK
    sub = CNT_BLK // K_BLK
    nch = ((qb + 1) * Q_BLK + CNT_BLK - 1) // CNT_BLK
    kgrp = K_BLK // SUBLANES
    sub_rows = [slice(j * K_BLK, (j + 1) * K_BLK) for j in range(sub)]

    zeros_half = jnp.zeros((LANES - HEAD_DIM, n_cols), BF16)
    qe_ref[0:HEAD_DIM, :] = qt_ref[0, 0]
    qe_ref[HEAD_DIM:LANES, :] = zeros_half
    qie_ref[0:IDX_DIM, :] = qit_ref[0, 0]
    qie_ref[IDX_DIM:LANES, :] = zeros_half

    q_pos = qb * Q_BLK + lax.broadcasted_iota(jnp.int32, (K_BLK, Q_BLK), 1)
    key_off = lax.broadcasted_iota(jnp.int32, (K_BLK, Q_BLK), 0)

    def score_body(ch, carry):
        rmax, rmin = carry
        for j in range(sub):
            ki_blk = ki_ref[0, ch, sub_rows[j], :]
            score = None
            for cc in range(n_chunks):
                cs = slice(cc * COL_BLK, (cc + 1) * COL_BLK)
                s_h = jnp.dot(ki_blk, qie_ref[:, cs], preferred_element_type=F32)
                s_h = jnp.maximum(s_h, 0.0) * w_ref[0, 0, :, cs]
                part = s_h[:, :Q_BLK] + s_h[:, Q_BLK:]
                score = part if score is None else score + part
            causal = (ch * CNT_BLK + j * K_BLK + key_off) <= q_pos
            masked = jnp.where(causal, score, -jnp.inf)
            sc_ref[ch, sub_rows[j], :] = masked
            hi_part = masked.reshape(kgrp, SUBLANES, Q_BLK).max(axis=0)
            lo_part = jnp.where(causal, score, jnp.inf).reshape(kgrp, SUBLANES, Q_BLK).min(axis=0)
            rmax, rmin = jnp.maximum(rmax, hi_part), jnp.minimum(rmin, lo_part)
        return rmax, rmin

    rmax8, rmin8 = lax.fori_loop(
        0, nch, score_body,
        (jnp.full((SUBLANES, Q_BLK), -jnp.inf, F32), jnp.full((SUBLANES, Q_BLK), jnp.inf, F32)))
    rowmax = jnp.max(rmax8, axis=0, keepdims=True)
    rowmin = jnp.min(rmin8, axis=0, keepdims=True)

    n_causal = (qb * Q_BLK + 1 + lax.broadcasted_iota(jnp.int32, (1, Q_BLK), 1)).astype(F32)
    kt = jnp.minimum(n_causal, topk)

    cgrp = CNT_BLK // CNT_ROWS

    def count_ge(t):
        def body(ch, acc):
            hit = jnp.where(sc_ref[ch] >= t, 1.0, 0.0)
            return acc + hit.reshape(cgrp, CNT_ROWS, Q_BLK).sum(axis=0)
        acc = lax.fori_loop(0, nch, body, jnp.zeros((CNT_ROWS, Q_BLK), F32))
        return jnp.sum(acc, axis=0, keepdims=True)

    def bisect_pass(state):
        lo, hi, c_lo, c_hi, thr, done = state
        mid = jnp.where(hi == jnp.inf, rowmax, lo + 0.5 * (hi - lo))
        c = count_ge(mid)
        hit = jnp.logical_and(done == 0.0, c == kt)
        thr = jnp.where(hit, mid, thr)
        done = jnp.where(hit, 1.0, done)
        go_up = c >= kt
        active = done == 0.0
        lo_n = jnp.where(jnp.logical_and(active, go_up), mid, lo)
        c_lo_n = jnp.where(jnp.logical_and(active, go_up), c, c_lo)
        hi_n = jnp.where(jnp.logical_and(active, jnp.logical_not(go_up)), mid, hi)
        c_hi_n = jnp.where(jnp.logical_and(active, jnp.logical_not(go_up)), c, c_hi)
        return lo_n, hi_n, c_lo_n, c_hi_n, thr, done

    def snap_pass(state):
        lo, hi, c_lo, c_hi, thr, done = state

        def body(ch, carry):
            a8, b8 = carry
            s = sc_ref[ch]
            a = jnp.where(s >= lo, s, jnp.inf).reshape(cgrp, CNT_ROWS, Q_BLK).min(axis=0)
            b = jnp.where(s < hi, s, -jnp.inf).reshape(cgrp, CNT_ROWS, Q_BLK).max(axis=0)
            return jnp.minimum(a8, a), jnp.maximum(b8, b)

        a8, b8 = lax.fori_loop(
            0, nch, body,
            (jnp.full((CNT_ROWS, Q_BLK), jnp.inf, F32), jnp.full((CNT_ROWS, Q_BLK), -jnp.inf, F32)))
        a = jnp.min(a8, axis=0, keepdims=True)
        b = jnp.max(b8, axis=0, keepdims=True)
        hit = jnp.logical_and(done == 0.0, a == b)
        thr = jnp.where(hit, a, thr)
        done = jnp.where(hit, 2.0, done)
        return lo, hi, c_lo, c_hi, thr, done

    few = n_causal <= topk
    state0 = (rowmin, jnp.full((1, Q_BLK), jnp.inf, F32), n_causal, jnp.zeros((1, Q_BLK), F32),
              jnp.where(few, F32_LOWEST, 0.0), jnp.where(few, 1.0, 0.0))

    def outer_cond(carry):
        return carry[1] > 0.0

    def outer_body(carry):
        state, _ = carry
        state = lax.fori_loop(0, 3, lambda i, st: bisect_pass(st), state)
        state = snap_pass(state)
        pending = jnp.max(jnp.where(state[5] == 0.0, 1.0, 0.0))
        return state, pending

    state1 = lax.fori_loop(0, 14, lambda i, st: bisect_pass(st), state0)
    state1 = snap_pass(state1)
    pending1 = jnp.max(jnp.where(state1[5] == 0.0, 1.0, 0.0))
    (lo, hi, c_lo, c_hi, thr, done), _ = lax.while_loop(outer_cond, outer_body, (state1, pending1))

    excess = jnp.where(done == 2.0, c_lo - kt, 0.0)
    need = kt - c_hi

    @pl.when(jnp.max(excess) > 0.0)
    def _():
        tri = (lax.broadcasted_iota(jnp.int32, (K_BLK, K_BLK), 0)
               >= lax.broadcasted_iota(jnp.int32, (K_BLK, K_BLK), 1)).astype(BF16)
        has_excess = excess > 0.0

        def drop_body(ch, run):
            for j in range(sub):
                s = sc_ref[ch, sub_rows[j], :]
                tied = jnp.logical_and(s == thr, has_excess)
                prefix = jnp.dot(tri, jnp.where(tied, 1.0, 0.0).astype(BF16), preferred_element_type=F32)
                drop = jnp.logical_and(tied, run + prefix > need)
                sc_ref[ch, sub_rows[j], :] = jnp.where(drop, -jnp.inf, s)
                run = run + jnp.max(prefix, axis=0, keepdims=True)
            return run

        lax.fori_loop(0, nch, drop_body, jnp.zeros((1, Q_BLK), F32))

    m_ref[...] = jnp.full(m_ref.shape, M_INIT, F32)
    acc_ref[...] = jnp.zeros(acc_ref.shape, F32)
    row_m = lambda j: slice(j, j + 1)
    row_a = lambda j: slice(sub + j, sub + j + 1)

    def logits_stage(ch, j):
        kv_blk = kv_ref[0, ch, sub_rows[j], :]
        bias = jnp.where(sc_ref[ch, sub_rows[j], :] >= thr, 0.0, MASKED)
        for cc in range(n_chunks):
            logits = jnp.dot(kv_blk, qe_ref[:, cc * COL_BLK:(cc + 1) * COL_BLK],
                             preferred_element_type=F32)
            for hh in range(COL_BLK // Q_BLK):
                cs = slice(cc * COL_BLK + hh * Q_BLK, cc * COL_BLK + (hh + 1) * Q_BLK)
                lg = logits[:, hh * Q_BLK:(hh + 1) * Q_BLK] + bias
                lg_ref[j, :, cs] = lg
                mx_ref[:, cs] = lg.reshape(kgrp, SUBLANES, Q_BLK).max(axis=0)
        m_old = m_ref[...]
        m_new = jnp.maximum(m_old, jnp.max(mx_ref[...], axis=0, keepdims=True))
        st_ref[row_m(j), :] = m_new
        st_ref[row_a(j), :] = jnp.exp2(m_old - m_new)
        m_ref[...] = m_new

    def probs_stage(j):
        p_ref[j] = jnp.exp2(lg_ref[j] - st_ref[row_m(j), :]).astype(BF16)

    def value_stage(kb, j):
        acc_ref[...] = acc_ref[...] * st_ref[row_a(j), :] + jnp.dot(
            vt_ref[0, kb], p_ref[j], preferred_element_type=F32)

    p_ref[sub - 1] = jnp.zeros(p_ref.shape[1:], BF16)
    st_ref[row_a(sub - 1), :] = jnp.ones((1, n_cols), F32)
    logits_stage(0, 0)

    def attn_body(ch, _):
        value_stage(jnp.maximum(ch * sub - 1, 0), 1)
        probs_stage(0)
        logits_stage(ch, 1)
        value_stage(ch * sub, 0)
        probs_stage(1)
        logits_stage(jnp.minimum(ch + 1, nch - 1), 0)
        return 0

    lax.fori_loop(0, nch, attn_body, 0)
    value_stage(nch * sub - 1, 1)
    dh = o_ref.shape[2]
    o_ref[0, 0] = (acc_ref[0:dh, :] / acc_ref[dh:dh + 1, :]).astype(o_ref.dtype)


def _dsa(qt, qit, w_t, kv4, ki4, vt4):
    b, nqb, dh, n_cols = qt.shape
    n_steps = kv4.shape[1]
    assert kv4.shape[2] == CNT_BLK and CNT_BLK == 2 * K_BLK and n_cols % COL_BLK == 0
    assert vt4.shape[1] * K_BLK == n_steps * CNT_BLK and vt4.shape[2] == dh + SUBLANES
    per_q = lambda a: pl.BlockSpec((1, 1) + a.shape[2:], lambda bi, qi: (bi, qi, 0, 0))
    per_b = lambda a: pl.BlockSpec((1,) + a.shape[1:], lambda bi, qi: (bi, 0, 0, 0))
    return pl.pallas_call(
        _dsa_kernel,
        out_shape=jax.ShapeDtypeStruct((b, nqb, dh, n_cols), BF16),
        grid=(b, nqb),
        in_specs=[per_q(qt), per_q(qit), per_q(w_t), per_b(kv4), per_b(ki4), per_b(vt4)],
        out_specs=pl.BlockSpec((1, 1, dh, n_cols), lambda bi, qi: (bi, qi, 0, 0)),
        scratch_shapes=[pltpu.VMEM((n_steps, CNT_BLK, Q_BLK), F32),
                        pltpu.VMEM((LANES, n_cols), BF16),
                        pltpu.VMEM((LANES, n_cols), BF16),
                        pltpu.VMEM((1, n_cols), F32),
                        pltpu.VMEM((SUBLANES, n_cols), F32),
                        pltpu.VMEM((SUBLANES, n_cols), F32),
                        pltpu.VMEM((dh + SUBLANES, n_cols), F32),
                        pltpu.VMEM((CNT_BLK // K_BLK, K_BLK, n_cols), F32),
                        pltpu.VMEM((CNT_BLK // K_BLK, K_BLK, n_cols), BF16)],
        compiler_params=_cparams(("arbitrary", "arbitrary")),
        name="dsa",
    )(qt, qit, w_t, kv4, ki4, vt4)


def _outproj_kernel(x_ref, attn_ref, pool_ref, woa_ref, wop_ref, gate1_ref, gffn_ref,
                    scale2_ref, shift2_ref, wr_ref, br_ref, x1_ref, h2_ref, gates_ref):
    tm = x_ref.shape[1]
    mix = (jnp.dot(attn_ref[0], woa_ref[...], preferred_element_type=F32)
           + jnp.dot(pool_ref[0], wop_ref[...], preferred_element_type=F32))
    x1 = x_ref[0] + gate1_ref[0] * mix
    x1_ref[0] = x1
    ms = jnp.mean(x1 * x1, axis=-1, keepdims=True)
    h2 = (x1 * lax.rsqrt(ms + EPS) * gffn_ref[...]) * (1.0 + scale2_ref[0]) + shift2_ref[0]
    h2_hi = h2.astype(BF16)
    h2_ref[0] = h2_hi

    h2_lo = (h2 - h2_hi.astype(F32)).astype(BF16)
    wr = wr_ref[...]
    wr_hi = wr.astype(BF16)
    wr_lo = (wr - wr_hi.astype(F32)).astype(BF16)
    logits = (jnp.dot(h2_hi, wr_hi, preferred_element_type=F32)
              + jnp.dot(h2_lo, wr_hi, preferred_element_type=F32)
              + jnp.dot(h2_hi, wr_lo, preferred_element_type=F32)) + br_ref[...]

    lane = lax.broadcasted_iota(jnp.int32, (tm, LANES), 1)
    big = jnp.int32(LANES)
    is_g = jnp.logical_and(lane >= N_EXPERTS, lane < N_EXPERTS + N_GROUPS)
    glog = jnp.where(is_g, logits, -jnp.inf)
    gmax = jnp.max(glog, axis=-1, keepdims=True)
    gsum = jnp.sum(jnp.exp(glog - gmax), axis=-1, keepdims=True)
    p_g = 1.0 / gsum
    g_sel = jnp.min(jnp.where(glog == gmax, lane, big), axis=-1, keepdims=True) - N_EXPERTS
    in_grp = jnp.logical_and(lane < N_EXPERTS, jnp.right_shift(lane, 3) == g_sel)
    elog = jnp.where(in_grp, logits, -jnp.inf)
    emax = jnp.max(elog, axis=-1, keepdims=True)
    eexp = jnp.exp(elog - emax)
    esum = jnp.sum(eexp, axis=-1, keepdims=True)
    p_e = jnp.where(in_grp, eexp / esum, -1.0)
    p1 = jnp.max(p_e, axis=-1, keepdims=True)
    i1 = jnp.min(jnp.where(p_e == p1, lane, big), axis=-1, keepdims=True)
    p_e2 = jnp.where(lane == i1, -1.0, p_e)
    p2 = jnp.max(p_e2, axis=-1, keepdims=True)
    i2 = jnp.min(jnp.where(p_e2 == p2, lane, big), axis=-1, keepdims=True)
    tot = p1 + p2
    gates_ref[0] = (jnp.where(lane == i1, p_g * (p1 / tot), 0.0)
                    + jnp.where(lane == i2, p_g * (p2 / tot), 0.0))


def _outproj(x, attn, pool, wo_a, wo_p, gate1, g_ffn, scale2, shift2, w_r, b_r):
    b, s, d = x.shape
    tm = TM_PROJ
    tok = lambda w: pl.BlockSpec((1, tm, w), lambda bi, si: (bi, si, 0))
    per_b = pl.BlockSpec((1, 1, d), lambda bi, si: (bi, 0, 0))
    full = lambda a: pl.BlockSpec(a.shape, lambda bi, si: (0,) * a.ndim)
    return pl.pallas_call(
        _outproj_kernel,
        out_shape=(jax.ShapeDtypeStruct((b, s, d), F32),
                   jax.ShapeDtypeStruct((b, s, d), BF16),
                   jax.ShapeDtypeStruct((b, s, LANES), F32)),
        grid=(b, s // tm),
        in_specs=[tok(d), tok(attn.shape[2]), tok(pool.shape[2]), full(wo_a), full(wo_p), per_b,
                  full(g_ffn), per_b, per_b, full(w_r), full(b_r)],
        out_specs=(tok(d), tok(d), tok(LANES)),
        compiler_params=_cparams(("arbitrary", "arbitrary")),
        name="outproj",
    )(x, attn, pool, wo_a, wo_p, gate1, g_ffn, scale2, shift2, w_r, b_r)


def _moe_kernel(x1_ref, h2_ref, gates_ref, gate2_ref, wgu_ref, wd_ref, o_ref, acc_ref):
    e = pl.program_id(2)
    d_exp = wd_ref.shape[1]
    tm = h2_ref.shape[1]

    @pl.when(e == 0)
    def _():
        acc_ref[...] = jnp.zeros(acc_ref.shape, F32)

    gu = jnp.dot(h2_ref[0], wgu_ref[0], preferred_element_type=F32)
    g = gu[:, :d_exp]
    a = (g * jax.nn.sigmoid(g)) * gu[:, d_exp:]
    lane = lax.broadcasted_iota(jnp.int32, (tm, LANES), 1)
    gate_e = jnp.sum(jnp.where(lane == e, gates_ref[0], 0.0), axis=-1, keepdims=True)
    acc_ref[...] += jnp.dot((a * gate_e).astype(BF16), wd_ref[0], preferred_element_type=F32)

    @pl.when(e == pl.num_programs(2) - 1)
    def _():
        o_ref[0] = x1_ref[0] + gate2_ref[0] * acc_ref[...]


def _moe(x1, h2, gates, gate2, w_gu, w_d):
    b, s, d = x1.shape
    tm = TM_MOE
    n_exp = w_gu.shape[0]
    tok = lambda w: pl.BlockSpec((1, tm, w), lambda bi, si, e: (bi, si, 0))
    return pl.pallas_call(
        _moe_kernel,
        out_shape=jax.ShapeDtypeStruct((b, s, d), F32),
        grid=(b, s // tm, n_exp),
        in_specs=[tok(d), tok(d), tok(LANES),
                  pl.BlockSpec((1, 1, d), lambda bi, si, e: (bi, 0, 0)),
                  pl.BlockSpec((1,) + w_gu.shape[1:], lambda bi, si, e: (e, 0, 0)),
                  pl.BlockSpec((1,) + w_d.shape[1:], lambda bi, si, e: (e, 0, 0))],
        out_specs=tok(d),
        scratch_shapes=[pltpu.VMEM((tm, d), F32)],
        compiler_params=_cparams(("arbitrary", "arbitrary", "arbitrary")),
        name="moe",
    )(x1, h2, gates, gate2, w_gu, w_d)


def _layer(x, mod, pos3, g_mix, g_ffn, w_in, g_q, g_k, g_kidx, w_pool, pool_scale, w_out,
           w_rg, b_rg, w_re, b_re, w_gate, w_up, w_down):
    b, s, d = x.shape
    d_attn = N_HEADS * HEAD_DIM
    nqb = s // Q_BLK
    nkb = s // K_BLK
    shift1, scale1, gate1, shift2, scale2, gate2 = [m[:, None, :] for m in jnp.split(mod, 6, axis=-1)]

    n_front = d_attn + 2 * HEAD_DIM + N_IDX_HEADS * IDX_DIM + IDX_DIM + N_IDX_HEADS
    pad = (-n_front) % LANES
    w_in_p = jnp.concatenate([w_in[:, :n_front], jnp.zeros((d, pad), w_in.dtype), w_in[:, n_front:]],
                             axis=1).astype(BF16)
    seg_id = jnp.arange(d_attn) // HEAD_DIM
    segsum = (seg_id[:, None] == seg_id[None, :]).astype(BF16)
    ones_half = jnp.ones((LANES - HEAD_DIM,), F32)
    gq_t = jnp.tile(g_q, N_HEADS)[None, :]
    gk_e = jnp.concatenate([g_k, ones_half])[None, :]
    gkidx_e = jnp.concatenate([g_kidx, ones_half])[None, :]
    half = HEAD_DIM // 2
    inv_freq = ROPE_THETA ** (-jnp.arange(0, HEAD_DIM, 2, dtype=F32) / HEAD_DIM)
    invf = jnp.tile(inv_freq, LANES // half)[None, :]

    q, kv, qi, ki, wi, pool = _inproj(pos3, x, scale1, shift1, g_mix[None, :], w_in_p, segsum, gq_t,
                                      gk_e, gkidx_e, invf, w_pool.astype(BF16), pool_scale[None, :])

    def to_cols(a, width):
        n_h = a.shape[2] // width
        a = a.reshape(b, nqb, Q_BLK, n_h, width)
        return jnp.transpose(a, (0, 1, 4, 3, 2)).reshape(b, nqb, width, n_h * Q_BLK)

    qt = to_cols(q, HEAD_DIM)
    qit = to_cols(qi, IDX_DIM)
    w_t = to_cols(wi, 1)
    kv4 = kv.reshape(b, s // CNT_BLK, CNT_BLK, LANES)
    ki4 = ki.reshape(b, s // CNT_BLK, CNT_BLK, LANES)
    vt4 = jnp.transpose(kv[:, :, HEAD_DIM:].reshape(b, nkb, K_BLK, HEAD_DIM), (0, 1, 3, 2))
    ones_rows = jnp.concatenate([jnp.ones((b, nkb, 1, K_BLK), BF16),
                                 jnp.zeros((b, nkb, SUBLANES - 1, K_BLK), BF16)], axis=2)
    vt4 = jnp.concatenate([vt4, ones_rows], axis=2)

    attn_t = _dsa(qt, qit, w_t, kv4, ki4, vt4)
    attn = jnp.transpose(attn_t.reshape(b, nqb, HEAD_DIM, N_HEADS, Q_BLK), (0, 1, 4, 3, 2))
    attn = attn.reshape(b, s, d_attn)

    w_out_b = w_out.astype(BF16)
    w_r = jnp.concatenate([w_re, w_rg, jnp.zeros((d, LANES - N_EXPERTS - N_GROUPS), F32)], axis=1)
    b_r = jnp.concatenate([b_re, b_rg, jnp.zeros((LANES - N_EXPERTS - N_GROUPS,), F32)])[None, :]
    x1, h2, gates = _outproj(x, attn, pool, w_out_b[:d_attn], w_out_b[d_attn:], gate1,
                             g_ffn[None, :], scale2, shift2, w_r, b_r)

    w_gu = jnp.concatenate([w_gate, w_up], axis=-1).astype(BF16)
    return _moe(x1, h2, gates, gate2, w_gu, w_down.astype(BF16))


def kernel(x, c, positions, w_ada, b_ada, g_norm_mix, g_norm_ffn, w_in, g_q, g_k, g_kidx, w_pool,
           pool_scale, w_out, w_router_group, b_router_group, w_router_expert, b_router_expert,
           w_gate, w_up, w_down):
    b, s, d = x.shape
    depth = w_ada.shape[0]
    assert s % TM_MOE == 0 and s % K_BLK == 0 and d % LANES == 0
    pos3 = positions[:, :, None]
    c_pad = jnp.concatenate([c, jnp.zeros((-b % SUBLANES, d), c.dtype)], axis=0)
    for l in range(depth):
        mod = _adaln(c_pad, w_ada[l], b_ada[l][None, :])[:b]
        x = _layer(x, mod, pos3, g_norm_mix[l], g_norm_ffn[l], w_in[l], g_q[l], g_k[l], g_kidx[l],
                   w_pool[l], pool_scale[l], w_out[l], w_router_group[l], b_router_group[l],
                   w_router_expert[l], b_router_expert[l], w_gate[l], w_up[l], w_down[l])
    return x
```

```python
import functools

import jax
import jax.numpy as jnp
from jax import lax
from jax.experimental import pallas as pl
from jax.experimental.pallas import tpu as pltpu

N_HEADS = 8
HEAD_DIM = 64
N_IDX_HEADS = 8
IDX_DIM = 64
TOPK_MAX = 256
ROPE_THETA = 10000.0
POOL_WINDOWS = (2, 4, 8, 16)
N_GROUPS = 4
EXPERTS_PER_GROUP = 8
N_EXPERTS = N_GROUPS * EXPERTS_PER_GROUP
EPS = 1e-6

LANES = 128
SUBLANES = 8
VMEM_LIMIT_BYTES = 56 * 1024 * 1024
VMEM_LIMIT_MOE_BYTES = 60 * 1024 * 1024

Q_BLK = 128
K_BLK = 256
COL_BLK = 256
CNT_BLK = 512
CNT_ROWS = 32
TM_PROJ = 512
TM_MOE = 1024
MOE_CH = 128
RANK_BLK = 256
MAX_WIN = max(POOL_WINDOWS)
M_INIT = -1e29
MASKED = -1e30
F32_LOWEST = -3.0e38
LOG2_E = 1.4426950408889634

BF16 = jnp.bfloat16
F32 = jnp.float32


def _cparams(sem):
    return pltpu.CompilerParams(dimension_semantics=sem, vmem_limit_bytes=VMEM_LIMIT_BYTES)


def _adaln_kernel(c_ref, w_ref, b_ref, o_ref):
    c = c_ref[...]
    c_act = c * jax.nn.sigmoid(c)
    o_ref[...] = jnp.dot(c_act, w_ref[...], preferred_element_type=F32) + b_ref[...]


def _adaln(c_pad, w_ada, b_ada):
    rows, d = c_pad.shape
    n = w_ada.shape[1]
    tn = n // 6
    return pl.pallas_call(
        _adaln_kernel,
        out_shape=jax.ShapeDtypeStruct((rows, n), F32),
        grid=(n // tn,),
        in_specs=[pl.BlockSpec((rows, d), lambda j: (0, 0)),
                  pl.BlockSpec((d, tn), lambda j: (0, j)),
                  pl.BlockSpec((1, tn), lambda j: (0, j))],
        out_specs=pl.BlockSpec((rows, tn), lambda j: (0, j)),
        compiler_params=_cparams(("arbitrary",)),
        name="adaln",
    )(c_pad, w_ada, b_ada)


def _rope_chunk(y, cos, sin_signed, first_half):
    from_hi = pltpu.roll(y, LANES - HEAD_DIM // 2, 1)
    from_lo = pltpu.roll(y, HEAD_DIM // 2, 1)
    return y * cos + jnp.where(first_half, from_hi, from_lo) * sin_signed


def _inproj_kernel(pos_ref, x_ref, scale_ref, shift_ref, gmix_ref, win_ref, segsum_ref,
                   gq_ref, gk_ref, gkidx_ref, invf_ref, wpool_ref, pscale_ref,
                   q_ref, kv_ref, qi_ref, ki_ref, wi_ref, pool_ref, ubuf_ref):
    tm = x_ref.shape[1]
    d_attn = N_HEADS * HEAD_DIM
    d_qidx = N_IDX_HEADS * IDX_DIM
    s_tile = pl.program_id(1)

    x = x_ref[0]
    ms = jnp.mean(x * x, axis=-1, keepdims=True)
    h = (x * lax.rsqrt(ms + EPS) * gmix_ref[...]) * (1.0 + scale_ref[0]) + shift_ref[0]
    proj = jnp.dot(h.astype(BF16), win_ref[...], preferred_element_type=F32)

    lane = lax.broadcasted_iota(jnp.int32, (tm, LANES), 1)
    first_half = (lane & (HEAD_DIM - 1)) < (HEAD_DIM // 2)
    ang = pos_ref[0].astype(F32) * invf_ref[...]
    cos = jnp.cos(ang)
    sin = jnp.sin(ang)
    sin_signed = jnp.where(first_half, -sin, sin)
    rope = functools.partial(_rope_chunk, cos=cos, sin_signed=sin_signed, first_half=first_half)

    qf = proj[:, :d_attn]
    qsq = qf * qf
    qsq_hi = qsq.astype(BF16)
    qsq_lo = (qsq - qsq_hi.astype(F32)).astype(BF16)
    seg = segsum_ref[...]
    ssq = (jnp.dot(qsq_hi, seg, preferred_element_type=F32)
           + jnp.dot(qsq_lo, seg, preferred_element_type=F32))
    qn = qf * lax.rsqrt(ssq * (1.0 / HEAD_DIM) + EPS) * gq_ref[...]
    for j in range(d_attn // LANES):
        sl = slice(j * LANES, (j + 1) * LANES)
        q_ref[0, :, sl] = (rope(qn[:, sl]) * (LOG2_E * HEAD_DIM ** -0.5)).astype(BF16)

    kvc = proj[:, d_attn:d_attn + LANES]
    is_k = lane < HEAD_DIM
    ksq = jnp.sum(jnp.where(is_k, kvc * kvc, 0.0), axis=-1, keepdims=True)
    kn = kvc * lax.rsqrt(ksq * (1.0 / HEAD_DIM) + EPS) * gk_ref[...]
    kv_ref[0] = jnp.where(is_k, rope(kn), kvc).astype(BF16)

    o_qi = d_attn + LANES
    for j in range(d_qidx // LANES):
        qi_ref[0, :, j * LANES:(j + 1) * LANES] = rope(
            proj[:, o_qi + j * LANES:o_qi + (j + 1) * LANES]).astype(BF16)

    o_ki = o_qi + d_qidx
    kic = proj[:, o_ki:o_ki + LANES]
    kisq = jnp.sum(jnp.where(is_k, kic * kic, 0.0), axis=-1, keepdims=True)
    kin = kic * lax.rsqrt(kisq * (1.0 / IDX_DIM) + EPS) * gkidx_ref[...]
    ki_ref[0] = jnp.where(is_k, rope(kin), 0.0).astype(BF16)
    wi_ref[0] = kic[:, IDX_DIM:IDX_DIM + N_IDX_HEADS] * (N_IDX_HEADS ** -0.5 * IDX_DIM ** -0.5)

    o_u = o_ki + LANES
    u = proj[:, o_u:o_u + LANES * len(POOL_WINDOWS)]

    @pl.when(s_tile == 0)
    def _():
        ubuf_ref[0:MAX_WIN, :] = jnp.zeros((MAX_WIN, u.shape[1]), F32)

    @pl.when(s_tile != 0)
    def _():
        ubuf_ref[0:MAX_WIN, :] = ubuf_ref[tm:tm + MAX_WIN, :]

    ubuf_ref[MAX_WIN:MAX_WIN + tm, :] = u
    t_idx = s_tile * tm + lax.broadcasted_iota(jnp.int32, (tm, 1), 0)
    for g, win in enumerate(POOL_WINDOWS):
        sl = slice(g * LANES, (g + 1) * LANES)
        wsum = u[:, sl]
        for j in range(1, win):
            wsum = wsum + ubuf_ref[MAX_WIN - j:MAX_WIN - j + tm, sl]
        cnt = jnp.minimum(t_idx + 1, win).astype(F32)
        pooled = wsum / cnt - u[:, sl]
        mixed = jnp.dot(pooled.astype(BF16), wpool_ref[g], preferred_element_type=F32)
        pool_ref[0, :, sl] = (mixed * pscale_ref[:, sl]).astype(BF16)


def _inproj(pos3, x, scale1, shift1, g_mix, w_in_p, segsum, gq_t, gk_e, gkidx_e, invf, w_pool, pscale):
    b, s, d = x.shape
    tm = TM_PROJ
    d_attn = N_HEADS * HEAD_DIM
    d_qidx = N_IDX_HEADS * IDX_DIM
    d_pool = LANES * len(POOL_WINDOWS)
    tok = lambda w: pl.BlockSpec((1, tm, w), lambda bi, si: (bi, si, 0))
    per_b = pl.BlockSpec((1, 1, d), lambda bi, si: (bi, 0, 0))
    full = lambda a: pl.BlockSpec(a.shape, lambda bi, si: (0,) * a.ndim)
    return pl.pallas_call(
        _inproj_kernel,
        out_shape=(jax.ShapeDtypeStruct((b, s, d_attn), BF16),
                   jax.ShapeDtypeStruct((b, s, LANES), BF16),
                   jax.ShapeDtypeStruct((b, s, d_qidx), BF16),
                   jax.ShapeDtypeStruct((b, s, LANES), BF16),
                   jax.ShapeDtypeStruct((b, s, N_IDX_HEADS), F32),
                   jax.ShapeDtypeStruct((b, s, d_pool), BF16)),
        grid=(b, s // tm),
        in_specs=[tok(1), tok(d), per_b, per_b, full(g_mix), full(w_in_p), full(segsum),
                  full(gq_t), full(gk_e), full(gkidx_e), full(invf), full(w_pool), full(pscale)],
        out_specs=(tok(d_attn), tok(LANES), tok(d_qidx), tok(LANES), tok(N_IDX_HEADS), tok(d_pool)),
        scratch_shapes=[pltpu.VMEM((tm + 2 * MAX_WIN, d_pool), F32)],
        compiler_params=_cparams(("arbitrary", "arbitrary")),
        name="inproj",
    )(pos3, x, scale1, shift1, g_mix, w_in_p, segsum, gq_t, gk_e, gkidx_e, invf, w_pool, pscale)


def _dsa_kernel(qt_ref, qit_ref, w_ref, kv_ref, ki_ref, vt_ref, o_ref,
                sc_ref, qe_ref, qie_ref, m_ref, mx_ref, st_ref, acc_ref, lg_ref, p_ref):
    topk = float(min(TOPK_MAX, (sc_ref.shape[0] * CNT_BLK) // 4))
    qb = pl.program_id(1)
    n_cols = qt_ref.shape[3]
    n_chunks = n_cols // COL_BLK
    sub = CNT_BLK // K_BLK
    nch = ((qb + 1) * Q_BLK + CNT_BLK - 1) // CNT_BLK
    kgrp = K_BLK // SUBLANES
    sub_rows = [slice(j * K_BLK, (j + 1) * K_BLK) for j in range(sub)]

    zeros_half = jnp.zeros((LANES - HEAD_DIM, n_cols), BF16)
    qe_ref[0:HEAD_DIM, :] = qt_ref[0, 0]
    qe_ref[HEAD_DIM:LANES, :] = zeros_half
    qie_ref[0:IDX_DIM, :] = qit_ref[0, 0]
    qie_ref[IDX_DIM:LANES, :] = zeros_half

    q_pos = qb * Q_BLK + lax.broadcasted_iota(jnp.int32, (K_BLK, Q_BLK), 1)
    key_off = lax.broadcasted_iota(jnp.int32, (K_BLK, Q_BLK), 0)

    def score_body(ch, carry):
        rmax, rmin = carry
        for j in range(sub):
            ki_blk = ki_ref[0, ch, sub_rows[j], :]
            score = None
            for cc in range(n_chunks):
                cs = slice(cc * COL_BLK, (cc + 1) * COL_BLK)
                s_h = jnp.dot(ki_blk, qie_ref[:, cs], preferred_element_type=F32)
                s_h = jnp.maximum(s_h, 0.0) * w_ref[0, 0, :, cs]
                part = s_h[:, :Q_BLK] + s_h[:, Q_BLK:]
                score = part if score is None else score + part
            causal = (ch * CNT_BLK + j * K_BLK + key_off) <= q_pos
            masked = jnp.where(causal, score, -jnp.inf)
            sc_ref[ch, sub_rows[j], :] = masked
            hi_part = masked.reshape(kgrp, SUBLANES, Q_BLK).max(axis=0)
            lo_part = jnp.where(causal, score, jnp.inf).reshape(kgrp, SUBLANES, Q_BLK).min(axis=0)
            rmax, rmin = jnp.maximum(rmax, hi_part), jnp.minimum(rmin, lo_part)
        return rmax, rmin

    rmax8, rmin8 = lax.fori_loop(
        0, nch, score_body,
        (jnp.full((SUBLANES, Q_BLK), -jnp.inf, F32), jnp.full((SUBLANES, Q_BLK), jnp.inf, F32)))
    rowmax = jnp.max(rmax8, axis=0, keepdims=True)
    rowmin = jnp.min(rmin8, axis=0, keepdims=True)

    n_causal = (qb * Q_BLK + 1 + lax.broadcasted_iota(jnp.int32, (1, Q_BLK), 1)).astype(F32)
    kt = jnp.minimum(n_causal, topk)

    cgrp = CNT_BLK // CNT_ROWS

    def count_ge(t):
        def body(ch, acc):
            hit = jnp.where(sc_ref[ch] >= t, 1.0, 0.0)
            return acc + hit.reshape(cgrp, CNT_ROWS, Q_BLK).sum(axis=0)
        acc = lax.fori_loop(0, nch, body, jnp.zeros((CNT_ROWS, Q_BLK), F32))
        return jnp.sum(acc, axis=0, keepdims=True)

    def bisect_pass(state):
        lo, hi, c_lo, c_hi, thr, done = state
        mid = jnp.where(hi == jnp.inf, rowmax, lo + 0.5 * (hi - lo))
        c = count_ge(mid)
        hit = jnp.logical_and(done == 0.0, c == kt)
        thr = jnp.where(hit, mid, thr)
        done = jnp.where(hit, 1.0, done)
        go_up = c >= kt
        active = done == 0.0
        lo_n = jnp.where(jnp.logical_and(active, go_up), mid, lo)
        c_lo_n = jnp.where(jnp.logical_and(active, go_up), c, c_lo)
        hi_n = jnp.where(jnp.logical_and(active, jnp.logical_not(go_up)), mid, hi)
        c_hi_n = jnp.where(jnp.logical_and(active, jnp.logical_not(go_up)), c, c_hi)
        return lo_n, hi_n, c_lo_n, c_hi_n, thr, done

    def snap_pass(state):
        lo, hi, c_lo, c_hi, thr, done = state

        def body(ch, carry):
            a8, b8 = carry
            s = sc_ref[ch]
            a = jnp.where(s >= lo, s, jnp.inf).reshape(cgrp, CNT_ROWS, Q_BLK).min(axis=0)
            b = jnp.where(s < hi, s, -jnp.inf).reshape(cgrp, CNT_ROWS, Q_BLK).max(axis=0)
            return jnp.minimum(a8, a), jnp.maximum(b8, b)

        a8, b8 = lax.fori_loop(
            0, nch, body,
            (jnp.full((CNT_ROWS, Q_BLK), jnp.inf, F32), jnp.full((CNT_ROWS, Q_BLK), -jnp.inf, F32)))
        a = jnp.min(a8, axis=0, keepdims=True)
        b = jnp.max(b8, axis=0, keepdims=True)
        hit = jnp.logical_and(done == 0.0, a == b)
        thr = jnp.where(hit, a, thr)
        done = jnp.where(hit, 2.0, done)
        return lo, hi, c_lo, c_hi, thr, done

    few = n_causal <= topk
    state0 = (rowmin, jnp.full((1, Q_BLK), jnp.inf, F32), n_causal, jnp.zeros((1, Q_BLK), F32),
              jnp.where(few, F32_LOWEST, 0.0), jnp.where(few, 1.0, 0.0))

    def outer_cond(carry):
        return carry[1] > 0.0

    def outer_body(carry):
        state, _ = carry
        state = lax.fori_loop(0, 3, lambda i, st: bisect_pass(st), state)
        state = snap_pass(state)
        pending = jnp.max(jnp.where(state[5] == 0.0, 1.0, 0.0))
        return state, pending

    state1 = lax.fori_loop(0, 14, lambda i, st: bisect_pass(st), state0)
    state1 = snap_pass(state1)
    pending1 = jnp.max(jnp.where(state1[5] == 0.0, 1.0, 0.0))
    (lo, hi, c_lo, c_hi, thr, done), _ = lax.while_loop(outer_cond, outer_body, (state1, pending1))

    excess = jnp.where(done == 2.0, c_lo - kt, 0.0)
    need = kt - c_hi

    @pl.when(jnp.max(excess) > 0.0)
    def _():
        tri = (lax.broadcasted_iota(jnp.int32, (K_BLK, K_BLK), 0)
               >= lax.broadcasted_iota(jnp.int32, (K_BLK, K_BLK), 1)).astype(BF16)
        has_excess = excess > 0.0

        def drop_body(ch, run):
            for j in range(sub):
                s = sc_ref[ch, sub_rows[j], :]
                tied = jnp.logical_and(s == thr, has_excess)
                prefix = jnp.dot(tri, jnp.where(tied, 1.0, 0.0).astype(BF16), preferred_element_type=F32)
                drop = jnp.logical_and(tied, run + prefix > need)
                sc_ref[ch, sub_rows[j], :] = jnp.where(drop, -jnp.inf, s)
                run = run + jnp.max(prefix, axis=0, keepdims=True)
            return run

        lax.fori_loop(0, nch, drop_body, jnp.zeros((1, Q_BLK), F32))

    m_ref[...] = jnp.full(m_ref.shape, M_INIT, F32)
    acc_ref[...] = jnp.zeros(acc_ref.shape, F32)
    row_m = lambda j: slice(j, j + 1)
    row_a = lambda j: slice(sub + j, sub + j + 1)

    def logits_stage(ch, j):
        kv_blk = kv_ref[0, ch, sub_rows[j], :]
        bias = jnp.where(sc_ref[ch, sub_rows[j], :] >= thr, 0.0, MASKED)
        for cc in range(n_chunks):
            logits = jnp.dot(kv_blk, qe_ref[:, cc * COL_BLK:(cc + 1) * COL_BLK],
                             preferred_element_type=F32)
            for hh in range(COL_BLK // Q_BLK):
                cs = slice(cc * COL_BLK + hh * Q_BLK, cc * COL_BLK + (hh + 1) * Q_BLK)
                lg = logits[:, hh * Q_BLK:(hh + 1) * Q_BLK] + bias
                lg_ref[j, :, cs] = lg
                mx_ref[:, cs] = lg.reshape(kgrp, SUBLANES, Q_BLK).max(axis=0)
        m_old = m_ref[...]
        m_new = jnp.maximum(m_old, jnp.max(mx_ref[...], axis=0, keepdims=True))
        st_ref[row_m(j), :] = m_new
        st_ref[row_a(j), :] = jnp.exp2(m_old - m_new)
        m_ref[...] = m_new

    def probs_stage(j):
        p_ref[j] = jnp.exp2(lg_ref[j] - st_ref[row_m(j), :]).astype(BF16)

    def value_stage(kb, j):
        acc_ref[...] = acc_ref[...] * st_ref[row_a(j), :] + jnp.dot(
            vt_ref[0, kb], p_ref[j], preferred_element_type=F32)

    p_ref[sub - 1] = jnp.zeros(p_ref.shape[1:], BF16)
    st_ref[row_a(sub - 1), :] = jnp.ones((1, n_cols), F32)
    logits_stage(0, 0)

    def attn_body(ch, _):
        value_stage(jnp.maximum(ch * sub - 1, 0), 1)
        probs_stage(0)
        logits_stage(ch, 1)
        value_stage(ch * sub, 0)
        probs_stage(1)
        logits_stage(jnp.minimum(ch + 1, nch - 1), 0)
        return 0

    lax.fori_loop(0, nch, attn_body, 0)
    value_stage(nch * sub - 1, 1)
    dh = o_ref.shape[2]
    o_ref[0, 0] = (acc_ref[0:dh, :] / acc_ref[dh:dh + 1, :]).astype(o_ref.dtype)


def _dsa(qt, qit, w_t, kv4, ki4, vt4):
    b, nqb, dh, n_cols = qt.shape
    n_steps = kv4.shape[1]
    assert kv4.shape[2] == CNT_BLK and CNT_BLK == 2 * K_BLK and n_cols % COL_BLK == 0
    assert vt4.shape[1] * K_BLK == n_steps * CNT_BLK and vt4.shape[2] == dh + SUBLANES
    per_q = lambda a: pl.BlockSpec((1, 1) + a.shape[2:], lambda bi, qi: (bi, qi, 0, 0))
    per_b = lambda a: pl.BlockSpec((1,) + a.shape[1:], lambda bi, qi: (bi, 0, 0, 0))
    return pl.pallas_call(
        _dsa_kernel,
        out_shape=jax.ShapeDtypeStruct((b, nqb, dh, n_cols), BF16),
        grid=(b, nqb),
        in_specs=[per_q(qt), per_q(qit), per_q(w_t), per_b(kv4), per_b(ki4), per_b(vt4)],
        out_specs=pl.BlockSpec((1, 1, dh, n_cols), lambda bi, qi: (bi, qi, 0, 0)),
        scratch_shapes=[pltpu.VMEM((n_steps, CNT_BLK, Q_BLK), F32),
                        pltpu.VMEM((LANES, n_cols), BF16),
                        pltpu.VMEM((LANES, n_cols), BF16),
                        pltpu.VMEM((1, n_cols), F32),
                        pltpu.VMEM((SUBLANES, n_cols), F32),
                        pltpu.VMEM((SUBLANES, n_cols), F32),
                        pltpu.VMEM((dh + SUBLANES, n_cols), F32),
                        pltpu.VMEM((CNT_BLK // K_BLK, K_BLK, n_cols), F32),
                        pltpu.VMEM((CNT_BLK // K_BLK, K_BLK, n_cols), BF16)],
        compiler_params=_cparams(("arbitrary", "arbitrary")),
        name="dsa",
    )(qt, qit, w_t, kv4, ki4, vt4)


def _outproj_kernel(x_ref, attn_ref, pool_ref, woa_ref, wop_ref, gate1_ref, gffn_ref,
                    scale2_ref, shift2_ref, wr_ref, br_ref, x1_ref, h2_ref, gates_ref):
    tm = x_ref.shape[1]
    mix = (jnp.dot(attn_ref[0], woa_ref[...], preferred_element_type=F32)
           + jnp.dot(pool_ref[0], wop_ref[...], preferred_element_type=F32))
    x1 = x_ref[0] + gate1_ref[0] * mix
    x1_ref[0] = x1
    ms = jnp.mean(x1 * x1, axis=-1, keepdims=True)
    h2 = (x1 * lax.rsqrt(ms + EPS) * gffn_ref[...]) * (1.0 + scale2_ref[0]) + shift2_ref[0]
    h2_hi = h2.astype(BF16)
    h2_ref[0] = h2_hi

    h2_lo = (h2 - h2_hi.astype(F32)).astype(BF16)
    wr = wr_ref[...]
    wr_hi = wr.astype(BF16)
    wr_lo = (wr - wr_hi.astype(F32)).astype(BF16)
    logits = (jnp.dot(h2_hi, wr_hi, preferred_element_type=F32)
              + jnp.dot(h2_lo, wr_hi, preferred_element_type=F32)
              + jnp.dot(h2_hi, wr_lo, preferred_element_type=F32)) + br_ref[...]

    lane = lax.broadcasted_iota(jnp.int32, (tm, LANES), 1)
    big = jnp.int32(LANES)
    is_g = jnp.logical_and(lane >= N_EXPERTS, lane < N_EXPERTS + N_GROUPS)
    glog = jnp.where(is_g, logits, -jnp.inf)
    gmax = jnp.max(glog, axis=-1, keepdims=True)
    gsum = jnp.sum(jnp.exp(glog - gmax), axis=-1, keepdims=True)
    p_g = 1.0 / gsum
    g_sel = jnp.min(jnp.where(glog == gmax, lane, big), axis=-1, keepdims=True) - N_EXPERTS
    in_grp = jnp.logical_and(lane < N_EXPERTS, jnp.right_shift(lane, 3) == g_sel)
    elog = jnp.where(in_grp, logits, -jnp.inf)
    emax = jnp.max(elog, axis=-1, keepdims=True)
    eexp = jnp.exp(elog - emax)
    esum = jnp.sum(eexp, axis=-1, keepdims=True)
    p_e = jnp.where(in_grp, eexp / esum, -1.0)
    p1 = jnp.max(p_e, axis=-1, keepdims=True)
    i1 = jnp.min(jnp.where(p_e == p1, lane, big), axis=-1, keepdims=True)
    p_e2 = jnp.where(lane == i1, -1.0, p_e)
    p2 = jnp.max(p_e2, axis=-1, keepdims=True)
    i2 = jnp.min(jnp.where(p_e2 == p2, lane, big), axis=-1, keepdims=True)
    tot = p1 + p2
    gates_ref[0] = (jnp.where(lane == i1, p_g * (p1 / tot), 0.0)
                    + jnp.where(lane == i2, p_g * (p2 / tot), 0.0)
                    + jnp.where(lane == N_EXPERTS, g_sel.astype(F32), 0.0))


def _outproj(x, attn, pool, wo_a, wo_p, gate1, g_ffn, scale2, shift2, w_r, b_r):
    b, s, d = x.shape
    tm = TM_PROJ
    tok = lambda w: pl.BlockSpec((1, tm, w), lambda bi, si: (bi, si, 0))
    per_b = pl.BlockSpec((1, 1, d), lambda bi, si: (bi, 0, 0))
    full = lambda a: pl.BlockSpec(a.shape, lambda bi, si: (0,) * a.ndim)
    return pl.pallas_call(
        _outproj_kernel,
        out_shape=(jax.ShapeDtypeStruct((b, s, d), F32),
                   jax.ShapeDtypeStruct((b, s, d), BF16),
                   jax.ShapeDtypeStruct((b, s, LANES), F32)),
        grid=(b, s // tm),
        in_specs=[tok(d), tok(attn.shape[2]), tok(pool.shape[2]), full(wo_a), full(wo_p), per_b,
                  full(g_ffn), per_b, per_b, full(w_r), full(b_r)],
        out_specs=(tok(d), tok(d), tok(LANES)),
        compiler_params=_cparams(("arbitrary", "arbitrary")),
        name="outproj",
    )(x, attn, pool, wo_a, wo_p, gate1, g_ffn, scale2, shift2, w_r, b_r)


def _moe_kernel(x1_ref, h2_ref, gates_ref, gate2_ref, wgu_ref, wd_ref, o_ref,
                xe_ref, rank_ref, rank_t_ref):
    g = pl.program_id(2)
    tm, d = h2_ref.shape[1], h2_ref.shape[2]
    n_e, d_exp = wd_ref.shape[0], wd_ref.shape[1]
    lane = lax.broadcasted_iota(jnp.int32, (tm, LANES), 1)
    gf = g.astype(F32)

    @pl.when(g == 0)
    def _():
        tri = (lax.broadcasted_iota(jnp.int32, (RANK_BLK, RANK_BLK), 0)
               >= lax.broadcasted_iota(jnp.int32, (RANK_BLK, RANK_BLK), 1)).astype(BF16)
        lane_b = lax.broadcasted_iota(jnp.int32, (RANK_BLK, LANES), 1)
        run = jnp.zeros((1, LANES), F32)
        for sb in range(tm // RANK_BLK):
            rows = slice(sb * RANK_BLK, (sb + 1) * RANK_BLK)
            gts_b = gates_ref[0, rows, :]
            grp = jnp.sum(jnp.where(lane_b == N_EXPERTS, gts_b, 0.0), axis=-1, keepdims=True)
            member = jnp.where(jnp.logical_and(lane_b < N_GROUPS, lane_b.astype(F32) == grp), 1.0, 0.0)
            pre = jnp.dot(tri, member.astype(BF16), preferred_element_type=F32) + run
            rank_ref[rows, :] = jnp.where(lane_b == N_GROUPS, grp, pre)
            run = jnp.max(pre, axis=0, keepdims=True)
        rank_t_ref[...] = rank_ref[...].T
        gts = gates_ref[0]
        g_hi = gts.astype(BF16)
        xe_ref[:, :d] = h2_ref[0]
        xe_ref[:, d:d + LANES] = g_hi
        xe_ref[:, d + LANES:d + 2 * LANES] = (gts - g_hi.astype(F32)).astype(BF16)
        o_ref[0] = jnp.zeros((tm, d), F32)

    rank_row = rank_t_ref[pl.ds(g, 1), :]
    pos_row = jnp.where(rank_t_ref[N_GROUPS:N_GROUPS + 1, :] == gf, rank_row - 1.0, -1.0)
    rk = rank_ref[...]
    rank_col = jnp.sum(jnp.where(lane == g, rk, 0.0), axis=-1, keepdims=True)
    grp_col = jnp.sum(jnp.where(lane == N_GROUPS, rk, 0.0), axis=-1, keepdims=True)
    pos_col = jnp.where(grp_col == gf, rank_col - 1.0, -1.0)
    n_rows = jnp.max(rank_row).astype(jnp.int32)
    row_id = lax.broadcasted_iota(jnp.int32, (MOE_CH, tm), 0).astype(F32)
    col_id = lax.broadcasted_iota(jnp.int32, (tm, MOE_CH), 1).astype(F32)
    lane_c = lax.broadcasted_iota(jnp.int32, (MOE_CH, LANES), 1)

    def chunk_body(c, _):
        r0 = (c * MOE_CH).astype(F32)
        gather = jnp.where(pos_row - r0 == row_id, 1.0, 0.0).astype(BF16)
        xg = jnp.dot(gather, xe_ref[...], preferred_element_type=F32)
        xb = xg[:, :d].astype(BF16)
        gates_c = xg[:, d:d + LANES] + xg[:, d + LANES:d + 2 * LANES]
        ya = None
        for e in range(n_e):
            gu = jnp.dot(xb, wgu_ref[e], preferred_element_type=F32)
            gt = gu[:, :d_exp]
            a = (gt * jax.nn.sigmoid(gt)) * gu[:, d_exp:]
            gate_e = jnp.sum(jnp.where(lane_c == g * n_e + e, gates_c, 0.0), axis=-1, keepdims=True)
            y = jnp.dot((a * gate_e).astype(BF16), wd_ref[e], preferred_element_type=F32)
            ya = y if ya is None else ya + y
        scatter = jnp.where(pos_col - r0 == col_id, 1.0, 0.0).astype(BF16)
        o_ref[0] += jnp.dot(scatter, ya.astype(BF16), preferred_element_type=F32)
        return 0

    lax.fori_loop(0, (n_rows + MOE_CH - 1) // MOE_CH, chunk_body, 0)

    @pl.when(g == pl.num_programs(2) - 1)
    def _():
        o_ref[0] = x1_ref[0] + gate2_ref[0] * o_ref[0]


def _moe(x1, h2, gates, gate2, w_gu, w_d):
    b, s, d = x1.shape
    tm = TM_MOE
    n_e = EXPERTS_PER_GROUP
    assert w_gu.shape[0] == N_GROUPS * n_e and tm % RANK_BLK == 0
    tok = lambda w: pl.BlockSpec((1, tm, w), lambda bi, si, g: (bi, si, 0))
    return pl.pallas_call(
        _moe_kernel,
        out_shape=jax.ShapeDtypeStruct((b, s, d), F32),
        grid=(b, s // tm, N_GROUPS),
        in_specs=[tok(d), tok(d), tok(LANES),
                  pl.BlockSpec((1, 1, d), lambda bi, si, g: (bi, 0, 0)),
                  pl.BlockSpec((n_e,) + w_gu.shape[1:], lambda bi, si, g: (g, 0, 0)),
                  pl.BlockSpec((n_e,) + w_d.shape[1:], lambda bi, si, g: (g, 0, 0))],
        out_specs=tok(d),
        scratch_shapes=[pltpu.VMEM((tm, d + 2 * LANES), BF16),
                        pltpu.VMEM((tm, LANES), F32),
                        pltpu.VMEM((LANES, tm), F32)],
        compiler_params=pltpu.CompilerParams(
            dimension_semantics=("arbitrary", "arbitrary", "arbitrary"),
            vmem_limit_bytes=VMEM_LIMIT_MOE_BYTES),
        name="moe",
    )(x1, h2, gates, gate2, w_gu, w_d)


def _layer(x, mod, pos3, g_mix, g_ffn, w_in, g_q, g_k, g_kidx, w_pool, pool_scale, w_out,
           w_rg, b_rg, w_re, b_re, w_gate, w_up, w_down):
    b, s, d = x.shape
    d_attn = N_HEADS * HEAD_DIM
    nqb = s // Q_BLK
    nkb = s // K_BLK
    shift1, scale1, gate1, shift2, scale2, gate2 = [m[:, None, :] for m in jnp.split(mod, 6, axis=-1)]

    n_front = d_attn + 2 * HEAD_DIM + N_IDX_HEADS * IDX_DIM + IDX_DIM + N_IDX_HEADS
    pad = (-n_front) % LANES
    w_in_p = jnp.concatenate([w_in[:, :n_front], jnp.zeros((d, pad), w_in.dtype), w_in[:, n_front:]],
                             axis=1).astype(BF16)
    seg_id = jnp.arange(d_attn) // HEAD_DIM
    segsum = (seg_id[:, None] == seg_id[None, :]).astype(BF16)
    ones_half = jnp.ones((LANES - HEAD_DIM,), F32)
    gq_t = jnp.tile(g_q, N_HEADS)[None, :]
    gk_e = jnp.concatenate([g_k, ones_half])[None, :]
    gkidx_e = jnp.concatenate([g_kidx, ones_half])[None, :]
    half = HEAD_DIM // 2
    inv_freq = ROPE_THETA ** (-jnp.arange(0, HEAD_DIM, 2, dtype=F32) / HEAD_DIM)
    invf = jnp.tile(inv_freq, LANES // half)[None, :]

    q, kv, qi, ki, wi, pool = _inproj(pos3, x, scale1, shift1, g_mix[None, :], w_in_p, segsum, gq_t,
                                      gk_e, gkidx_e, invf, w_pool.astype(BF16), pool_scale[None, :])

    def to_cols(a, width):
        n_h = a.shape[2] // width
        a = a.reshape(b, nqb, Q_BLK, n_h, width)
        return jnp.transpose(a, (0, 1, 4, 3, 2)).reshape(b, nqb, width, n_h * Q_BLK)

    qt = to_cols(q, HEAD_DIM)
    qit = to_cols(qi, IDX_DIM)
    w_t = to_cols(wi, 1)
    kv4 = kv.reshape(b, s // CNT_BLK, CNT_BLK, LANES)
    ki4 = ki.reshape(b, s // CNT_BLK, CNT_BLK, LANES)
    vt4 = jnp.transpose(kv[:, :, HEAD_DIM:].reshape(b, nkb, K_BLK, HEAD_DIM), (0, 1, 3, 2))
    ones_rows = jnp.concatenate([jnp.ones((b, nkb, 1, K_BLK), BF16),
                                 jnp.zeros((b, nkb, SUBLANES - 1, K_BLK), BF16)], axis=2)
    vt4 = jnp.concatenate([vt4, ones_rows], axis=2)

    attn_t = _dsa(qt, qit, w_t, kv4, ki4, vt4)
    attn = jnp.transpose(attn_t.reshape(b, nqb, HEAD_DIM, N_HEADS, Q_BLK), (0, 1, 4, 3, 2))
    attn = attn.reshape(b, s, d_attn)

    w_out_b = w_out.astype(BF16)
    w_r = jnp.concatenate([w_re, w_rg, jnp.zeros((d, LANES - N_EXPERTS - N_GROUPS), F32)], axis=1)
    b_r = jnp.concatenate([b_re, b_rg, jnp.zeros((LANES - N_EXPERTS - N_GROUPS,), F32)])[None, :]
    x1, h2, gates = _outproj(x, attn, pool, w_out_b[:d_attn], w_out_b[d_attn:], gate1,
                             g_ffn[None, :], scale2, shift2, w_r, b_r)

    w_gu = jnp.concatenate([w_gate, w_up], axis=-1).astype(BF16)
    return _moe(x1, h2, gates, gate2, w_gu, w_down.astype(BF16))


def kernel(x, c, positions, w_ada, b_ada, g_norm_mix, g_norm_ffn, w_in, g_q, g_k, g_kidx, w_pool,
           pool_scale, w_out, w_router_group, b_router_group, w_router_expert, b_router_expert,
           w_gate, w_up, w_down):
    b, s, d = x.shape
    depth = w_ada.shape[0]
    assert s % TM_MOE == 0 and s % K_BLK == 0 and d % LANES == 0
    pos3 = positions[:, :, None]
    c_pad = jnp.concatenate([c, jnp.zeros((-b % SUBLANES, d), c.dtype)], axis=0)
    for l in range(depth):
        mod = _adaln(c_pad, w_ada[l], b_ada[l][None, :])[:b]
        x = _layer(x, mod, pos3, g_norm_mix[l], g_norm_ffn[l], w_in[l], g_q[l], g_k[l], g_kidx[l],
                   w_pool[l], pool_scale[l], w_out[l], w_router_group[l], b_router_group[l],
                   w_router_expert[l], b_router_expert[l], w_gate[l], w_up[l], w_down[l])
    return x
```

```python
import functools

import jax
import jax.numpy as jnp
from jax import lax
from jax.experimental import pallas as pl
from jax.experimental.pallas import tpu as pltpu

N_HEADS = 8
HEAD_DIM = 64
N_IDX_HEADS = 8
IDX_DIM = 64
TOPK_MAX = 256
ROPE_THETA = 10000.0
POOL_WINDOWS = (2, 4, 8, 16)
N_GROUPS = 4
EXPERTS_PER_GROUP = 8
N_EXPERTS = N_GROUPS * EXPERTS_PER_GROUP
EPS = 1e-6

LANES = 128
SUBLANES = 8
VMEM_LIMIT_BYTES = 56 * 1024 * 1024
VMEM_LIMIT_MOE_BYTES = 60 * 1024 * 1024

Q_BLK = 256
K_BLK = 256
COL_BLK = 256
CNT_BLK = 512
CNT_ROWS = 32
TM_PROJ = 512
TM_MOE = 1024
MOE_CH = 128
RANK_BLK = 256
MAX_WIN = max(POOL_WINDOWS)
M_INIT = -1e29
MASKED = -1e30
F32_LOWEST = -3.0e38
LOG2_E = 1.4426950408889634

BF16 = jnp.bfloat16
F32 = jnp.float32


def _cparams(sem):
    return pltpu.CompilerParams(dimension_semantics=sem, vmem_limit_bytes=VMEM_LIMIT_BYTES)


def _adaln_kernel(c_ref, w_ref, b_ref, o_ref):
    c = c_ref[...]
    c_act = c * jax.nn.sigmoid(c)
    o_ref[...] = jnp.dot(c_act, w_ref[...], preferred_element_type=F32) + b_ref[...]


def _adaln(c_pad, w_ada, b_ada):
    rows, d = c_pad.shape
    n = w_ada.shape[1]
    tn = n // 6
    return pl.pallas_call(
        _adaln_kernel,
        out_shape=jax.ShapeDtypeStruct((rows, n), F32),
        grid=(n // tn,),
        in_specs=[pl.BlockSpec((rows, d), lambda j: (0, 0)),
                  pl.BlockSpec((d, tn), lambda j: (0, j)),
                  pl.BlockSpec((1, tn), lambda j: (0, j))],
        out_specs=pl.BlockSpec((rows, tn), lambda j: (0, j)),
        compiler_params=_cparams(("arbitrary",)),
        name="adaln",
    )(c_pad, w_ada, b_ada)


def _rope_chunk(y, cos, sin_signed, first_half):
    from_hi = pltpu.roll(y, LANES - HEAD_DIM // 2, 1)
    from_lo = pltpu.roll(y, HEAD_DIM // 2, 1)
    return y * cos + jnp.where(first_half, from_hi, from_lo) * sin_signed


def _inproj_kernel(pos_ref, x_ref, scale_ref, shift_ref, gmix_ref, win_ref, segsum_ref,
                   gq_ref, gk_ref, gkidx_ref, invf_ref, wpool_ref, pscale_ref,
                   q_ref, kv_ref, qi_ref, ki_ref, wi_ref, pool_ref, ubuf_ref):
    tm = x_ref.shape[1]
    d_attn = N_HEADS * HEAD_DIM
    d_qidx = N_IDX_HEADS * IDX_DIM
    s_tile = pl.program_id(1)

    x = x_ref[0]
    ms = jnp.mean(x * x, axis=-1, keepdims=True)
    h = (x * lax.rsqrt(ms + EPS) * gmix_ref[...]) * (1.0 + scale_ref[0]) + shift_ref[0]
    proj = jnp.dot(h.astype(BF16), win_ref[...], preferred_element_type=F32)

    lane = lax.broadcasted_iota(jnp.int32, (tm, LANES), 1)
    first_half = (lane & (HEAD_DIM - 1)) < (HEAD_DIM // 2)
    ang = pos_ref[0].astype(F32) * invf_ref[...]
    cos = jnp.cos(ang)
    sin = jnp.sin(ang)
    sin_signed = jnp.where(first_half, -sin, sin)
    rope = functools.partial(_rope_chunk, cos=cos, sin_signed=sin_signed, first_half=first_half)

    qf = proj[:, :d_attn]
    qsq = qf * qf
    qsq_hi = qsq.astype(BF16)
    qsq_lo = (qsq - qsq_hi.astype(F32)).astype(BF16)
    seg = segsum_ref[...]
    ssq = (jnp.dot(qsq_hi, seg, preferred_element_type=F32)
           + jnp.dot(qsq_lo, seg, preferred_element_type=F32))
    qn = qf * lax.rsqrt(ssq * (1.0 / HEAD_DIM) + EPS) * gq_ref[...]
    for j in range(d_attn // LANES):
        sl = slice(j * LANES, (j + 1) * LANES)
        q_ref[0, :, sl] = (rope(qn[:, sl]) * (LOG2_E * HEAD_DIM ** -0.5)).astype(BF16)

    kvc = proj[:, d_attn:d_attn + LANES]
    is_k = lane < HEAD_DIM
    ksq = jnp.sum(jnp.where(is_k, kvc * kvc, 0.0), axis=-1, keepdims=True)
    kn = kvc * lax.rsqrt(ksq * (1.0 / HEAD_DIM) + EPS) * gk_ref[...]
    kv_ref[0] = jnp.where(is_k, rope(kn), kvc).astype(BF16)

    o_qi = d_attn + LANES
    for j in range(d_qidx // LANES):
        qi_ref[0, :, j * LANES:(j + 1) * LANES] = rope(
            proj[:, o_qi + j * LANES:o_qi + (j + 1) * LANES]).astype(BF16)

    o_ki = o_qi + d_qidx
    kic = proj[:, o_ki:o_ki + LANES]
    kisq = jnp.sum(jnp.where(is_k, kic * kic, 0.0), axis=-1, keepdims=True)
    kin = kic * lax.rsqrt(kisq * (1.0 / IDX_DIM) + EPS) * gkidx_ref[...]
    ki_ref[0] = jnp.where(is_k, rope(kin), 0.0).astype(BF16)
    wi_ref[0] = kic[:, IDX_DIM:IDX_DIM + N_IDX_HEADS] * (N_IDX_HEADS ** -0.5 * IDX_DIM ** -0.5)

    o_u = o_ki + LANES
    u = proj[:, o_u:o_u + LANES * len(POOL_WINDOWS)]

    @pl.when(s_tile == 0)
    def _():
        ubuf_ref[0:MAX_WIN, :] = jnp.zeros((MAX_WIN, u.shape[1]), F32)

    @pl.when(s_tile != 0)
    def _():
        ubuf_ref[0:MAX_WIN, :] = ubuf_ref[tm:tm + MAX_WIN, :]

    ubuf_ref[MAX_WIN:MAX_WIN + tm, :] = u
    t_idx = s_tile * tm + lax.broadcasted_iota(jnp.int32, (tm, 1), 0)
    for g, win in enumerate(POOL_WINDOWS):
        sl = slice(g * LANES, (g + 1) * LANES)
        wsum = u[:, sl]
        for j in range(1, win):
            wsum = wsum + ubuf_ref[MAX_WIN - j:MAX_WIN - j + tm, sl]
        cnt = jnp.minimum(t_idx + 1, win).astype(F32)
        pooled = wsum / cnt - u[:, sl]
        mixed = jnp.dot(pooled.astype(BF16), wpool_ref[g], preferred_element_type=F32)
        pool_ref[0, :, sl] = (mixed * pscale_ref[:, sl]).astype(BF16)


def _inproj(pos3, x, scale1, shift1, g_mix, w_in_p, segsum, gq_t, gk_e, gkidx_e, invf, w_pool, pscale):
    b, s, d = x.shape
    tm = TM_PROJ
    d_attn = N_HEADS * HEAD_DIM
    d_qidx = N_IDX_HEADS * IDX_DIM
    d_pool = LANES * len(POOL_WINDOWS)
    tok = lambda w: pl.BlockSpec((1, tm, w), lambda bi, si: (bi, si, 0))
    per_b = pl.BlockSpec((1, 1, d), lambda bi, si: (bi, 0, 0))
    full = lambda a: pl.BlockSpec(a.shape, lambda bi, si: (0,) * a.ndim)
    return pl.pallas_call(
        _inproj_kernel,
        out_shape=(jax.ShapeDtypeStruct((b, s, d_attn), BF16),
                   jax.ShapeDtypeStruct((b, s, LANES), BF16),
                   jax.ShapeDtypeStruct((b, s, d_qidx), BF16),
                   jax.ShapeDtypeStruct((b, s, LANES), BF16),
                   jax.ShapeDtypeStruct((b, s, N_IDX_HEADS), F32),
                   jax.ShapeDtypeStruct((b, s, d_pool), BF16)),
        grid=(b, s // tm),
        in_specs=[tok(1), tok(d), per_b, per_b, full(g_mix), full(w_in_p), full(segsum),
                  full(gq_t), full(gk_e), full(gkidx_e), full(invf), full(w_pool), full(pscale)],
        out_specs=(tok(d_attn), tok(LANES), tok(d_qidx), tok(LANES), tok(N_IDX_HEADS), tok(d_pool)),
        scratch_shapes=[pltpu.VMEM((tm + 2 * MAX_WIN, d_pool), F32)],
        compiler_params=_cparams(("arbitrary", "arbitrary")),
        name="inproj",
    )(pos3, x, scale1, shift1, g_mix, w_in_p, segsum, gq_t, gk_e, gkidx_e, invf, w_pool, pscale)


def _dsa_kernel(qt_ref, qit_ref, w_ref, kv_ref, ki_ref, vt_ref, o_ref,
                sc_ref, qe_ref, qie_ref, m_ref, mx_ref, st_ref, acc_ref, lg_ref, p_ref):
    topk = float(min(TOPK_MAX, (sc_ref.shape[0] * CNT_BLK) // 4))
    qb = pl.program_id(1)
    n_cols = qt_ref.shape[3]
    n_chunks = n_cols // COL_BLK
    sub = CNT_BLK // K_BLK
    nch = ((qb + 1) * Q_BLK + CNT_BLK - 1) // CNT_BLK
    kgrp = K_BLK // SUBLANES
    sub_rows = [slice(j * K_BLK, (j + 1) * K_BLK) for j in range(sub)]

    zeros_half = jnp.zeros((LANES - HEAD_DIM, n_cols), BF16)
    qe_ref[0:HEAD_DIM, :] = qt_ref[0, 0]
    qe_ref[HEAD_DIM:LANES, :] = zeros_half
    qie_ref[0:IDX_DIM, :] = qit_ref[0, 0]
    qie_ref[IDX_DIM:LANES, :] = zeros_half

    q_pos = qb * Q_BLK + lax.broadcasted_iota(jnp.int32, (K_BLK, Q_BLK), 1)
    key_off = lax.broadcasted_iota(jnp.int32, (K_BLK, Q_BLK), 0)

    def score_body(ch, carry):
        rmax, rmin = carry
        for j in range(sub):
            ki_blk = ki_ref[0, ch, sub_rows[j], :]
            score = None
            for cc in range(n_chunks):
                cs = slice(cc * COL_BLK, (cc + 1) * COL_BLK)
                s_h = jnp.dot(ki_blk, qie_ref[:, cs], preferred_element_type=F32)
                s_h = jnp.maximum(s_h, 0.0) * w_ref[0, 0, :, cs]
                for hh in range(COL_BLK // Q_BLK):
                    part = s_h[:, hh * Q_BLK:(hh + 1) * Q_BLK]
                    score = part if score is None else score + part
            causal = (ch * CNT_BLK + j * K_BLK + key_off) <= q_pos
            masked = jnp.where(causal, score, -jnp.inf)
            sc_ref[ch, sub_rows[j], :] = masked
            hi_part = masked.reshape(kgrp, SUBLANES, Q_BLK).max(axis=0)
            lo_part = jnp.where(causal, score, jnp.inf).reshape(kgrp, SUBLANES, Q_BLK).min(axis=0)
            rmax, rmin = jnp.maximum(rmax, hi_part), jnp.minimum(rmin, lo_part)
        return rmax, rmin

    rmax8, rmin8 = lax.fori_loop(
        0, nch, score_body,
        (jnp.full((SUBLANES, Q_BLK), -jnp.inf, F32), jnp.full((SUBLANES, Q_BLK), jnp.inf, F32)))
    rowmax = jnp.max(rmax8, axis=0, keepdims=True)
    rowmin = jnp.min(rmin8, axis=0, keepdims=True)

    n_causal = (qb * Q_BLK + 1 + lax.broadcasted_iota(jnp.int32, (1, Q_BLK), 1)).astype(F32)
    kt = jnp.minimum(n_causal, topk)

    cgrp = CNT_BLK // CNT_ROWS

    def count_ge(t):
        def body(ch, acc):
            for r in range(cgrp):
                rows = sc_ref[ch, r * CNT_ROWS:(r + 1) * CNT_ROWS, :]
                acc = acc + jnp.where(rows >= t, 1.0, 0.0)
            return acc
        acc = lax.fori_loop(0, nch, body, jnp.zeros((CNT_ROWS, Q_BLK), F32))
        return jnp.sum(acc, axis=0, keepdims=True)

    def bisect_pass(state):
        lo, hi, c_lo, c_hi, thr, done = state
        mid = jnp.where(hi == jnp.inf, rowmax, lo + 0.5 * (hi - lo))
        c = count_ge(mid)
        hit = jnp.logical_and(done == 0.0, c == kt)
        thr = jnp.where(hit, mid, thr)
        done = jnp.where(hit, 1.0, done)
        go_up = c >= kt
        active = done == 0.0
        lo_n = jnp.where(jnp.logical_and(active, go_up), mid, lo)
        c_lo_n = jnp.where(jnp.logical_and(active, go_up), c, c_lo)
        hi_n = jnp.where(jnp.logical_and(active, jnp.logical_not(go_up)), mid, hi)
        c_hi_n = jnp.where(jnp.logical_and(active, jnp.logical_not(go_up)), c, c_hi)
        return lo_n, hi_n, c_lo_n, c_hi_n, thr, done

    def snap_pass(state):
        lo, hi, c_lo, c_hi, thr, done = state

        def body(ch, carry):
            a8, b8 = carry
            for r in range(cgrp):
                s = sc_ref[ch, r * CNT_ROWS:(r + 1) * CNT_ROWS, :]
                a8 = jnp.minimum(a8, jnp.where(s >= lo, s, jnp.inf))
                b8 = jnp.maximum(b8, jnp.where(s < hi, s, -jnp.inf))
            return a8, b8

        a8, b8 = lax.fori_loop(
            0, nch, body,
            (jnp.full((CNT_ROWS, Q_BLK), jnp.inf, F32), jnp.full((CNT_ROWS, Q_BLK), -jnp.inf, F32)))
        a = jnp.min(a8, axis=0, keepdims=True)
        b = jnp.max(b8, axis=0, keepdims=True)
        hit = jnp.logical_and(done == 0.0, a == b)
        thr = jnp.where(hit, a, thr)
        done = jnp.where(hit, 2.0, done)
        return lo, hi, c_lo, c_hi, thr, done

    few = n_causal <= topk
    state0 = (rowmin, jnp.full((1, Q_BLK), jnp.inf, F32), n_causal, jnp.zeros((1, Q_BLK), F32),
              jnp.where(few, F32_LOWEST, 0.0), jnp.where(few, 1.0, 0.0))

    def outer_cond(carry):
        return carry[1] > 0.0

    def outer_body(carry):
        state, _ = carry
        state = lax.fori_loop(0, 3, lambda i, st: bisect_pass(st), state)
        state = snap_pass(state)
        pending = jnp.max(jnp.where(state[5] == 0.0, 1.0, 0.0))
        return state, pending

    state1 = lax.fori_loop(0, 14, lambda i, st: bisect_pass(st), state0)
    state1 = snap_pass(state1)
    pending1 = jnp.max(jnp.where(state1[5] == 0.0, 1.0, 0.0))
    (lo, hi, c_lo, c_hi, thr, done), _ = lax.while_loop(outer_cond, outer_body, (state1, pending1))

    excess = jnp.where(done == 2.0, c_lo - kt, 0.0)
    need = kt - c_hi

    @pl.when(jnp.max(excess) > 0.0)
    def _():
        tri = (lax.broadcasted_iota(jnp.int32, (K_BLK, K_BLK), 0)
               >= lax.broadcasted_iota(jnp.int32, (K_BLK, K_BLK), 1)).astype(BF16)
        has_excess = excess > 0.0

        def drop_body(ch, run):
            for j in range(sub):
                s = sc_ref[ch, sub_rows[j], :]
                tied = jnp.logical_and(s == thr, has_excess)
                prefix = jnp.dot(tri, jnp.where(tied, 1.0, 0.0).astype(BF16), preferred_element_type=F32)
                drop = jnp.logical_and(tied, run + prefix > need)
                sc_ref[ch, sub_rows[j], :] = jnp.where(drop, -jnp.inf, s)
                run = run + jnp.max(prefix, axis=0, keepdims=True)
            return run

        lax.fori_loop(0, nch, drop_body, jnp.zeros((1, Q_BLK), F32))

    m_ref[...] = jnp.full(m_ref.shape, M_INIT, F32)
    acc_ref[...] = jnp.zeros(acc_ref.shape, F32)
    row_m = lambda j: slice(j, j + 1)
    row_a = lambda j: slice(sub + j, sub + j + 1)

    def logits_stage(ch, j):
        kv_blk = kv_ref[0, ch, sub_rows[j], :]
        bias = jnp.where(sc_ref[ch, sub_rows[j], :] >= thr, 0.0, MASKED)
        for cc in range(n_chunks):
            logits = jnp.dot(kv_blk, qe_ref[:, cc * COL_BLK:(cc + 1) * COL_BLK],
                             preferred_element_type=F32)
            for hh in range(COL_BLK // Q_BLK):
                cs = slice(cc * COL_BLK + hh * Q_BLK, cc * COL_BLK + (hh + 1) * Q_BLK)
                lg = logits[:, hh * Q_BLK:(hh + 1) * Q_BLK] + bias
                lg_ref[j, :, cs] = lg
                mx_ref[:, cs] = lg.reshape(kgrp, SUBLANES, Q_BLK).max(axis=0)
        m_old = m_ref[...]
        m_new = jnp.maximum(m_old, jnp.max(mx_ref[...], axis=0, keepdims=True))
        st_ref[row_m(j), :] = m_new
        st_ref[row_a(j), :] = jnp.exp2(m_old - m_new)
        m_ref[...] = m_new

    def probs_stage(j):
        p_ref[j] = jnp.exp2(lg_ref[j] - st_ref[row_m(j), :]).astype(BF16)

    def value_stage(kb, j):
        acc_ref[...] = acc_ref[...] * st_ref[row_a(j), :] + jnp.dot(
            vt_ref[0, kb], p_ref[j], preferred_element_type=F32)

    p_ref[sub - 1] = jnp.zeros(p_ref.shape[1:], BF16)
    st_ref[row_a(sub - 1), :] = jnp.ones((1, n_cols), F32)
    logits_stage(0, 0)

    def attn_body(ch, _):
        value_stage(jnp.maximum(ch * sub - 1, 0), 1)
        probs_stage(0)
        logits_stage(ch, 1)
        value_stage(ch * sub, 0)
        probs_stage(1)
        logits_stage(jnp.minimum(ch + 1, nch - 1), 0)
        return 0

    lax.fori_loop(0, nch, attn_body, 0)
    value_stage(nch * sub - 1, 1)
    dh = o_ref.shape[2]
    o_ref[0, 0] = (acc_ref[0:dh, :] / acc_ref[dh:dh + 1, :]).astype(o_ref.dtype)


def _dsa(qt, qit, w_t, kv4, ki4, vt4):
    b, nqb, dh, n_cols = qt.shape
    n_steps = kv4.shape[1]
    assert kv4.shape[2] == CNT_BLK and CNT_BLK == 2 * K_BLK and n_cols % COL_BLK == 0
    assert vt4.shape[1] * K_BLK == n_steps * CNT_BLK and vt4.shape[2] == dh + SUBLANES
    per_q = lambda a: pl.BlockSpec((1, 1) + a.shape[2:], lambda bi, qi: (bi, qi, 0, 0))
    per_b = lambda a: pl.BlockSpec((1,) + a.shape[1:], lambda bi, qi: (bi, 0, 0, 0))
    return pl.pallas_call(
        _dsa_kernel,
        out_shape=jax.ShapeDtypeStruct((b, nqb, dh, n_cols), BF16),
        grid=(b, nqb),
        in_specs=[per_q(qt), per_q(qit), per_q(w_t), per_b(kv4), per_b(ki4), per_b(vt4)],
        out_specs=pl.BlockSpec((1, 1, dh, n_cols), lambda bi, qi: (bi, qi, 0, 0)),
        scratch_shapes=[pltpu.VMEM((n_steps, CNT_BLK, Q_BLK), F32),
                        pltpu.VMEM((LANES, n_cols), BF16),
                        pltpu.VMEM((LANES, n_cols), BF16),
                        pltpu.VMEM((1, n_cols), F32),
                        pltpu.VMEM((SUBLANES, n_cols), F32),
                        pltpu.VMEM((SUBLANES, n_cols), F32),
                        pltpu.VMEM((dh + SUBLANES, n_cols), F32),
                        pltpu.VMEM((CNT_BLK // K_BLK, K_BLK, n_cols), F32),
                        pltpu.VMEM((CNT_BLK // K_BLK, K_BLK, n_cols), BF16)],
        compiler_params=_cparams(("arbitrary", "arbitrary")),
        name="dsa",
    )(qt, qit, w_t, kv4, ki4, vt4)


def _outproj_kernel(x_ref, attn_ref, pool_ref, woa_ref, wop_ref, gate1_ref, gffn_ref,
                    scale2_ref, shift2_ref, wr_ref, br_ref, x1_ref, h2_ref, gates_ref):
    tm = x_ref.shape[1]
    mix = (jnp.dot(attn_ref[0], woa_ref[...], preferred_element_type=F32)
           + jnp.dot(pool_ref[0], wop_ref[...], preferred_element_type=F32))
    x1 = x_ref[0] + gate1_ref[0] * mix
    x1_ref[0] = x1
    ms = jnp.mean(x1 * x1, axis=-1, keepdims=True)
    h2 = (x1 * lax.rsqrt(ms + EPS) * gffn_ref[...]) * (1.0 + scale2_ref[0]) + shift2_ref[0]
    h2_hi = h2.astype(BF16)
    h2_ref[0] = h2_hi

    h2_lo = (h2 - h2_hi.astype(F32)).astype(BF16)
    wr = wr_ref[...]
    wr_hi = wr.astype(BF16)
    wr_lo = (wr - wr_hi.astype(F32)).astype(BF16)
    logits = (jnp.dot(h2_hi, wr_hi, preferred_element_type=F32)
              + jnp.dot(h2_lo, wr_hi, preferred_element_type=F32)
              + jnp.dot(h2_hi, wr_lo, preferred_element_type=F32)) + br_ref[...]

    lane = lax.broadcasted_iota(jnp.int32, (tm, LANES), 1)
    big = jnp.int32(LANES)
    is_g = jnp.logical_and(lane >= N_EXPERTS, lane < N_EXPERTS + N_GROUPS)
    glog = jnp.where(is_g, logits, -jnp.inf)
    gmax = jnp.max(glog, axis=-1, keepdims=True)
    gsum = jnp.sum(jnp.exp(glog - gmax), axis=-1, keepdims=True)
    p_g = 1.0 / gsum
    g_sel = jnp.min(jnp.where(glog == gmax, lane, big), axis=-1, keepdims=True) - N_EXPERTS
    in_grp = jnp.logical_and(lane < N_EXPERTS, jnp.right_shift(lane, 3) == g_sel)
    elog = jnp.where(in_grp, logits, -jnp.inf)
    emax = jnp.max(elog, axis=-1, keepdims=True)
    eexp = jnp.exp(elog - emax)
    esum = jnp.sum(eexp, axis=-1, keepdims=True)
    p_e = jnp.where(in_grp, eexp / esum, -1.0)
    p1 = jnp.max(p_e, axis=-1, keepdims=True)
    i1 = jnp.min(jnp.where(p_e == p1, lane, big), axis=-1, keepdims=True)
    p_e2 = jnp.where(lane == i1, -1.0, p_e)
    p2 = jnp.max(p_e2, axis=-1, keepdims=True)
    i2 = jnp.min(jnp.where(p_e2 == p2, lane, big), axis=-1, keepdims=True)
    tot = p1 + p2
    gates_ref[0] = (jnp.where(lane == i1, p_g * (p1 / tot), 0.0)
                    + jnp.where(lane == i2, p_g * (p2 / tot), 0.0)
                    + jnp.where(lane == N_EXPERTS, g_sel.astype(F32), 0.0))


def _outproj(x, attn, pool, wo_a, wo_p, gate1, g_ffn, scale2, shift2, w_r, b_r):
    b, s, d = x.shape
    tm = TM_PROJ
    tok = lambda w: pl.BlockSpec((1, tm, w), lambda bi, si: (bi, si, 0))
    per_b = pl.BlockSpec((1, 1, d), lambda bi, si: (bi, 0, 0))
    full = lambda a: pl.BlockSpec(a.shape, lambda bi, si: (0,) * a.ndim)
    return pl.pallas_call(
        _outproj_kernel,
        out_shape=(jax.ShapeDtypeStruct((b, s, d), F32),
                   jax.ShapeDtypeStruct((b, s, d), BF16),
                   jax.ShapeDtypeStruct((b, s, LANES), F32)),
        grid=(b, s // tm),
        in_specs=[tok(d), tok(attn.shape[2]), tok(pool.shape[2]), full(wo_a), full(wo_p), per_b,
                  full(g_ffn), per_b, per_b, full(w_r), full(b_r)],
        out_specs=(tok(d), tok(d), tok(LANES)),
        compiler_params=_cparams(("arbitrary", "arbitrary")),
        name="outproj",
    )(x, attn, pool, wo_a, wo_p, gate1, g_ffn, scale2, shift2, w_r, b_r)


def _moe_kernel(x1_ref, h2_ref, gates_ref, gate2_ref, wgu_ref, wd_ref, o_ref,
                xe_ref, rank_ref, rank_t_ref):
    g = pl.program_id(2)
    tm, d = h2_ref.shape[1], h2_ref.shape[2]
    n_e, d_exp = wd_ref.shape[0], wd_ref.shape[1]
    lane = lax.broadcasted_iota(jnp.int32, (tm, LANES), 1)
    gf = g.astype(F32)

    @pl.when(g == 0)
    def _():
        tri = (lax.broadcasted_iota(jnp.int32, (RANK_BLK, RANK_BLK), 0)
               >= lax.broadcasted_iota(jnp.int32, (RANK_BLK, RANK_BLK), 1)).astype(BF16)
        lane_b = lax.broadcasted_iota(jnp.int32, (RANK_BLK, LANES), 1)
        run = jnp.zeros((1, LANES), F32)
        for sb in range(tm // RANK_BLK):
            rows = slice(sb * RANK_BLK, (sb + 1) * RANK_BLK)
            gts_b = gates_ref[0, rows, :]
            grp = jnp.sum(jnp.where(lane_b == N_EXPERTS, gts_b, 0.0), axis=-1, keepdims=True)
            member = jnp.where(jnp.logical_and(lane_b < N_GROUPS, lane_b.astype(F32) == grp), 1.0, 0.0)
            pre = jnp.dot(tri, member.astype(BF16), preferred_element_type=F32) + run
            rank_ref[rows, :] = jnp.where(lane_b == N_GROUPS, grp, pre)
            run = jnp.max(pre, axis=0, keepdims=True)
        rank_t_ref[...] = rank_ref[...].T
        gts = gates_ref[0]
        g_hi = gts.astype(BF16)
        xe_ref[:, :d] = h2_ref[0]
        xe_ref[:, d:d + LANES] = g_hi
        xe_ref[:, d + LANES:d + 2 * LANES] = (gts - g_hi.astype(F32)).astype(BF16)
        o_ref[0] = jnp.zeros((tm, d), F32)

    rank_row = rank_t_ref[pl.ds(g, 1), :]
    pos_row = jnp.where(rank_t_ref[N_GROUPS:N_GROUPS + 1, :] == gf, rank_row - 1.0, -1.0)
    rk = rank_ref[...]
    rank_col = jnp.sum(jnp.where(lane == g, rk, 0.0), axis=-1, keepdims=True)
    grp_col = jnp.sum(jnp.where(lane == N_GROUPS, rk, 0.0), axis=-1, keepdims=True)
    pos_col = jnp.where(grp_col == gf, rank_col - 1.0, -1.0)
    n_rows = jnp.max(rank_row).astype(jnp.int32)
    row_id = lax.broadcasted_iota(jnp.int32, (MOE_CH, tm), 0).astype(F32)
    col_id = lax.broadcasted_iota(jnp.int32, (tm, MOE_CH), 1).astype(F32)
    lane_c = lax.broadcasted_iota(jnp.int32, (MOE_CH, LANES), 1)

    def chunk_body(c, _):
        r0 = (c * MOE_CH).astype(F32)
        gather = jnp.where(pos_row - r0 == row_id, 1.0, 0.0).astype(BF16)
        xg = jnp.dot(gather, xe_ref[...], preferred_element_type=F32)
        xb = xg[:, :d].astype(BF16)
        gates_c = xg[:, d:d + LANES] + xg[:, d + LANES:d + 2 * LANES]
        ya = None
        for e in range(n_e):
            gu = jnp.dot(xb, wgu_ref[e], preferred_element_type=F32)
            gt = gu[:, :d_exp]
            a = (gt * jax.nn.sigmoid(gt)) * gu[:, d_exp:]
            gate_e = jnp.sum(jnp.where(lane_c == g * n_e + e, gates_c, 0.0), axis=-1, keepdims=True)
            y = jnp.dot((a * gate_e).astype(BF16), wd_ref[e], preferred_element_type=F32)
            ya = y if ya is None else ya + y
        scatter = jnp.where(pos_col - r0 == col_id, 1.0, 0.0).astype(BF16)
        o_ref[0] += jnp.dot(scatter, ya.astype(BF16), preferred_element_type=F32)
        return 0

    lax.fori_loop(0, (n_rows + MOE_CH - 1) // MOE_CH, chunk_body, 0)

    @pl.when(g == pl.num_programs(2) - 1)
    def _():
        o_ref[0] = x1_ref[0] + gate2_ref[0] * o_ref[0]


def _moe(x1, h2, gates, gate2, w_gu, w_d):
    b, s, d = x1.shape
    tm = TM_MOE
    n_e = EXPERTS_PER_GROUP
    assert w_gu.shape[0] == N_GROUPS * n_e and tm % RANK_BLK == 0
    tok = lambda w: pl.BlockSpec((1, tm, w), lambda bi, si, g: (bi, si, 0))
    return pl.pallas_call(
        _moe_kernel,
        out_shape=jax.ShapeDtypeStruct((b, s, d), F32),
        grid=(b, s // tm, N_GROUPS),
        in_specs=[tok(d), tok(d), tok(LANES),
                  pl.BlockSpec((1, 1, d), lambda bi, si, g: (bi, 0, 0)),
                  pl.BlockSpec((n_e,) + w_gu.shape[1:], lambda bi, si, g: (g, 0, 0)),
                  pl.BlockSpec((n_e,) + w_d.shape[1:], lambda bi, si, g: (g, 0, 0))],
        out_specs=tok(d),
        scratch_shapes=[pltpu.VMEM((tm, d + 2 * LANES), BF16),
                        pltpu.VMEM((tm, LANES), F32),
                        pltpu.VMEM((LANES, tm), F32)],
        compiler_params=pltpu.CompilerParams(
            dimension_semantics=("arbitrary", "arbitrary", "arbitrary"),
            vmem_limit_bytes=VMEM_LIMIT_MOE_BYTES),
        name="moe",
    )(x1, h2, gates, gate2, w_gu, w_d)


def _layer(x, mod, pos3, g_mix, g_ffn, w_in, g_q, g_k, g_kidx, w_pool, pool_scale, w_out,
           w_rg, b_rg, w_re, b_re, w_gate, w_up, w_down):
    b, s, d = x.shape
    d_attn = N_HEADS * HEAD_DIM
    nqb = s // Q_BLK
    nkb = s // K_BLK
    shift1, scale1, gate1, shift2, scale2, gate2 = [m[:, None, :] for m in jnp.split(mod, 6, axis=-1)]

    n_front = d_attn + 2 * HEAD_DIM + N_IDX_HEADS * IDX_DIM + IDX_DIM + N_IDX_HEADS
    pad = (-n_front) % LANES
    w_in_p = jnp.concatenate([w_in[:, :n_front], jnp.zeros((d, pad), w_in.dtype), w_in[:, n_front:]],
                             axis=1).astype(BF16)
    seg_id = jnp.arange(d_attn) // HEAD_DIM
    segsum = (seg_id[:, None] == seg_id[None, :]).astype(BF16)
    ones_half = jnp.ones((LANES - HEAD_DIM,), F32)
    gq_t = jnp.tile(g_q, N_HEADS)[None, :]
    gk_e = jnp.concatenate([g_k, ones_half])[None, :]
    gkidx_e = jnp.concatenate([g_kidx, ones_half])[None, :]
    half = HEAD_DIM // 2
    inv_freq = ROPE_THETA ** (-jnp.arange(0, HEAD_DIM, 2, dtype=F32) / HEAD_DIM)
    invf = jnp.tile(inv_freq, LANES // half)[None, :]

    q, kv, qi, ki, wi, pool = _inproj(pos3, x, scale1, shift1, g_mix[None, :], w_in_p, segsum, gq_t,
                                      gk_e, gkidx_e, invf, w_pool.astype(BF16), pool_scale[None, :])

    def to_cols(a, width):
        n_h = a.shape[2] // width
        a = a.reshape(b, nqb, Q_BLK, n_h, width)
        return jnp.transpose(a, (0, 1, 4, 3, 2)).reshape(b, nqb, width, n_h * Q_BLK)

    qt = to_cols(q, HEAD_DIM)
    qit = to_cols(qi, IDX_DIM)
    w_t = to_cols(wi, 1)
    kv4 = kv.reshape(b, s // CNT_BLK, CNT_BLK, LANES)
    ki4 = ki.reshape(b, s // CNT_BLK, CNT_BLK, LANES)
    vt4 = jnp.transpose(kv[:, :, HEAD_DIM:].reshape(b, nkb, K_BLK, HEAD_DIM), (0, 1, 3, 2))
    ones_rows = jnp.concatenate([jnp.ones((b, nkb, 1, K_BLK), BF16),
                                 jnp.zeros((b, nkb, SUBLANES - 1, K_BLK), BF16)], axis=2)
    vt4 = jnp.concatenate([vt4, ones_rows], axis=2)

    attn_t = _dsa(qt, qit, w_t, kv4, ki4, vt4)
    attn = jnp.transpose(attn_t.reshape(b, nqb, HEAD_DIM, N_HEADS, Q_BLK), (0, 1, 4, 3, 2))
    attn = attn.reshape(b, s, d_attn)

    w_out_b = w_out.astype(BF16)
    w_r = jnp.concatenate([w_re, w_rg, jnp.zeros((d, LANES - N_EXPERTS - N_GROUPS), F32)], axis=1)
    b_r = jnp.concatenate([b_re, b_rg, jnp.zeros((LANES - N_EXPERTS - N_GROUPS,), F32)])[None, :]
    x1, h2, gates = _outproj(x, attn, pool, w_out_b[:d_attn], w_out_b[d_attn:], gate1,
                             g_ffn[None, :], scale2, shift2, w_r, b_r)

    w_gu = jnp.concatenate([w_gate, w_up], axis=-1).astype(BF16)
    return _moe(x1, h2, gates, gate2, w_gu, w_down.astype(BF16))


def kernel(x, c, positions, w_ada, b_ada, g_norm_mix, g_norm_ffn, w_in, g_q, g_k, g_kidx, w_pool,
           pool_scale, w_out, w_router_group, b_router_group, w_router_expert, b_router_expert,
           w_gate, w_up, w_down):
    b, s, d = x.shape
    depth = w_ada.shape[0]
    assert s % TM_MOE == 0 and s % K_BLK == 0 and d % LANES == 0
    pos3 = positions[:, :, None]
    c_pad = jnp.concatenate([c, jnp.zeros((-b % SUBLANES, d), c.dtype)], axis=0)
    for l in range(depth):
        mod = _adaln(c_pad, w_ada[l], b_ada[l][None, :])[:b]
        x = _layer(x, mod, pos3, g_norm_mix[l], g_norm_ffn[l], w_in[l], g_q[l], g_k[l], g_kidx[l],
                   w_pool[l], pool_scale[l], w_out[l], w_router_group[l], b_router_group[l],
                   w_router_expert[l], b_router_expert[l], w_gate[l], w_up[l], w_down[l])
    return x
```

```python
import functools

import jax
import jax.numpy as jnp
from jax import lax
from jax.experimental import pallas as pl
from jax.experimental.pallas import tpu as pltpu

N_HEADS = 8
HEAD_DIM = 64
N_IDX_HEADS = 8
IDX_DIM = 64
TOPK_MAX = 256
ROPE_THETA = 10000.0
POOL_WINDOWS = (2, 4, 8, 16)
N_GROUPS = 4
EXPERTS_PER_GROUP = 8
N_EXPERTS = N_GROUPS * EXPERTS_PER_GROUP
EPS = 1e-6

LANES = 128
SUBLANES = 8
VMEM_LIMIT_BYTES = 56 * 1024 * 1024
VMEM_LIMIT_MOE_BYTES = 60 * 1024 * 1024

Q_BLK = 256
K_BLK = 256
COL_BLK = 256
CNT_BLK = 512
CNT_ROWS = 32
SEARCH_FIRST = 15
SEARCH_PERIOD = 2
TM_PROJ = 512
TM_MOE = 1024
MOE_CH = 128
RANK_BLK = 256
MAX_WIN = max(POOL_WINDOWS)
M_INIT = -1e29
MASKED = -1e30
F32_LOWEST = -3.0e38
LOG2_E = 1.4426950408889634

BF16 = jnp.bfloat16
F32 = jnp.float32


def _cparams(sem):
    return pltpu.CompilerParams(dimension_semantics=sem, vmem_limit_bytes=VMEM_LIMIT_BYTES)


def _adaln_kernel(c_ref, w_ref, b_ref, o_ref):
    c = c_ref[...]
    c_act = c * jax.nn.sigmoid(c)
    o_ref[...] = jnp.dot(c_act, w_ref[...], preferred_element_type=F32) + b_ref[...]


def _adaln(c_pad, w_ada, b_ada):
    rows, d = c_pad.shape
    n = w_ada.shape[1]
    tn = n // 6
    return pl.pallas_call(
        _adaln_kernel,
        out_shape=jax.ShapeDtypeStruct((rows, n), F32),
        grid=(n // tn,),
        in_specs=[pl.BlockSpec((rows, d), lambda j: (0, 0)),
                  pl.BlockSpec((d, tn), lambda j: (0, j)),
                  pl.BlockSpec((1, tn), lambda j: (0, j))],
        out_specs=pl.BlockSpec((rows, tn), lambda j: (0, j)),
        compiler_params=_cparams(("arbitrary",)),
        name="adaln",
    )(c_pad, w_ada, b_ada)


def _rope_chunk(y, cos, sin_signed, first_half):
    from_hi = pltpu.roll(y, LANES - HEAD_DIM // 2, 1)
    from_lo = pltpu.roll(y, HEAD_DIM // 2, 1)
    return y * cos + jnp.where(first_half, from_hi, from_lo) * sin_signed


def _inproj_kernel(pos_ref, x_ref, scale_ref, shift_ref, gmix_ref, win_ref, segsum_ref,
                   gq_ref, gk_ref, gkidx_ref, invf_ref, wpool_ref, pscale_ref,
                   qt_ref, kv_ref, qit_ref, ki_ref, wt_ref, vt_ref, pool_ref, ubuf_ref):
    tm = x_ref.shape[1]
    d_attn = N_HEADS * HEAD_DIM
    d_qidx = N_IDX_HEADS * IDX_DIM
    s_tile = pl.program_id(1)

    def store_cols(dst_ref, chunk, j):
        for t in range(tm // Q_BLK):
            ct = chunk[t * Q_BLK:(t + 1) * Q_BLK, :].T
            for hh in range(2):
                col = (2 * j + hh) * Q_BLK
                dst_ref[0, t, :, col:col + Q_BLK] = ct[hh * HEAD_DIM:(hh + 1) * HEAD_DIM, :].astype(dst_ref.dtype)

    x = x_ref[0]
    ms = jnp.mean(x * x, axis=-1, keepdims=True)
    h = (x * lax.rsqrt(ms + EPS) * gmix_ref[...]) * (1.0 + scale_ref[0]) + shift_ref[0]
    proj = jnp.dot(h.astype(BF16), win_ref[...], preferred_element_type=F32)

    lane = lax.broadcasted_iota(jnp.int32, (tm, LANES), 1)
    first_half = (lane & (HEAD_DIM - 1)) < (HEAD_DIM // 2)
    ang = pos_ref[0].astype(F32) * invf_ref[...]
    cos = jnp.cos(ang)
    sin = jnp.sin(ang)
    sin_signed = jnp.where(first_half, -sin, sin)
    rope = functools.partial(_rope_chunk, cos=cos, sin_signed=sin_signed, first_half=first_half)

    qf = proj[:, :d_attn]
    qsq = qf * qf
    qsq_hi = qsq.astype(BF16)
    qsq_lo = (qsq - qsq_hi.astype(F32)).astype(BF16)
    seg = segsum_ref[...]
    ssq = (jnp.dot(qsq_hi, seg, preferred_element_type=F32)
           + jnp.dot(qsq_lo, seg, preferred_element_type=F32))
    qn = qf * lax.rsqrt(ssq * (1.0 / HEAD_DIM) + EPS) * gq_ref[...]
    for j in range(d_attn // LANES):
        sl = slice(j * LANES, (j + 1) * LANES)
        store_cols(qt_ref, rope(qn[:, sl]) * (LOG2_E * HEAD_DIM ** -0.5), j)

    kvc = proj[:, d_attn:d_attn + LANES]
    is_k = lane < HEAD_DIM
    ksq = jnp.sum(jnp.where(is_k, kvc * kvc, 0.0), axis=-1, keepdims=True)
    kn = kvc * lax.rsqrt(ksq * (1.0 / HEAD_DIM) + EPS) * gk_ref[...]
    kv_ref[0] = jnp.where(is_k, rope(kn), kvc).astype(BF16)
    row8 = lax.broadcasted_iota(jnp.int32, (SUBLANES, K_BLK), 0)
    for t in range(tm // K_BLK):
        vt_ref[0, t, 0:HEAD_DIM, :] = kvc[t * K_BLK:(t + 1) * K_BLK, :].T[HEAD_DIM:, :].astype(BF16)
        vt_ref[0, t, HEAD_DIM:HEAD_DIM + SUBLANES, :] = jnp.where(row8 == 0, 1.0, 0.0).astype(BF16)

    o_qi = d_attn + LANES
    for j in range(d_qidx // LANES):
        store_cols(qit_ref, rope(proj[:, o_qi + j * LANES:o_qi + (j + 1) * LANES]), j)

    o_ki = o_qi + d_qidx
    kic = proj[:, o_ki:o_ki + LANES]
    kisq = jnp.sum(jnp.where(is_k, kic * kic, 0.0), axis=-1, keepdims=True)
    kin = kic * lax.rsqrt(kisq * (1.0 / IDX_DIM) + EPS) * gkidx_ref[...]
    ki_ref[0] = jnp.where(is_k, rope(kin), 0.0).astype(BF16)
    for t in range(tm // Q_BLK):
        wt_ref[0, t] = kic[t * Q_BLK:(t + 1) * Q_BLK, :].T[IDX_DIM:IDX_DIM + N_IDX_HEADS, :] * (
            N_IDX_HEADS ** -0.5 * IDX_DIM ** -0.5)

    o_u = o_ki + LANES
    u = proj[:, o_u:o_u + LANES * len(POOL_WINDOWS)]

    @pl.when(s_tile == 0)
    def _():
        ubuf_ref[0:MAX_WIN, :] = jnp.zeros((MAX_WIN, u.shape[1]), F32)

    @pl.when(s_tile != 0)
    def _():
        ubuf_ref[0:MAX_WIN, :] = ubuf_ref[tm:tm + MAX_WIN, :]

    ubuf_ref[MAX_WIN:MAX_WIN + tm, :] = u
    t_idx = s_tile * tm + lax.broadcasted_iota(jnp.int32, (tm, 1), 0)
    for g, win in enumerate(POOL_WINDOWS):
        sl = slice(g * LANES, (g + 1) * LANES)
        wsum = u[:, sl]
        for j in range(1, win):
            wsum = wsum + ubuf_ref[MAX_WIN - j:MAX_WIN - j + tm, sl]
        cnt = jnp.minimum(t_idx + 1, win).astype(F32)
        pooled = wsum / cnt - u[:, sl]
        mixed = jnp.dot(pooled.astype(BF16), wpool_ref[g], preferred_element_type=F32)
        pool_ref[0, :, sl] = (mixed * pscale_ref[:, sl]).astype(BF16)


def _inproj(pos3, x, scale1, shift1, g_mix, w_in_p, segsum, gq_t, gk_e, gkidx_e, invf, w_pool, pscale):
    b, s, d = x.shape
    tm = TM_PROJ
    d_attn = N_HEADS * HEAD_DIM
    d_qidx = N_IDX_HEADS * IDX_DIM
    d_pool = LANES * len(POOL_WINDOWS)
    assert tm % Q_BLK == 0 and tm % K_BLK == 0 and HEAD_DIM == IDX_DIM and 2 * HEAD_DIM == LANES
    tok = lambda w: pl.BlockSpec((1, tm, w), lambda bi, si: (bi, si, 0))
    blk = lambda n, r, c: pl.BlockSpec((1, tm // n, r, c), lambda bi, si: (bi, si, 0, 0))
    per_b = pl.BlockSpec((1, 1, d), lambda bi, si: (bi, 0, 0))
    full = lambda a: pl.BlockSpec(a.shape, lambda bi, si: (0,) * a.ndim)
    nqb, nkb = s // Q_BLK, s // K_BLK
    return pl.pallas_call(
        _inproj_kernel,
        out_shape=(jax.ShapeDtypeStruct((b, nqb, HEAD_DIM, N_HEADS * Q_BLK), BF16),
                   jax.ShapeDtypeStruct((b, s, LANES), BF16),
                   jax.ShapeDtypeStruct((b, nqb, IDX_DIM, N_IDX_HEADS * Q_BLK), BF16),
                   jax.ShapeDtypeStruct((b, s, LANES), BF16),
                   jax.ShapeDtypeStruct((b, nqb, N_IDX_HEADS, Q_BLK), F32),
                   jax.ShapeDtypeStruct((b, nkb, HEAD_DIM + SUBLANES, K_BLK), BF16),
                   jax.ShapeDtypeStruct((b, s, d_pool), BF16)),
        grid=(b, s // tm),
        in_specs=[tok(1), tok(d), per_b, per_b, full(g_mix), full(w_in_p), full(segsum),
                  full(gq_t), full(gk_e), full(gkidx_e), full(invf), full(w_pool), full(pscale)],
        out_specs=(blk(Q_BLK, HEAD_DIM, N_HEADS * Q_BLK), tok(LANES),
                   blk(Q_BLK, IDX_DIM, N_IDX_HEADS * Q_BLK), tok(LANES),
                   blk(Q_BLK, N_IDX_HEADS, Q_BLK), blk(K_BLK, HEAD_DIM + SUBLANES, K_BLK), tok(d_pool)),
        scratch_shapes=[pltpu.VMEM((tm + 2 * MAX_WIN, d_pool), F32)],
        compiler_params=_cparams(("arbitrary", "arbitrary")),
        name="inproj",
    )(pos3, x, scale1, shift1, g_mix, w_in_p, segsum, gq_t, gk_e, gkidx_e, invf, w_pool, pscale)


def _dsa_kernel(qt_ref, qit_ref, w_ref, kv_ref, ki_ref, vt_ref, o_ref,
                sc_ref, qe_ref, qie_ref, m_ref, mx_ref, st_ref, acc_ref, lg_ref, p_ref):
    topk = float(min(TOPK_MAX, (sc_ref.shape[0] * CNT_BLK) // 4))
    qb = pl.program_id(1)
    n_cols = qt_ref.shape[3]
    n_chunks = n_cols // COL_BLK
    sub = CNT_BLK // K_BLK
    nch = ((qb + 1) * Q_BLK + CNT_BLK - 1) // CNT_BLK
    kgrp = K_BLK // SUBLANES
    sub_rows = [slice(j * K_BLK, (j + 1) * K_BLK) for j in range(sub)]

    zeros_half = jnp.zeros((LANES - HEAD_DIM, n_cols), BF16)
    qe_ref[0:HEAD_DIM, :] = qt_ref[0, 0]
    qe_ref[HEAD_DIM:LANES, :] = zeros_half
    qie_ref[0:IDX_DIM, :] = qit_ref[0, 0]
    qie_ref[IDX_DIM:LANES, :] = zeros_half

    q_pos = qb * Q_BLK + lax.broadcasted_iota(jnp.int32, (K_BLK, Q_BLK), 1)
    key_off = lax.broadcasted_iota(jnp.int32, (K_BLK, Q_BLK), 0)

    def score_body(ch, carry):
        rmax, rmin = carry
        for j in range(sub):
            ki_blk = ki_ref[0, ch, sub_rows[j], :]
            score = None
            for cc in range(n_chunks):
                cs = slice(cc * COL_BLK, (cc + 1) * COL_BLK)
                s_h = jnp.dot(ki_blk, qie_ref[:, cs], preferred_element_type=F32)
                s_h = jnp.maximum(s_h, 0.0)
                for hh in range(COL_BLK // Q_BLK):
                    head = cc * (COL_BLK // Q_BLK) + hh
                    part = s_h[:, hh * Q_BLK:(hh + 1) * Q_BLK] * w_ref[0, 0, head:head + 1, :]
                    score = part if score is None else score + part
            causal = (ch * CNT_BLK + j * K_BLK + key_off) <= q_pos
            masked = jnp.where(causal, score, -jnp.inf)
            sc_ref[ch, sub_rows[j], :] = masked
            hi_part = masked.reshape(kgrp, SUBLANES, Q_BLK).max(axis=0)
            lo_part = jnp.where(causal, score, jnp.inf).reshape(kgrp, SUBLANES, Q_BLK).min(axis=0)
            rmax, rmin = jnp.maximum(rmax, hi_part), jnp.minimum(rmin, lo_part)
        return rmax, rmin

    rmax8, rmin8 = lax.fori_loop(
        0, nch, score_body,
        (jnp.full((SUBLANES, Q_BLK), -jnp.inf, F32), jnp.full((SUBLANES, Q_BLK), jnp.inf, F32)))
    rowmax = jnp.max(rmax8, axis=0, keepdims=True)
    rowmin = jnp.min(rmin8, axis=0, keepdims=True)

    n_causal = (qb * Q_BLK + 1 + lax.broadcasted_iota(jnp.int32, (1, Q_BLK), 1)).astype(F32)
    kt = jnp.minimum(n_causal, topk)

    cgrp = CNT_BLK // CNT_ROWS

    def count_ge(t):
        def body(ch, acc):
            for r in range(cgrp):
                rows = sc_ref[ch, r * CNT_ROWS:(r + 1) * CNT_ROWS, :]
                acc = acc + jnp.where(rows >= t, 1.0, 0.0)
            return acc
        acc = lax.fori_loop(0, nch, body, jnp.zeros((CNT_ROWS, Q_BLK), F32))
        return jnp.sum(acc, axis=0, keepdims=True)

    def bisect_pass(state):
        lo, hi, top, c_lo, c_hi, thr, done = state
        cap = jnp.minimum(hi, top)
        mid = lo + 0.5 * (cap - lo)
        mid = jnp.where(mid <= lo, cap, mid)
        c = count_ge(mid)
        hit = jnp.logical_and(done == 0.0, c == kt)
        thr = jnp.where(hit, mid, thr)
        done = jnp.where(hit, 1.0, done)
        active = done == 0.0
        up = jnp.logical_and(active, c >= kt)
        down = jnp.logical_and(active, c < kt)
        return (jnp.where(up, mid, lo), jnp.where(down, mid, hi), jnp.where(down, jnp.inf, top),
                jnp.where(up, c, c_lo), jnp.where(down, c, c_hi), thr, done)

    def snap_pass(state):
        lo, hi, top, c_lo, c_hi, thr, done = state

        def body(ch, carry):
            a8, b8 = carry
            for r in range(cgrp):
                s = sc_ref[ch, r * CNT_ROWS:(r + 1) * CNT_ROWS, :]
                a8 = jnp.minimum(a8, jnp.where(s >= lo, s, jnp.inf))
                b8 = jnp.maximum(b8, jnp.where(s < hi, s, -jnp.inf))
            return a8, b8

        a8, b8 = lax.fori_loop(
            0, nch, body,
            (jnp.full((CNT_ROWS, Q_BLK), jnp.inf, F32), jnp.full((CNT_ROWS, Q_BLK), -jnp.inf, F32)))
        a = jnp.min(a8, axis=0, keepdims=True)
        b = jnp.max(b8, axis=0, keepdims=True)
        active = done == 0.0
        hit = jnp.logical_and(active, a == b)
        thr = jnp.where(hit, a, thr)
        done = jnp.where(hit, 2.0, done)
        return jnp.where(active, a, lo), hi, jnp.where(active, b, top), c_lo, c_hi, thr, done

    few = n_causal <= topk
    state0 = (rowmin, jnp.full((1, Q_BLK), jnp.inf, F32), rowmax, n_causal,
              jnp.zeros((1, Q_BLK), F32), jnp.where(few, F32_LOWEST, 0.0), jnp.where(few, 1.0, 0.0))

    def outer_cond(carry):
        return carry[1] > 0.0

    def outer_body(carry):
        state, _ = carry
        state = lax.fori_loop(0, SEARCH_PERIOD, lambda i, st: bisect_pass(st), state)
        state = snap_pass(state)
        pending = jnp.max(jnp.where(state[6] == 0.0, 1.0, 0.0))
        return state, pending

    state1 = lax.fori_loop(0, SEARCH_FIRST, lambda i, st: bisect_pass(st), state0)
    state1 = snap_pass(state1)
    pending1 = jnp.max(jnp.where(state1[6] == 0.0, 1.0, 0.0))
    (lo, hi, _, c_lo, c_hi, thr, done), _ = lax.while_loop(outer_cond, outer_body, (state1, pending1))

    excess = jnp.where(done == 2.0, c_lo - kt, 0.0)
    need = kt - c_hi

    @pl.when(jnp.max(excess) > 0.0)
    def _():
        tri = (lax.broadcasted_iota(jnp.int32, (K_BLK, K_BLK), 0)
               >= lax.broadcasted_iota(jnp.int32, (K_BLK, K_BLK), 1)).astype(BF16)
        has_excess = excess > 0.0

        def drop_body(ch, run):
            for j in range(sub):
                s = sc_ref[ch, sub_rows[j], :]
                tied = jnp.logical_and(s == thr, has_excess)
                prefix = jnp.dot(tri, jnp.where(tied, 1.0, 0.0).astype(BF16), preferred_element_type=F32)
                drop = jnp.logical_and(tied, run + prefix > need)
                sc_ref[ch, sub_rows[j], :] = jnp.where(drop, -jnp.inf, s)
                run = run + jnp.max(prefix, axis=0, keepdims=True)
            return run

        lax.fori_loop(0, nch, drop_body, jnp.zeros((1, Q_BLK), F32))

    m_ref[...] = jnp.full(m_ref.shape, M_INIT, F32)
    acc_ref[...] = jnp.zeros(acc_ref.shape, F32)
    row_m = lambda j: slice(j, j + 1)
    row_a = lambda j: slice(sub + j, sub + j + 1)

    def logits_stage(ch, j):
        kv_blk = kv_ref[0, ch, sub_rows[j], :]
        bias = jnp.where(sc_ref[ch, sub_rows[j], :] >= thr, 0.0, MASKED)
        for cc in range(n_chunks):
            logits = jnp.dot(kv_blk, qe_ref[:, cc * COL_BLK:(cc + 1) * COL_BLK],
                             preferred_element_type=F32)
            for hh in range(COL_BLK // Q_BLK):
                cs = slice(cc * COL_BLK + hh * Q_BLK, cc * COL_BLK + (hh + 1) * Q_BLK)
                lg = logits[:, hh * Q_BLK:(hh + 1) * Q_BLK] + bias
                lg_ref[j, :, cs] = lg
                mx_ref[:, cs] = lg.reshape(kgrp, SUBLANES, Q_BLK).max(axis=0)
        m_old = m_ref[...]
        m_new = jnp.maximum(m_old, jnp.max(mx_ref[...], axis=0, keepdims=True))
        st_ref[row_m(j), :] = m_new
        st_ref[row_a(j), :] = jnp.exp2(m_old - m_new)
        m_ref[...] = m_new

    def probs_stage(j):
        p_ref[j] = jnp.exp2(lg_ref[j] - st_ref[row_m(j), :]).astype(BF16)

    def value_stage(kb, j):
        acc_ref[...] = acc_ref[...] * st_ref[row_a(j), :] + jnp.dot(
            vt_ref[0, kb], p_ref[j], preferred_element_type=F32)

    p_ref[sub - 1] = jnp.zeros(p_ref.shape[1:], BF16)
    st_ref[row_a(sub - 1), :] = jnp.ones((1, n_cols), F32)
    logits_stage(0, 0)

    def attn_body(ch, _):
        value_stage(jnp.maximum(ch * sub - 1, 0), 1)
        probs_stage(0)
        logits_stage(ch, 1)
        value_stage(ch * sub, 0)
        probs_stage(1)
        logits_stage(jnp.minimum(ch + 1, nch - 1), 0)
        return 0

    lax.fori_loop(0, nch, attn_body, 0)
    value_stage(nch * sub - 1, 1)
    dh = qt_ref.shape[2]
    inv_l = 1.0 / acc_ref[dh:dh + 1, :]
    for j in range(n_cols // Q_BLK // 2):
        pair = [acc_ref[0:dh, (2 * j + hh) * Q_BLK:(2 * j + hh + 1) * Q_BLK]
                * inv_l[:, (2 * j + hh) * Q_BLK:(2 * j + hh + 1) * Q_BLK] for hh in range(2)]
        o_ref[0, :, j * 2 * dh:(j + 1) * 2 * dh] = jnp.concatenate(pair, axis=0).T.astype(o_ref.dtype)


def _dsa(qt, qit, w_t, kv4, ki4, vt4):
    b, nqb, dh, n_cols = qt.shape
    n_steps = kv4.shape[1]
    assert kv4.shape[2] == CNT_BLK and CNT_BLK == 2 * K_BLK and n_cols % COL_BLK == 0
    assert vt4.shape[1] * K_BLK == n_steps * CNT_BLK and vt4.shape[2] == dh + SUBLANES
    per_q = lambda a: pl.BlockSpec((1, 1) + a.shape[2:], lambda bi, qi: (bi, qi, 0, 0))
    per_b = lambda a: pl.BlockSpec((1,) + a.shape[1:], lambda bi, qi: (bi, 0, 0, 0))
    return pl.pallas_call(
        _dsa_kernel,
        out_shape=jax.ShapeDtypeStruct((b, nqb * Q_BLK, (n_cols // Q_BLK) * dh), BF16),
        grid=(b, nqb),
        in_specs=[per_q(qt), per_q(qit), per_q(w_t), per_b(kv4), per_b(ki4), per_b(vt4)],
        out_specs=pl.BlockSpec((1, Q_BLK, (n_cols // Q_BLK) * dh), lambda bi, qi: (bi, qi, 0)),
        scratch_shapes=[pltpu.VMEM((n_steps, CNT_BLK, Q_BLK), F32),
                        pltpu.VMEM((LANES, n_cols), BF16),
                        pltpu.VMEM((LANES, n_cols), BF16),
                        pltpu.VMEM((1, n_cols), F32),
                        pltpu.VMEM((SUBLANES, n_cols), F32),
                        pltpu.VMEM((SUBLANES, n_cols), F32),
                        pltpu.VMEM((dh + SUBLANES, n_cols), F32),
                        pltpu.VMEM((CNT_BLK // K_BLK, K_BLK, n_cols), F32),
                        pltpu.VMEM((CNT_BLK // K_BLK, K_BLK, n_cols), BF16)],
        compiler_params=_cparams(("arbitrary", "arbitrary")),
        name="dsa",
    )(qt, qit, w_t, kv4, ki4, vt4)


def _outproj_kernel(x_ref, attn_ref, pool_ref, woa_ref, wop_ref, gate1_ref, gffn_ref,
                    scale2_ref, shift2_ref, wr_ref, br_ref, x1_ref, h2_ref, gates_ref):
    tm = x_ref.shape[1]
    mix = (jnp.dot(attn_ref[0], woa_ref[...], preferred_element_type=F32)
           + jnp.dot(pool_ref[0], wop_ref[...], preferred_element_type=F32))
    x1 = x_ref[0] + gate1_ref[0] * mix
    x1_ref[0] = x1
    ms = jnp.mean(x1 * x1, axis=-1, keepdims=True)
    h2 = (x1 * lax.rsqrt(ms + EPS) * gffn_ref[...]) * (1.0 + scale2_ref[0]) + shift2_ref[0]
    h2_hi = h2.astype(BF16)
    h2_ref[0] = h2_hi

    h2_lo = (h2 - h2_hi.astype(F32)).astype(BF16)
    wr = wr_ref[...]
    wr_hi = wr.astype(BF16)
    wr_lo = (wr - wr_hi.astype(F32)).astype(BF16)
    logits = (jnp.dot(h2_hi, wr_hi, preferred_element_type=F32)
              + jnp.dot(h2_lo, wr_hi, preferred_element_type=F32)
              + jnp.dot(h2_hi, wr_lo, preferred_element_type=F32)) + br_ref[...]

    lane = lax.broadcasted_iota(jnp.int32, (tm, LANES), 1)
    big = jnp.int32(LANES)
    is_g = jnp.logical_and(lane >= N_EXPERTS, lane < N_EXPERTS + N_GROUPS)
    glog = jnp.where(is_g, logits, -jnp.inf)
    gmax = jnp.max(glog, axis=-1, keepdims=True)
    gsum = jnp.sum(jnp.exp(glog - gmax), axis=-1, keepdims=True)
    p_g = 1.0 / gsum
    g_sel = jnp.min(jnp.where(glog == gmax, lane, big), axis=-1, keepdims=True) - N_EXPERTS
    in_grp = jnp.logical_and(lane < N_EXPERTS, jnp.right_shift(lane, 3) == g_sel)
    elog = jnp.where(in_grp, logits, -jnp.inf)
    emax = jnp.max(elog, axis=-1, keepdims=True)
    eexp = jnp.exp(elog - emax)
    esum = jnp.sum(eexp, axis=-1, keepdims=True)
    p_e = jnp.where(in_grp, eexp / esum, -1.0)
    p1 = jnp.max(p_e, axis=-1, keepdims=True)
    i1 = jnp.min(jnp.where(p_e == p1, lane, big), axis=-1, keepdims=True)
    p_e2 = jnp.where(lane == i1, -1.0, p_e)
    p2 = jnp.max(p_e2, axis=-1, keepdims=True)
    i2 = jnp.min(jnp.where(p_e2 == p2, lane, big), axis=-1, keepdims=True)
    tot = p1 + p2
    gates_ref[0] = (jnp.where(lane == i1, p_g * (p1 / tot), 0.0)
                    + jnp.where(lane == i2, p_g * (p2 / tot), 0.0)
                    + jnp.where(lane == N_EXPERTS, g_sel.astype(F32), 0.0))


def _outproj(x, attn, pool, wo_a, wo_p, gate1, g_ffn, scale2, shift2, w_r, b_r):
    b, s, d = x.shape
    tm = TM_PROJ
    tok = lambda w: pl.BlockSpec((1, tm, w), lambda bi, si: (bi, si, 0))
    per_b = pl.BlockSpec((1, 1, d), lambda bi, si: (bi, 0, 0))
    full = lambda a: pl.BlockSpec(a.shape, lambda bi, si: (0,) * a.ndim)
    return pl.pallas_call(
        _outproj_kernel,
        out_shape=(jax.ShapeDtypeStruct((b, s, d), F32),
                   jax.ShapeDtypeStruct((b, s, d), BF16),
                   jax.ShapeDtypeStruct((b, s, LANES), F32)),
        grid=(b, s // tm),
        in_specs=[tok(d), tok(attn.shape[2]), tok(pool.shape[2]), full(wo_a), full(wo_p), per_b,
                  full(g_ffn), per_b, per_b, full(w_r), full(b_r)],
        out_specs=(tok(d), tok(d), tok(LANES)),
        compiler_params=_cparams(("arbitrary", "arbitrary")),
        name="outproj",
    )(x, attn, pool, wo_a, wo_p, gate1, g_ffn, scale2, shift2, w_r, b_r)


def _moe_kernel(x1_ref, h2_ref, gates_ref, gate2_ref, wgu_ref, wd_ref, o_ref,
                xe_ref, rank_ref, rank_t_ref):
    g = pl.program_id(2)
    tm, d = h2_ref.shape[1], h2_ref.shape[2]
    n_e, d_exp = wd_ref.shape[0], wd_ref.shape[1]
    lane = lax.broadcasted_iota(jnp.int32, (tm, LANES), 1)
    gf = g.astype(F32)

    @pl.when(g == 0)
    def _():
        tri = (lax.broadcasted_iota(jnp.int32, (RANK_BLK, RANK_BLK), 0)
               >= lax.broadcasted_iota(jnp.int32, (RANK_BLK, RANK_BLK), 1)).astype(BF16)
        lane_b = lax.broadcasted_iota(jnp.int32, (RANK_BLK, LANES), 1)
        run = jnp.zeros((1, LANES), F32)
        for sb in range(tm // RANK_BLK):
            rows = slice(sb * RANK_BLK, (sb + 1) * RANK_BLK)
            gts_b = gates_ref[0, rows, :]
            grp = jnp.sum(jnp.where(lane_b == N_EXPERTS, gts_b, 0.0), axis=-1, keepdims=True)
            member = jnp.where(jnp.logical_and(lane_b < N_GROUPS, lane_b.astype(F32) == grp), 1.0, 0.0)
            pre = jnp.dot(tri, member.astype(BF16), preferred_element_type=F32) + run
            rank_ref[rows, :] = jnp.where(lane_b == N_GROUPS, grp, pre)
            run = jnp.max(pre, axis=0, keepdims=True)
        rank_t_ref[...] = rank_ref[...].T
        gts = gates_ref[0]
        g_hi = gts.astype(BF16)
        xe_ref[:, :d] = h2_ref[0]
        xe_ref[:, d:d + LANES] = g_hi
        xe_ref[:, d + LANES:d + 2 * LANES] = (gts - g_hi.astype(F32)).astype(BF16)
        o_ref[0] = jnp.zeros((tm, d), F32)

    rank_row = rank_t_ref[pl.ds(g, 1), :]
    pos_row = jnp.where(rank_t_ref[N_GROUPS:N_GROUPS + 1, :] == gf, rank_row - 1.0, -1.0)
    rk = rank_ref[...]
    rank_col = jnp.sum(jnp.where(lane == g, rk, 0.0), axis=-1, keepdims=True)
    grp_col = jnp.sum(jnp.where(lane == N_GROUPS, rk, 0.0), axis=-1, keepdims=True)
    pos_col = jnp.where(grp_col == gf, rank_col - 1.0, -1.0)
    n_rows = jnp.max(rank_row).astype(jnp.int32)
    row_id = lax.broadcasted_iota(jnp.int32, (MOE_CH, tm), 0).astype(F32)
    col_id = lax.broadcasted_iota(jnp.int32, (tm, MOE_CH), 1).astype(F32)
    lane_c = lax.broadcasted_iota(jnp.int32, (MOE_CH, LANES), 1)

    def chunk_body(c, _):
        r0 = (c * MOE_CH).astype(F32)
        gather = jnp.where(pos_row - r0 == row_id, 1.0, 0.0).astype(BF16)
        xg = jnp.dot(gather, xe_ref[...], preferred_element_type=F32)
        xb = xg[:, :d].astype(BF16)
        gates_c = xg[:, d:d + LANES] + xg[:, d + LANES:d + 2 * LANES]
        ya = None
        for e in range(n_e):
            gu = jnp.dot(xb, wgu_ref[e], preferred_element_type=F32)
            gt = gu[:, :d_exp]
            a = (gt * jax.nn.sigmoid(gt)) * gu[:, d_exp:]
            gate_e = jnp.sum(jnp.where(lane_c == g * n_e + e, gates_c, 0.0), axis=-1, keepdims=True)
            y = jnp.dot((a * gate_e).astype(BF16), wd_ref[e], preferred_element_type=F32)
            ya = y if ya is None else ya + y
        scatter = jnp.where(pos_col - r0 == col_id, 1.0, 0.0).astype(BF16)
        o_ref[0] += jnp.dot(scatter, ya.astype(BF16), preferred_element_type=F32)
        return 0

    lax.fori_loop(0, (n_rows + MOE_CH - 1) // MOE_CH, chunk_body, 0)

    @pl.when(g == pl.num_programs(2) - 1)
    def _():
        o_ref[0] = x1_ref[0] + gate2_ref[0] * o_ref[0]


def _moe(x1, h2, gates, gate2, w_gu, w_d):
    b, s, d = x1.shape
    tm = TM_MOE
    n_e = EXPERTS_PER_GROUP
    assert w_gu.shape[0] == N_GROUPS * n_e and tm % RANK_BLK == 0
    tok = lambda w: pl.BlockSpec((1, tm, w), lambda bi, si, g: (bi, si, 0))
    return pl.pallas_call(
        _moe_kernel,
        out_shape=jax.ShapeDtypeStruct((b, s, d), F32),
        grid=(b, s // tm, N_GROUPS),
        in_specs=[tok(d), tok(d), tok(LANES),
                  pl.BlockSpec((1, 1, d), lambda bi, si, g: (bi, 0, 0)),
                  pl.BlockSpec((n_e,) + w_gu.shape[1:], lambda bi, si, g: (g, 0, 0)),
                  pl.BlockSpec((n_e,) + w_d.shape[1:], lambda bi, si, g: (g, 0, 0))],
        out_specs=tok(d),
        scratch_shapes=[pltpu.VMEM((tm, d + 2 * LANES), BF16),
                        pltpu.VMEM((tm, LANES), F32),
                        pltpu.VMEM((LANES, tm), F32)],
        compiler_params=pltpu.CompilerParams(
            dimension_semantics=("arbitrary", "arbitrary", "arbitrary"),
            vmem_limit_bytes=VMEM_LIMIT_MOE_BYTES),
        name="moe",
    )(x1, h2, gates, gate2, w_gu, w_d)


def _layer(x, mod, pos3, g_mix, g_ffn, w_in, g_q, g_k, g_kidx, w_pool, pool_scale, w_out,
           w_rg, b_rg, w_re, b_re, w_gate, w_up, w_down):
    b, s, d = x.shape
    d_attn = N_HEADS * HEAD_DIM
    nqb = s // Q_BLK
    nkb = s // K_BLK
    shift1, scale1, gate1, shift2, scale2, gate2 = [m[:, None, :] for m in jnp.split(mod, 6, axis=-1)]

    n_front = d_attn + 2 * HEAD_DIM + N_IDX_HEADS * IDX_DIM + IDX_DIM + N_IDX_HEADS
    pad = (-n_front) % LANES
    w_in_p = jnp.concatenate([w_in[:, :n_front], jnp.zeros((d, pad), w_in.dtype), w_in[:, n_front:]],
                             axis=1).astype(BF16)
    seg_id = jnp.arange(d_attn) // HEAD_DIM
    segsum = (seg_id[:, None] == seg_id[None, :]).astype(BF16)
    ones_half = jnp.ones((LANES - HEAD_DIM,), F32)
    gq_t = jnp.tile(g_q, N_HEADS)[None, :]
    gk_e = jnp.concatenate([g_k, ones_half])[None, :]
    gkidx_e = jnp.concatenate([g_kidx, ones_half])[None, :]
    half = HEAD_DIM // 2
    inv_freq = ROPE_THETA ** (-jnp.arange(0, HEAD_DIM, 2, dtype=F32) / HEAD_DIM)
    invf = jnp.tile(inv_freq, LANES // half)[None, :]

    qt, kv, qit, ki, w_t, vt4, pool = _inproj(pos3, x, scale1, shift1, g_mix[None, :], w_in_p, segsum,
                                              gq_t, gk_e, gkidx_e, invf, w_pool.astype(BF16),
                                              pool_scale[None, :])
    kv4 = kv.reshape(b, s // CNT_BLK, CNT_BLK, LANES)
    ki4 = ki.reshape(b, s // CNT_BLK, CNT_BLK, LANES)
    attn = _dsa(qt, qit, w_t, kv4, ki4, vt4)

    w_out_b = w_out.astype(BF16)
    w_r = jnp.concatenate([w_re, w_rg, jnp.zeros((d, LANES - N_EXPERTS - N_GROUPS), F32)], axis=1)
    b_r = jnp.concatenate([b_re, b_rg, jnp.zeros((LANES - N_EXPERTS - N_GROUPS,), F32)])[None, :]
    x1, h2, gates = _outproj(x, attn, pool, w_out_b[:d_attn], w_out_b[d_attn:], gate1,
                             g_ffn[None, :], scale2, shift2, w_r, b_r)

    w_gu = jnp.concatenate([w_gate, w_up], axis=-1).astype(BF16)
    return _moe(x1, h2, gates, gate2, w_gu, w_down.astype(BF16))


def kernel(x, c, positions, w_ada, b_ada, g_norm_mix, g_norm_ffn, w_in, g_q, g_k, g_kidx, w_pool,
           pool_scale, w_out, w_router_group, b_router_group, w_router_expert, b_router_expert,
           w_gate, w_up, w_down):
    b, s, d = x.shape
    depth = w_ada.shape[0]
    assert s % TM_MOE == 0 and s % K_BLK == 0 and d % LANES == 0
    pos3 = positions[:, :, None]
    c_pad = jnp.concatenate([c, jnp.zeros((-b % SUBLANES, d), c.dtype)], axis=0)
    for l in range(depth):
        mod = _adaln(c_pad, w_ada[l], b_ada[l][None, :])[:b]
        x = _layer(x, mod, pos3, g_norm_mix[l], g_norm_ffn[l], w_in[l], g_q[l], g_k[l], g_kidx[l],
                   w_pool[l], pool_scale[l], w_out[l], w_router_group[l], b_router_group[l],
                   w_router_expert[l], b_router_expert[l], w_gate[l], w_up[l], w_down[l])
    return x
```

```python
import functools

import jax
import jax.numpy as jnp
from jax import lax
from jax.experimental import pallas as pl
from jax.experimental.pallas import tpu as pltpu

N_HEADS = 8
HEAD_DIM = 64
N_IDX_HEADS = 8
IDX_DIM = 64
TOPK_MAX = 256
ROPE_THETA = 10000.0
POOL_WINDOWS = (2, 4, 8, 16)
N_GROUPS = 4
EXPERTS_PER_GROUP = 8
N_EXPERTS = N_GROUPS * EXPERTS_PER_GROUP
EPS = 1e-6

LANES = 128
SUBLANES = 8
VMEM_LIMIT_BYTES = 56 * 1024 * 1024
VMEM_LIMIT_MOE_BYTES = 60 * 1024 * 1024

Q_BLK = 256
K_BLK = 256
COL_BLK = 256
CNT_BLK = 512
CNT_ROWS = 32
SEARCH_FIRST = 15
SEARCH_PERIOD = 2
TM_PROJ = 512
TM_MOE = 1024
MOE_CH = 320
RANK_BLK = 256
MAX_WIN = max(POOL_WINDOWS)
M_INIT = -1e29
MASKED = -1e30
F32_LOWEST = -3.0e38
LOG2_E = 1.4426950408889634

BF16 = jnp.bfloat16
F32 = jnp.float32


def _cparams(sem):
    return pltpu.CompilerParams(dimension_semantics=sem, vmem_limit_bytes=VMEM_LIMIT_BYTES)


def _adaln_kernel(c_ref, w_ref, b_ref, o_ref):
    c = c_ref[...]
    c_act = c * jax.nn.sigmoid(c)
    o_ref[...] = jnp.dot(c_act, w_ref[...], preferred_element_type=F32) + b_ref[...]


def _adaln(c_pad, w_ada, b_ada):
    rows, d = c_pad.shape
    n = w_ada.shape[1]
    tn = n // 6
    return pl.pallas_call(
        _adaln_kernel,
        out_shape=jax.ShapeDtypeStruct((rows, n), F32),
        grid=(n // tn,),
        in_specs=[pl.BlockSpec((rows, d), lambda j: (0, 0)),
                  pl.BlockSpec((d, tn), lambda j: (0, j)),
                  pl.BlockSpec((1, tn), lambda j: (0, j))],
        out_specs=pl.BlockSpec((rows, tn), lambda j: (0, j)),
        compiler_params=_cparams(("arbitrary",)),
        name="adaln",
    )(c_pad, w_ada, b_ada)


def _rope_chunk(y, cos, sin_signed, first_half):
    from_hi = pltpu.roll(y, LANES - HEAD_DIM // 2, 1)
    from_lo = pltpu.roll(y, HEAD_DIM // 2, 1)
    return y * cos + jnp.where(first_half, from_hi, from_lo) * sin_signed


def _inproj_kernel(pos_ref, x_ref, scale_ref, shift_ref, gmix_ref, win_ref, segsum_ref,
                   gq_ref, gk_ref, gkidx_ref, invf_ref, wpool_ref, pscale_ref,
                   qt_ref, kv_ref, qit_ref, ki_ref, wt_ref, vt_ref, pool_ref, ubuf_ref):
    tm = x_ref.shape[1]
    d_attn = N_HEADS * HEAD_DIM
    d_qidx = N_IDX_HEADS * IDX_DIM
    s_tile = pl.program_id(1)

    def store_cols(dst_ref, chunk, j):
        for t in range(tm // Q_BLK):
            ct = chunk[t * Q_BLK:(t + 1) * Q_BLK, :].T
            for hh in range(2):
                col = (2 * j + hh) * Q_BLK
                dst_ref[0, t, :, col:col + Q_BLK] = ct[hh * HEAD_DIM:(hh + 1) * HEAD_DIM, :].astype(dst_ref.dtype)

    x = x_ref[0]
    ms = jnp.mean(x * x, axis=-1, keepdims=True)
    h = (x * lax.rsqrt(ms + EPS) * gmix_ref[...]) * (1.0 + scale_ref[0]) + shift_ref[0]
    proj = jnp.dot(h.astype(BF16), win_ref[...], preferred_element_type=F32)

    lane = lax.broadcasted_iota(jnp.int32, (tm, LANES), 1)
    first_half = (lane & (HEAD_DIM - 1)) < (HEAD_DIM // 2)
    ang = pos_ref[0].astype(F32) * invf_ref[...]
    cos = jnp.cos(ang)
    sin = jnp.sin(ang)
    sin_signed = jnp.where(first_half, -sin, sin)
    rope = functools.partial(_rope_chunk, cos=cos, sin_signed=sin_signed, first_half=first_half)

    qf = proj[:, :d_attn]
    qsq = qf * qf
    qsq_hi = qsq.astype(BF16)
    qsq_lo = (qsq - qsq_hi.astype(F32)).astype(BF16)
    seg = segsum_ref[...]
    ssq = (jnp.dot(qsq_hi, seg, preferred_element_type=F32)
           + jnp.dot(qsq_lo, seg, preferred_element_type=F32))
    qn = qf * lax.rsqrt(ssq * (1.0 / HEAD_DIM) + EPS) * gq_ref[...]
    for j in range(d_attn // LANES):
        sl = slice(j * LANES, (j + 1) * LANES)
        store_cols(qt_ref, rope(qn[:, sl]) * (LOG2_E * HEAD_DIM ** -0.5), j)

    kvc = proj[:, d_attn:d_attn + LANES]
    is_k = lane < HEAD_DIM
    ksq = jnp.sum(jnp.where(is_k, kvc * kvc, 0.0), axis=-1, keepdims=True)
    kn = kvc * lax.rsqrt(ksq * (1.0 / HEAD_DIM) + EPS) * gk_ref[...]
    kv_ref[0] = jnp.where(is_k, rope(kn), kvc).astype(BF16)
    row8 = lax.broadcasted_iota(jnp.int32, (SUBLANES, K_BLK), 0)
    for t in range(tm // K_BLK):
        vt_ref[0, t, 0:HEAD_DIM, :] = kvc[t * K_BLK:(t + 1) * K_BLK, :].T[HEAD_DIM:, :].astype(BF16)
        vt_ref[0, t, HEAD_DIM:HEAD_DIM + SUBLANES, :] = jnp.where(row8 == 0, 1.0, 0.0).astype(BF16)

    o_qi = d_attn + LANES
    for j in range(d_qidx // LANES):
        store_cols(qit_ref, rope(proj[:, o_qi + j * LANES:o_qi + (j + 1) * LANES]), j)

    o_ki = o_qi + d_qidx
    kic = proj[:, o_ki:o_ki + LANES]
    kisq = jnp.sum(jnp.where(is_k, kic * kic, 0.0), axis=-1, keepdims=True)
    kin = kic * lax.rsqrt(kisq * (1.0 / IDX_DIM) + EPS) * gkidx_ref[...]
    ki_ref[0] = jnp.where(is_k, rope(kin), 0.0).astype(BF16)
    for t in range(tm // Q_BLK):
        wt_ref[0, t] = kic[t * Q_BLK:(t + 1) * Q_BLK, :].T[IDX_DIM:IDX_DIM + N_IDX_HEADS, :] * (
            N_IDX_HEADS ** -0.5 * IDX_DIM ** -0.5)

    o_u = o_ki + LANES
    u = proj[:, o_u:o_u + LANES * len(POOL_WINDOWS)]

    @pl.when(s_tile == 0)
    def _():
        ubuf_ref[0:MAX_WIN, :] = jnp.zeros((MAX_WIN, u.shape[1]), F32)

    @pl.when(s_tile != 0)
    def _():
        ubuf_ref[0:MAX_WIN, :] = ubuf_ref[tm:tm + MAX_WIN, :]

    ubuf_ref[MAX_WIN:MAX_WIN + tm, :] = u
    t_idx = s_tile * tm + lax.broadcasted_iota(jnp.int32, (tm, 1), 0)
    for g, win in enumerate(POOL_WINDOWS):
        sl = slice(g * LANES, (g + 1) * LANES)
        wsum = u[:, sl]
        for j in range(1, win):
            wsum = wsum + ubuf_ref[MAX_WIN - j:MAX_WIN - j + tm, sl]
        cnt = jnp.minimum(t_idx + 1, win).astype(F32)
        pooled = wsum / cnt - u[:, sl]
        mixed = jnp.dot(pooled.astype(BF16), wpool_ref[g], preferred_element_type=F32)
        pool_ref[0, :, sl] = (mixed * pscale_ref[:, sl]).astype(BF16)


def _inproj(pos3, x, scale1, shift1, g_mix, w_in_p, segsum, gq_t, gk_e, gkidx_e, invf, w_pool, pscale):
    b, s, d = x.shape
    tm = TM_PROJ
    d_attn = N_HEADS * HEAD_DIM
    d_qidx = N_IDX_HEADS * IDX_DIM
    d_pool = LANES * len(POOL_WINDOWS)
    assert tm % Q_BLK == 0 and tm % K_BLK == 0 and HEAD_DIM == IDX_DIM and 2 * HEAD_DIM == LANES
    tok = lambda w: pl.BlockSpec((1, tm, w), lambda bi, si: (bi, si, 0))
    blk = lambda n, r, c: pl.BlockSpec((1, tm // n, r, c), lambda bi, si: (bi, si, 0, 0))
    per_b = pl.BlockSpec((1, 1, d), lambda bi, si: (bi, 0, 0))
    full = lambda a: pl.BlockSpec(a.shape, lambda bi, si: (0,) * a.ndim)
    nqb, nkb = s // Q_BLK, s // K_BLK
    return pl.pallas_call(
        _inproj_kernel,
        out_shape=(jax.ShapeDtypeStruct((b, nqb, HEAD_DIM, N_HEADS * Q_BLK), BF16),
                   jax.ShapeDtypeStruct((b, s, LANES), BF16),
                   jax.ShapeDtypeStruct((b, nqb, IDX_DIM, N_IDX_HEADS * Q_BLK), BF16),
                   jax.ShapeDtypeStruct((b, s, LANES), BF16),
                   jax.ShapeDtypeStruct((b, nqb, N_IDX_HEADS, Q_BLK), F32),
                   jax.ShapeDtypeStruct((b, nkb, HEAD_DIM + SUBLANES, K_BLK), BF16),
                   jax.ShapeDtypeStruct((b, s, d_pool), BF16)),
        grid=(b, s // tm),
        in_specs=[tok(1), tok(d), per_b, per_b, full(g_mix), full(w_in_p), full(segsum),
                  full(gq_t), full(gk_e), full(gkidx_e), full(invf), full(w_pool), full(pscale)],
        out_specs=(blk(Q_BLK, HEAD_DIM, N_HEADS * Q_BLK), tok(LANES),
                   blk(Q_BLK, IDX_DIM, N_IDX_HEADS * Q_BLK), tok(LANES),
                   blk(Q_BLK, N_IDX_HEADS, Q_BLK), blk(K_BLK, HEAD_DIM + SUBLANES, K_BLK), tok(d_pool)),
        scratch_shapes=[pltpu.VMEM((tm + 2 * MAX_WIN, d_pool), F32)],
        compiler_params=_cparams(("arbitrary", "arbitrary")),
        name="inproj",
    )(pos3, x, scale1, shift1, g_mix, w_in_p, segsum, gq_t, gk_e, gkidx_e, invf, w_pool, pscale)


def _dsa_kernel(qt_ref, qit_ref, w_ref, kv_ref, ki_ref, vt_ref, o_ref,
                sc_ref, qe_ref, qie_ref, m_ref, mx_ref, st_ref, acc_ref, lg_ref, p_ref):
    topk = float(min(TOPK_MAX, (sc_ref.shape[0] * CNT_BLK) // 4))
    qb = pl.program_id(1)
    n_cols = qt_ref.shape[3]
    n_chunks = n_cols // COL_BLK
    sub = CNT_BLK // K_BLK
    nch = ((qb + 1) * Q_BLK + CNT_BLK - 1) // CNT_BLK
    kgrp = K_BLK // SUBLANES
    sub_rows = [slice(j * K_BLK, (j + 1) * K_BLK) for j in range(sub)]

    zeros_half = jnp.zeros((LANES - HEAD_DIM, n_cols), BF16)
    qe_ref[0:HEAD_DIM, :] = qt_ref[0, 0]
    qe_ref[HEAD_DIM:LANES, :] = zeros_half
    qie_ref[0:IDX_DIM, :] = qit_ref[0, 0]
    qie_ref[IDX_DIM:LANES, :] = zeros_half

    q_pos = qb * Q_BLK + lax.broadcasted_iota(jnp.int32, (K_BLK, Q_BLK), 1)
    key_off = lax.broadcasted_iota(jnp.int32, (K_BLK, Q_BLK), 0)

    def score_body(ch, carry):
        rmax, rmin = carry
        for j in range(sub):
            ki_blk = ki_ref[0, ch, sub_rows[j], :]
            score = None
            for cc in range(n_chunks):
                cs = slice(cc * COL_BLK, (cc + 1) * COL_BLK)
                s_h = jnp.dot(ki_blk, qie_ref[:, cs], preferred_element_type=F32)
                s_h = jnp.maximum(s_h, 0.0)
                for hh in range(COL_BLK // Q_BLK):
                    head = cc * (COL_BLK // Q_BLK) + hh
                    part = s_h[:, hh * Q_BLK:(hh + 1) * Q_BLK] * w_ref[0, 0, head:head + 1, :]
                    score = part if score is None else score + part
            causal = (ch * CNT_BLK + j * K_BLK + key_off) <= q_pos
            masked = jnp.where(causal, score, -jnp.inf)
            sc_ref[ch, sub_rows[j], :] = masked
            hi_part = masked.reshape(kgrp, SUBLANES, Q_BLK).max(axis=0)
            lo_part = jnp.where(causal, score, jnp.inf).reshape(kgrp, SUBLANES, Q_BLK).min(axis=0)
            rmax, rmin = jnp.maximum(rmax, hi_part), jnp.minimum(rmin, lo_part)
        return rmax, rmin

    rmax8, rmin8 = lax.fori_loop(
        0, nch, score_body,
        (jnp.full((SUBLANES, Q_BLK), -jnp.inf, F32), jnp.full((SUBLANES, Q_BLK), jnp.inf, F32)))
    rowmax = jnp.max(rmax8, axis=0, keepdims=True)
    rowmin = jnp.min(rmin8, axis=0, keepdims=True)

    n_causal = (qb * Q_BLK + 1 + lax.broadcasted_iota(jnp.int32, (1, Q_BLK), 1)).astype(F32)
    kt = jnp.minimum(n_causal, topk)

    cgrp = CNT_BLK // CNT_ROWS

    def count_ge(t):
        def body(ch, acc):
            for r in range(cgrp):
                rows = sc_ref[ch, r * CNT_ROWS:(r + 1) * CNT_ROWS, :]
                acc = acc + jnp.where(rows >= t, 1.0, 0.0)
            return acc
        acc = lax.fori_loop(0, nch, body, jnp.zeros((CNT_ROWS, Q_BLK), F32))
        return jnp.sum(acc, axis=0, keepdims=True)

    def bisect_pass(state):
        lo, hi, top, c_lo, c_hi, thr, done = state
        cap = jnp.minimum(hi, top)
        mid = lo + 0.5 * (cap - lo)
        mid = jnp.where(mid <= lo, cap, mid)
        c = count_ge(mid)
        hit = jnp.logical_and(done == 0.0, c == kt)
        thr = jnp.where(hit, mid, thr)
        done = jnp.where(hit, 1.0, done)
        active = done == 0.0
        up = jnp.logical_and(active, c >= kt)
        down = jnp.logical_and(active, c < kt)
        return (jnp.where(up, mid, lo), jnp.where(down, mid, hi), jnp.where(down, jnp.inf, top),
                jnp.where(up, c, c_lo), jnp.where(down, c, c_hi), thr, done)

    def snap_pass(state):
        lo, hi, top, c_lo, c_hi, thr, done = state

        def body(ch, carry):
            a8, b8 = carry
            for r in range(cgrp):
                s = sc_ref[ch, r * CNT_ROWS:(r + 1) * CNT_ROWS, :]
                a8 = jnp.minimum(a8, jnp.where(s >= lo, s, jnp.inf))
                b8 = jnp.maximum(b8, jnp.where(s < hi, s, -jnp.inf))
            return a8, b8

        a8, b8 = lax.fori_loop(
            0, nch, body,
            (jnp.full((CNT_ROWS, Q_BLK), jnp.inf, F32), jnp.full((CNT_ROWS, Q_BLK), -jnp.inf, F32)))
        a = jnp.min(a8, axis=0, keepdims=True)
        b = jnp.max(b8, axis=0, keepdims=True)
        active = done == 0.0
        hit = jnp.logical_and(active, a == b)
        thr = jnp.where(hit, a, thr)
        done = jnp.where(hit, 2.0, done)
        return jnp.where(active, a, lo), hi, jnp.where(active, b, top), c_lo, c_hi, thr, done

    few = n_causal <= topk
    state0 = (rowmin, jnp.full((1, Q_BLK), jnp.inf, F32), rowmax, n_causal,
              jnp.zeros((1, Q_BLK), F32), jnp.where(few, F32_LOWEST, 0.0), jnp.where(few, 1.0, 0.0))

    def outer_cond(carry):
        return carry[1] > 0.0

    def outer_body(carry):
        state, _ = carry
        state = lax.fori_loop(0, SEARCH_PERIOD, lambda i, st: bisect_pass(st), state)
        state = snap_pass(state)
        pending = jnp.max(jnp.where(state[6] == 0.0, 1.0, 0.0))
        return state, pending

    state1 = lax.fori_loop(0, SEARCH_FIRST, lambda i, st: bisect_pass(st), state0)
    state1 = snap_pass(state1)
    pending1 = jnp.max(jnp.where(state1[6] == 0.0, 1.0, 0.0))
    (lo, hi, _, c_lo, c_hi, thr, done), _ = lax.while_loop(outer_cond, outer_body, (state1, pending1))

    excess = jnp.where(done == 2.0, c_lo - kt, 0.0)
    need = kt - c_hi

    @pl.when(jnp.max(excess) > 0.0)
    def _():
        tri = (lax.broadcasted_iota(jnp.int32, (K_BLK, K_BLK), 0)
               >= lax.broadcasted_iota(jnp.int32, (K_BLK, K_BLK), 1)).astype(BF16)
        has_excess = excess > 0.0

        def drop_body(ch, run):
            scores = [sc_ref[ch, sub_rows[j], :] for j in range(sub)]
            tied = [jnp.logical_and(s == thr, has_excess) for s in scores]
            prefix = [jnp.dot(tri, jnp.where(t, 1.0, 0.0).astype(BF16), preferred_element_type=F32)
                      for t in tied]
            for j in range(sub):
                drop = jnp.logical_and(tied[j], run + prefix[j] > need)
                sc_ref[ch, sub_rows[j], :] = jnp.where(drop, -jnp.inf, scores[j])
                run = run + jnp.max(prefix[j], axis=0, keepdims=True)
            return run

        lax.fori_loop(0, nch, drop_body, jnp.zeros((1, Q_BLK), F32))

    m_ref[...] = jnp.full(m_ref.shape, M_INIT, F32)
    acc_ref[...] = jnp.zeros(acc_ref.shape, F32)
    row_m = lambda j: slice(j, j + 1)
    row_a = lambda j: slice(sub + j, sub + j + 1)

    def logits_stage(ch, j):
        kv_blk = kv_ref[0, ch, sub_rows[j], :]
        bias = jnp.where(sc_ref[ch, sub_rows[j], :] >= thr, 0.0, MASKED)
        for cc in range(n_chunks):
            logits = jnp.dot(kv_blk, qe_ref[:, cc * COL_BLK:(cc + 1) * COL_BLK],
                             preferred_element_type=F32)
            for hh in range(COL_BLK // Q_BLK):
                cs = slice(cc * COL_BLK + hh * Q_BLK, cc * COL_BLK + (hh + 1) * Q_BLK)
                lg = logits[:, hh * Q_BLK:(hh + 1) * Q_BLK] + bias
                lg_ref[j, :, cs] = lg
                mx_ref[:, cs] = lg.reshape(kgrp, SUBLANES, Q_BLK).max(axis=0)
        m_old = m_ref[...]
        m_new = jnp.maximum(m_old, jnp.max(mx_ref[...], axis=0, keepdims=True))
        st_ref[row_m(j), :] = m_new
        st_ref[row_a(j), :] = jnp.exp2(m_old - m_new)
        m_ref[...] = m_new

    def probs_stage(j):
        p_ref[j] = jnp.exp2(lg_ref[j] - st_ref[row_m(j), :]).astype(BF16)

    def value_stage(kb, j):
        acc_ref[...] = acc_ref[...] * st_ref[row_a(j), :] + jnp.dot(
            vt_ref[0, kb], p_ref[j], preferred_element_type=F32)

    p_ref[sub - 1] = jnp.zeros(p_ref.shape[1:], BF16)
    st_ref[row_a(sub - 1), :] = jnp.ones((1, n_cols), F32)
    logits_stage(0, 0)

    def attn_body(ch, _):
        probs_stage(0)
        value_stage(jnp.maximum(ch * sub - 1, 0), 1)
        logits_stage(ch, 1)
        value_stage(ch * sub, 0)
        probs_stage(1)
        logits_stage(jnp.minimum(ch + 1, nch - 1), 0)
        return 0

    lax.fori_loop(0, nch, attn_body, 0)
    value_stage(nch * sub - 1, 1)
    dh = qt_ref.shape[2]
    inv_l = 1.0 / acc_ref[dh:dh + 1, :]
    for j in range(n_cols // Q_BLK // 2):
        pair = [acc_ref[0:dh, (2 * j + hh) * Q_BLK:(2 * j + hh + 1) * Q_BLK]
                * inv_l[:, (2 * j + hh) * Q_BLK:(2 * j + hh + 1) * Q_BLK] for hh in range(2)]
        o_ref[0, :, j * 2 * dh:(j + 1) * 2 * dh] = jnp.concatenate(pair, axis=0).T.astype(o_ref.dtype)


def _dsa(qt, qit, w_t, kv4, ki4, vt4):
    b, nqb, dh, n_cols = qt.shape
    n_steps = kv4.shape[1]
    assert kv4.shape[2] == CNT_BLK and CNT_BLK == 2 * K_BLK and n_cols % COL_BLK == 0
    assert vt4.shape[1] * K_BLK == n_steps * CNT_BLK and vt4.shape[2] == dh + SUBLANES
    per_q = lambda a: pl.BlockSpec((1, 1) + a.shape[2:], lambda bi, qi: (bi, qi, 0, 0))
    per_b = lambda a: pl.BlockSpec((1,) + a.shape[1:], lambda bi, qi: (bi, 0, 0, 0))
    return pl.pallas_call(
        _dsa_kernel,
        out_shape=jax.ShapeDtypeStruct((b, nqb * Q_BLK, (n_cols // Q_BLK) * dh), BF16),
        grid=(b, nqb),
        in_specs=[per_q(qt), per_q(qit), per_q(w_t), per_b(kv4), per_b(ki4), per_b(vt4)],
        out_specs=pl.BlockSpec((1, Q_BLK, (n_cols // Q_BLK) * dh), lambda bi, qi: (bi, qi, 0)),
        scratch_shapes=[pltpu.VMEM((n_steps, CNT_BLK, Q_BLK), F32),
                        pltpu.VMEM((LANES, n_cols), BF16),
                        pltpu.VMEM((LANES, n_cols), BF16),
                        pltpu.VMEM((1, n_cols), F32),
                        pltpu.VMEM((SUBLANES, n_cols), F32),
                        pltpu.VMEM((SUBLANES, n_cols), F32),
                        pltpu.VMEM((dh + SUBLANES, n_cols), F32),
                        pltpu.VMEM((CNT_BLK // K_BLK, K_BLK, n_cols), F32),
                        pltpu.VMEM((CNT_BLK // K_BLK, K_BLK, n_cols), BF16)],
        compiler_params=_cparams(("arbitrary", "arbitrary")),
        name="dsa",
    )(qt, qit, w_t, kv4, ki4, vt4)


def _outproj_kernel(x_ref, attn_ref, pool_ref, woa_ref, wop_ref, gate1_ref, gffn_ref,
                    scale2_ref, shift2_ref, wr_ref, br_ref, x1_ref, h2_ref, gates_ref):
    tm = x_ref.shape[1]
    mix = (jnp.dot(attn_ref[0], woa_ref[...], preferred_element_type=F32)
           + jnp.dot(pool_ref[0], wop_ref[...], preferred_element_type=F32))
    x1 = x_ref[0] + gate1_ref[0] * mix
    x1_ref[0] = x1
    ms = jnp.mean(x1 * x1, axis=-1, keepdims=True)
    h2 = (x1 * lax.rsqrt(ms + EPS) * gffn_ref[...]) * (1.0 + scale2_ref[0]) + shift2_ref[0]
    h2_hi = h2.astype(BF16)
    h2_ref[0] = h2_hi

    h2_lo = (h2 - h2_hi.astype(F32)).astype(BF16)
    wr = wr_ref[...]
    wr_hi = wr.astype(BF16)
    wr_lo = (wr - wr_hi.astype(F32)).astype(BF16)
    logits = (jnp.dot(h2_hi, wr_hi, preferred_element_type=F32)
              + jnp.dot(h2_lo, wr_hi, preferred_element_type=F32)
              + jnp.dot(h2_hi, wr_lo, preferred_element_type=F32)) + br_ref[...]

    lane = lax.broadcasted_iota(jnp.int32, (tm, LANES), 1)
    big = jnp.int32(LANES)
    is_g = jnp.logical_and(lane >= N_EXPERTS, lane < N_EXPERTS + N_GROUPS)
    glog = jnp.where(is_g, logits, -jnp.inf)
    gmax = jnp.max(glog, axis=-1, keepdims=True)
    gsum = jnp.sum(jnp.exp(glog - gmax), axis=-1, keepdims=True)
    p_g = 1.0 / gsum
    g_sel = jnp.min(jnp.where(glog == gmax, lane, big), axis=-1, keepdims=True) - N_EXPERTS
    in_grp = jnp.logical_and(lane < N_EXPERTS, jnp.right_shift(lane, 3) == g_sel)
    elog = jnp.where(in_grp, logits, -jnp.inf)
    emax = jnp.max(elog, axis=-1, keepdims=True)
    eexp = jnp.exp(elog - emax)
    esum = jnp.sum(eexp, axis=-1, keepdims=True)
    p_e = jnp.where(in_grp, eexp / esum, -1.0)
    p1 = jnp.max(p_e, axis=-1, keepdims=True)
    i1 = jnp.min(jnp.where(p_e == p1, lane, big), axis=-1, keepdims=True)
    p_e2 = jnp.where(lane == i1, -1.0, p_e)
    p2 = jnp.max(p_e2, axis=-1, keepdims=True)
    i2 = jnp.min(jnp.where(p_e2 == p2, lane, big), axis=-1, keepdims=True)
    tot = p1 + p2
    gates_ref[0] = (jnp.where(lane == i1, p_g * (p1 / tot), 0.0)
                    + jnp.where(lane == i2, p_g * (p2 / tot), 0.0)
                    + jnp.where(lane == N_EXPERTS, g_sel.astype(F32), 0.0))


def _outproj(x, attn, pool, wo_a, wo_p, gate1, g_ffn, scale2, shift2, w_r, b_r):
    b, s, d = x.shape
    tm = TM_PROJ
    tok = lambda w: pl.BlockSpec((1, tm, w), lambda bi, si: (bi, si, 0))
    per_b = pl.BlockSpec((1, 1, d), lambda bi, si: (bi, 0, 0))
    full = lambda a: pl.BlockSpec(a.shape, lambda bi, si: (0,) * a.ndim)
    return pl.pallas_call(
        _outproj_kernel,
        out_shape=(jax.ShapeDtypeStruct((b, s, d), F32),
                   jax.ShapeDtypeStruct((b, s, d), BF16),
                   jax.ShapeDtypeStruct((b, s, LANES), F32)),
        grid=(b, s // tm),
        in_specs=[tok(d), tok(attn.shape[2]), tok(pool.shape[2]), full(wo_a), full(wo_p), per_b,
                  full(g_ffn), per_b, per_b, full(w_r), full(b_r)],
        out_specs=(tok(d), tok(d), tok(LANES)),
        compiler_params=_cparams(("arbitrary", "arbitrary")),
        name="outproj",
    )(x, attn, pool, wo_a, wo_p, gate1, g_ffn, scale2, shift2, w_r, b_r)


def _moe_kernel(x1_ref, h2_ref, gates_ref, gate2_ref, wgu_ref, wd_ref, o_ref,
                xe_ref, rank_ref, rank_t_ref):
    g = pl.program_id(2)
    tm, d = h2_ref.shape[1], h2_ref.shape[2]
    n_e, d_exp = wd_ref.shape[0], wd_ref.shape[1]
    lane = lax.broadcasted_iota(jnp.int32, (tm, LANES), 1)
    gf = g.astype(F32)

    @pl.when(g == 0)
    def _():
        tri = (lax.broadcasted_iota(jnp.int32, (RANK_BLK, RANK_BLK), 0)
               >= lax.broadcasted_iota(jnp.int32, (RANK_BLK, RANK_BLK), 1)).astype(BF16)
        lane_b = lax.broadcasted_iota(jnp.int32, (RANK_BLK, LANES), 1)
        run = jnp.zeros((1, LANES), F32)
        for sb in range(tm // RANK_BLK):
            rows = slice(sb * RANK_BLK, (sb + 1) * RANK_BLK)
            gts_b = gates_ref[0, rows, :]
            grp = jnp.sum(jnp.where(lane_b == N_EXPERTS, gts_b, 0.0), axis=-1, keepdims=True)
            member = jnp.where(jnp.logical_and(lane_b < N_GROUPS, lane_b.astype(F32) == grp), 1.0, 0.0)
            pre = jnp.dot(tri, member.astype(BF16), preferred_element_type=F32) + run
            rank_ref[rows, :] = jnp.where(lane_b == N_GROUPS, grp, pre)
            run = jnp.max(pre, axis=0, keepdims=True)
        rank_t_ref[...] = rank_ref[...].T
        gts = gates_ref[0]
        g_hi = gts.astype(BF16)
        xe_ref[:, :d] = h2_ref[0]
        xe_ref[:, d:d + LANES] = g_hi
        xe_ref[:, d + LANES:d + 2 * LANES] = (gts - g_hi.astype(F32)).astype(BF16)
        o_ref[0] = jnp.zeros((tm, d), F32)

    rank_row = rank_t_ref[pl.ds(g, 1), :]
    pos_row = jnp.where(rank_t_ref[N_GROUPS:N_GROUPS + 1, :] == gf, rank_row - 1.0, -1.0)
    rk = rank_ref[...]
    rank_col = jnp.sum(jnp.where(lane == g, rk, 0.0), axis=-1, keepdims=True)
    grp_col = jnp.sum(jnp.where(lane == N_GROUPS, rk, 0.0), axis=-1, keepdims=True)
    pos_col = jnp.where(grp_col == gf, rank_col - 1.0, -1.0)
    n_rows = jnp.max(rank_row).astype(jnp.int32)
    row_id = lax.broadcasted_iota(jnp.int32, (MOE_CH, tm), 0).astype(F32)
    col_id = lax.broadcasted_iota(jnp.int32, (tm, MOE_CH), 1).astype(F32)
    lane_c = lax.broadcasted_iota(jnp.int32, (MOE_CH, LANES), 1)

    def chunk_body(c, _):
        r0 = (c * MOE_CH).astype(F32)
        gather = jnp.where(pos_row - r0 == row_id, 1.0, 0.0).astype(BF16)
        xg = jnp.dot(gather, xe_ref[...], preferred_element_type=F32)
        xb = xg[:, :d].astype(BF16)
        gates_c = xg[:, d:d + LANES] + xg[:, d + LANES:d + 2 * LANES]
        ya = None
        for e in range(n_e):
            gu = jnp.dot(xb, wgu_ref[e], preferred_element_type=F32)
            gt = gu[:, :d_exp]
            a = (gt * jax.nn.sigmoid(gt)) * gu[:, d_exp:]
            gate_e = jnp.sum(jnp.where(lane_c == g * n_e + e, gates_c, 0.0), axis=-1, keepdims=True)
            y = jnp.dot((a * gate_e).astype(BF16), wd_ref[e], preferred_element_type=F32)
            ya = y if ya is None else ya + y
        scatter = jnp.where(pos_col - r0 == col_id, 1.0, 0.0).astype(BF16)
        o_ref[0] += jnp.dot(scatter, ya.astype(BF16), preferred_element_type=F32)
        return 0

    lax.fori_loop(0, (n_rows + MOE_CH - 1) // MOE_CH, chunk_body, 0)

    @pl.when(g == pl.num_programs(2) - 1)
    def _():
        o_ref[0] = x1_ref[0] + gate2_ref[0] * o_ref[0]


def _moe(x1, h2, gates, gate2, w_gu, w_d):
    b, s, d = x1.shape
    tm = TM_MOE
    n_e = EXPERTS_PER_GROUP
    assert w_gu.shape[0] == N_GROUPS * n_e and tm % RANK_BLK == 0
    tok = lambda w: pl.BlockSpec((1, tm, w), lambda bi, si, g: (bi, si, 0))
    return pl.pallas_call(
        _moe_kernel,
        out_shape=jax.ShapeDtypeStruct((b, s, d), F32),
        grid=(b, s // tm, N_GROUPS),
        in_specs=[tok(d), tok(d), tok(LANES),
                  pl.BlockSpec((1, 1, d), lambda bi, si, g: (bi, 0, 0)),
                  pl.BlockSpec((n_e,) + w_gu.shape[1:], lambda bi, si, g: (g, 0, 0)),
                  pl.BlockSpec((n_e,) + w_d.shape[1:], lambda bi, si, g: (g, 0, 0))],
        out_specs=tok(d),
        scratch_shapes=[pltpu.VMEM((tm, d + 2 * LANES), BF16),
                        pltpu.VMEM((tm, LANES), F32),
                        pltpu.VMEM((LANES, tm), F32)],
        compiler_params=pltpu.CompilerParams(
            dimension_semantics=("arbitrary", "arbitrary", "arbitrary"),
            vmem_limit_bytes=VMEM_LIMIT_MOE_BYTES),
        name="moe",
    )(x1, h2, gates, gate2, w_gu, w_d)


def _layer(x, mod, pos3, g_mix, g_ffn, w_in, g_q, g_k, g_kidx, w_pool, pool_scale, w_out,
           w_rg, b_rg, w_re, b_re, w_gate, w_up, w_down):
    b, s, d = x.shape
    d_attn = N_HEADS * HEAD_DIM
    nqb = s // Q_BLK
    nkb = s // K_BLK
    shift1, scale1, gate1, shift2, scale2, gate2 = [m[:, None, :] for m in jnp.split(mod, 6, axis=-1)]

    n_front = d_attn + 2 * HEAD_DIM + N_IDX_HEADS * IDX_DIM + IDX_DIM + N_IDX_HEADS
    pad = (-n_front) % LANES
    w_in_p = jnp.concatenate([w_in[:, :n_front], jnp.zeros((d, pad), w_in.dtype), w_in[:, n_front:]],
                             axis=1).astype(BF16)
    seg_id = jnp.arange(d_attn) // HEAD_DIM
    segsum = (seg_id[:, None] == seg_id[None, :]).astype(BF16)
    ones_half = jnp.ones((LANES - HEAD_DIM,), F32)
    gq_t = jnp.tile(g_q, N_HEADS)[None, :]
    gk_e = jnp.concatenate([g_k, ones_half])[None, :]
    gkidx_e = jnp.concatenate([g_kidx, ones_half])[None, :]
    half = HEAD_DIM // 2
    inv_freq = ROPE_THETA ** (-jnp.arange(0, HEAD_DIM, 2, dtype=F32) / HEAD_DIM)
    invf = jnp.tile(inv_freq, LANES // half)[None, :]

    qt, kv, qit, ki, w_t, vt4, pool = _inproj(pos3, x, scale1, shift1, g_mix[None, :], w_in_p, segsum,
                                              gq_t, gk_e, gkidx_e, invf, w_pool.astype(BF16),
                                              pool_scale[None, :])
    kv4 = kv.reshape(b, s // CNT_BLK, CNT_BLK, LANES)
    ki4 = ki.reshape(b, s // CNT_BLK, CNT_BLK, LANES)
    attn = _dsa(qt, qit, w_t, kv4, ki4, vt4)

    w_out_b = w_out.astype(BF16)
    w_r = jnp.concatenate([w_re, w_rg, jnp.zeros((d, LANES - N_EXPERTS - N_GROUPS), F32)], axis=1)
    b_r = jnp.concatenate([b_re, b_rg, jnp.zeros((LANES - N_EXPERTS - N_GROUPS,), F32)])[None, :]
    x1, h2, gates = _outproj(x, attn, pool, w_out_b[:d_attn], w_out_b[d_attn:], gate1,
                             g_ffn[None, :], scale2, shift2, w_r, b_r)

    w_gu = jnp.concatenate([w_gate, w_up], axis=-1).astype(BF16)
    return _moe(x1, h2, gates, gate2, w_gu, w_down.astype(BF16))


def kernel(x, c, positions, w_ada, b_ada, g_norm_mix, g_norm_ffn, w_in, g_q, g_k, g_kidx, w_pool,
           pool_scale, w_out, w_router_group, b_router_group, w_router_expert, b_router_expert,
           w_gate, w_up, w_down):
    b, s, d = x.shape
    depth = w_ada.shape[0]
    assert s % TM_MOE == 0 and s % K_BLK == 0 and d % LANES == 0
    pos3 = positions[:, :, None]
    c_pad = jnp.concatenate([c, jnp.zeros((-b % SUBLANES, d), c.dtype)], axis=0)
    for l in range(depth):
        mod = _adaln(c_pad, w_ada[l], b_ada[l][None, :])[:b]
        x = _layer(x, mod, pos3, g_norm_mix[l], g_norm_ffn[l], w_in[l], g_q[l], g_k[l], g_kidx[l],
                   w_pool[l], pool_scale[l], w_out[l], w_router_group[l], b_router_group[l],
                   w_router_expert[l], b_router_expert[l], w_gate[l], w_up[l], w_down[l])
    return x
```

```python
import functools

import jax
import jax.numpy as jnp
from jax import lax
from jax.experimental import pallas as pl
from jax.experimental.pallas import tpu as pltpu

N_HEADS = 8
HEAD_DIM = 64
N_IDX_HEADS = 8
IDX_DIM = 64
TOPK_MAX = 256
ROPE_THETA = 10000.0
POOL_WINDOWS = (2, 4, 8, 16)
N_GROUPS = 4
EXPERTS_PER_GROUP = 8
N_EXPERTS = N_GROUPS * EXPERTS_PER_GROUP
EPS = 1e-6

LANES = 128
SUBLANES = 8
VMEM_LIMIT_BYTES = 56 * 1024 * 1024
VMEM_LIMIT_MOE_BYTES = 60 * 1024 * 1024

Q_BLK = 256
K_BLK = 256
COL_BLK = 256
CNT_BLK = 512
CNT_ROWS = 32
SEARCH_FIRST = 15
SEARCH_PERIOD = 2
TM_PROJ = 512
TM_MOE = 1024
MOE_CH = 320
RANK_BLK = 256
MAX_WIN = max(POOL_WINDOWS)
M_INIT = -1e29
MASKED = -1e30
F32_LOWEST = -3.0e38
LOG2_E = 1.4426950408889634

BF16 = jnp.bfloat16
F32 = jnp.float32


def _cparams(sem):
    return pltpu.CompilerParams(dimension_semantics=sem, vmem_limit_bytes=VMEM_LIMIT_BYTES)


def _adaln_kernel(c_ref, w_ref, b_ref, o_ref):
    c = c_ref[...]
    c_act = c * jax.nn.sigmoid(c)
    o_ref[...] = jnp.dot(c_act, w_ref[...], preferred_element_type=F32) + b_ref[...]


def _adaln(c_pad, w_ada, b_ada):
    rows, d = c_pad.shape
    n = w_ada.shape[1]
    tn = n // 6
    return pl.pallas_call(
        _adaln_kernel,
        out_shape=jax.ShapeDtypeStruct((rows, n), F32),
        grid=(n // tn,),
        in_specs=[pl.BlockSpec((rows, d), lambda j: (0, 0)),
                  pl.BlockSpec((d, tn), lambda j: (0, j)),
                  pl.BlockSpec((1, tn), lambda j: (0, j))],
        out_specs=pl.BlockSpec((rows, tn), lambda j: (0, j)),
        compiler_params=_cparams(("arbitrary",)),
        name="adaln",
    )(c_pad, w_ada, b_ada)


def _rope_chunk(y, cos, sin_signed, first_half):
    from_hi = pltpu.roll(y, LANES - HEAD_DIM // 2, 1)
    from_lo = pltpu.roll(y, HEAD_DIM // 2, 1)
    return y * cos + jnp.where(first_half, from_hi, from_lo) * sin_signed


def _inproj_kernel(pos_ref, x_ref, scale_ref, shift_ref, gmix_ref, win_ref, segsum_ref,
                   gq_ref, gk_ref, gkidx_ref, invf_ref, wpool_ref, pscale_ref,
                   qt_ref, kv_ref, qit_ref, ki_ref, wt_ref, vt_ref, pool_ref, ubuf_ref):
    tm = x_ref.shape[1]
    d_attn = N_HEADS * HEAD_DIM
    d_qidx = N_IDX_HEADS * IDX_DIM
    s_tile = pl.program_id(1)

    def store_cols(dst_ref, chunk, j):
        for t in range(tm // Q_BLK):
            ct = chunk[t * Q_BLK:(t + 1) * Q_BLK, :].T
            for hh in range(2):
                col = (2 * j + hh) * Q_BLK
                dst_ref[0, t, :, col:col + Q_BLK] = ct[hh * HEAD_DIM:(hh + 1) * HEAD_DIM, :].astype(dst_ref.dtype)

    x = x_ref[0]
    ms = jnp.mean(x * x, axis=-1, keepdims=True)
    h = (x * lax.rsqrt(ms + EPS) * gmix_ref[...]) * (1.0 + scale_ref[0]) + shift_ref[0]
    proj = jnp.dot(h.astype(BF16), win_ref[...], preferred_element_type=F32)

    lane = lax.broadcasted_iota(jnp.int32, (tm, LANES), 1)
    first_half = (lane & (HEAD_DIM - 1)) < (HEAD_DIM // 2)
    ang = pos_ref[0].astype(F32) * invf_ref[...]
    cos = jnp.cos(ang)
    sin = jnp.sin(ang)
    sin_signed = jnp.where(first_half, -sin, sin)
    rope = functools.partial(_rope_chunk, cos=cos, sin_signed=sin_signed, first_half=first_half)

    qf = proj[:, :d_attn]
    qsq = qf * qf
    qsq_hi = qsq.astype(BF16)
    qsq_lo = (qsq - qsq_hi.astype(F32)).astype(BF16)
    seg = segsum_ref[...]
    ssq = (jnp.dot(qsq_hi, seg, preferred_element_type=F32)
           + jnp.dot(qsq_lo, seg, preferred_element_type=F32))
    qn = qf * lax.rsqrt(ssq * (1.0 / HEAD_DIM) + EPS) * gq_ref[...]
    for j in range(d_attn // LANES):
        sl = slice(j * LANES, (j + 1) * LANES)
        store_cols(qt_ref, rope(qn[:, sl]) * (LOG2_E * HEAD_DIM ** -0.5), j)

    kvc = proj[:, d_attn:d_attn + LANES]
    is_k = lane < HEAD_DIM
    ksq = jnp.sum(jnp.where(is_k, kvc * kvc, 0.0), axis=-1, keepdims=True)
    kn = kvc * lax.rsqrt(ksq * (1.0 / HEAD_DIM) + EPS) * gk_ref[...]
    kv_ref[0] = jnp.where(is_k, rope(kn), kvc).astype(BF16)
    row8 = lax.broadcasted_iota(jnp.int32, (SUBLANES, K_BLK), 0)
    for t in range(tm // K_BLK):
        vt_ref[0, t, 0:HEAD_DIM, :] = kvc[t * K_BLK:(t + 1) * K_BLK, :].T[HEAD_DIM:, :].astype(BF16)
        vt_ref[0, t, HEAD_DIM:HEAD_DIM + SUBLANES, :] = jnp.where(row8 == 0, 1.0, 0.0).astype(BF16)

    o_qi = d_attn + LANES
    for j in range(d_qidx // LANES):
        store_cols(qit_ref, rope(proj[:, o_qi + j * LANES:o_qi + (j + 1) * LANES]), j)

    o_ki = o_qi + d_qidx
    kic = proj[:, o_ki:o_ki + LANES]
    kisq = jnp.sum(jnp.where(is_k, kic * kic, 0.0), axis=-1, keepdims=True)
    kin = kic * lax.rsqrt(kisq * (1.0 / IDX_DIM) + EPS) * gkidx_ref[...]
    ki_ref[0] = jnp.where(is_k, rope(kin), 0.0).astype(BF16)
    for t in range(tm // Q_BLK):
        wt_ref[0, t] = kic[t * Q_BLK:(t + 1) * Q_BLK, :].T[IDX_DIM:IDX_DIM + N_IDX_HEADS, :] * (
            N_IDX_HEADS ** -0.5 * IDX_DIM ** -0.5)

    o_u = o_ki + LANES
    u = proj[:, o_u:o_u + LANES * len(POOL_WINDOWS)]

    @pl.when(s_tile == 0)
    def _():
        ubuf_ref[0:MAX_WIN, :] = jnp.zeros((MAX_WIN, u.shape[1]), F32)

    @pl.when(s_tile != 0)
    def _():
        ubuf_ref[0:MAX_WIN, :] = ubuf_ref[tm:tm + MAX_WIN, :]

    ubuf_ref[MAX_WIN:MAX_WIN + tm, :] = u
    t_idx = s_tile * tm + lax.broadcasted_iota(jnp.int32, (tm, 1), 0)
    for g, win in enumerate(POOL_WINDOWS):
        sl = slice(g * LANES, (g + 1) * LANES)
        wsum = u[:, sl]
        for j in range(1, win):
            wsum = wsum + ubuf_ref[MAX_WIN - j:MAX_WIN - j + tm, sl]
        cnt = jnp.minimum(t_idx + 1, win).astype(F32)
        pooled = wsum / cnt - u[:, sl]
        mixed = jnp.dot(pooled.astype(BF16), wpool_ref[g], preferred_element_type=F32)
        pool_ref[0, :, sl] = (mixed * pscale_ref[:, sl]).astype(BF16)


def _inproj(pos3, x, scale1, shift1, g_mix, w_in_p, segsum, gq_t, gk_e, gkidx_e, invf, w_pool, pscale):
    b, s, d = x.shape
    tm = TM_PROJ
    d_attn = N_HEADS * HEAD_DIM
    d_qidx = N_IDX_HEADS * IDX_DIM
    d_pool = LANES * len(POOL_WINDOWS)
    assert tm % Q_BLK == 0 and tm % K_BLK == 0 and HEAD_DIM == IDX_DIM and 2 * HEAD_DIM == LANES
    tok = lambda w: pl.BlockSpec((1, tm, w), lambda bi, si: (bi, si, 0))
    blk = lambda n, r, c: pl.BlockSpec((1, tm // n, r, c), lambda bi, si: (bi, si, 0, 0))
    per_b = pl.BlockSpec((1, 1, d), lambda bi, si: (bi, 0, 0))
    full = lambda a: pl.BlockSpec(a.shape, lambda bi, si: (0,) * a.ndim)
    nqb, nkb = s // Q_BLK, s // K_BLK
    return pl.pallas_call(
        _inproj_kernel,
        out_shape=(jax.ShapeDtypeStruct((b, nqb, HEAD_DIM, N_HEADS * Q_BLK), BF16),
                   jax.ShapeDtypeStruct((b, s, LANES), BF16),
                   jax.ShapeDtypeStruct((b, nqb, IDX_DIM, N_IDX_HEADS * Q_BLK), BF16),
                   jax.ShapeDtypeStruct((b, s, LANES), BF16),
                   jax.ShapeDtypeStruct((b, nqb, N_IDX_HEADS, Q_BLK), F32),
                   jax.ShapeDtypeStruct((b, nkb, HEAD_DIM + SUBLANES, K_BLK), BF16),
                   jax.ShapeDtypeStruct((b, s, d_pool), BF16)),
        grid=(b, s // tm),
        in_specs=[tok(1), tok(d), per_b, per_b, full(g_mix), full(w_in_p), full(segsum),
                  full(gq_t), full(gk_e), full(gkidx_e), full(invf), full(w_pool), full(pscale)],
        out_specs=(blk(Q_BLK, HEAD_DIM, N_HEADS * Q_BLK), tok(LANES),
                   blk(Q_BLK, IDX_DIM, N_IDX_HEADS * Q_BLK), tok(LANES),
                   blk(Q_BLK, N_IDX_HEADS, Q_BLK), blk(K_BLK, HEAD_DIM + SUBLANES, K_BLK), tok(d_pool)),
        scratch_shapes=[pltpu.VMEM((tm + 2 * MAX_WIN, d_pool), F32)],
        compiler_params=_cparams(("arbitrary", "arbitrary")),
        name="inproj",
    )(pos3, x, scale1, shift1, g_mix, w_in_p, segsum, gq_t, gk_e, gkidx_e, invf, w_pool, pscale)


def _dsa_kernel(qt_ref, qit_ref, w_ref, kv_ref, ki_ref, vt_ref, o_ref,
                sc_ref, qe_ref, qie_ref, m_ref, mx_ref, st_ref, acc_ref, lg_ref, p_ref):
    topk = float(min(TOPK_MAX, (sc_ref.shape[0] * CNT_BLK) // 4))
    qb = pl.program_id(1)
    n_cols = qt_ref.shape[3]
    n_chunks = n_cols // COL_BLK
    sub = CNT_BLK // K_BLK
    nch = ((qb + 1) * Q_BLK + CNT_BLK - 1) // CNT_BLK
    kgrp = K_BLK // SUBLANES
    sub_rows = [slice(j * K_BLK, (j + 1) * K_BLK) for j in range(sub)]

    zeros_half = jnp.zeros((LANES - HEAD_DIM, n_cols), BF16)
    qe_ref[0:HEAD_DIM, :] = qt_ref[0, 0]
    qe_ref[HEAD_DIM:LANES, :] = zeros_half
    qie_ref[0:IDX_DIM, :] = qit_ref[0, 0]
    qie_ref[IDX_DIM:LANES, :] = zeros_half

    q_pos = qb * Q_BLK + lax.broadcasted_iota(jnp.int32, (K_BLK, Q_BLK), 1)
    key_off = lax.broadcasted_iota(jnp.int32, (K_BLK, Q_BLK), 0)

    def score_body(ch, carry):
        rmax, rmin = carry
        for j in range(sub):
            ki_blk = ki_ref[0, ch, sub_rows[j], :]
            score = None
            for cc in range(n_chunks):
                cs = slice(cc * COL_BLK, (cc + 1) * COL_BLK)
                s_h = jnp.dot(ki_blk, qie_ref[:, cs], preferred_element_type=F32)
                s_h = jnp.maximum(s_h, 0.0)
                for hh in range(COL_BLK // Q_BLK):
                    head = cc * (COL_BLK // Q_BLK) + hh
                    part = s_h[:, hh * Q_BLK:(hh + 1) * Q_BLK] * w_ref[0, 0, head:head + 1, :]
                    score = part if score is None else score + part
            causal = (ch * CNT_BLK + j * K_BLK + key_off) <= q_pos
            masked = jnp.where(causal, score, -jnp.inf)
            sc_ref[ch, sub_rows[j], :] = masked
            hi_part = masked.reshape(kgrp, SUBLANES, Q_BLK).max(axis=0)
            lo_part = jnp.where(causal, score, jnp.inf).reshape(kgrp, SUBLANES, Q_BLK).min(axis=0)
            rmax, rmin = jnp.maximum(rmax, hi_part), jnp.minimum(rmin, lo_part)
        return rmax, rmin

    rmax8, rmin8 = lax.fori_loop(
        0, nch, score_body,
        (jnp.full((SUBLANES, Q_BLK), -jnp.inf, F32), jnp.full((SUBLANES, Q_BLK), jnp.inf, F32)))
    rowmax = jnp.max(rmax8, axis=0, keepdims=True)
    rowmin = jnp.min(rmin8, axis=0, keepdims=True)

    n_causal = (qb * Q_BLK + 1 + lax.broadcasted_iota(jnp.int32, (1, Q_BLK), 1)).astype(F32)
    kt = jnp.minimum(n_causal, topk)

    cgrp = CNT_BLK // CNT_ROWS

    def count_ge(t):
        def body(ch, acc):
            for r in range(cgrp):
                rows = sc_ref[ch, r * CNT_ROWS:(r + 1) * CNT_ROWS, :]
                acc = acc + jnp.where(rows >= t, 1.0, 0.0)
            return acc
        acc = lax.fori_loop(0, nch, body, jnp.zeros((CNT_ROWS, Q_BLK), F32))
        return jnp.sum(acc, axis=0, keepdims=True)

    def bisect_pass(state):
        lo, hi, top, c_lo, c_hi, thr, done = state
        cap = jnp.minimum(hi, top)
        mid = lo + 0.5 * (cap - lo)
        mid = jnp.where(mid <= lo, cap, mid)
        c = count_ge(mid)
        hit = jnp.logical_and(done == 0.0, c == kt)
        thr = jnp.where(hit, mid, thr)
        done = jnp.where(hit, 1.0, done)
        active = done == 0.0
        up = jnp.logical_and(active, c >= kt)
        down = jnp.logical_and(active, c < kt)
        return (jnp.where(up, mid, lo), jnp.where(down, mid, hi), jnp.where(down, jnp.inf, top),
                jnp.where(up, c, c_lo), jnp.where(down, c, c_hi), thr, done)

    def snap_pass(state):
        lo, hi, top, c_lo, c_hi, thr, done = state

        def body(ch, carry):
            a8, b8 = carry
            for r in range(cgrp):
                s = sc_ref[ch, r * CNT_ROWS:(r + 1) * CNT_ROWS, :]
                a8 = jnp.minimum(a8, jnp.where(s >= lo, s, jnp.inf))
                b8 = jnp.maximum(b8, jnp.where(s < hi, s, -jnp.inf))
            return a8, b8

        a8, b8 = lax.fori_loop(
            0, nch, body,
            (jnp.full((CNT_ROWS, Q_BLK), jnp.inf, F32), jnp.full((CNT_ROWS, Q_BLK), -jnp.inf, F32)))
        a = jnp.min(a8, axis=0, keepdims=True)
        b = jnp.max(b8, axis=0, keepdims=True)
        active = done == 0.0
        hit = jnp.logical_and(active, a == b)
        thr = jnp.where(hit, a, thr)
        done = jnp.where(hit, 2.0, done)
        return jnp.where(active, a, lo), hi, jnp.where(active, b, top), c_lo, c_hi, thr, done

    few = n_causal <= topk
    state0 = (rowmin, jnp.full((1, Q_BLK), jnp.inf, F32), rowmax, n_causal,
              jnp.zeros((1, Q_BLK), F32), jnp.where(few, F32_LOWEST, 0.0), jnp.where(few, 1.0, 0.0))

    def outer_cond(carry):
        return carry[1] > 0.0

    def outer_body(carry):
        state, _ = carry
        state = lax.fori_loop(0, SEARCH_PERIOD, lambda i, st: bisect_pass(st), state)
        state = snap_pass(state)
        pending = jnp.max(jnp.where(state[6] == 0.0, 1.0, 0.0))
        return state, pending

    state1 = lax.fori_loop(0, SEARCH_FIRST, lambda i, st: bisect_pass(st), state0)
    state1 = snap_pass(state1)
    pending1 = jnp.max(jnp.where(state1[6] == 0.0, 1.0, 0.0))
    (lo, hi, _, c_lo, c_hi, thr, done), _ = lax.while_loop(outer_cond, outer_body, (state1, pending1))

    excess = jnp.where(done == 2.0, c_lo - kt, 0.0)
    need = kt - c_hi

    @pl.when(jnp.max(excess) > 0.0)
    def _():
        tri = (lax.broadcasted_iota(jnp.int32, (K_BLK, K_BLK), 0)
               >= lax.broadcasted_iota(jnp.int32, (K_BLK, K_BLK), 1)).astype(BF16)
        has_excess = excess > 0.0

        def drop_body(ch, run):
            scores = [sc_ref[ch, sub_rows[j], :] for j in range(sub)]
            tied = [jnp.logical_and(s == thr, has_excess) for s in scores]
            prefix = [jnp.dot(tri, jnp.where(t, 1.0, 0.0).astype(BF16), preferred_element_type=F32)
                      for t in tied]
            for j in range(sub):
                drop = jnp.logical_and(tied[j], run + prefix[j] > need)
                sc_ref[ch, sub_rows[j], :] = jnp.where(drop, -jnp.inf, scores[j])
                run = run + jnp.max(prefix[j], axis=0, keepdims=True)
            return run

        lax.fori_loop(0, nch, drop_body, jnp.zeros((1, Q_BLK), F32))

    m_ref[...] = jnp.full(m_ref.shape, M_INIT, F32)
    acc_ref[...] = jnp.zeros(acc_ref.shape, F32)
    row_m = lambda j: slice(j, j + 1)
    row_a = lambda j: slice(sub + j, sub + j + 1)

    def logits_stage(ch, j):
        kv_blk = kv_ref[0, ch, sub_rows[j], :]
        bias = jnp.where(sc_ref[ch, sub_rows[j], :] >= thr, 0.0, MASKED)
        for cc in range(n_chunks):
            logits = jnp.dot(kv_blk, qe_ref[:, cc * COL_BLK:(cc + 1) * COL_BLK],
                             preferred_element_type=F32)
            for hh in range(COL_BLK // Q_BLK):
                cs = slice(cc * COL_BLK + hh * Q_BLK, cc * COL_BLK + (hh + 1) * Q_BLK)
                lg = logits[:, hh * Q_BLK:(hh + 1) * Q_BLK] + bias
                lg_ref[j, :, cs] = lg
                mx_ref[:, cs] = lg.reshape(kgrp, SUBLANES, Q_BLK).max(axis=0)
        m_old = m_ref[...]
        m_new = jnp.maximum(m_old, jnp.max(mx_ref[...], axis=0, keepdims=True))
        st_ref[row_m(j), :] = m_new
        st_ref[row_a(j), :] = jnp.exp2(m_old - m_new)
        m_ref[...] = m_new

    def probs_stage(j):
        p_ref[j] = jnp.exp2(lg_ref[j] - st_ref[row_m(j), :]).astype(BF16)

    def value_stage(kb, j, alpha):
        acc_ref[...] = acc_ref[...] * alpha + jnp.dot(
            vt_ref[0, kb], p_ref[j], preferred_element_type=F32)

    p_ref[sub - 1] = jnp.zeros(p_ref.shape[1:], BF16)
    st_ref[row_a(sub - 1), :] = jnp.ones((1, n_cols), F32)
    logits_stage(0, 0)

    def attn_body(ch, _):
        alpha_prev = st_ref[row_a(1), :]
        logits_stage(ch, 1)
        value_stage(jnp.maximum(ch * sub - 1, 0), 1, alpha_prev)
        probs_stage(0)
        alpha_cur = st_ref[row_a(0), :]
        logits_stage(jnp.minimum(ch + 1, nch - 1), 0)
        value_stage(ch * sub, 0, alpha_cur)
        probs_stage(1)
        return 0

    lax.fori_loop(0, nch, attn_body, 0)
    value_stage(nch * sub - 1, 1, st_ref[row_a(1), :])
    dh = qt_ref.shape[2]
    inv_l = 1.0 / acc_ref[dh:dh + 1, :]
    for j in range(n_cols // Q_BLK // 2):
        pair = [acc_ref[0:dh, (2 * j + hh) * Q_BLK:(2 * j + hh + 1) * Q_BLK]
                * inv_l[:, (2 * j + hh) * Q_BLK:(2 * j + hh + 1) * Q_BLK] for hh in range(2)]
        o_ref[0, :, j * 2 * dh:(j + 1) * 2 * dh] = jnp.concatenate(pair, axis=0).T.astype(o_ref.dtype)


def _dsa(qt, qit, w_t, kv4, ki4, vt4):
    b, nqb, dh, n_cols = qt.shape
    n_steps = kv4.shape[1]
    assert kv4.shape[2] == CNT_BLK and CNT_BLK == 2 * K_BLK and n_cols % COL_BLK == 0
    assert vt4.shape[1] * K_BLK == n_steps * CNT_BLK and vt4.shape[2] == dh + SUBLANES
    per_q = lambda a: pl.BlockSpec((1, 1) + a.shape[2:], lambda bi, qi: (bi, qi, 0, 0))
    per_b = lambda a: pl.BlockSpec((1,) + a.shape[1:], lambda bi, qi: (bi, 0, 0, 0))
    return pl.pallas_call(
        _dsa_kernel,
        out_shape=jax.ShapeDtypeStruct((b, nqb * Q_BLK, (n_cols // Q_BLK) * dh), BF16),
        grid=(b, nqb),
        in_specs=[per_q(qt), per_q(qit), per_q(w_t), per_b(kv4), per_b(ki4), per_b(vt4)],
        out_specs=pl.BlockSpec((1, Q_BLK, (n_cols // Q_BLK) * dh), lambda bi, qi: (bi, qi, 0)),
        scratch_shapes=[pltpu.VMEM((n_steps, CNT_BLK, Q_BLK), F32),
                        pltpu.VMEM((LANES, n_cols), BF16),
                        pltpu.VMEM((LANES, n_cols), BF16),
                        pltpu.VMEM((1, n_cols), F32),
                        pltpu.VMEM((SUBLANES, n_cols), F32),
                        pltpu.VMEM((SUBLANES, n_cols), F32),
                        pltpu.VMEM((dh + SUBLANES, n_cols), F32),
                        pltpu.VMEM((CNT_BLK // K_BLK, K_BLK, n_cols), F32),
                        pltpu.VMEM((CNT_BLK // K_BLK, K_BLK, n_cols), BF16)],
        compiler_params=_cparams(("arbitrary", "arbitrary")),
        name="dsa",
    )(qt, qit, w_t, kv4, ki4, vt4)


def _outproj_kernel(x_ref, attn_ref, pool_ref, woa_ref, wop_ref, gate1_ref, gffn_ref,
                    scale2_ref, shift2_ref, wr_ref, br_ref, x1_ref, h2_ref, gates_ref):
    tm = x_ref.shape[1]
    mix = (jnp.dot(attn_ref[0], woa_ref[...], preferred_element_type=F32)
           + jnp.dot(pool_ref[0], wop_ref[...], preferred_element_type=F32))
    x1 = x_ref[0] + gate1_ref[0] * mix
    x1_ref[0] = x1
    ms = jnp.mean(x1 * x1, axis=-1, keepdims=True)
    h2 = (x1 * lax.rsqrt(ms + EPS) * gffn_ref[...]) * (1.0 + scale2_ref[0]) + shift2_ref[0]
    h2_hi = h2.astype(BF16)
    h2_ref[0] = h2_hi

    h2_lo = (h2 - h2_hi.astype(F32)).astype(BF16)
    wr = wr_ref[...]
    wr_hi = wr.astype(BF16)
    wr_lo = (wr - wr_hi.astype(F32)).astype(BF16)
    logits = (jnp.dot(h2_hi, wr_hi, preferred_element_type=F32)
              + jnp.dot(h2_lo, wr_hi, preferred_element_type=F32)
              + jnp.dot(h2_hi, wr_lo, preferred_element_type=F32)) + br_ref[...]

    lane = lax.broadcasted_iota(jnp.int32, (tm, LANES), 1)
    big = jnp.int32(LANES)
    is_g = jnp.logical_and(lane >= N_EXPERTS, lane < N_EXPERTS + N_GROUPS)
    glog = jnp.where(is_g, logits, -jnp.inf)
    gmax = jnp.max(glog, axis=-1, keepdims=True)
    gsum = jnp.sum(jnp.exp(glog - gmax), axis=-1, keepdims=True)
    p_g = 1.0 / gsum
    g_sel = jnp.min(jnp.where(glog == gmax, lane, big), axis=-1, keepdims=True) - N_EXPERTS
    in_grp = jnp.logical_and(lane < N_EXPERTS, jnp.right_shift(lane, 3) == g_sel)
    elog = jnp.where(in_grp, logits, -jnp.inf)
    emax = jnp.max(elog, axis=-1, keepdims=True)
    eexp = jnp.exp(elog - emax)
    esum = jnp.sum(eexp, axis=-1, keepdims=True)
    p_e = jnp.where(in_grp, eexp / esum, -1.0)
    p1 = jnp.max(p_e, axis=-1, keepdims=True)
    i1 = jnp.min(jnp.where(p_e == p1, lane, big), axis=-1, keepdims=True)
    p_e2 = jnp.where(lane == i1, -1.0, p_e)
    p2 = jnp.max(p_e2, axis=-1, keepdims=True)
    i2 = jnp.min(jnp.where(p_e2 == p2, lane, big), axis=-1, keepdims=True)
    tot = p1 + p2
    gates_ref[0] = (jnp.where(lane == i1, p_g * (p1 / tot), 0.0)
                    + jnp.where(lane == i2, p_g * (p2 / tot), 0.0)
                    + jnp.where(lane == N_EXPERTS, g_sel.astype(F32), 0.0))


def _outproj(x, attn, pool, wo_a, wo_p, gate1, g_ffn, scale2, shift2, w_r, b_r):
    b, s, d = x.shape
    tm = TM_PROJ
    tok = lambda w: pl.BlockSpec((1, tm, w), lambda bi, si: (bi, si, 0))
    per_b = pl.BlockSpec((1, 1, d), lambda bi, si: (bi, 0, 0))
    full = lambda a: pl.BlockSpec(a.shape, lambda bi, si: (0,) * a.ndim)
    return pl.pallas_call(
        _outproj_kernel,
        out_shape=(jax.ShapeDtypeStruct((b, s, d), F32),
                   jax.ShapeDtypeStruct((b, s, d), BF16),
                   jax.ShapeDtypeStruct((b, s, LANES), F32)),
        grid=(b, s // tm),
        in_specs=[tok(d), tok(attn.shape[2]), tok(pool.shape[2]), full(wo_a), full(wo_p), per_b,
                  full(g_ffn), per_b, per_b, full(w_r), full(b_r)],
        out_specs=(tok(d), tok(d), tok(LANES)),
        compiler_params=_cparams(("arbitrary", "arbitrary")),
        name="outproj",
    )(x, attn, pool, wo_a, wo_p, gate1, g_ffn, scale2, shift2, w_r, b_r)


def _moe_kernel(x1_ref, h2_ref, gates_ref, gate2_ref, wgu_ref, wd_ref, o_ref,
                xe_ref, rank_ref, rank_t_ref):
    g = pl.program_id(2)
    tm, d = h2_ref.shape[1], h2_ref.shape[2]
    n_e, d_exp = wd_ref.shape[0], wd_ref.shape[1]
    lane = lax.broadcasted_iota(jnp.int32, (tm, LANES), 1)
    gf = g.astype(F32)

    @pl.when(g == 0)
    def _():
        tri = (lax.broadcasted_iota(jnp.int32, (RANK_BLK, RANK_BLK), 0)
               >= lax.broadcasted_iota(jnp.int32, (RANK_BLK, RANK_BLK), 1)).astype(BF16)
        lane_b = lax.broadcasted_iota(jnp.int32, (RANK_BLK, LANES), 1)
        run = jnp.zeros((1, LANES), F32)
        for sb in range(tm // RANK_BLK):
            rows = slice(sb * RANK_BLK, (sb + 1) * RANK_BLK)
            gts_b = gates_ref[0, rows, :]
            grp = jnp.sum(jnp.where(lane_b == N_EXPERTS, gts_b, 0.0), axis=-1, keepdims=True)
            member = jnp.where(jnp.logical_and(lane_b < N_GROUPS, lane_b.astype(F32) == grp), 1.0, 0.0)
            pre = jnp.dot(tri, member.astype(BF16), preferred_element_type=F32) + run
            rank_ref[rows, :] = jnp.where(lane_b == N_GROUPS, grp, pre)
            run = jnp.max(pre, axis=0, keepdims=True)
        rank_t_ref[...] = rank_ref[...].T
        gts = gates_ref[0]
        g_hi = gts.astype(BF16)
        xe_ref[:, :d] = h2_ref[0]
        xe_ref[:, d:d + LANES] = g_hi
        xe_ref[:, d + LANES:d + 2 * LANES] = (gts - g_hi.astype(F32)).astype(BF16)
        o_ref[0] = jnp.zeros((tm, d), F32)

    rank_row = rank_t_ref[pl.ds(g, 1), :]
    pos_row = jnp.where(rank_t_ref[N_GROUPS:N_GROUPS + 1, :] == gf, rank_row - 1.0, -1.0)
    rk = rank_ref[...]
    rank_col = jnp.sum(jnp.where(lane == g, rk, 0.0), axis=-1, keepdims=True)
    grp_col = jnp.sum(jnp.where(lane == N_GROUPS, rk, 0.0), axis=-1, keepdims=True)
    pos_col = jnp.where(grp_col == gf, rank_col - 1.0, -1.0)
    n_rows = jnp.max(rank_row).astype(jnp.int32)
    row_id = lax.broadcasted_iota(jnp.int32, (MOE_CH, tm), 0).astype(F32)
    col_id = lax.broadcasted_iota(jnp.int32, (tm, MOE_CH), 1).astype(F32)
    lane_c = lax.broadcasted_iota(jnp.int32, (MOE_CH, LANES), 1)

    def chunk_body(c, _):
        r0 = (c * MOE_CH).astype(F32)
        gather = jnp.where(pos_row - r0 == row_id, 1.0, 0.0).astype(BF16)
        xg = jnp.dot(gather, xe_ref[...], preferred_element_type=F32)
        xb = xg[:, :d].astype(BF16)
        gates_c = xg[:, d:d + LANES] + xg[:, d + LANES:d + 2 * LANES]
        ya = None
        for e in range(n_e):
            gu = jnp.dot(xb, wgu_ref[e], preferred_element_type=F32)
            gt = gu[:, :d_exp]
            a = (gt * jax.nn.sigmoid(gt)) * gu[:, d_exp:]
            gate_e = jnp.sum(jnp.where(lane_c == g * n_e + e, gates_c, 0.0), axis=-1, keepdims=True)
            y = jnp.dot((a * gate_e).astype(BF16), wd_ref[e], preferred_element_type=F32)
            ya = y if ya is None else ya + y
        scatter = jnp.where(pos_col - r0 == col_id, 1.0, 0.0).astype(BF16)
        o_ref[0] += jnp.dot(scatter, ya.astype(BF16), preferred_element_type=F32)
        return 0

    lax.fori_loop(0, (n_rows + MOE_CH - 1) // MOE_CH, chunk_body, 0)

    @pl.when(g == pl.num_programs(2) - 1)
    def _():
        o_ref[0] = x1_ref[0] + gate2_ref[0] * o_ref[0]


def _moe(x1, h2, gates, gate2, w_gu, w_d):
    b, s, d = x1.shape
    tm = TM_MOE
    n_e = EXPERTS_PER_GROUP
    assert w_gu.shape[0] == N_GROUPS * n_e and tm % RANK_BLK == 0
    tok = lambda w: pl.BlockSpec((1, tm, w), lambda bi, si, g: (bi, si, 0))
    return pl.pallas_call(
        _moe_kernel,
        out_shape=jax.ShapeDtypeStruct((b, s, d), F32),
        grid=(b, s // tm, N_GROUPS),
        in_specs=[tok(d), tok(d), tok(LANES),
                  pl.BlockSpec((1, 1, d), lambda bi, si, g: (bi, 0, 0)),
                  pl.BlockSpec((n_e,) + w_gu.shape[1:], lambda bi, si, g: (g, 0, 0)),
                  pl.BlockSpec((n_e,) + w_d.shape[1:], lambda bi, si, g: (g, 0, 0))],
        out_specs=tok(d),
        scratch_shapes=[pltpu.VMEM((tm, d + 2 * LANES), BF16),
                        pltpu.VMEM((tm, LANES), F32),
                        pltpu.VMEM((LANES, tm), F32)],
        compiler_params=pltpu.CompilerParams(
            dimension_semantics=("arbitrary", "arbitrary", "arbitrary"),
            vmem_limit_bytes=VMEM_LIMIT_MOE_BYTES),
        name="moe",
    )(x1, h2, gates, gate2, w_gu, w_d)


def _layer(x, mod, pos3, g_mix, g_ffn, w_in, g_q, g_k, g_kidx, w_pool, pool_scale, w_out,
           w_rg, b_rg, w_re, b_re, w_gate, w_up, w_down):
    b, s, d = x.shape
    d_attn = N_HEADS * HEAD_DIM
    nqb = s // Q_BLK
    nkb = s // K_BLK
    shift1, scale1, gate1, shift2, scale2, gate2 = [m[:, None, :] for m in jnp.split(mod, 6, axis=-1)]

    n_front = d_attn + 2 * HEAD_DIM + N_IDX_HEADS * IDX_DIM + IDX_DIM + N_IDX_HEADS
    pad = (-n_front) % LANES
    w_in_p = jnp.concatenate([w_in[:, :n_front], jnp.zeros((d, pad), w_in.dtype), w_in[:, n_front:]],
                             axis=1).astype(BF16)
    seg_id = jnp.arange(d_attn) // HEAD_DIM
    segsum = (seg_id[:, None] == seg_id[None, :]).astype(BF16)
    ones_half = jnp.ones((LANES - HEAD_DIM,), F32)
    gq_t = jnp.tile(g_q, N_HEADS)[None, :]
    gk_e = jnp.concatenate([g_k, ones_half])[None, :]
    gkidx_e = jnp.concatenate([g_kidx, ones_half])[None, :]
    half = HEAD_DIM // 2
    inv_freq = ROPE_THETA ** (-jnp.arange(0, HEAD_DIM, 2, dtype=F32) / HEAD_DIM)
    invf = jnp.tile(inv_freq, LANES // half)[None, :]

    qt, kv, qit, ki, w_t, vt4, pool = _inproj(pos3, x, scale1, shift1, g_mix[None, :], w_in_p, segsum,
                                              gq_t, gk_e, gkidx_e, invf, w_pool.astype(BF16),
                                              pool_scale[None, :])
    kv4 = kv.reshape(b, s // CNT_BLK, CNT_BLK, LANES)
    ki4 = ki.reshape(b, s // CNT_BLK, CNT_BLK, LANES)
    attn = _dsa(qt, qit, w_t, kv4, ki4, vt4)

    w_out_b = w_out.astype(BF16)
    w_r = jnp.concatenate([w_re, w_rg, jnp.zeros((d, LANES - N_EXPERTS - N_GROUPS), F32)], axis=1)
    b_r = jnp.concatenate([b_re, b_rg, jnp.zeros((LANES - N_EXPERTS - N_GROUPS,), F32)])[None, :]
    x1, h2, gates = _outproj(x, attn, pool, w_out_b[:d_attn], w_out_b[d_attn:], gate1,
                             g_ffn[None, :], scale2, shift2, w_r, b_r)

    w_gu = jnp.concatenate([w_gate, w_up], axis=-1).astype(BF16)
    return _moe(x1, h2, gates, gate2, w_gu, w_down.astype(BF16))


def kernel(x, c, positions, w_ada, b_ada, g_norm_mix, g_norm_ffn, w_in, g_q, g_k, g_kidx, w_pool,
           pool_scale, w_out, w_router_group, b_router_group, w_router_expert, b_router_expert,
           w_gate, w_up, w_down):
    b, s, d = x.shape
    depth = w_ada.shape[0]
    assert s % TM_MOE == 0 and s % K_BLK == 0 and d % LANES == 0
    pos3 = positions[:, :, None]
    c_pad = jnp.concatenate([c, jnp.zeros((-b % SUBLANES, d), c.dtype)], axis=0)
    for l in range(depth):
        mod = _adaln(c_pad, w_ada[l], b_ada[l][None, :])[:b]
        x = _layer(x, mod, pos3, g_norm_mix[l], g_norm_ffn[l], w_in[l], g_q[l], g_k[l], g_kidx[l],
                   w_pool[l], pool_scale[l], w_out[l], w_router_group[l], b_router_group[l],
                   w_router_expert[l], b_router_expert[l], w_gate[l], w_up[l], w_down[l])
    return x
```

```python
import functools

import jax
import jax.numpy as jnp
from jax import lax
from jax.experimental import pallas as pl
from jax.experimental.pallas import tpu as pltpu

N_HEADS = 8
HEAD_DIM = 64
N_IDX_HEADS = 8
IDX_DIM = 64
TOPK_MAX = 256
ROPE_THETA = 10000.0
POOL_WINDOWS = (2, 4, 8, 16)
N_GROUPS = 4
EXPERTS_PER_GROUP = 8
N_EXPERTS = N_GROUPS * EXPERTS_PER_GROUP
EPS = 1e-6

LANES = 128
SUBLANES = 8
VMEM_LIMIT_BYTES = 56 * 1024 * 1024
VMEM_LIMIT_MOE_BYTES = 60 * 1024 * 1024

Q_BLK = 256
K_BLK = 256
COL_BLK = 256
CNT_BLK = 512
CNT_ROWS = 32
SEARCH_FIRST = 15
SEARCH_PERIOD = 2
TM_PROJ = 512
TM_MOE = 1024
MOE_CH = 320
RANK_BLK = 256
MAX_WIN = max(POOL_WINDOWS)
M_INIT = -1e29
MASKED = -1e30
F32_LOWEST = -3.0e38
LOG2_E = 1.4426950408889634

BF16 = jnp.bfloat16
F32 = jnp.float32


def _cparams(sem):
    return pltpu.CompilerParams(dimension_semantics=sem, vmem_limit_bytes=VMEM_LIMIT_BYTES)


def _adaln_kernel(c_ref, w_ref, b_ref, o_ref):
    c = c_ref[...]
    c_act = c * jax.nn.sigmoid(c)
    o_ref[...] = jnp.dot(c_act, w_ref[...], preferred_element_type=F32) + b_ref[...]


def _adaln(c_pad, w_ada, b_ada):
    rows, d = c_pad.shape
    n = w_ada.shape[1]
    tn = n // 6
    return pl.pallas_call(
        _adaln_kernel,
        out_shape=jax.ShapeDtypeStruct((rows, n), F32),
        grid=(n // tn,),
        in_specs=[pl.BlockSpec((rows, d), lambda j: (0, 0)),
                  pl.BlockSpec((d, tn), lambda j: (0, j)),
                  pl.BlockSpec((1, tn), lambda j: (0, j))],
        out_specs=pl.BlockSpec((rows, tn), lambda j: (0, j)),
        compiler_params=_cparams(("arbitrary",)),
        name="adaln",
    )(c_pad, w_ada, b_ada)


def _rope_chunk(y, cos, sin_signed, first_half):
    from_hi = pltpu.roll(y, LANES - HEAD_DIM // 2, 1)
    from_lo = pltpu.roll(y, HEAD_DIM // 2, 1)
    return y * cos + jnp.where(first_half, from_hi, from_lo) * sin_signed


def _inproj_kernel(pos_ref, x_ref, scale_ref, shift_ref, gmix_ref, win_ref, segsum_ref,
                   gq_ref, gk_ref, gkidx_ref, invf_ref, wpool_ref, pscale_ref,
                   qt_ref, kv_ref, qit_ref, ki_ref, wt_ref, vt_ref, pool_ref, ubuf_ref):
    tm = x_ref.shape[1]
    d_attn = N_HEADS * HEAD_DIM
    d_qidx = N_IDX_HEADS * IDX_DIM
    s_tile = pl.program_id(1)

    def store_cols(dst_ref, chunk, j):
        for t in range(tm // Q_BLK):
            ct = chunk[t * Q_BLK:(t + 1) * Q_BLK, :].T
            for hh in range(2):
                col = (2 * j + hh) * Q_BLK
                dst_ref[0, t, :, col:col + Q_BLK] = ct[hh * HEAD_DIM:(hh + 1) * HEAD_DIM, :].astype(dst_ref.dtype)

    x = x_ref[0]
    ms = jnp.mean(x * x, axis=-1, keepdims=True)
    h = (x * lax.rsqrt(ms + EPS) * gmix_ref[...]) * (1.0 + scale_ref[0]) + shift_ref[0]
    proj = jnp.dot(h.astype(BF16), win_ref[...], preferred_element_type=F32)

    lane = lax.broadcasted_iota(jnp.int32, (tm, LANES), 1)
    first_half = (lane & (HEAD_DIM - 1)) < (HEAD_DIM // 2)
    ang = pos_ref[0].astype(F32) * invf_ref[...]
    cos = jnp.cos(ang)
    sin = jnp.sin(ang)
    sin_signed = jnp.where(first_half, -sin, sin)
    rope = functools.partial(_rope_chunk, cos=cos, sin_signed=sin_signed, first_half=first_half)

    qf = proj[:, :d_attn]
    qsq = qf * qf
    qsq_hi = qsq.astype(BF16)
    qsq_lo = (qsq - qsq_hi.astype(F32)).astype(BF16)
    seg = segsum_ref[...]
    ssq = (jnp.dot(qsq_hi, seg, preferred_element_type=F32)
           + jnp.dot(qsq_lo, seg, preferred_element_type=F32))
    qn = qf * lax.rsqrt(ssq * (1.0 / HEAD_DIM) + EPS) * gq_ref[...]
    for j in range(d_attn // LANES):
        sl = slice(j * LANES, (j + 1) * LANES)
        store_cols(qt_ref, rope(qn[:, sl]) * (LOG2_E * HEAD_DIM ** -0.5), j)

    kvc = proj[:, d_attn:d_attn + LANES]
    is_k = lane < HEAD_DIM
    ksq = jnp.sum(jnp.where(is_k, kvc * kvc, 0.0), axis=-1, keepdims=True)
    kn = kvc * lax.rsqrt(ksq * (1.0 / HEAD_DIM) + EPS) * gk_ref[...]
    kv_ref[0] = jnp.where(is_k, rope(kn), kvc).astype(BF16)
    row8 = lax.broadcasted_iota(jnp.int32, (SUBLANES, K_BLK), 0)
    for t in range(tm // K_BLK):
        vt_ref[0, t, 0:HEAD_DIM, :] = kvc[t * K_BLK:(t + 1) * K_BLK, :].T[HEAD_DIM:, :].astype(BF16)
        vt_ref[0, t, HEAD_DIM:HEAD_DIM + SUBLANES, :] = jnp.where(row8 == 0, 1.0, 0.0).astype(BF16)

    o_qi = d_attn + LANES
    for j in range(d_qidx // LANES):
        store_cols(qit_ref, rope(proj[:, o_qi + j * LANES:o_qi + (j + 1) * LANES]), j)

    o_ki = o_qi + d_qidx
    kic = proj[:, o_ki:o_ki + LANES]
    kisq = jnp.sum(jnp.where(is_k, kic * kic, 0.0), axis=-1, keepdims=True)
    kin = kic * lax.rsqrt(kisq * (1.0 / IDX_DIM) + EPS) * gkidx_ref[...]
    ki_ref[0] = jnp.where(is_k, rope(kin), 0.0).astype(BF16)
    for t in range(tm // Q_BLK):
        wt_ref[0, t] = kic[t * Q_BLK:(t + 1) * Q_BLK, :].T[IDX_DIM:IDX_DIM + N_IDX_HEADS, :] * (
            N_IDX_HEADS ** -0.5 * IDX_DIM ** -0.5)

    o_u = o_ki + LANES
    u = proj[:, o_u:o_u + LANES * len(POOL_WINDOWS)]

    @pl.when(s_tile == 0)
    def _():
        ubuf_ref[0:MAX_WIN, :] = jnp.zeros((MAX_WIN, u.shape[1]), F32)

    @pl.when(s_tile != 0)
    def _():
        ubuf_ref[0:MAX_WIN, :] = ubuf_ref[tm:tm + MAX_WIN, :]

    ubuf_ref[MAX_WIN:MAX_WIN + tm, :] = u
    t_idx = s_tile * tm + lax.broadcasted_iota(jnp.int32, (tm, 1), 0)
    for g, win in enumerate(POOL_WINDOWS):
        sl = slice(g * LANES, (g + 1) * LANES)
        wsum = u[:, sl]
        for j in range(1, win):
            wsum = wsum + ubuf_ref[MAX_WIN - j:MAX_WIN - j + tm, sl]
        cnt = jnp.minimum(t_idx + 1, win).astype(F32)
        pooled = wsum / cnt - u[:, sl]
        mixed = jnp.dot(pooled.astype(BF16), wpool_ref[g], preferred_element_type=F32)
        pool_ref[0, :, sl] = (mixed * pscale_ref[:, sl]).astype(BF16)


def _inproj(pos3, x, scale1, shift1, g_mix, w_in_p, segsum, gq_t, gk_e, gkidx_e, invf, w_pool, pscale):
    b, s, d = x.shape
    tm = TM_PROJ
    d_attn = N_HEADS * HEAD_DIM
    d_qidx = N_IDX_HEADS * IDX_DIM
    d_pool = LANES * len(POOL_WINDOWS)
    assert tm % Q_BLK == 0 and tm % K_BLK == 0 and HEAD_DIM == IDX_DIM and 2 * HEAD_DIM == LANES
    tok = lambda w: pl.BlockSpec((1, tm, w), lambda bi, si: (bi, si, 0))
    blk = lambda n, r, c: pl.BlockSpec((1, tm // n, r, c), lambda bi, si: (bi, si, 0, 0))
    per_b = pl.BlockSpec((1, 1, d), lambda bi, si: (bi, 0, 0))
    full = lambda a: pl.BlockSpec(a.shape, lambda bi, si: (0,) * a.ndim)
    nqb, nkb = s // Q_BLK, s // K_BLK
    return pl.pallas_call(
        _inproj_kernel,
        out_shape=(jax.ShapeDtypeStruct((b, nqb, HEAD_DIM, N_HEADS * Q_BLK), BF16),
                   jax.ShapeDtypeStruct((b, s, LANES), BF16),
                   jax.ShapeDtypeStruct((b, nqb, IDX_DIM, N_IDX_HEADS * Q_BLK), BF16),
                   jax.ShapeDtypeStruct((b, s, LANES), BF16),
                   jax.ShapeDtypeStruct((b, nqb, N_IDX_HEADS, Q_BLK), F32),
                   jax.ShapeDtypeStruct((b, nkb, HEAD_DIM + SUBLANES, K_BLK), BF16),
                   jax.ShapeDtypeStruct((b, s, d_pool), BF16)),
        grid=(b, s // tm),
        in_specs=[tok(1), tok(d), per_b, per_b, full(g_mix), full(w_in_p), full(segsum),
                  full(gq_t), full(gk_e), full(gkidx_e), full(invf), full(w_pool), full(pscale)],
        out_specs=(blk(Q_BLK, HEAD_DIM, N_HEADS * Q_BLK), tok(LANES),
                   blk(Q_BLK, IDX_DIM, N_IDX_HEADS * Q_BLK), tok(LANES),
                   blk(Q_BLK, N_IDX_HEADS, Q_BLK), blk(K_BLK, HEAD_DIM + SUBLANES, K_BLK), tok(d_pool)),
        scratch_shapes=[pltpu.VMEM((tm + 2 * MAX_WIN, d_pool), F32)],
        compiler_params=_cparams(("arbitrary", "arbitrary")),
        name="inproj",
    )(pos3, x, scale1, shift1, g_mix, w_in_p, segsum, gq_t, gk_e, gkidx_e, invf, w_pool, pscale)


def _dsa_kernel(qt_ref, qit_ref, w_ref, kv_ref, ki_ref, vt_ref, o_ref,
                sc_ref, qe_ref, qie_ref, m_ref, mx_ref, st_ref, acc_ref, lg_ref, p_ref):
    topk = float(min(TOPK_MAX, (sc_ref.shape[0] * CNT_BLK) // 4))
    qb = pl.program_id(1)
    n_cols = qt_ref.shape[3]
    n_chunks = n_cols // COL_BLK
    sub = CNT_BLK // K_BLK
    nch = ((qb + 1) * Q_BLK + CNT_BLK - 1) // CNT_BLK
    kgrp = K_BLK // SUBLANES
    sub_rows = [slice(j * K_BLK, (j + 1) * K_BLK) for j in range(sub)]

    zeros_half = jnp.zeros((LANES - HEAD_DIM, n_cols), BF16)
    qe_ref[0:HEAD_DIM, :] = qt_ref[0, 0]
    qe_ref[HEAD_DIM:LANES, :] = zeros_half
    qie_ref[0:IDX_DIM, :] = qit_ref[0, 0]
    qie_ref[IDX_DIM:LANES, :] = zeros_half

    q_pos = qb * Q_BLK + lax.broadcasted_iota(jnp.int32, (K_BLK, Q_BLK), 1)
    key_off = lax.broadcasted_iota(jnp.int32, (K_BLK, Q_BLK), 0)

    def score_step(ch, carry):
        rmax, rmin = carry
        for j in range(sub):
            ki_blk = ki_ref[0, ch, sub_rows[j], :]
            score = None
            for cc in range(n_chunks):
                cs = slice(cc * COL_BLK, (cc + 1) * COL_BLK)
                s_h = jnp.dot(ki_blk, qie_ref[:, cs], preferred_element_type=F32)
                s_h = jnp.maximum(s_h, 0.0)
                for hh in range(COL_BLK // Q_BLK):
                    head = cc * (COL_BLK // Q_BLK) + hh
                    part = s_h[:, hh * Q_BLK:(hh + 1) * Q_BLK] * w_ref[0, 0, head:head + 1, :]
                    score = part if score is None else score + part
            causal = (ch * CNT_BLK + j * K_BLK + key_off) <= q_pos
            masked = jnp.where(causal, score, -jnp.inf)
            sc_ref[ch, sub_rows[j], :] = masked
            hi_part = masked.reshape(kgrp, SUBLANES, Q_BLK).max(axis=0)
            lo_part = jnp.where(causal, score, jnp.inf).reshape(kgrp, SUBLANES, Q_BLK).min(axis=0)
            rmax, rmin = jnp.maximum(rmax, hi_part), jnp.minimum(rmin, lo_part)
        return rmax, rmin

    def score_body(i, carry):
        return score_step(2 * i + 1, score_step(2 * i, carry))

    stats = lax.fori_loop(
        0, nch // 2, score_body,
        (jnp.full((SUBLANES, Q_BLK), -jnp.inf, F32), jnp.full((SUBLANES, Q_BLK), jnp.inf, F32)))
    rmax8, rmin8 = lax.cond(nch % 2 == 1, lambda c: score_step(nch - 1, c), lambda c: c, stats)
    rowmax = jnp.max(rmax8, axis=0, keepdims=True)
    rowmin = jnp.min(rmin8, axis=0, keepdims=True)

    n_causal = (qb * Q_BLK + 1 + lax.broadcasted_iota(jnp.int32, (1, Q_BLK), 1)).astype(F32)
    kt = jnp.minimum(n_causal, topk)

    cgrp = CNT_BLK // CNT_ROWS

    def count_ge(t):
        def body(ch, acc):
            for r in range(cgrp):
                rows = sc_ref[ch, r * CNT_ROWS:(r + 1) * CNT_ROWS, :]
                acc = acc + jnp.where(rows >= t, 1.0, 0.0)
            return acc
        acc = lax.fori_loop(0, nch, body, jnp.zeros((CNT_ROWS, Q_BLK), F32))
        return jnp.sum(acc, axis=0, keepdims=True)

    def bisect_pass(state):
        lo, hi, top, c_lo, c_hi, thr, done = state
        cap = jnp.minimum(hi, top)
        mid = lo + 0.5 * (cap - lo)
        mid = jnp.where(mid <= lo, cap, mid)
        c = count_ge(mid)
        hit = jnp.logical_and(done == 0.0, c == kt)
        thr = jnp.where(hit, mid, thr)
        done = jnp.where(hit, 1.0, done)
        active = done == 0.0
        up = jnp.logical_and(active, c >= kt)
        down = jnp.logical_and(active, c < kt)
        return (jnp.where(up, mid, lo), jnp.where(down, mid, hi), jnp.where(down, jnp.inf, top),
                jnp.where(up, c, c_lo), jnp.where(down, c, c_hi), thr, done)

    def snap_pass(state):
        lo, hi, top, c_lo, c_hi, thr, done = state

        def body(ch, carry):
            a8, b8 = carry
            for r in range(cgrp):
                s = sc_ref[ch, r * CNT_ROWS:(r + 1) * CNT_ROWS, :]
                a8 = jnp.minimum(a8, jnp.where(s >= lo, s, jnp.inf))
                b8 = jnp.maximum(b8, jnp.where(s < hi, s, -jnp.inf))
            return a8, b8

        a8, b8 = lax.fori_loop(
            0, nch, body,
            (jnp.full((CNT_ROWS, Q_BLK), jnp.inf, F32), jnp.full((CNT_ROWS, Q_BLK), -jnp.inf, F32)))
        a = jnp.min(a8, axis=0, keepdims=True)
        b = jnp.max(b8, axis=0, keepdims=True)
        active = done == 0.0
        hit = jnp.logical_and(active, a == b)
        thr = jnp.where(hit, a, thr)
        done = jnp.where(hit, 2.0, done)
        return jnp.where(active, a, lo), hi, jnp.where(active, b, top), c_lo, c_hi, thr, done

    few = n_causal <= topk
    state0 = (rowmin, jnp.full((1, Q_BLK), jnp.inf, F32), rowmax, n_causal,
              jnp.zeros((1, Q_BLK), F32), jnp.where(few, F32_LOWEST, 0.0), jnp.where(few, 1.0, 0.0))

    def outer_cond(carry):
        return carry[1] > 0.0

    def outer_body(carry):
        state, _ = carry
        state = lax.fori_loop(0, SEARCH_PERIOD, lambda i, st: bisect_pass(st), state)
        state = snap_pass(state)
        pending = jnp.max(jnp.where(state[6] == 0.0, 1.0, 0.0))
        return state, pending

    state1 = lax.fori_loop(0, SEARCH_FIRST, lambda i, st: bisect_pass(st), state0)
    state1 = snap_pass(state1)
    pending1 = jnp.max(jnp.where(state1[6] == 0.0, 1.0, 0.0))
    (lo, hi, _, c_lo, c_hi, thr, done), _ = lax.while_loop(outer_cond, outer_body, (state1, pending1))

    excess = jnp.where(done == 2.0, c_lo - kt, 0.0)
    need = kt - c_hi

    @pl.when(jnp.max(excess) > 0.0)
    def _():
        tri = (lax.broadcasted_iota(jnp.int32, (K_BLK, K_BLK), 0)
               >= lax.broadcasted_iota(jnp.int32, (K_BLK, K_BLK), 1)).astype(BF16)
        has_excess = excess > 0.0

        def drop_step(ch, run):
            for j in range(sub):
                s = sc_ref[ch, sub_rows[j], :]
                tied = jnp.logical_and(s == thr, has_excess)
                prefix = jnp.dot(tri, jnp.where(tied, 1.0, 0.0).astype(BF16), preferred_element_type=F32)
                drop = jnp.logical_and(tied, run + prefix > need)
                sc_ref[ch, sub_rows[j], :] = jnp.where(drop, -jnp.inf, s)
                run = run + jnp.max(prefix, axis=0, keepdims=True)
            return run

        run = lax.fori_loop(0, nch // 2, lambda i, r: drop_step(2 * i + 1, drop_step(2 * i, r)),
                            jnp.zeros((1, Q_BLK), F32))

        @pl.when(nch % 2 == 1)
        def _():
            drop_step(nch - 1, run)

    m_ref[...] = jnp.full(m_ref.shape, M_INIT, F32)
    acc_ref[...] = jnp.zeros(acc_ref.shape, F32)
    row_m = lambda j: slice(j, j + 1)
    row_a = lambda j: slice(sub + j, sub + j + 1)

    def logits_stage(ch, j):
        kv_blk = kv_ref[0, ch, sub_rows[j], :]
        bias = jnp.where(sc_ref[ch, sub_rows[j], :] >= thr, 0.0, MASKED)
        for cc in range(n_chunks):
            logits = jnp.dot(kv_blk, qe_ref[:, cc * COL_BLK:(cc + 1) * COL_BLK],
                             preferred_element_type=F32)
            for hh in range(COL_BLK // Q_BLK):
                cs = slice(cc * COL_BLK + hh * Q_BLK, cc * COL_BLK + (hh + 1) * Q_BLK)
                lg = logits[:, hh * Q_BLK:(hh + 1) * Q_BLK] + bias
                lg_ref[j, :, cs] = lg
                mx_ref[:, cs] = lg.reshape(kgrp, SUBLANES, Q_BLK).max(axis=0)
        m_old = m_ref[...]
        m_new = jnp.maximum(m_old, jnp.max(mx_ref[...], axis=0, keepdims=True))
        st_ref[row_m(j), :] = m_new
        st_ref[row_a(j), :] = jnp.exp2(m_old - m_new)
        m_ref[...] = m_new

    def probs_stage(j):
        p_ref[j] = jnp.exp2(lg_ref[j] - st_ref[row_m(j), :]).astype(BF16)

    def value_stage(kb, j, alpha):
        acc_ref[...] = acc_ref[...] * alpha + jnp.dot(
            vt_ref[0, kb], p_ref[j], preferred_element_type=F32)

    p_ref[sub - 1] = jnp.zeros(p_ref.shape[1:], BF16)
    st_ref[row_a(sub - 1), :] = jnp.ones((1, n_cols), F32)
    logits_stage(0, 0)

    def attn_body(ch, _):
        alpha_prev = st_ref[row_a(1), :]
        logits_stage(ch, 1)
        value_stage(jnp.maximum(ch * sub - 1, 0), 1, alpha_prev)
        probs_stage(0)
        alpha_cur = st_ref[row_a(0), :]
        logits_stage(jnp.minimum(ch + 1, nch - 1), 0)
        value_stage(ch * sub, 0, alpha_cur)
        probs_stage(1)
        return 0

    lax.fori_loop(0, nch, attn_body, 0)
    value_stage(nch * sub - 1, 1, st_ref[row_a(1), :])
    dh = qt_ref.shape[2]
    inv_l = 1.0 / acc_ref[dh:dh + 1, :]
    for j in range(n_cols // Q_BLK // 2):
        pair = [acc_ref[0:dh, (2 * j + hh) * Q_BLK:(2 * j + hh + 1) * Q_BLK]
                * inv_l[:, (2 * j + hh) * Q_BLK:(2 * j + hh + 1) * Q_BLK] for hh in range(2)]
        o_ref[0, :, j * 2 * dh:(j + 1) * 2 * dh] = jnp.concatenate(pair, axis=0).T.astype(o_ref.dtype)


def _dsa(qt, qit, w_t, kv4, ki4, vt4):
    b, nqb, dh, n_cols = qt.shape
    n_steps = kv4.shape[1]
    assert kv4.shape[2] == CNT_BLK and CNT_BLK == 2 * K_BLK and n_cols % COL_BLK == 0
    assert vt4.shape[1] * K_BLK == n_steps * CNT_BLK and vt4.shape[2] == dh + SUBLANES
    per_q = lambda a: pl.BlockSpec((1, 1) + a.shape[2:], lambda bi, qi: (bi, qi, 0, 0))
    per_b = lambda a: pl.BlockSpec((1,) + a.shape[1:], lambda bi, qi: (bi, 0, 0, 0))
    return pl.pallas_call(
        _dsa_kernel,
        out_shape=jax.ShapeDtypeStruct((b, nqb * Q_BLK, (n_cols // Q_BLK) * dh), BF16),
        grid=(b, nqb),
        in_specs=[per_q(qt), per_q(qit), per_q(w_t), per_b(kv4), per_b(ki4), per_b(vt4)],
        out_specs=pl.BlockSpec((1, Q_BLK, (n_cols // Q_BLK) * dh), lambda bi, qi: (bi, qi, 0)),
        scratch_shapes=[pltpu.VMEM((n_steps, CNT_BLK, Q_BLK), F32),
                        pltpu.VMEM((LANES, n_cols), BF16),
                        pltpu.VMEM((LANES, n_cols), BF16),
                        pltpu.VMEM((1, n_cols), F32),
                        pltpu.VMEM((SUBLANES, n_cols), F32),
                        pltpu.VMEM((SUBLANES, n_cols), F32),
                        pltpu.VMEM((dh + SUBLANES, n_cols), F32),
                        pltpu.VMEM((CNT_BLK // K_BLK, K_BLK, n_cols), F32),
                        pltpu.VMEM((CNT_BLK // K_BLK, K_BLK, n_cols), BF16)],
        compiler_params=_cparams(("arbitrary", "arbitrary")),
        name="dsa",
    )(qt, qit, w_t, kv4, ki4, vt4)


def _outproj_kernel(x_ref, attn_ref, pool_ref, woa_ref, wop_ref, gate1_ref, gffn_ref,
                    scale2_ref, shift2_ref, wr_ref, br_ref, x1_ref, h2_ref, gates_ref):
    tm = x_ref.shape[1]
    mix = (jnp.dot(attn_ref[0], woa_ref[...], preferred_element_type=F32)
           + jnp.dot(pool_ref[0], wop_ref[...], preferred_element_type=F32))
    x1 = x_ref[0] + gate1_ref[0] * mix
    x1_ref[0] = x1
    ms = jnp.mean(x1 * x1, axis=-1, keepdims=True)
    h2 = (x1 * lax.rsqrt(ms + EPS) * gffn_ref[...]) * (1.0 + scale2_ref[0]) + shift2_ref[0]
    h2_hi = h2.astype(BF16)
    h2_ref[0] = h2_hi

    h2_lo = (h2 - h2_hi.astype(F32)).astype(BF16)
    wr = wr_ref[...]
    wr_hi = wr.astype(BF16)
    wr_lo = (wr - wr_hi.astype(F32)).astype(BF16)
    logits = (jnp.dot(h2_hi, wr_hi, preferred_element_type=F32)
              + jnp.dot(h2_lo, wr_hi, preferred_element_type=F32)
              + jnp.dot(h2_hi, wr_lo, preferred_element_type=F32)) + br_ref[...]

    lane = lax.broadcasted_iota(jnp.int32, (tm, LANES), 1)
    big = jnp.int32(LANES)
    is_g = jnp.logical_and(lane >= N_EXPERTS, lane < N_EXPERTS + N_GROUPS)
    glog = jnp.where(is_g, logits, -jnp.inf)
    gmax = jnp.max(glog, axis=-1, keepdims=True)
    gsum = jnp.sum(jnp.exp(glog - gmax), axis=-1, keepdims=True)
    p_g = 1.0 / gsum
    g_sel = jnp.min(jnp.where(glog == gmax, lane, big), axis=-1, keepdims=True) - N_EXPERTS
    in_grp = jnp.logical_and(lane < N_EXPERTS, jnp.right_shift(lane, 3) == g_sel)
    elog = jnp.where(in_grp, logits, -jnp.inf)
    emax = jnp.max(elog, axis=-1, keepdims=True)
    eexp = jnp.exp(elog - emax)
    esum = jnp.sum(eexp, axis=-1, keepdims=True)
    p_e = jnp.where(in_grp, eexp / esum, -1.0)
    p1 = jnp.max(p_e, axis=-1, keepdims=True)
    i1 = jnp.min(jnp.where(p_e == p1, lane, big), axis=-1, keepdims=True)
    p_e2 = jnp.where(lane == i1, -1.0, p_e)
    p2 = jnp.max(p_e2, axis=-1, keepdims=True)
    i2 = jnp.min(jnp.where(p_e2 == p2, lane, big), axis=-1, keepdims=True)
    tot = p1 + p2
    gates_ref[0] = (jnp.where(lane == i1, p_g * (p1 / tot), 0.0)
                    + jnp.where(lane == i2, p_g * (p2 / tot), 0.0)
                    + jnp.where(lane == N_EXPERTS, g_sel.astype(F32), 0.0))


def _outproj(x, attn, pool, wo_a, wo_p, gate1, g_ffn, scale2, shift2, w_r, b_r):
    b, s, d = x.shape
    tm = TM_PROJ
    tok = lambda w: pl.BlockSpec((1, tm, w), lambda bi, si: (bi, si, 0))
    per_b = pl.BlockSpec((1, 1, d), lambda bi, si: (bi, 0, 0))
    full = lambda a: pl.BlockSpec(a.shape, lambda bi, si: (0,) * a.ndim)
    return pl.pallas_call(
        _outproj_kernel,
        out_shape=(jax.ShapeDtypeStruct((b, s, d), F32),
                   jax.ShapeDtypeStruct((b, s, d), BF16),
                   jax.ShapeDtypeStruct((b, s, LANES), F32)),
        grid=(b, s // tm),
        in_specs=[tok(d), tok(attn.shape[2]), tok(pool.shape[2]), full(wo_a), full(wo_p), per_b,
                  full(g_ffn), per_b, per_b, full(w_r), full(b_r)],
        out_specs=(tok(d), tok(d), tok(LANES)),
        compiler_params=_cparams(("arbitrary", "arbitrary")),
        name="outproj",
    )(x, attn, pool, wo_a, wo_p, gate1, g_ffn, scale2, shift2, w_r, b_r)


def _moe_kernel(x1_ref, h2_ref, gates_ref, gate2_ref, wgu_ref, wd_ref, o_ref,
                xe_ref, rank_ref, rank_t_ref):
    g = pl.program_id(2)
    tm, d = h2_ref.shape[1], h2_ref.shape[2]
    n_e, d_exp = wd_ref.shape[0], wd_ref.shape[1]
    lane = lax.broadcasted_iota(jnp.int32, (tm, LANES), 1)
    gf = g.astype(F32)

    @pl.when(g == 0)
    def _():
        tri = (lax.broadcasted_iota(jnp.int32, (RANK_BLK, RANK_BLK), 0)
               >= lax.broadcasted_iota(jnp.int32, (RANK_BLK, RANK_BLK), 1)).astype(BF16)
        lane_b = lax.broadcasted_iota(jnp.int32, (RANK_BLK, LANES), 1)
        run = jnp.zeros((1, LANES), F32)
        for sb in range(tm // RANK_BLK):
            rows = slice(sb * RANK_BLK, (sb + 1) * RANK_BLK)
            gts_b = gates_ref[0, rows, :]
            grp = jnp.sum(jnp.where(lane_b == N_EXPERTS, gts_b, 0.0), axis=-1, keepdims=True)
            member = jnp.where(jnp.logical_and(lane_b < N_GROUPS, lane_b.astype(F32) == grp), 1.0, 0.0)
            pre = jnp.dot(tri, member.astype(BF16), preferred_element_type=F32) + run
            rank_ref[rows, :] = jnp.where(lane_b == N_GROUPS, grp, pre)
            run = jnp.max(pre, axis=0, keepdims=True)
        rank_t_ref[...] = rank_ref[...].T
        gts = gates_ref[0]
        g_hi = gts.astype(BF16)
        xe_ref[:, :d] = h2_ref[0]
        xe_ref[:, d:d + LANES] = g_hi
        xe_ref[:, d + LANES:d + 2 * LANES] = (gts - g_hi.astype(F32)).astype(BF16)
        o_ref[0] = jnp.zeros((tm, d), F32)

    rank_row = rank_t_ref[pl.ds(g, 1), :]
    pos_row = jnp.where(rank_t_ref[N_GROUPS:N_GROUPS + 1, :] == gf, rank_row - 1.0, -1.0)
    rk = rank_ref[...]
    rank_col = jnp.sum(jnp.where(lane == g, rk, 0.0), axis=-1, keepdims=True)
    grp_col = jnp.sum(jnp.where(lane == N_GROUPS, rk, 0.0), axis=-1, keepdims=True)
    pos_col = jnp.where(grp_col == gf, rank_col - 1.0, -1.0)
    n_rows = jnp.max(rank_row).astype(jnp.int32)
    row_id = lax.broadcasted_iota(jnp.int32, (MOE_CH, tm), 0).astype(F32)
    col_id = lax.broadcasted_iota(jnp.int32, (tm, MOE_CH), 1).astype(F32)
    lane_c = lax.broadcasted_iota(jnp.int32, (MOE_CH, LANES), 1)

    def chunk_body(c, _):
        r0 = (c * MOE_CH).astype(F32)
        gather = jnp.where(pos_row - r0 == row_id, 1.0, 0.0).astype(BF16)
        xg = jnp.dot(gather, xe_ref[...], preferred_element_type=F32)
        xb = xg[:, :d].astype(BF16)
        gates_c = xg[:, d:d + LANES] + xg[:, d + LANES:d + 2 * LANES]
        ya = None
        for e in range(n_e):
            gu = jnp.dot(xb, wgu_ref[e], preferred_element_type=F32)
            gt = gu[:, :d_exp]
            a = (gt * jax.nn.sigmoid(gt)) * gu[:, d_exp:]
            gate_e = jnp.sum(jnp.where(lane_c == g * n_e + e, gates_c, 0.0), axis=-1, keepdims=True)
            y = jnp.dot((a * gate_e).astype(BF16), wd_ref[e], preferred_element_type=F32)
            ya = y if ya is None else ya + y
        scatter = jnp.where(pos_col - r0 == col_id, 1.0, 0.0).astype(BF16)
        o_ref[0] += jnp.dot(scatter, ya.astype(BF16), preferred_element_type=F32)
        return 0

    lax.fori_loop(0, (n_rows + MOE_CH - 1) // MOE_CH, chunk_body, 0)

    @pl.when(g == pl.num_programs(2) - 1)
    def _():
        o_ref[0] = x1_ref[0] + gate2_ref[0] * o_ref[0]


def _moe(x1, h2, gates, gate2, w_gu, w_d):
    b, s, d = x1.shape
    tm = TM_MOE
    n_e = EXPERTS_PER_GROUP
    assert w_gu.shape[0] == N_GROUPS * n_e and tm % RANK_BLK == 0
    tok = lambda w: pl.BlockSpec((1, tm, w), lambda bi, si, g: (bi, si, 0))
    return pl.pallas_call(
        _moe_kernel,
        out_shape=jax.ShapeDtypeStruct((b, s, d), F32),
        grid=(b, s // tm, N_GROUPS),
        in_specs=[tok(d), tok(d), tok(LANES),
                  pl.BlockSpec((1, 1, d), lambda bi, si, g: (bi, 0, 0)),
                  pl.BlockSpec((n_e,) + w_gu.shape[1:], lambda bi, si, g: (g, 0, 0)),
                  pl.BlockSpec((n_e,) + w_d.shape[1:], lambda bi, si, g: (g, 0, 0))],
        out_specs=tok(d),
        scratch_shapes=[pltpu.VMEM((tm, d + 2 * LANES), BF16),
                        pltpu.VMEM((tm, LANES), F32),
                        pltpu.VMEM((LANES, tm), F32)],
        compiler_params=pltpu.CompilerParams(
            dimension_semantics=("arbitrary", "arbitrary", "arbitrary"),
            vmem_limit_bytes=VMEM_LIMIT_MOE_BYTES),
        name="moe",
    )(x1, h2, gates, gate2, w_gu, w_d)


def _layer(x, mod, pos3, g_mix, g_ffn, w_in, g_q, g_k, g_kidx, w_pool, pool_scale, w_out,
           w_rg, b_rg, w_re, b_re, w_gate, w_up, w_down):
    b, s, d = x.shape
    d_attn = N_HEADS * HEAD_DIM
    nqb = s // Q_BLK
    nkb = s // K_BLK
    shift1, scale1, gate1, shift2, scale2, gate2 = [m[:, None, :] for m in jnp.split(mod, 6, axis=-1)]

    n_front = d_attn + 2 * HEAD_DIM + N_IDX_HEADS * IDX_DIM + IDX_DIM + N_IDX_HEADS
    pad = (-n_front) % LANES
    w_in_p = jnp.concatenate([w_in[:, :n_front], jnp.zeros((d, pad), w_in.dtype), w_in[:, n_front:]],
                             axis=1).astype(BF16)
    seg_id = jnp.arange(d_attn) // HEAD_DIM
    segsum = (seg_id[:, None] == seg_id[None, :]).astype(BF16)
    ones_half = jnp.ones((LANES - HEAD_DIM,), F32)
    gq_t = jnp.tile(g_q, N_HEADS)[None, :]
    gk_e = jnp.concatenate([g_k, ones_half])[None, :]
    gkidx_e = jnp.concatenate([g_kidx, ones_half])[None, :]
    half = HEAD_DIM // 2
    inv_freq = ROPE_THETA ** (-jnp.arange(0, HEAD_DIM, 2, dtype=F32) / HEAD_DIM)
    invf = jnp.tile(inv_freq, LANES // half)[None, :]

    qt, kv, qit, ki, w_t, vt4, pool = _inproj(pos3, x, scale1, shift1, g_mix[None, :], w_in_p, segsum,
                                              gq_t, gk_e, gkidx_e, invf, w_pool.astype(BF16),
                                              pool_scale[None, :])
    kv4 = kv.reshape(b, s // CNT_BLK, CNT_BLK, LANES)
    ki4 = ki.reshape(b, s // CNT_BLK, CNT_BLK, LANES)
    attn = _dsa(qt, qit, w_t, kv4, ki4, vt4)

    w_out_b = w_out.astype(BF16)
    w_r = jnp.concatenate([w_re, w_rg, jnp.zeros((d, LANES - N_EXPERTS - N_GROUPS), F32)], axis=1)
    b_r = jnp.concatenate([b_re, b_rg, jnp.zeros((LANES - N_EXPERTS - N_GROUPS,), F32)])[None, :]
    x1, h2, gates = _outproj(x, attn, pool, w_out_b[:d_attn], w_out_b[d_attn:], gate1,
                             g_ffn[None, :], scale2, shift2, w_r, b_r)

    w_gu = jnp.concatenate([w_gate, w_up], axis=-1).astype(BF16)
    return _moe(x1, h2, gates, gate2, w_gu, w_down.astype(BF16))


def kernel(x, c, positions, w_ada, b_ada, g_norm_mix, g_norm_ffn, w_in, g_q, g_k, g_kidx, w_pool,
           pool_scale, w_out, w_router_group, b_router_group, w_router_expert, b_router_expert,
           w_gate, w_up, w_down):
    b, s, d = x.shape
    depth = w_ada.shape[0]
    assert s % TM_MOE == 0 and s % K_BLK == 0 and d % LANES == 0
    pos3 = positions[:, :, None]
    c_pad = jnp.concatenate([c, jnp.zeros((-b % SUBLANES, d), c.dtype)], axis=0)
    for l in range(depth):
        mod = _adaln(c_pad, w_ada[l], b_ada[l][None, :])[:b]
        x = _layer(x, mod, pos3, g_norm_mix[l], g_norm_ffn[l], w_in[l], g_q[l], g_k[l], g_kidx[l],
                   w_pool[l], pool_scale[l], w_out[l], w_router_group[l], b_router_group[l],
                   w_router_expert[l], b_router_expert[l], w_gate[l], w_up[l], w_down[l])
    return x
```

```python
import functools

import jax
import jax.numpy as jnp
from jax import lax
from jax.experimental import pallas as pl
from jax.experimental.pallas import tpu as pltpu

N_HEADS = 8
HEAD_DIM = 64
N_IDX_HEADS = 8
IDX_DIM = 64
TOPK_MAX = 256
ROPE_THETA = 10000.0
POOL_WINDOWS = (2, 4, 8, 16)
N_GROUPS = 4
EXPERTS_PER_GROUP = 8
N_EXPERTS = N_GROUPS * EXPERTS_PER_GROUP
EPS = 1e-6

LANES = 128
SUBLANES = 8
VMEM_LIMIT_BYTES = 56 * 1024 * 1024
VMEM_LIMIT_MOE_BYTES = 60 * 1024 * 1024

Q_BLK = 256
K_BLK = 256
COL_BLK = 256
CNT_BLK = 512
CNT_ROWS = 32
SEARCH_FIRST = 15
SEARCH_PERIOD = 2
TM_PROJ = 512
TM_MOE = 1024
MOE_CH = 320
RANK_BLK = 256
MOE_W_PARTS = 4
MAX_WIN = max(POOL_WINDOWS)
assert all(w == 2 ** (g + 1) for g, w in enumerate(POOL_WINDOWS))
M_INIT = -1e29
MASKED = -1e30
F32_LOWEST = -3.0e38
LOG2_E = 1.4426950408889634

BF16 = jnp.bfloat16
F32 = jnp.float32


def _cparams(sem):
    return pltpu.CompilerParams(dimension_semantics=sem, vmem_limit_bytes=VMEM_LIMIT_BYTES)


def _adaln_kernel(c_ref, w_ref, b_ref, o_ref):
    c = c_ref[...]
    c_act = c * jax.nn.sigmoid(c)
    o_ref[...] = jnp.dot(c_act, w_ref[...], preferred_element_type=F32) + b_ref[...]


def _adaln(c_pad, w_ada, b_ada):
    rows, d = c_pad.shape
    n = w_ada.shape[1]
    tn = n // 6
    return pl.pallas_call(
        _adaln_kernel,
        out_shape=jax.ShapeDtypeStruct((rows, n), F32),
        grid=(n // tn,),
        in_specs=[pl.BlockSpec((rows, d), lambda j: (0, 0)),
                  pl.BlockSpec((d, tn), lambda j: (0, j)),
                  pl.BlockSpec((1, tn), lambda j: (0, j))],
        out_specs=pl.BlockSpec((rows, tn), lambda j: (0, j)),
        compiler_params=_cparams(("arbitrary",)),
        name="adaln",
    )(c_pad, w_ada, b_ada)


_PIO2_HI, _PIO2_MID, _PIO2_LO = 1.5703125, 4.837512969970703125e-4, 7.54978995489188e-8
_SIN_COEF = (-1.9515295891e-4, 8.3321608736e-3, -1.6666654611e-1)
_COS_COEF = (2.443315711809948e-5, -1.388731625493765e-3, 4.166664568298827e-2)


def _sincos(x):
    k = jnp.floor(x * (2.0 / jnp.pi) + 0.5)
    r = ((x - k * _PIO2_HI) - k * _PIO2_MID) - k * _PIO2_LO
    z = r * r
    s = r + r * z * (_SIN_COEF[2] + z * (_SIN_COEF[1] + z * _SIN_COEF[0]))
    c = 1.0 - 0.5 * z + z * z * (_COS_COEF[2] + z * (_COS_COEF[1] + z * _COS_COEF[0]))
    q = k - 4.0 * jnp.floor(k * 0.25)
    odd = jnp.logical_or(q == 1.0, q == 3.0)
    sin_b = jnp.where(odd, c, s)
    cos_b = jnp.where(odd, s, c)
    return (jnp.where(q >= 2.0, -sin_b, sin_b),
            jnp.where(jnp.logical_or(q == 1.0, q == 2.0), -cos_b, cos_b))


def _rope_chunk(y, cos, sin_signed, first_half):
    from_hi = pltpu.roll(y, LANES - HEAD_DIM // 2, 1)
    from_lo = pltpu.roll(y, HEAD_DIM // 2, 1)
    return y * cos + jnp.where(first_half, from_hi, from_lo) * sin_signed


def _inproj_kernel(pos_ref, x_ref, scale_ref, shift_ref, gmix_ref, win_ref, segsum_ref,
                   gq_ref, gk_ref, gkidx_ref, invf_ref, wpool_ref, pscale_ref,
                   qt_ref, kv_ref, qit_ref, ki_ref, wt_ref, vt_ref, pool_ref, ubuf_ref):
    tm = x_ref.shape[1]
    sb = Q_BLK
    s_tile = pl.program_id(1)

    gain = gmix_ref[...] * (1.0 + scale_ref[0])
    hs = []
    for t in range(tm // sb):
        x = x_ref[0, t * sb:(t + 1) * sb, :]
        ms = jnp.mean(x * x, axis=-1, keepdims=True)
        hs.append((x * lax.rsqrt(ms + EPS) * gain + shift_ref[0]).astype(BF16))
    projs = [jnp.dot(h, win_ref[...], preferred_element_type=F32) for h in hs]

    @pl.when(s_tile == 0)
    def _():
        for lvl in range(len(POOL_WINDOWS)):
            ubuf_ref[lvl, 0:MAX_WIN, lvl * LANES:] = jnp.zeros(
                (MAX_WIN, ubuf_ref.shape[2] - lvl * LANES), F32)

    @pl.when(s_tile != 0)
    def _():
        for lvl in range(len(POOL_WINDOWS)):
            ubuf_ref[lvl, 0:MAX_WIN, lvl * LANES:] = ubuf_ref[lvl, tm:tm + MAX_WIN, lvl * LANES:]

    for t in range(tm // sb):
        _inproj_post(t, projs[t], s_tile * tm + t * sb, pos_ref, segsum_ref, gq_ref, gk_ref, gkidx_ref,
                     invf_ref, wpool_ref, pscale_ref, qt_ref, kv_ref, qit_ref, ki_ref, wt_ref, vt_ref,
                     pool_ref, ubuf_ref)


def _inproj_post(t, proj, t0, pos_ref, segsum_ref, gq_ref, gk_ref, gkidx_ref, invf_ref, wpool_ref,
                 pscale_ref, qt_ref, kv_ref, qit_ref, ki_ref, wt_ref, vt_ref, pool_ref, ubuf_ref):
    sb = Q_BLK
    rows = slice(t * sb, (t + 1) * sb)
    d_attn = N_HEADS * HEAD_DIM
    d_qidx = N_IDX_HEADS * IDX_DIM

    def store_cols(dst_ref, chunk, j):
        ct = chunk.T
        for hh in range(2):
            col = (2 * j + hh) * Q_BLK
            dst_ref[0, t, :, col:col + Q_BLK] = ct[hh * HEAD_DIM:(hh + 1) * HEAD_DIM, :].astype(dst_ref.dtype)

    lane = lax.broadcasted_iota(jnp.int32, (sb, LANES), 1)
    first_half = (lane & (HEAD_DIM - 1)) < (HEAD_DIM // 2)
    ang = pos_ref[0, rows, :].astype(F32) * invf_ref[...]
    sin, cos = _sincos(ang)
    sin_signed = jnp.where(first_half, -sin, sin)
    rope = functools.partial(_rope_chunk, cos=cos, sin_signed=sin_signed, first_half=first_half)

    qf = proj[:, :d_attn]
    qsq = qf * qf
    qsq_hi = qsq.astype(BF16)
    qsq_lo = (qsq - qsq_hi.astype(F32)).astype(BF16)
    seg = segsum_ref[...]
    ssq = (jnp.dot(qsq_hi, seg, preferred_element_type=F32)
           + jnp.dot(qsq_lo, seg, preferred_element_type=F32))
    qn = qf * lax.rsqrt(ssq * (1.0 / HEAD_DIM) + EPS) * gq_ref[...]
    for j in range(d_attn // LANES):
        sl = slice(j * LANES, (j + 1) * LANES)
        store_cols(qt_ref, rope(qn[:, sl]), j)

    kvc = proj[:, d_attn:d_attn + LANES]
    is_k = lane < HEAD_DIM
    ksq = jnp.sum(jnp.where(is_k, kvc * kvc, 0.0), axis=-1, keepdims=True)
    kn = kvc * lax.rsqrt(ksq * (1.0 / HEAD_DIM) + EPS) * gk_ref[...]
    kv_ref[0, rows, :] = jnp.where(is_k, rope(kn), kvc).astype(BF16)
    row8 = lax.broadcasted_iota(jnp.int32, (SUBLANES, K_BLK), 0)
    vt_ref[0, t, 0:HEAD_DIM, :] = kvc.T[HEAD_DIM:, :].astype(BF16)
    vt_ref[0, t, HEAD_DIM:HEAD_DIM + SUBLANES, :] = jnp.where(row8 == 0, 1.0, 0.0).astype(BF16)

    o_qi = d_attn + LANES
    for j in range(d_qidx // LANES):
        store_cols(qit_ref, rope(proj[:, o_qi + j * LANES:o_qi + (j + 1) * LANES]), j)

    o_ki = o_qi + d_qidx
    kic = proj[:, o_ki:o_ki + LANES]
    kisq = jnp.sum(jnp.where(is_k, kic * kic, 0.0), axis=-1, keepdims=True)
    kin = kic * lax.rsqrt(kisq * (1.0 / IDX_DIM) + EPS) * gkidx_ref[...]
    ki_ref[0, rows, :] = jnp.where(is_k, rope(kin), 0.0).astype(BF16)
    wt_ref[0, t] = kic.T[IDX_DIM:IDX_DIM + N_IDX_HEADS, :] * (N_IDX_HEADS ** -0.5 * IDX_DIM ** -0.5)

    o_u = o_ki + LANES
    u = proj[:, o_u:o_u + LANES * len(POOL_WINDOWS)]

    base = MAX_WIN + t * sb
    t_idx = t0 + lax.broadcasted_iota(jnp.int32, (sb, 1), 0)
    level = u
    for g, win in enumerate(POOL_WINDOWS):
        sl = slice(g * LANES, (g + 1) * LANES)
        shift = win // 2
        ubuf_ref[g, base:base + sb, g * LANES:] = level
        level = level + ubuf_ref[g, base - shift:base - shift + sb, g * LANES:]
        wsum = level[:, :LANES]
        if g + 1 < len(POOL_WINDOWS):
            level = level[:, LANES:]
        cnt = jnp.minimum(t_idx + 1, win).astype(F32)
        pooled = wsum / cnt - u[:, sl]
        mixed = jnp.dot(pooled.astype(BF16), wpool_ref[g], preferred_element_type=F32)
        pool_ref[0, rows, sl] = (mixed * pscale_ref[:, sl]).astype(BF16)


def _inproj(pos3, x, scale1, shift1, g_mix, w_in_p, segsum, gq_t, gk_e, gkidx_e, invf, w_pool, pscale):
    b, s, d = x.shape
    tm = TM_PROJ
    d_attn = N_HEADS * HEAD_DIM
    d_qidx = N_IDX_HEADS * IDX_DIM
    d_pool = LANES * len(POOL_WINDOWS)
    assert tm % Q_BLK == 0 and Q_BLK == K_BLK and HEAD_DIM == IDX_DIM and 2 * HEAD_DIM == LANES
    tok = lambda w: pl.BlockSpec((1, tm, w), lambda bi, si: (bi, si, 0))
    blk = lambda n, r, c: pl.BlockSpec((1, tm // n, r, c), lambda bi, si: (bi, si, 0, 0))
    per_b = pl.BlockSpec((1, 1, d), lambda bi, si: (bi, 0, 0))
    full = lambda a: pl.BlockSpec(a.shape, lambda bi, si: (0,) * a.ndim)
    nqb, nkb = s // Q_BLK, s // K_BLK
    return pl.pallas_call(
        _inproj_kernel,
        out_shape=(jax.ShapeDtypeStruct((b, nqb, HEAD_DIM, N_HEADS * Q_BLK), BF16),
                   jax.ShapeDtypeStruct((b, s, LANES), BF16),
                   jax.ShapeDtypeStruct((b, nqb, IDX_DIM, N_IDX_HEADS * Q_BLK), BF16),
                   jax.ShapeDtypeStruct((b, s, LANES), BF16),
                   jax.ShapeDtypeStruct((b, nqb, N_IDX_HEADS, Q_BLK), F32),
                   jax.ShapeDtypeStruct((b, nkb, HEAD_DIM + SUBLANES, K_BLK), BF16),
                   jax.ShapeDtypeStruct((b, s, d_pool), BF16)),
        grid=(b, s // tm),
        in_specs=[tok(1), tok(d), per_b, per_b, full(g_mix), full(w_in_p), full(segsum),
                  full(gq_t), full(gk_e), full(gkidx_e), full(invf), full(w_pool), full(pscale)],
        out_specs=(blk(Q_BLK, HEAD_DIM, N_HEADS * Q_BLK), tok(LANES),
                   blk(Q_BLK, IDX_DIM, N_IDX_HEADS * Q_BLK), tok(LANES),
                   blk(Q_BLK, N_IDX_HEADS, Q_BLK), blk(K_BLK, HEAD_DIM + SUBLANES, K_BLK), tok(d_pool)),
        scratch_shapes=[pltpu.VMEM((len(POOL_WINDOWS), tm + MAX_WIN, d_pool), F32)],
        compiler_params=_cparams(("arbitrary", "arbitrary")),
        name="inproj",
    )(pos3, x, scale1, shift1, g_mix, w_in_p, segsum, gq_t, gk_e, gkidx_e, invf, w_pool, pscale)


def _dsa_kernel(qt_ref, qit_ref, w_ref, kv_ref, ki_ref, vt_ref, o_ref,
                sc_ref, qe_ref, qie_ref, m_ref, mx_ref, st_ref, acc_ref, lg_ref, p_ref):
    topk = float(min(TOPK_MAX, (sc_ref.shape[0] * CNT_BLK) // 4))
    qb = pl.program_id(1)
    n_cols = qt_ref.shape[3]
    n_chunks = n_cols // COL_BLK
    sub = CNT_BLK // K_BLK
    nch = ((qb + 1) * Q_BLK + CNT_BLK - 1) // CNT_BLK
    kgrp = K_BLK // SUBLANES
    sub_rows = [slice(j * K_BLK, (j + 1) * K_BLK) for j in range(sub)]

    zeros_half = jnp.zeros((LANES - HEAD_DIM, n_cols), BF16)
    qe_ref[0:HEAD_DIM, :] = qt_ref[0, 0]
    qe_ref[HEAD_DIM:LANES, :] = zeros_half
    qie_ref[0:IDX_DIM, :] = qit_ref[0, 0]
    qie_ref[IDX_DIM:LANES, :] = zeros_half

    q_pos = qb * Q_BLK + lax.broadcasted_iota(jnp.int32, (K_BLK, Q_BLK), 1)
    key_off = lax.broadcasted_iota(jnp.int32, (K_BLK, Q_BLK), 0)

    def score_step(ch, carry):
        rmax, rmin = carry
        for j in range(sub):
            ki_blk = ki_ref[0, ch, sub_rows[j], :]
            score = None
            for cc in range(n_chunks):
                cs = slice(cc * COL_BLK, (cc + 1) * COL_BLK)
                s_h = jnp.dot(ki_blk, qie_ref[:, cs], preferred_element_type=F32)
                s_h = jnp.maximum(s_h, 0.0)
                for hh in range(COL_BLK // Q_BLK):
                    head = cc * (COL_BLK // Q_BLK) + hh
                    part = s_h[:, hh * Q_BLK:(hh + 1) * Q_BLK] * w_ref[0, 0, head:head + 1, :]
                    score = part if score is None else score + part
            causal = (ch * CNT_BLK + j * K_BLK + key_off) <= q_pos
            masked = jnp.where(causal, score, -jnp.inf)
            sc_ref[ch, sub_rows[j], :] = masked
            hi_part = masked.reshape(kgrp, SUBLANES, Q_BLK).max(axis=0)
            lo_part = jnp.where(causal, score, jnp.inf).reshape(kgrp, SUBLANES, Q_BLK).min(axis=0)
            rmax, rmin = jnp.maximum(rmax, hi_part), jnp.minimum(rmin, lo_part)
        return rmax, rmin

    def score_body(i, carry):
        return score_step(2 * i + 1, score_step(2 * i, carry))

    stats = lax.fori_loop(
        0, nch // 2, score_body,
        (jnp.full((SUBLANES, Q_BLK), -jnp.inf, F32), jnp.full((SUBLANES, Q_BLK), jnp.inf, F32)))
    rmax8, rmin8 = lax.cond(nch % 2 == 1, lambda c: score_step(nch - 1, c), lambda c: c, stats)
    rowmax = jnp.max(rmax8, axis=0, keepdims=True)
    rowmin = jnp.min(rmin8, axis=0, keepdims=True)

    n_causal = (qb * Q_BLK + 1 + lax.broadcasted_iota(jnp.int32, (1, Q_BLK), 1)).astype(F32)
    kt = jnp.minimum(n_causal, topk)

    cgrp = CNT_BLK // CNT_ROWS

    def count_ge(t):
        def body(ch, acc):
            for r in range(cgrp):
                rows = sc_ref[ch, r * CNT_ROWS:(r + 1) * CNT_ROWS, :]
                acc = acc + jnp.where(rows >= t, 1.0, 0.0)
            return acc
        acc = lax.fori_loop(0, nch, body, jnp.zeros((CNT_ROWS, Q_BLK), F32))
        return jnp.sum(acc, axis=0, keepdims=True)

    def bisect_pass(state):
        lo, hi, top, c_lo, c_hi, thr, done = state
        cap = jnp.minimum(hi, top)
        mid = lo + 0.5 * (cap - lo)
        mid = jnp.where(mid <= lo, cap, mid)
        c = count_ge(mid)
        hit = jnp.logical_and(done == 0.0, c == kt)
        thr = jnp.where(hit, mid, thr)
        done = jnp.where(hit, 1.0, done)
        active = done == 0.0
        up = jnp.logical_and(active, c >= kt)
        down = jnp.logical_and(active, c < kt)
        return (jnp.where(up, mid, lo), jnp.where(down, mid, hi), jnp.where(down, jnp.inf, top),
                jnp.where(up, c, c_lo), jnp.where(down, c, c_hi), thr, done)

    def snap_pass(state):
        lo, hi, top, c_lo, c_hi, thr, done = state

        def body(ch, carry):
            a8, b8 = carry
            for r in range(cgrp):
                s = sc_ref[ch, r * CNT_ROWS:(r + 1) * CNT_ROWS, :]
                a8 = jnp.minimum(a8, jnp.where(s >= lo, s, jnp.inf))
                b8 = jnp.maximum(b8, jnp.where(s < hi, s, -jnp.inf))
            return a8, b8

        a8, b8 = lax.fori_loop(
            0, nch, body,
            (jnp.full((CNT_ROWS, Q_BLK), jnp.inf, F32), jnp.full((CNT_ROWS, Q_BLK), -jnp.inf, F32)))
        a = jnp.min(a8, axis=0, keepdims=True)
        b = jnp.max(b8, axis=0, keepdims=True)
        active = done == 0.0
        hit = jnp.logical_and(active, a == b)
        thr = jnp.where(hit, a, thr)
        done = jnp.where(hit, 2.0, done)
        return jnp.where(active, a, lo), hi, jnp.where(active, b, top), c_lo, c_hi, thr, done

    few = n_causal <= topk
    state0 = (rowmin, jnp.full((1, Q_BLK), jnp.inf, F32), rowmax, n_causal,
              jnp.zeros((1, Q_BLK), F32), jnp.where(few, F32_LOWEST, 0.0), jnp.where(few, 1.0, 0.0))

    def outer_cond(carry):
        return carry[1] > 0.0

    def outer_body(carry):
        state, _ = carry
        state = lax.fori_loop(0, SEARCH_PERIOD, lambda i, st: bisect_pass(st), state)
        state = snap_pass(state)
        pending = jnp.max(jnp.where(state[6] == 0.0, 1.0, 0.0))
        return state, pending

    state1 = lax.fori_loop(0, SEARCH_FIRST, lambda i, st: bisect_pass(st), state0)
    state1 = snap_pass(state1)
    pending1 = jnp.max(jnp.where(state1[6] == 0.0, 1.0, 0.0))
    (lo, hi, _, c_lo, c_hi, thr, done), _ = lax.while_loop(outer_cond, outer_body, (state1, pending1))

    excess = jnp.where(done == 2.0, c_lo - kt, 0.0)
    need = kt - c_hi

    @pl.when(jnp.max(excess) > 0.0)
    def _():
        tri = (lax.broadcasted_iota(jnp.int32, (K_BLK, K_BLK), 0)
               >= lax.broadcasted_iota(jnp.int32, (K_BLK, K_BLK), 1)).astype(BF16)
        has_excess = excess > 0.0

        def drop_step(ch, run):
            for j in range(sub):
                s = sc_ref[ch, sub_rows[j], :]
                tied = jnp.logical_and(s == thr, has_excess)
                prefix = jnp.dot(tri, jnp.where(tied, 1.0, 0.0).astype(BF16), preferred_element_type=F32)
                drop = jnp.logical_and(tied, run + prefix > need)
                sc_ref[ch, sub_rows[j], :] = jnp.where(drop, -jnp.inf, s)
                run = run + jnp.max(prefix, axis=0, keepdims=True)
            return run

        run = lax.fori_loop(0, nch // 2, lambda i, r: drop_step(2 * i + 1, drop_step(2 * i, r)),
                            jnp.zeros((1, Q_BLK), F32))

        @pl.when(nch % 2 == 1)
        def _():
            drop_step(nch - 1, run)

    m_ref[...] = jnp.full(m_ref.shape, M_INIT, F32)
    acc_ref[...] = jnp.zeros(acc_ref.shape, F32)
    row_m = lambda j: slice(j, j + 1)
    row_a = lambda j: slice(sub + j, sub + j + 1)

    def logits_stage(ch, j):
        kv_blk = kv_ref[0, ch, sub_rows[j], :]
        bias = jnp.where(sc_ref[ch, sub_rows[j], :] >= thr, 0.0, MASKED)
        for cc in range(n_chunks):
            logits = jnp.dot(kv_blk, qe_ref[:, cc * COL_BLK:(cc + 1) * COL_BLK],
                             preferred_element_type=F32)
            for hh in range(COL_BLK // Q_BLK):
                cs = slice(cc * COL_BLK + hh * Q_BLK, cc * COL_BLK + (hh + 1) * Q_BLK)
                lg = logits[:, hh * Q_BLK:(hh + 1) * Q_BLK] + bias
                lg_ref[j, :, cs] = lg
                mx_ref[:, cs] = lg.reshape(kgrp, SUBLANES, Q_BLK).max(axis=0)
        m_old = m_ref[...]
        m_new = jnp.maximum(m_old, jnp.max(mx_ref[...], axis=0, keepdims=True))
        st_ref[row_m(j), :] = m_new
        st_ref[row_a(j), :] = jnp.exp2(m_old - m_new)
        m_ref[...] = m_new

    def probs_stage(j):
        p_ref[j] = jnp.exp2(lg_ref[j] - st_ref[row_m(j), :]).astype(BF16)

    def value_stage(kb, j, alpha):
        acc_ref[...] = acc_ref[...] * alpha + jnp.dot(
            vt_ref[0, kb], p_ref[j], preferred_element_type=F32)

    p_ref[sub - 1] = jnp.zeros(p_ref.shape[1:], BF16)
    st_ref[row_a(sub - 1), :] = jnp.ones((1, n_cols), F32)
    logits_stage(0, 0)

    def attn_body(ch, _):
        alpha_prev = st_ref[row_a(1), :]
        logits_stage(ch, 1)
        value_stage(jnp.maximum(ch * sub - 1, 0), 1, alpha_prev)
        probs_stage(0)
        alpha_cur = st_ref[row_a(0), :]
        logits_stage(jnp.minimum(ch + 1, nch - 1), 0)
        value_stage(ch * sub, 0, alpha_cur)
        probs_stage(1)
        return 0

    lax.fori_loop(0, nch, attn_body, 0)
    value_stage(nch * sub - 1, 1, st_ref[row_a(1), :])
    dh = qt_ref.shape[2]
    inv_l = 1.0 / acc_ref[dh:dh + 1, :]
    for j in range(n_cols // Q_BLK // 2):
        pair = [acc_ref[0:dh, (2 * j + hh) * Q_BLK:(2 * j + hh + 1) * Q_BLK]
                * inv_l[:, (2 * j + hh) * Q_BLK:(2 * j + hh + 1) * Q_BLK] for hh in range(2)]
        o_ref[0, :, j * 2 * dh:(j + 1) * 2 * dh] = jnp.concatenate(pair, axis=0).T.astype(o_ref.dtype)


def _dsa(qt, qit, w_t, kv4, ki4, vt4):
    b, nqb, dh, n_cols = qt.shape
    n_steps = kv4.shape[1]
    assert kv4.shape[2] == CNT_BLK and CNT_BLK == 2 * K_BLK and n_cols % COL_BLK == 0
    assert vt4.shape[1] * K_BLK == n_steps * CNT_BLK and vt4.shape[2] == dh + SUBLANES
    per_q = lambda a: pl.BlockSpec((1, 1) + a.shape[2:], lambda bi, qi: (bi, qi, 0, 0))
    per_b = lambda a: pl.BlockSpec((1,) + a.shape[1:], lambda bi, qi: (bi, 0, 0, 0))
    return pl.pallas_call(
        _dsa_kernel,
        out_shape=jax.ShapeDtypeStruct((b, nqb * Q_BLK, (n_cols // Q_BLK) * dh), BF16),
        grid=(b, nqb),
        in_specs=[per_q(qt), per_q(qit), per_q(w_t), per_b(kv4), per_b(ki4), per_b(vt4)],
        out_specs=pl.BlockSpec((1, Q_BLK, (n_cols // Q_BLK) * dh), lambda bi, qi: (bi, qi, 0)),
        scratch_shapes=[pltpu.VMEM((n_steps, CNT_BLK, Q_BLK), F32),
                        pltpu.VMEM((LANES, n_cols), BF16),
                        pltpu.VMEM((LANES, n_cols), BF16),
                        pltpu.VMEM((1, n_cols), F32),
                        pltpu.VMEM((SUBLANES, n_cols), F32),
                        pltpu.VMEM((SUBLANES, n_cols), F32),
                        pltpu.VMEM((dh + SUBLANES, n_cols), F32),
                        pltpu.VMEM((CNT_BLK // K_BLK, K_BLK, n_cols), F32),
                        pltpu.VMEM((CNT_BLK // K_BLK, K_BLK, n_cols), BF16)],
        compiler_params=_cparams(("arbitrary", "arbitrary")),
        name="dsa",
    )(qt, qit, w_t, kv4, ki4, vt4)


def _outproj_kernel(x_ref, attn_ref, pool_ref, woa_ref, wop_ref, gate1_ref, gffn_ref,
                    scale2_ref, shift2_ref, wr_ref, br_ref, x1_ref, h2_ref, gates_ref):
    tm = x_ref.shape[1]
    mix = (jnp.dot(attn_ref[0], woa_ref[...], preferred_element_type=F32)
           + jnp.dot(pool_ref[0], wop_ref[...], preferred_element_type=F32))
    x1 = x_ref[0] + gate1_ref[0] * mix
    x1_ref[0] = x1
    ms = jnp.mean(x1 * x1, axis=-1, keepdims=True)
    h2 = (x1 * lax.rsqrt(ms + EPS) * gffn_ref[...]) * (1.0 + scale2_ref[0]) + shift2_ref[0]
    h2_hi = h2.astype(BF16)
    h2_ref[0] = h2_hi

    h2_lo = (h2 - h2_hi.astype(F32)).astype(BF16)
    wr = wr_ref[...]
    wr_hi = wr.astype(BF16)
    wr_lo = (wr - wr_hi.astype(F32)).astype(BF16)
    logits = (jnp.dot(h2_hi, wr_hi, preferred_element_type=F32)
              + jnp.dot(h2_lo, wr_hi, preferred_element_type=F32)
              + jnp.dot(h2_hi, wr_lo, preferred_element_type=F32)) + br_ref[...]

    lane = lax.broadcasted_iota(jnp.int32, (tm, LANES), 1)
    big = jnp.int32(LANES)
    is_g = jnp.logical_and(lane >= N_EXPERTS, lane < N_EXPERTS + N_GROUPS)
    glog = jnp.where(is_g, logits, -jnp.inf)
    gmax = jnp.max(glog, axis=-1, keepdims=True)
    gsum = jnp.sum(jnp.exp(glog - gmax), axis=-1, keepdims=True)
    p_g = 1.0 / gsum
    g_sel = jnp.min(jnp.where(glog == gmax, lane, big), axis=-1, keepdims=True) - N_EXPERTS
    in_grp = jnp.logical_and(lane < N_EXPERTS, jnp.right_shift(lane, 3) == g_sel)
    elog = jnp.where(in_grp, logits, -jnp.inf)
    emax = jnp.max(elog, axis=-1, keepdims=True)
    eexp = jnp.exp(elog - emax)
    esum = jnp.sum(eexp, axis=-1, keepdims=True)
    p_e = jnp.where(in_grp, eexp / esum, -1.0)
    p1 = jnp.max(p_e, axis=-1, keepdims=True)
    i1 = jnp.min(jnp.where(p_e == p1, lane, big), axis=-1, keepdims=True)
    p_e2 = jnp.where(lane == i1, -1.0, p_e)
    p2 = jnp.max(p_e2, axis=-1, keepdims=True)
    i2 = jnp.min(jnp.where(p_e2 == p2, lane, big), axis=-1, keepdims=True)
    tot = p1 + p2
    gates_ref[0] = (jnp.where(lane == i1, p_g * (p1 / tot), 0.0)
                    + jnp.where(lane == i2, p_g * (p2 / tot), 0.0)
                    + jnp.where(lane == N_EXPERTS, g_sel.astype(F32), 0.0))


def _outproj(x, attn, pool, wo_a, wo_p, gate1, g_ffn, scale2, shift2, w_r, b_r):
    b, s, d = x.shape
    tm = TM_PROJ
    tok = lambda w: pl.BlockSpec((1, tm, w), lambda bi, si: (bi, si, 0))
    per_b = pl.BlockSpec((1, 1, d), lambda bi, si: (bi, 0, 0))
    full = lambda a: pl.BlockSpec(a.shape, lambda bi, si: (0,) * a.ndim)
    return pl.pallas_call(
        _outproj_kernel,
        out_shape=(jax.ShapeDtypeStruct((b, s, d), F32),
                   jax.ShapeDtypeStruct((b, s, d), BF16),
                   jax.ShapeDtypeStruct((b, s, LANES), F32)),
        grid=(b, s // tm),
        in_specs=[tok(d), tok(attn.shape[2]), tok(pool.shape[2]), full(wo_a), full(wo_p), per_b,
                  full(g_ffn), per_b, per_b, full(w_r), full(b_r)],
        out_specs=(tok(d), tok(d), tok(LANES)),
        compiler_params=_cparams(("arbitrary", "arbitrary")),
        name="outproj",
    )(x, attn, pool, wo_a, wo_p, gate1, g_ffn, scale2, shift2, w_r, b_r)


def _moe_kernel(x1_ref, h2_ref, gates_ref, gate2_ref, *refs):
    wgu_parts, wd_parts = refs[:MOE_W_PARTS], refs[MOE_W_PARTS:2 * MOE_W_PARTS]
    o_ref, xe_ref, rank_ref, rank_t_ref = refs[2 * MOE_W_PARTS:]
    g = pl.program_id(2)
    tm, d = h2_ref.shape[1], h2_ref.shape[2]
    per_part = wd_parts[0].shape[0]
    n_e, d_exp = per_part * MOE_W_PARTS, wd_parts[0].shape[1]
    lane = lax.broadcasted_iota(jnp.int32, (tm, LANES), 1)
    gf = g.astype(F32)

    @pl.when(g == 0)
    def _():
        tri = (lax.broadcasted_iota(jnp.int32, (RANK_BLK, RANK_BLK), 0)
               >= lax.broadcasted_iota(jnp.int32, (RANK_BLK, RANK_BLK), 1)).astype(BF16)
        lane_b = lax.broadcasted_iota(jnp.int32, (RANK_BLK, LANES), 1)
        run = jnp.zeros((1, LANES), F32)
        for sb in range(tm // RANK_BLK):
            rows = slice(sb * RANK_BLK, (sb + 1) * RANK_BLK)
            gts_b = gates_ref[0, rows, :]
            grp = jnp.sum(jnp.where(lane_b == N_EXPERTS, gts_b, 0.0), axis=-1, keepdims=True)
            member = jnp.where(jnp.logical_and(lane_b < N_GROUPS, lane_b.astype(F32) == grp), 1.0, 0.0)
            pre = jnp.dot(tri, member.astype(BF16), preferred_element_type=F32) + run
            rank_ref[rows, :] = jnp.where(lane_b == N_GROUPS, grp, pre)
            run = jnp.max(pre, axis=0, keepdims=True)
        rank_t_ref[...] = rank_ref[...].T
        gts = gates_ref[0]
        g_hi = gts.astype(BF16)
        xe_ref[:, :d] = h2_ref[0]
        xe_ref[:, d:d + LANES] = g_hi
        xe_ref[:, d + LANES:d + 2 * LANES] = (gts - g_hi.astype(F32)).astype(BF16)
        o_ref[0] = jnp.zeros((tm, d), F32)

    rank_row = rank_t_ref[pl.ds(g, 1), :]
    pos_row = jnp.where(rank_t_ref[N_GROUPS:N_GROUPS + 1, :] == gf, rank_row - 1.0, -1.0)
    rk = rank_ref[...]
    rank_col = jnp.sum(jnp.where(lane == g, rk, 0.0), axis=-1, keepdims=True)
    grp_col = jnp.sum(jnp.where(lane == N_GROUPS, rk, 0.0), axis=-1, keepdims=True)
    pos_col = jnp.where(grp_col == gf, rank_col - 1.0, -1.0)
    n_rows = jnp.max(rank_row).astype(jnp.int32)
    row_id = lax.broadcasted_iota(jnp.int32, (MOE_CH, tm), 0).astype(F32)
    col_id = lax.broadcasted_iota(jnp.int32, (tm, MOE_CH), 1).astype(F32)
    lane_c = lax.broadcasted_iota(jnp.int32, (MOE_CH, LANES), 1)

    def chunk_body(c, _):
        r0 = (c * MOE_CH).astype(F32)
        gather = jnp.where(pos_row - r0 == row_id, 1.0, 0.0).astype(BF16)
        xg = jnp.dot(gather, xe_ref[...], preferred_element_type=F32)
        xb = xg[:, :d].astype(BF16)
        gates_c = xg[:, d:d + LANES] + xg[:, d + LANES:d + 2 * LANES]
        ya = None
        for e in range(n_e):
            w_gu_e = wgu_parts[e // per_part][e % per_part]
            w_d_e = wd_parts[e // per_part][e % per_part]
            gu = jnp.dot(xb, w_gu_e, preferred_element_type=F32)
            gt = gu[:, :d_exp]
            a = (gt * jax.nn.sigmoid(gt)) * gu[:, d_exp:]
            gate_e = jnp.sum(jnp.where(lane_c == g * n_e + e, gates_c, 0.0), axis=-1, keepdims=True)
            y = jnp.dot((a * gate_e).astype(BF16), w_d_e, preferred_element_type=F32)
            ya = y if ya is None else ya + y
        scatter = jnp.where(pos_col - r0 == col_id, 1.0, 0.0).astype(BF16)
        o_ref[0] += jnp.dot(scatter, ya.astype(BF16), preferred_element_type=F32)
        return 0

    lax.fori_loop(0, (n_rows + MOE_CH - 1) // MOE_CH, chunk_body, 0)

    @pl.when(g == pl.num_programs(2) - 1)
    def _():
        o_ref[0] = x1_ref[0] + gate2_ref[0] * o_ref[0]


def _moe(x1, h2, gates, gate2, w_gu, w_d):
    b, s, d = x1.shape
    tm = TM_MOE
    n_e = EXPERTS_PER_GROUP
    assert w_gu.shape[0] == N_GROUPS * n_e and tm % RANK_BLK == 0 and n_e % MOE_W_PARTS == 0
    per_part = n_e // MOE_W_PARTS
    tok = lambda w: pl.BlockSpec((1, tm, w), lambda bi, si, g: (bi, si, 0))

    def slab(w, k):
        return pl.BlockSpec((per_part,) + w.shape[1:], lambda bi, si, g: (g * MOE_W_PARTS + k, 0, 0))

    return pl.pallas_call(
        _moe_kernel,
        out_shape=jax.ShapeDtypeStruct((b, s, d), F32),
        grid=(b, s // tm, N_GROUPS),
        in_specs=([tok(d), tok(d), tok(LANES), pl.BlockSpec((1, 1, d), lambda bi, si, g: (bi, 0, 0))]
                  + [slab(w_gu, k) for k in range(MOE_W_PARTS)]
                  + [slab(w_d, k) for k in range(MOE_W_PARTS)]),
        out_specs=tok(d),
        scratch_shapes=[pltpu.VMEM((tm, d + 2 * LANES), BF16),
                        pltpu.VMEM((tm, LANES), F32),
                        pltpu.VMEM((LANES, tm), F32)],
        compiler_params=pltpu.CompilerParams(
            dimension_semantics=("arbitrary", "arbitrary", "arbitrary"),
            vmem_limit_bytes=VMEM_LIMIT_MOE_BYTES),
        name="moe",
    )(x1, h2, gates, gate2, *([w_gu] * MOE_W_PARTS), *([w_d] * MOE_W_PARTS))


def _layer(x, mod, pos3, g_mix, g_ffn, w_in, g_q, g_k, g_kidx, w_pool, pool_scale, w_out,
           w_rg, b_rg, w_re, b_re, w_gate, w_up, w_down):
    b, s, d = x.shape
    d_attn = N_HEADS * HEAD_DIM
    nqb = s // Q_BLK
    nkb = s // K_BLK
    shift1, scale1, gate1, shift2, scale2, gate2 = [m[:, None, :] for m in jnp.split(mod, 6, axis=-1)]

    n_front = d_attn + 2 * HEAD_DIM + N_IDX_HEADS * IDX_DIM + IDX_DIM + N_IDX_HEADS
    pad = (-n_front) % LANES
    w_in_p = jnp.concatenate([w_in[:, :n_front], jnp.zeros((d, pad), w_in.dtype), w_in[:, n_front:]],
                             axis=1).astype(BF16)
    seg_id = jnp.arange(d_attn) // HEAD_DIM
    segsum = (seg_id[:, None] == seg_id[None, :]).astype(BF16)
    ones_half = jnp.ones((LANES - HEAD_DIM,), F32)
    gq_t = (jnp.tile(g_q, N_HEADS) * (LOG2_E * HEAD_DIM ** -0.5))[None, :]
    gk_e = jnp.concatenate([g_k, ones_half])[None, :]
    gkidx_e = jnp.concatenate([g_kidx, ones_half])[None, :]
    half = HEAD_DIM // 2
    inv_freq = ROPE_THETA ** (-jnp.arange(0, HEAD_DIM, 2, dtype=F32) / HEAD_DIM)
    invf = jnp.tile(inv_freq, LANES // half)[None, :]

    qt, kv, qit, ki, w_t, vt4, pool = _inproj(pos3, x, scale1, shift1, g_mix[None, :], w_in_p, segsum,
                                              gq_t, gk_e, gkidx_e, invf, w_pool.astype(BF16),
                                              pool_scale[None, :])
    kv4 = kv.reshape(b, s // CNT_BLK, CNT_BLK, LANES)
    ki4 = ki.reshape(b, s // CNT_BLK, CNT_BLK, LANES)
    attn = _dsa(qt, qit, w_t, kv4, ki4, vt4)

    w_out_b = w_out.astype(BF16)
    w_r = jnp.concatenate([w_re, w_rg, jnp.zeros((d, LANES - N_EXPERTS - N_GROUPS), F32)], axis=1)
    b_r = jnp.concatenate([b_re, b_rg, jnp.zeros((LANES - N_EXPERTS - N_GROUPS,), F32)])[None, :]
    x1, h2, gates = _outproj(x, attn, pool, w_out_b[:d_attn], w_out_b[d_attn:], gate1,
                             g_ffn[None, :], scale2, shift2, w_r, b_r)

    w_gu = jnp.concatenate([w_gate, w_up], axis=-1).astype(BF16)
    return _moe(x1, h2, gates, gate2, w_gu, w_down.astype(BF16))


def kernel(x, c, positions, w_ada, b_ada, g_norm_mix, g_norm_ffn, w_in, g_q, g_k, g_kidx, w_pool,
           pool_scale, w_out, w_router_group, b_router_group, w_router_expert, b_router_expert,
           w_gate, w_up, w_down):
    b, s, d = x.shape
    depth = w_ada.shape[0]
    assert s % TM_MOE == 0 and s % K_BLK == 0 and d % LANES == 0
    pos3 = positions[:, :, None]
    c_pad = jnp.concatenate([c, jnp.zeros((-b % SUBLANES, d), c.dtype)], axis=0)
    for l in range(depth):
        mod = _adaln(c_pad, w_ada[l], b_ada[l][None, :])[:b]
        x = _layer(x, mod, pos3, g_norm_mix[l], g_norm_ffn[l], w_in[l], g_q[l], g_k[l], g_kidx[l],
                   w_pool[l], pool_scale[l], w_out[l], w_router_group[l], b_router_group[l],
                   w_router_expert[l], b_router_expert[l], w_gate[l], w_up[l], w_down[l])
    return x
```

```python
import functools

import jax
import jax.numpy as jnp
from jax import lax
from jax.experimental import pallas as pl
from jax.experimental.pallas import tpu as pltpu

N_HEADS = 8
HEAD_DIM = 64
N_IDX_HEADS = 8
IDX_DIM = 64
TOPK_MAX = 256
ROPE_THETA = 10000.0
POOL_WINDOWS = (2, 4, 8, 16)
N_GROUPS = 4
EXPERTS_PER_GROUP = 8
N_EXPERTS = N_GROUPS * EXPERTS_PER_GROUP
EPS = 1e-6
N_MOD = 6
assert EXPERTS_PER_GROUP & (EXPERTS_PER_GROUP - 1) == 0

LANES = 128
SUBLANES = 8
VMEM_LIMIT_BYTES = 56 * 1024 * 1024
VMEM_LIMIT_MOE_BYTES = 60 * 1024 * 1024

Q_BLK = 256
K_BLK = 256
COL_BLK = 256
CNT_BLK = 512
CNT_ROWS = 32
SEARCH_FIRST = 15
SEARCH_PERIOD = 2
TM_PROJ = 512
TM_MOE = 1024
MOE_CH = 320
RANK_BLK = 256
MOE_W_PARTS = 4
MAX_WIN = max(POOL_WINDOWS)
assert all(w == 2 ** (g + 1) for g, w in enumerate(POOL_WINDOWS))
M_INIT = -1e29
MASKED = -1e30
F32_LOWEST = -3.0e38
LOG2_E = 1.4426950408889634

BF16 = jnp.bfloat16
F32 = jnp.float32


def _cparams(sem):
    return pltpu.CompilerParams(dimension_semantics=sem, vmem_limit_bytes=VMEM_LIMIT_BYTES)


def _adaln_kernel(c_ref, w_ref, b_ref, o_ref):
    c = c_ref[...]
    c_act = c * jax.nn.sigmoid(c)
    o_ref[...] = jnp.dot(c_act, w_ref[...], preferred_element_type=F32) + b_ref[...]


def _adaln(c_pad, w_ada, b_ada):
    rows, d = c_pad.shape
    n = w_ada.shape[1]
    tn = n // N_MOD
    return pl.pallas_call(
        _adaln_kernel,
        out_shape=jax.ShapeDtypeStruct((rows, n), F32),
        grid=(n // tn,),
        in_specs=[pl.BlockSpec((rows, d), lambda j: (0, 0)),
                  pl.BlockSpec((d, tn), lambda j: (0, j)),
                  pl.BlockSpec((1, tn), lambda j: (0, j))],
        out_specs=pl.BlockSpec((rows, tn), lambda j: (0, j)),
        compiler_params=_cparams(("arbitrary",)),
        name="adaln",
    )(c_pad, w_ada, b_ada)


_PIO2_HI, _PIO2_MID, _PIO2_LO = 1.5703125, 4.837512969970703125e-4, 7.54978995489188e-8
_SIN_COEF = (-1.9515295891e-4, 8.3321608736e-3, -1.6666654611e-1)
_COS_COEF = (2.443315711809948e-5, -1.388731625493765e-3, 4.166664568298827e-2)


def _sincos(x):
    k = jnp.floor(x * (2.0 / jnp.pi) + 0.5)
    r = ((x - k * _PIO2_HI) - k * _PIO2_MID) - k * _PIO2_LO
    z = r * r
    s = r + r * z * (_SIN_COEF[2] + z * (_SIN_COEF[1] + z * _SIN_COEF[0]))
    c = 1.0 - 0.5 * z + z * z * (_COS_COEF[2] + z * (_COS_COEF[1] + z * _COS_COEF[0]))
    q = k - 4.0 * jnp.floor(k * 0.25)
    odd = jnp.logical_or(q == 1.0, q == 3.0)
    sin_b = jnp.where(odd, c, s)
    cos_b = jnp.where(odd, s, c)
    return (jnp.where(q >= 2.0, -sin_b, sin_b),
            jnp.where(jnp.logical_or(q == 1.0, q == 2.0), -cos_b, cos_b))


def _rope_chunk(y, cos, sin_signed, first_half):
    from_hi = pltpu.roll(y, LANES - HEAD_DIM // 2, 1)
    from_lo = pltpu.roll(y, HEAD_DIM // 2, 1)
    return y * cos + jnp.where(first_half, from_hi, from_lo) * sin_signed


def _inproj_kernel(pos_ref, x_ref, scale_ref, shift_ref, gmix_ref, win_ref, segsum_ref,
                   gq_ref, gk_ref, gkidx_ref, invf_ref, wpool_ref, pscale_ref,
                   qt_ref, kv_ref, qit_ref, ki_ref, wt_ref, vt_ref, pool_ref,
                   ubuf_ref, proj_a_ref, proj_b_ref):
    tm = x_ref.shape[1]
    sb = Q_BLK
    step = pl.program_id(1)
    tile = jnp.maximum(step - 1, 0)

    @pl.when(step == 0)
    def _():
        proj_b_ref[...] = jnp.zeros(proj_b_ref.shape, F32)

    @pl.when(step <= 1)
    def _():
        for lvl in range(len(POOL_WINDOWS)):
            ubuf_ref[lvl, 0:MAX_WIN, lvl * LANES:] = jnp.zeros(
                (MAX_WIN, ubuf_ref.shape[2] - lvl * LANES), F32)

    @pl.when(step > 1)
    def _():
        for lvl in range(len(POOL_WINDOWS)):
            ubuf_ref[lvl, 0:MAX_WIN, lvl * LANES:] = ubuf_ref[lvl, tm:tm + MAX_WIN, lvl * LANES:]

    def run(write_ref, read_ref):
        for t in range(tm // sb):
            _inproj_post(t, read_ref[t], tile * tm + t * sb, pos_ref, segsum_ref, gq_ref, gk_ref,
                         gkidx_ref, invf_ref, wpool_ref, pscale_ref, qt_ref, kv_ref, qit_ref, ki_ref,
                         wt_ref, vt_ref, pool_ref, ubuf_ref)
        gain = gmix_ref[...] * (1.0 + scale_ref[0])
        for t in range(tm // sb):
            x = x_ref[0, t * sb:(t + 1) * sb, :]
            ms = jnp.mean(x * x, axis=-1, keepdims=True)
            h = (x * lax.rsqrt(ms + EPS) * gain + shift_ref[0]).astype(BF16)
            write_ref[t] = jnp.dot(h, win_ref[...], preferred_element_type=F32)

    @pl.when(step % 2 == 0)
    def _():
        run(proj_a_ref, proj_b_ref)

    @pl.when(step % 2 == 1)
    def _():
        run(proj_b_ref, proj_a_ref)


def _inproj_post(t, proj, t0, pos_ref, segsum_ref, gq_ref, gk_ref, gkidx_ref, invf_ref, wpool_ref,
                 pscale_ref, qt_ref, kv_ref, qit_ref, ki_ref, wt_ref, vt_ref, pool_ref, ubuf_ref):
    sb = Q_BLK
    rows = slice(t * sb, (t + 1) * sb)
    d_attn = N_HEADS * HEAD_DIM
    d_qidx = N_IDX_HEADS * IDX_DIM

    def store_cols(dst_ref, chunk, j):
        ct = chunk.T
        for hh in range(2):
            col = (2 * j + hh) * Q_BLK
            dst_ref[0, t, :, col:col + Q_BLK] = ct[hh * HEAD_DIM:(hh + 1) * HEAD_DIM, :].astype(dst_ref.dtype)

    lane = lax.broadcasted_iota(jnp.int32, (sb, LANES), 1)
    first_half = (lane & (HEAD_DIM - 1)) < (HEAD_DIM // 2)
    ang = pos_ref[0, rows, :].astype(F32) * invf_ref[...]
    sin, cos = _sincos(ang)
    sin_signed = jnp.where(first_half, -sin, sin)
    rope = functools.partial(_rope_chunk, cos=cos, sin_signed=sin_signed, first_half=first_half)

    qf = proj[:, :d_attn]
    qsq = qf * qf
    qsq_hi = qsq.astype(BF16)
    qsq_lo = (qsq - qsq_hi.astype(F32)).astype(BF16)
    seg = segsum_ref[...]
    ssq = (jnp.dot(qsq_hi, seg, preferred_element_type=F32)
           + jnp.dot(qsq_lo, seg, preferred_element_type=F32))
    qn = qf * lax.rsqrt(ssq * (1.0 / HEAD_DIM) + EPS) * gq_ref[...]
    for j in range(d_attn // LANES):
        sl = slice(j * LANES, (j + 1) * LANES)
        store_cols(qt_ref, rope(qn[:, sl]), j)

    kvc = proj[:, d_attn:d_attn + LANES]
    is_k = lane < HEAD_DIM
    ksq = jnp.sum(jnp.where(is_k, kvc * kvc, 0.0), axis=-1, keepdims=True)
    kn = kvc * lax.rsqrt(ksq * (1.0 / HEAD_DIM) + EPS) * gk_ref[...]
    kv_ref[0, rows, :] = jnp.where(is_k, rope(kn), kvc).astype(BF16)
    row8 = lax.broadcasted_iota(jnp.int32, (SUBLANES, K_BLK), 0)
    vt_ref[0, t, 0:HEAD_DIM, :] = kvc.T[HEAD_DIM:, :].astype(BF16)
    vt_ref[0, t, HEAD_DIM:HEAD_DIM + SUBLANES, :] = jnp.where(row8 == 0, 1.0, 0.0).astype(BF16)

    o_qi = d_attn + LANES
    for j in range(d_qidx // LANES):
        store_cols(qit_ref, rope(proj[:, o_qi + j * LANES:o_qi + (j + 1) * LANES]), j)

    o_ki = o_qi + d_qidx
    kic = proj[:, o_ki:o_ki + LANES]
    kisq = jnp.sum(jnp.where(is_k, kic * kic, 0.0), axis=-1, keepdims=True)
    kin = kic * lax.rsqrt(kisq * (1.0 / IDX_DIM) + EPS) * gkidx_ref[...]
    ki_ref[0, rows, :] = jnp.where(is_k, rope(kin), 0.0).astype(BF16)
    wt_ref[0, t] = kic.T[IDX_DIM:IDX_DIM + N_IDX_HEADS, :] * (N_IDX_HEADS ** -0.5 * IDX_DIM ** -0.5)

    o_u = o_ki + LANES
    u = proj[:, o_u:o_u + LANES * len(POOL_WINDOWS)]

    base = MAX_WIN + t * sb
    t_idx = t0 + lax.broadcasted_iota(jnp.int32, (sb, 1), 0)
    level = u
    for g, win in enumerate(POOL_WINDOWS):
        sl = slice(g * LANES, (g + 1) * LANES)
        shift = win // 2
        ubuf_ref[g, base:base + sb, g * LANES:] = level
        level = level + ubuf_ref[g, base - shift:base - shift + sb, g * LANES:]
        wsum = level[:, :LANES]
        if g + 1 < len(POOL_WINDOWS):
            level = level[:, LANES:]
        cnt = jnp.minimum(t_idx + 1, win).astype(F32)
        pooled = wsum / cnt - u[:, sl]
        mixed = jnp.dot(pooled.astype(BF16), wpool_ref[g], preferred_element_type=F32)
        pool_ref[0, rows, sl] = (mixed * pscale_ref[:, sl]).astype(BF16)


def _inproj(pos3, x, scale1, shift1, g_mix, w_in_p, segsum, gq_t, gk_e, gkidx_e, invf, w_pool, pscale):
    b, s, d = x.shape
    tm = TM_PROJ
    d_attn = N_HEADS * HEAD_DIM
    d_qidx = N_IDX_HEADS * IDX_DIM
    d_pool = LANES * len(POOL_WINDOWS)
    assert tm % Q_BLK == 0 and Q_BLK == K_BLK and HEAD_DIM == IDX_DIM and 2 * HEAD_DIM == LANES
    n_tiles = s // tm
    ahead = lambda si: jnp.minimum(si, n_tiles - 1)
    behind = lambda si: jnp.maximum(si - 1, 0)
    tok = lambda w: pl.BlockSpec((1, tm, w), lambda bi, si: (bi, behind(si), 0))
    blk = lambda n, r, c: pl.BlockSpec((1, tm // n, r, c), lambda bi, si: (bi, behind(si), 0, 0))
    per_b = pl.BlockSpec((1, 1, d), lambda bi, si: (bi, 0, 0))
    full = lambda a: pl.BlockSpec(a.shape, lambda bi, si: (0,) * a.ndim)
    nqb, nkb = s // Q_BLK, s // K_BLK
    proj_buf = pltpu.VMEM((tm // Q_BLK, Q_BLK, w_in_p.shape[1]), F32)
    return pl.pallas_call(
        _inproj_kernel,
        out_shape=(jax.ShapeDtypeStruct((b, nqb, HEAD_DIM, N_HEADS * Q_BLK), BF16),
                   jax.ShapeDtypeStruct((b, s, LANES), BF16),
                   jax.ShapeDtypeStruct((b, nqb, IDX_DIM, N_IDX_HEADS * Q_BLK), BF16),
                   jax.ShapeDtypeStruct((b, s, LANES), BF16),
                   jax.ShapeDtypeStruct((b, nqb, N_IDX_HEADS, Q_BLK), F32),
                   jax.ShapeDtypeStruct((b, nkb, HEAD_DIM + SUBLANES, K_BLK), BF16),
                   jax.ShapeDtypeStruct((b, s, d_pool), BF16)),
        grid=(b, n_tiles + 1),
        in_specs=[tok(1), pl.BlockSpec((1, tm, d), lambda bi, si: (bi, ahead(si), 0)), per_b, per_b,
                  full(g_mix), full(w_in_p), full(segsum),
                  full(gq_t), full(gk_e), full(gkidx_e), full(invf), full(w_pool), full(pscale)],
        out_specs=(blk(Q_BLK, HEAD_DIM, N_HEADS * Q_BLK), tok(LANES),
                   blk(Q_BLK, IDX_DIM, N_IDX_HEADS * Q_BLK), tok(LANES),
                   blk(Q_BLK, N_IDX_HEADS, Q_BLK), blk(K_BLK, HEAD_DIM + SUBLANES, K_BLK), tok(d_pool)),
        scratch_shapes=[pltpu.VMEM((len(POOL_WINDOWS), tm + MAX_WIN, d_pool), F32),
                        proj_buf, proj_buf],
        compiler_params=_cparams(("arbitrary", "arbitrary")),
        name="inproj",
    )(pos3, x, scale1, shift1, g_mix, w_in_p, segsum, gq_t, gk_e, gkidx_e, invf, w_pool, pscale)


def _dsa_kernel(qt_ref, qit_ref, w_ref, kv_ref, ki_ref, vt_ref, o_ref,
                sc_ref, qe_ref, qie_ref, m_ref, mx_ref, st_ref, acc_ref, lg_ref, p_ref):
    topk = float(min(TOPK_MAX, (sc_ref.shape[0] * CNT_BLK) // 4))
    qb = pl.program_id(1)
    n_cols = qt_ref.shape[3]
    n_chunks = n_cols // COL_BLK
    sub = CNT_BLK // K_BLK
    nch = ((qb + 1) * Q_BLK + CNT_BLK - 1) // CNT_BLK
    kgrp = K_BLK // SUBLANES
    sub_rows = [slice(j * K_BLK, (j + 1) * K_BLK) for j in range(sub)]

    zeros_half = jnp.zeros((LANES - HEAD_DIM, n_cols), BF16)
    qe_ref[0:HEAD_DIM, :] = qt_ref[0, 0]
    qe_ref[HEAD_DIM:LANES, :] = zeros_half
    qie_ref[0:IDX_DIM, :] = qit_ref[0, 0]
    qie_ref[IDX_DIM:LANES, :] = zeros_half

    q_pos = qb * Q_BLK + lax.broadcasted_iota(jnp.int32, (K_BLK, Q_BLK), 1)
    key_off = lax.broadcasted_iota(jnp.int32, (K_BLK, Q_BLK), 0)

    def score_step(ch, carry):
        rmax, rmin = carry
        for j in range(sub):
            ki_blk = ki_ref[0, ch, sub_rows[j], :]
            score = None
            for cc in range(n_chunks):
                cs = slice(cc * COL_BLK, (cc + 1) * COL_BLK)
                s_h = jnp.dot(ki_blk, qie_ref[:, cs], preferred_element_type=F32)
                s_h = jnp.maximum(s_h, 0.0)
                for hh in range(COL_BLK // Q_BLK):
                    head = cc * (COL_BLK // Q_BLK) + hh
                    part = s_h[:, hh * Q_BLK:(hh + 1) * Q_BLK] * w_ref[0, 0, head:head + 1, :]
                    score = part if score is None else score + part
            causal = (ch * CNT_BLK + j * K_BLK + key_off) <= q_pos
            masked = jnp.where(causal, score, -jnp.inf)
            sc_ref[ch, sub_rows[j], :] = masked
            hi_part = masked.reshape(kgrp, SUBLANES, Q_BLK).max(axis=0)
            lo_part = jnp.where(causal, score, jnp.inf).reshape(kgrp, SUBLANES, Q_BLK).min(axis=0)
            rmax, rmin = jnp.maximum(rmax, hi_part), jnp.minimum(rmin, lo_part)
        return rmax, rmin

    def score_body(i, carry):
        return score_step(2 * i + 1, score_step(2 * i, carry))

    stats = lax.fori_loop(
        0, nch // 2, score_body,
        (jnp.full((SUBLANES, Q_BLK), -jnp.inf, F32), jnp.full((SUBLANES, Q_BLK), jnp.inf, F32)))
    rmax8, rmin8 = lax.cond(nch % 2 == 1, lambda c: score_step(nch - 1, c), lambda c: c, stats)
    rowmax = jnp.max(rmax8, axis=0, keepdims=True)
    rowmin = jnp.min(rmin8, axis=0, keepdims=True)

    n_causal = (qb * Q_BLK + 1 + lax.broadcasted_iota(jnp.int32, (1, Q_BLK), 1)).astype(F32)
    kt = jnp.minimum(n_causal, topk)

    cgrp = CNT_BLK // CNT_ROWS

    def count_ge(t):
        def body(ch, acc):
            for r in range(cgrp):
                rows = sc_ref[ch, r * CNT_ROWS:(r + 1) * CNT_ROWS, :]
                acc = acc + jnp.where(rows >= t, 1.0, 0.0)
            return acc
        acc = lax.fori_loop(0, nch, body, jnp.zeros((CNT_ROWS, Q_BLK), F32))
        return jnp.sum(acc, axis=0, keepdims=True)

    def bisect_pass(state):
        lo, hi, top, c_lo, c_hi, thr, done = state
        cap = jnp.minimum(hi, top)
        mid = lo + 0.5 * (cap - lo)
        mid = jnp.where(mid <= lo, cap, mid)
        c = count_ge(mid)
        hit = jnp.logical_and(done == 0.0, c == kt)
        thr = jnp.where(hit, mid, thr)
        done = jnp.where(hit, 1.0, done)
        active = done == 0.0
        up = jnp.logical_and(active, c >= kt)
        down = jnp.logical_and(active, c < kt)
        return (jnp.where(up, mid, lo), jnp.where(down, mid, hi), jnp.where(down, jnp.inf, top),
                jnp.where(up, c, c_lo), jnp.where(down, c, c_hi), thr, done)

    def snap_pass(state):
        lo, hi, top, c_lo, c_hi, thr, done = state

        def body(ch, carry):
            a8, b8 = carry
            for r in range(cgrp):
                s = sc_ref[ch, r * CNT_ROWS:(r + 1) * CNT_ROWS, :]
                a8 = jnp.minimum(a8, jnp.where(s >= lo, s, jnp.inf))
                b8 = jnp.maximum(b8, jnp.where(s < hi, s, -jnp.inf))
            return a8, b8

        a8, b8 = lax.fori_loop(
            0, nch, body,
            (jnp.full((CNT_ROWS, Q_BLK), jnp.inf, F32), jnp.full((CNT_ROWS, Q_BLK), -jnp.inf, F32)))
        a = jnp.min(a8, axis=0, keepdims=True)
        b = jnp.max(b8, axis=0, keepdims=True)
        active = done == 0.0
        hit = jnp.logical_and(active, a == b)
        thr = jnp.where(hit, a, thr)
        done = jnp.where(hit, 2.0, done)
        return jnp.where(active, a, lo), hi, jnp.where(active, b, top), c_lo, c_hi, thr, done

    few = n_causal <= topk
    state0 = (rowmin, jnp.full((1, Q_BLK), jnp.inf, F32), rowmax, n_causal,
              jnp.zeros((1, Q_BLK), F32), jnp.where(few, F32_LOWEST, 0.0), jnp.where(few, 1.0, 0.0))

    def outer_cond(carry):
        return carry[1] > 0.0

    def outer_body(carry):
        state, _ = carry
        state = lax.fori_loop(0, SEARCH_PERIOD, lambda i, st: bisect_pass(st), state)
        state = snap_pass(state)
        pending = jnp.max(jnp.where(state[6] == 0.0, 1.0, 0.0))
        return state, pending

    state1 = lax.fori_loop(0, SEARCH_FIRST, lambda i, st: bisect_pass(st), state0)
    state1 = snap_pass(state1)
    pending1 = jnp.max(jnp.where(state1[6] == 0.0, 1.0, 0.0))
    (lo, hi, _, c_lo, c_hi, thr, done), _ = lax.while_loop(outer_cond, outer_body, (state1, pending1))

    excess = jnp.where(done == 2.0, c_lo - kt, 0.0)
    need = kt - c_hi

    @pl.when(jnp.max(excess) > 0.0)
    def _():
        tri = (lax.broadcasted_iota(jnp.int32, (K_BLK, K_BLK), 0)
               >= lax.broadcasted_iota(jnp.int32, (K_BLK, K_BLK), 1)).astype(BF16)
        has_excess = excess > 0.0

        def drop_step(ch, run):
            for j in range(sub):
                s = sc_ref[ch, sub_rows[j], :]
                tied = jnp.logical_and(s == thr, has_excess)
                prefix = jnp.dot(tri, jnp.where(tied, 1.0, 0.0).astype(BF16), preferred_element_type=F32)
                drop = jnp.logical_and(tied, run + prefix > need)
                sc_ref[ch, sub_rows[j], :] = jnp.where(drop, -jnp.inf, s)
                run = run + jnp.max(prefix, axis=0, keepdims=True)
            return run

        run = lax.fori_loop(0, nch // 2, lambda i, r: drop_step(2 * i + 1, drop_step(2 * i, r)),
                            jnp.zeros((1, Q_BLK), F32))

        @pl.when(nch % 2 == 1)
        def _():
            drop_step(nch - 1, run)

    m_ref[...] = jnp.full(m_ref.shape, M_INIT, F32)
    acc_ref[...] = jnp.zeros(acc_ref.shape, F32)
    row_m = lambda j: slice(j, j + 1)
    row_a = lambda j: slice(sub + j, sub + j + 1)

    def logits_stage(ch, j):
        kv_blk = kv_ref[0, ch, sub_rows[j], :]
        bias = jnp.where(sc_ref[ch, sub_rows[j], :] >= thr, 0.0, MASKED)
        for cc in range(n_chunks):
            logits = jnp.dot(kv_blk, qe_ref[:, cc * COL_BLK:(cc + 1) * COL_BLK],
                             preferred_element_type=F32)
            for hh in range(COL_BLK // Q_BLK):
                cs = slice(cc * COL_BLK + hh * Q_BLK, cc * COL_BLK + (hh + 1) * Q_BLK)
                lg = logits[:, hh * Q_BLK:(hh + 1) * Q_BLK] + bias
                lg_ref[j, :, cs] = lg
                mx_ref[:, cs] = lg.reshape(kgrp, SUBLANES, Q_BLK).max(axis=0)
        m_old = m_ref[...]
        m_new = jnp.maximum(m_old, jnp.max(mx_ref[...], axis=0, keepdims=True))
        st_ref[row_m(j), :] = m_new
        st_ref[row_a(j), :] = jnp.exp2(m_old - m_new)
        m_ref[...] = m_new

    def probs_stage(j):
        p_ref[j] = jnp.exp2(lg_ref[j] - st_ref[row_m(j), :]).astype(BF16)

    def value_stage(kb, j, alpha):
        acc_ref[...] = acc_ref[...] * alpha + jnp.dot(
            vt_ref[0, kb], p_ref[j], preferred_element_type=F32)

    p_ref[sub - 1] = jnp.zeros(p_ref.shape[1:], BF16)
    st_ref[row_a(sub - 1), :] = jnp.ones((1, n_cols), F32)
    logits_stage(0, 0)

    def attn_body(ch, _):
        alpha_prev = st_ref[row_a(1), :]
        logits_stage(ch, 1)
        value_stage(jnp.maximum(ch * sub - 1, 0), 1, alpha_prev)
        probs_stage(0)
        alpha_cur = st_ref[row_a(0), :]
        logits_stage(jnp.minimum(ch + 1, nch - 1), 0)
        value_stage(ch * sub, 0, alpha_cur)
        probs_stage(1)
        return 0

    lax.fori_loop(0, nch, attn_body, 0)
    value_stage(nch * sub - 1, 1, st_ref[row_a(1), :])
    dh = qt_ref.shape[2]
    inv_l = 1.0 / acc_ref[dh:dh + 1, :]
    for j in range(n_cols // Q_BLK // 2):
        pair = [acc_ref[0:dh, (2 * j + hh) * Q_BLK:(2 * j + hh + 1) * Q_BLK]
                * inv_l[:, (2 * j + hh) * Q_BLK:(2 * j + hh + 1) * Q_BLK] for hh in range(2)]
        o_ref[0, :, j * 2 * dh:(j + 1) * 2 * dh] = jnp.concatenate(pair, axis=0).T.astype(o_ref.dtype)


def _dsa(qt, qit, w_t, kv4, ki4, vt4):
    b, nqb, dh, n_cols = qt.shape
    n_steps = kv4.shape[1]
    assert kv4.shape[2] == CNT_BLK and CNT_BLK == 2 * K_BLK and n_cols % COL_BLK == 0
    assert vt4.shape[1] * K_BLK == n_steps * CNT_BLK and vt4.shape[2] == dh + SUBLANES
    per_q = lambda a: pl.BlockSpec((1, 1) + a.shape[2:], lambda bi, qi: (bi, qi, 0, 0))
    per_b = lambda a: pl.BlockSpec((1,) + a.shape[1:], lambda bi, qi: (bi, 0, 0, 0))
    return pl.pallas_call(
        _dsa_kernel,
        out_shape=jax.ShapeDtypeStruct((b, nqb * Q_BLK, (n_cols // Q_BLK) * dh), BF16),
        grid=(b, nqb),
        in_specs=[per_q(qt), per_q(qit), per_q(w_t), per_b(kv4), per_b(ki4), per_b(vt4)],
        out_specs=pl.BlockSpec((1, Q_BLK, (n_cols // Q_BLK) * dh), lambda bi, qi: (bi, qi, 0)),
        scratch_shapes=[pltpu.VMEM((n_steps, CNT_BLK, Q_BLK), F32),
                        pltpu.VMEM((LANES, n_cols), BF16),
                        pltpu.VMEM((LANES, n_cols), BF16),
                        pltpu.VMEM((1, n_cols), F32),
                        pltpu.VMEM((SUBLANES, n_cols), F32),
                        pltpu.VMEM((SUBLANES, n_cols), F32),
                        pltpu.VMEM((dh + SUBLANES, n_cols), F32),
                        pltpu.VMEM((CNT_BLK // K_BLK, K_BLK, n_cols), F32),
                        pltpu.VMEM((CNT_BLK // K_BLK, K_BLK, n_cols), BF16)],
        compiler_params=_cparams(("arbitrary", "arbitrary")),
        name="dsa",
    )(qt, qit, w_t, kv4, ki4, vt4)


def _outproj_kernel(x_ref, attn_ref, pool_ref, woa_ref, wop_ref, gate1_ref, gffn_ref,
                    scale2_ref, shift2_ref, wr_ref, br_ref, x1_ref, h2_ref, gates_ref):
    tm = x_ref.shape[1]
    mix = (jnp.dot(attn_ref[0], woa_ref[...], preferred_element_type=F32)
           + jnp.dot(pool_ref[0], wop_ref[...], preferred_element_type=F32))
    x1 = x_ref[0] + gate1_ref[0] * mix
    x1_ref[0] = x1
    ms = jnp.mean(x1 * x1, axis=-1, keepdims=True)
    h2 = (x1 * lax.rsqrt(ms + EPS) * gffn_ref[...]) * (1.0 + scale2_ref[0]) + shift2_ref[0]
    h2_hi = h2.astype(BF16)
    h2_ref[0] = h2_hi

    h2_lo = (h2 - h2_hi.astype(F32)).astype(BF16)
    wr = wr_ref[...]
    wr_hi = wr.astype(BF16)
    wr_lo = (wr - wr_hi.astype(F32)).astype(BF16)
    logits = (jnp.dot(h2_hi, wr_hi, preferred_element_type=F32)
              + jnp.dot(h2_lo, wr_hi, preferred_element_type=F32)
              + jnp.dot(h2_hi, wr_lo, preferred_element_type=F32)) + br_ref[...]

    lane = lax.broadcasted_iota(jnp.int32, (tm, LANES), 1)
    big = jnp.int32(LANES)
    is_g = jnp.logical_and(lane >= N_EXPERTS, lane < N_EXPERTS + N_GROUPS)
    glog = jnp.where(is_g, logits, -jnp.inf)
    gmax = jnp.max(glog, axis=-1, keepdims=True)
    gsum = jnp.sum(jnp.exp(glog - gmax), axis=-1, keepdims=True)
    p_g = 1.0 / gsum
    g_sel = jnp.min(jnp.where(glog == gmax, lane, big), axis=-1, keepdims=True) - N_EXPERTS
    in_grp = jnp.logical_and(lane < N_EXPERTS,
                             jnp.right_shift(lane, EXPERTS_PER_GROUP.bit_length() - 1) == g_sel)
    elog = jnp.where(in_grp, logits, -jnp.inf)
    emax = jnp.max(elog, axis=-1, keepdims=True)
    eexp = jnp.exp(elog - emax)
    esum = jnp.sum(eexp, axis=-1, keepdims=True)
    p_e = jnp.where(in_grp, eexp / esum, -1.0)
    p1 = jnp.max(p_e, axis=-1, keepdims=True)
    i1 = jnp.min(jnp.where(p_e == p1, lane, big), axis=-1, keepdims=True)
    p_e2 = jnp.where(lane == i1, -1.0, p_e)
    p2 = jnp.max(p_e2, axis=-1, keepdims=True)
    i2 = jnp.min(jnp.where(p_e2 == p2, lane, big), axis=-1, keepdims=True)
    tot = p1 + p2
    gates_ref[0] = (jnp.where(lane == i1, p_g * (p1 / tot), 0.0)
                    + jnp.where(lane == i2, p_g * (p2 / tot), 0.0)
                    + jnp.where(lane == N_EXPERTS, g_sel.astype(F32), 0.0))


def _outproj(x, attn, pool, wo_a, wo_p, gate1, g_ffn, scale2, shift2, w_r, b_r):
    b, s, d = x.shape
    tm = TM_PROJ
    tok = lambda w: pl.BlockSpec((1, tm, w), lambda bi, si: (bi, si, 0))
    per_b = pl.BlockSpec((1, 1, d), lambda bi, si: (bi, 0, 0))
    full = lambda a: pl.BlockSpec(a.shape, lambda bi, si: (0,) * a.ndim)
    return pl.pallas_call(
        _outproj_kernel,
        out_shape=(jax.ShapeDtypeStruct((b, s, d), F32),
                   jax.ShapeDtypeStruct((b, s, d), BF16),
                   jax.ShapeDtypeStruct((b, s, LANES), F32)),
        grid=(b, s // tm),
        in_specs=[tok(d), tok(attn.shape[2]), tok(pool.shape[2]), full(wo_a), full(wo_p), per_b,
                  full(g_ffn), per_b, per_b, full(w_r), full(b_r)],
        out_specs=(tok(d), tok(d), tok(LANES)),
        compiler_params=_cparams(("arbitrary", "arbitrary")),
        name="outproj",
    )(x, attn, pool, wo_a, wo_p, gate1, g_ffn, scale2, shift2, w_r, b_r)


def _moe_kernel(x1_ref, h2_ref, gates_ref, gate2_ref, *refs):
    wgu_parts, wd_parts = refs[:MOE_W_PARTS], refs[MOE_W_PARTS:2 * MOE_W_PARTS]
    o_ref, xe_ref, rank_ref, rank_t_ref = refs[2 * MOE_W_PARTS:]
    g = pl.program_id(2)
    tm, d = h2_ref.shape[1], h2_ref.shape[2]
    per_part = wd_parts[0].shape[0]
    n_e, d_exp = per_part * MOE_W_PARTS, wd_parts[0].shape[1]
    lane = lax.broadcasted_iota(jnp.int32, (tm, LANES), 1)
    gf = g.astype(F32)

    @pl.when(g == 0)
    def _():
        tri = (lax.broadcasted_iota(jnp.int32, (RANK_BLK, RANK_BLK), 0)
               >= lax.broadcasted_iota(jnp.int32, (RANK_BLK, RANK_BLK), 1)).astype(BF16)
        lane_b = lax.broadcasted_iota(jnp.int32, (RANK_BLK, LANES), 1)
        run = jnp.zeros((1, LANES), F32)
        for sb in range(tm // RANK_BLK):
            rows = slice(sb * RANK_BLK, (sb + 1) * RANK_BLK)
            gts_b = gates_ref[0, rows, :]
            grp = jnp.sum(jnp.where(lane_b == N_EXPERTS, gts_b, 0.0), axis=-1, keepdims=True)
            member = jnp.where(jnp.logical_and(lane_b < N_GROUPS, lane_b.astype(F32) == grp), 1.0, 0.0)
            pre = jnp.dot(tri, member.astype(BF16), preferred_element_type=F32) + run
            rank_ref[rows, :] = jnp.where(lane_b == N_GROUPS, grp, pre)
            run = jnp.max(pre, axis=0, keepdims=True)
        rank_t_ref[...] = rank_ref[...].T
        gts = gates_ref[0]
        g_hi = gts.astype(BF16)
        xe_ref[:, :d] = h2_ref[0]
        xe_ref[:, d:d + LANES] = g_hi
        xe_ref[:, d + LANES:d + 2 * LANES] = (gts - g_hi.astype(F32)).astype(BF16)
        o_ref[0] = jnp.zeros((tm, d), F32)

    rank_row = rank_t_ref[pl.ds(g, 1), :]
    pos_row = jnp.where(rank_t_ref[N_GROUPS:N_GROUPS + 1, :] == gf, rank_row - 1.0, -1.0)
    rk = rank_ref[...]
    rank_col = jnp.sum(jnp.where(lane == g, rk, 0.0), axis=-1, keepdims=True)
    grp_col = jnp.sum(jnp.where(lane == N_GROUPS, rk, 0.0), axis=-1, keepdims=True)
    pos_col = jnp.where(grp_col == gf, rank_col - 1.0, -1.0)
    n_rows = jnp.max(rank_row).astype(jnp.int32)
    row_id = lax.broadcasted_iota(jnp.int32, (MOE_CH, tm), 0).astype(F32)
    col_id = lax.broadcasted_iota(jnp.int32, (tm, MOE_CH), 1).astype(F32)
    lane_c = lax.broadcasted_iota(jnp.int32, (MOE_CH, LANES), 1)

    def chunk_body(c, _):
        r0 = (c * MOE_CH).astype(F32)
        gather = jnp.where(pos_row - r0 == row_id, 1.0, 0.0).astype(BF16)
        xg = jnp.dot(gather, xe_ref[...], preferred_element_type=F32)
        xb = xg[:, :d].astype(BF16)
        gates_c = xg[:, d:d + LANES] + xg[:, d + LANES:d + 2 * LANES]
        ya = None
        for e in range(n_e):
            w_gu_e = wgu_parts[e // per_part][e % per_part]
            w_d_e = wd_parts[e // per_part][e % per_part]
            gu = jnp.dot(xb, w_gu_e, preferred_element_type=F32)
            gt = gu[:, :d_exp]
            a = (gt * jax.nn.sigmoid(gt)) * gu[:, d_exp:]
            gate_e = jnp.sum(jnp.where(lane_c == g * n_e + e, gates_c, 0.0), axis=-1, keepdims=True)
            y = jnp.dot((a * gate_e).astype(BF16), w_d_e, preferred_element_type=F32)
            ya = y if ya is None else ya + y
        scatter = jnp.where(pos_col - r0 == col_id, 1.0, 0.0).astype(BF16)
        o_ref[0] += jnp.dot(scatter, ya.astype(BF16), preferred_element_type=F32)
        return 0

    lax.fori_loop(0, (n_rows + MOE_CH - 1) // MOE_CH, chunk_body, 0)

    @pl.when(g == pl.num_programs(2) - 1)
    def _():
        o_ref[0] = x1_ref[0] + gate2_ref[0] * o_ref[0]


def _moe(x1, h2, gates, gate2, w_gu, w_d):
    b, s, d = x1.shape
    tm = TM_MOE
    n_e = EXPERTS_PER_GROUP
    assert w_gu.shape[0] == N_GROUPS * n_e and tm % RANK_BLK == 0 and n_e % MOE_W_PARTS == 0
    per_part = n_e // MOE_W_PARTS
    tok = lambda w: pl.BlockSpec((1, tm, w), lambda bi, si, g: (bi, si, 0))

    def slab(w, k):
        return pl.BlockSpec((per_part,) + w.shape[1:], lambda bi, si, g: (g * MOE_W_PARTS + k, 0, 0))

    return pl.pallas_call(
        _moe_kernel,
        out_shape=jax.ShapeDtypeStruct((b, s, d), F32),
        grid=(b, s // tm, N_GROUPS),
        in_specs=([tok(d), tok(d), tok(LANES), pl.BlockSpec((1, 1, d), lambda bi, si, g: (bi, 0, 0))]
                  + [slab(w_gu, k) for k in range(MOE_W_PARTS)]
                  + [slab(w_d, k) for k in range(MOE_W_PARTS)]),
        out_specs=tok(d),
        scratch_shapes=[pltpu.VMEM((tm, d + 2 * LANES), BF16),
                        pltpu.VMEM((tm, LANES), F32),
                        pltpu.VMEM((LANES, tm), F32)],
        compiler_params=pltpu.CompilerParams(
            dimension_semantics=("arbitrary", "arbitrary", "arbitrary"),
            vmem_limit_bytes=VMEM_LIMIT_MOE_BYTES),
        name="moe",
    )(x1, h2, gates, gate2, *([w_gu] * MOE_W_PARTS), *([w_d] * MOE_W_PARTS))


def _layer(x, mod, pos3, g_mix, g_ffn, w_in, g_q, g_k, g_kidx, w_pool, pool_scale, w_out,
           w_rg, b_rg, w_re, b_re, w_gate, w_up, w_down):
    b, s, d = x.shape
    d_attn = N_HEADS * HEAD_DIM
    nqb = s // Q_BLK
    nkb = s // K_BLK
    shift1, scale1, gate1, shift2, scale2, gate2 = [m[:, None, :] for m in jnp.split(mod, 6, axis=-1)]

    n_front = d_attn + 2 * HEAD_DIM + N_IDX_HEADS * IDX_DIM + IDX_DIM + N_IDX_HEADS
    pad = (-n_front) % LANES
    w_in_p = jnp.concatenate([w_in[:, :n_front], jnp.zeros((d, pad), w_in.dtype), w_in[:, n_front:]],
                             axis=1).astype(BF16)
    seg_id = jnp.arange(d_attn) // HEAD_DIM
    segsum = (seg_id[:, None] == seg_id[None, :]).astype(BF16)
    ones_half = jnp.ones((LANES - HEAD_DIM,), F32)
    gq_t = (jnp.tile(g_q, N_HEADS) * (LOG2_E * HEAD_DIM ** -0.5))[None, :]
    gk_e = jnp.concatenate([g_k, ones_half])[None, :]
    gkidx_e = jnp.concatenate([g_kidx, ones_half])[None, :]
    half = HEAD_DIM // 2
    inv_freq = ROPE_THETA ** (-jnp.arange(0, HEAD_DIM, 2, dtype=F32) / HEAD_DIM)
    invf = jnp.tile(inv_freq, LANES // half)[None, :]

    qt, kv, qit, ki, w_t, vt4, pool = _inproj(pos3, x, scale1, shift1, g_mix[None, :], w_in_p, segsum,
                                              gq_t, gk_e, gkidx_e, invf, w_pool.astype(BF16),
                                              pool_scale[None, :])
    kv4 = kv.reshape(b, s // CNT_BLK, CNT_BLK, LANES)
    ki4 = ki.reshape(b, s // CNT_BLK, CNT_BLK, LANES)
    attn = _dsa(qt, qit, w_t, kv4, ki4, vt4)

    w_out_b = w_out.astype(BF16)
    w_r = jnp.concatenate([w_re, w_rg, jnp.zeros((d, LANES - N_EXPERTS - N_GROUPS), F32)], axis=1)
    b_r = jnp.concatenate([b_re, b_rg, jnp.zeros((LANES - N_EXPERTS - N_GROUPS,), F32)])[None, :]
    x1, h2, gates = _outproj(x, attn, pool, w_out_b[:d_attn], w_out_b[d_attn:], gate1,
                             g_ffn[None, :], scale2, shift2, w_r, b_r)

    w_gu = jnp.concatenate([w_gate, w_up], axis=-1).astype(BF16)
    return _moe(x1, h2, gates, gate2, w_gu, w_down.astype(BF16))


def kernel(x, c, positions, w_ada, b_ada, g_norm_mix, g_norm_ffn, w_in, g_q, g_k, g_kidx, w_pool,
           pool_scale, w_out, w_router_group, b_router_group, w_router_expert, b_router_expert,
           w_gate, w_up, w_down):
    b, s, d = x.shape
    depth = w_ada.shape[0]
    assert s % TM_MOE == 0 and s % K_BLK == 0 and d % LANES == 0
    pos3 = positions[:, :, None]
    c_pad = jnp.concatenate([c, jnp.zeros((-b % SUBLANES, d), c.dtype)], axis=0)
    for l in range(depth):
        mod = _adaln(c_pad, w_ada[l], b_ada[l][None, :])[:b]
        x = _layer(x, mod, pos3, g_norm_mix[l], g_norm_ffn[l], w_in[l], g_q[l], g_k[l], g_kidx[l],
                   w_pool[l], pool_scale[l], w_out[l], w_router_group[l], b_router_group[l],
                   w_router_expert[l], b_router_expert[l], w_gate[l], w_up[l], w_down[l])
    return x
```

```python
import functools

import jax
import jax.numpy as jnp
from jax import lax
from jax.experimental import pallas as pl
from jax.experimental.pallas import tpu as pltpu

N_HEADS = 8
HEAD_DIM = 64
N_IDX_HEADS = 8
IDX_DIM = 64
TOPK_MAX = 256
ROPE_THETA = 10000.0
POOL_WINDOWS = (2, 4, 8, 16)
N_GROUPS = 4
EXPERTS_PER_GROUP = 8
N_EXPERTS = N_GROUPS * EXPERTS_PER_GROUP
EPS = 1e-6
N_MOD = 6
assert EXPERTS_PER_GROUP & (EXPERTS_PER_GROUP - 1) == 0

LANES = 128
SUBLANES = 8
VMEM_LIMIT_BYTES = 56 * 1024 * 1024
VMEM_LIMIT_MOE_BYTES = 60 * 1024 * 1024

Q_BLK = 256
K_BLK = 256
COL_BLK = 256
CNT_BLK = 512
CNT_ROWS = 32
SEARCH_FIRST = 15
SEARCH_PERIOD = 2
TM_PROJ = 512
TM_MOE = 1024
MOE_CH = 320
RANK_BLK = 256
MOE_W_PARTS = 4
MAX_WIN = max(POOL_WINDOWS)
assert all(w == 2 ** (g + 1) for g, w in enumerate(POOL_WINDOWS))
M_INIT = -1e29
MASKED = -1e30
F32_LOWEST = -3.0e38
LOG2_E = 1.4426950408889634

BF16 = jnp.bfloat16
F32 = jnp.float32


def _cparams(sem):
    return pltpu.CompilerParams(dimension_semantics=sem, vmem_limit_bytes=VMEM_LIMIT_BYTES)


def _adaln_kernel(c_ref, w_ref, b_ref, o_ref):
    c = c_ref[...]
    c_act = c * jax.nn.sigmoid(c)
    o_ref[...] = jnp.dot(c_act, w_ref[...], preferred_element_type=F32) + b_ref[...]


def _adaln(c_pad, w_ada, b_ada):
    rows, d = c_pad.shape
    n = w_ada.shape[1]
    tn = n // N_MOD
    return pl.pallas_call(
        _adaln_kernel,
        out_shape=jax.ShapeDtypeStruct((rows, n), F32),
        grid=(n // tn,),
        in_specs=[pl.BlockSpec((rows, d), lambda j: (0, 0)),
                  pl.BlockSpec((d, tn), lambda j: (0, j)),
                  pl.BlockSpec((1, tn), lambda j: (0, j))],
        out_specs=pl.BlockSpec((rows, tn), lambda j: (0, j)),
        compiler_params=_cparams(("arbitrary",)),
        name="adaln",
    )(c_pad, w_ada, b_ada)


_PIO2_HI, _PIO2_MID, _PIO2_LO = 1.5703125, 4.837512969970703125e-4, 7.54978995489188e-8
_SIN_COEF = (-1.9515295891e-4, 8.3321608736e-3, -1.6666654611e-1)
_COS_COEF = (2.443315711809948e-5, -1.388731625493765e-3, 4.166664568298827e-2)


def _sincos(x):
    k = jnp.floor(x * (2.0 / jnp.pi) + 0.5)
    r = ((x - k * _PIO2_HI) - k * _PIO2_MID) - k * _PIO2_LO
    z = r * r
    s = r + r * z * (_SIN_COEF[2] + z * (_SIN_COEF[1] + z * _SIN_COEF[0]))
    c = 1.0 - 0.5 * z + z * z * (_COS_COEF[2] + z * (_COS_COEF[1] + z * _COS_COEF[0]))
    q = k - 4.0 * jnp.floor(k * 0.25)
    odd = jnp.logical_or(q == 1.0, q == 3.0)
    sin_b = jnp.where(odd, c, s)
    cos_b = jnp.where(odd, s, c)
    return (jnp.where(q >= 2.0, -sin_b, sin_b),
            jnp.where(jnp.logical_or(q == 1.0, q == 2.0), -cos_b, cos_b))


def _rope_chunk(y, cos, sin_signed, first_half):
    from_hi = pltpu.roll(y, LANES - HEAD_DIM // 2, 1)
    from_lo = pltpu.roll(y, HEAD_DIM // 2, 1)
    return y * cos + jnp.where(first_half, from_hi, from_lo) * sin_signed


def _inproj_kernel(pos_ref, x_ref, scale_ref, shift_ref, gmix_ref, win_ref, segsum_ref,
                   gq_ref, gk_ref, gkidx_ref, invf_ref, wpool_ref, pscale_ref,
                   qt_ref, kv_ref, qit_ref, ki_ref, wt_ref, vt_ref, pool_ref,
                   ubuf_ref, proj_a_ref, proj_b_ref):
    tm = x_ref.shape[1]
    sb = Q_BLK
    step = pl.program_id(1)
    tile = jnp.maximum(step - 1, 0)

    @pl.when(step == 0)
    def _():
        proj_b_ref[...] = jnp.zeros(proj_b_ref.shape, F32)

    @pl.when(step <= 1)
    def _():
        for lvl in range(len(POOL_WINDOWS)):
            ubuf_ref[lvl, 0:MAX_WIN, lvl * LANES:] = jnp.zeros(
                (MAX_WIN, ubuf_ref.shape[2] - lvl * LANES), F32)

    @pl.when(step > 1)
    def _():
        for lvl in range(len(POOL_WINDOWS)):
            ubuf_ref[lvl, 0:MAX_WIN, lvl * LANES:] = ubuf_ref[lvl, tm:tm + MAX_WIN, lvl * LANES:]

    def run(write_ref, read_ref):
        for t in range(tm // sb):
            _inproj_post(t, read_ref[t], tile * tm + t * sb, pos_ref, segsum_ref, gq_ref, gk_ref,
                         gkidx_ref, invf_ref, wpool_ref, pscale_ref, qt_ref, kv_ref, qit_ref, ki_ref,
                         wt_ref, vt_ref, pool_ref, ubuf_ref)
        gain = gmix_ref[...] * (1.0 + scale_ref[0])
        for t in range(tm // sb):
            x = x_ref[0, t * sb:(t + 1) * sb, :]
            ms = jnp.mean(x * x, axis=-1, keepdims=True)
            h = (x * lax.rsqrt(ms + EPS) * gain + shift_ref[0]).astype(BF16)
            write_ref[t] = jnp.dot(h, win_ref[...], preferred_element_type=F32)

    @pl.when(step % 2 == 0)
    def _():
        run(proj_a_ref, proj_b_ref)

    @pl.when(step % 2 == 1)
    def _():
        run(proj_b_ref, proj_a_ref)


def _inproj_post(t, proj, t0, pos_ref, segsum_ref, gq_ref, gk_ref, gkidx_ref, invf_ref, wpool_ref,
                 pscale_ref, qt_ref, kv_ref, qit_ref, ki_ref, wt_ref, vt_ref, pool_ref, ubuf_ref):
    sb = Q_BLK
    rows = slice(t * sb, (t + 1) * sb)
    d_attn = N_HEADS * HEAD_DIM
    d_qidx = N_IDX_HEADS * IDX_DIM

    def store_cols(dst_ref, chunk, j):
        ct = chunk.T
        for hh in range(2):
            col = (2 * j + hh) * Q_BLK
            dst_ref[0, t, :, col:col + Q_BLK] = ct[hh * HEAD_DIM:(hh + 1) * HEAD_DIM, :].astype(dst_ref.dtype)

    lane = lax.broadcasted_iota(jnp.int32, (sb, LANES), 1)
    first_half = (lane & (HEAD_DIM - 1)) < (HEAD_DIM // 2)
    ang = pos_ref[0, rows, :].astype(F32) * invf_ref[...]
    sin, cos = _sincos(ang)
    sin_signed = jnp.where(first_half, -sin, sin)
    rope = functools.partial(_rope_chunk, cos=cos, sin_signed=sin_signed, first_half=first_half)

    qf = proj[:, :d_attn]
    qsq = qf * qf
    qsq_hi = qsq.astype(BF16)
    qsq_lo = (qsq - qsq_hi.astype(F32)).astype(BF16)
    seg = segsum_ref[...]
    ssq = (jnp.dot(qsq_hi, seg, preferred_element_type=F32)
           + jnp.dot(qsq_lo, seg, preferred_element_type=F32))
    qn = qf * lax.rsqrt(ssq * (1.0 / HEAD_DIM) + EPS) * gq_ref[...]
    for j in range(d_attn // LANES):
        sl = slice(j * LANES, (j + 1) * LANES)
        store_cols(qt_ref, rope(qn[:, sl]), j)

    kvc = proj[:, d_attn:d_attn + LANES]
    is_k = lane < HEAD_DIM
    ksq = jnp.sum(jnp.where(is_k, kvc * kvc, 0.0), axis=-1, keepdims=True)
    kn = kvc * lax.rsqrt(ksq * (1.0 / HEAD_DIM) + EPS) * gk_ref[...]
    kv_ref[0, rows, :] = jnp.where(is_k, rope(kn), kvc).astype(BF16)
    row8 = lax.broadcasted_iota(jnp.int32, (SUBLANES, K_BLK), 0)
    vt_ref[0, t, 0:HEAD_DIM, :] = kvc.T[HEAD_DIM:, :].astype(BF16)
    vt_ref[0, t, HEAD_DIM:HEAD_DIM + SUBLANES, :] = jnp.where(row8 == 0, 1.0, 0.0).astype(BF16)

    o_qi = d_attn + LANES
    for j in range(d_qidx // LANES):
        store_cols(qit_ref, rope(proj[:, o_qi + j * LANES:o_qi + (j + 1) * LANES]), j)

    o_ki = o_qi + d_qidx
    kic = proj[:, o_ki:o_ki + LANES]
    kisq = jnp.sum(jnp.where(is_k, kic * kic, 0.0), axis=-1, keepdims=True)
    kin = kic * lax.rsqrt(kisq * (1.0 / IDX_DIM) + EPS) * gkidx_ref[...]
    ki_ref[0, rows, :] = jnp.where(is_k, rope(kin), 0.0).astype(BF16)
    wt_ref[0, t] = kic.T[IDX_DIM:IDX_DIM + N_IDX_HEADS, :] * (N_IDX_HEADS ** -0.5 * IDX_DIM ** -0.5)

    o_u = o_ki + LANES
    u = proj[:, o_u:o_u + LANES * len(POOL_WINDOWS)]

    base = MAX_WIN + t * sb
    t_idx = t0 + lax.broadcasted_iota(jnp.int32, (sb, 1), 0)
    level = u
    for g, win in enumerate(POOL_WINDOWS):
        sl = slice(g * LANES, (g + 1) * LANES)
        shift = win // 2
        ubuf_ref[g, base:base + sb, g * LANES:] = level
        level = level + ubuf_ref[g, base - shift:base - shift + sb, g * LANES:]
        wsum = level[:, :LANES]
        if g + 1 < len(POOL_WINDOWS):
            level = level[:, LANES:]
        cnt = jnp.minimum(t_idx + 1, win).astype(F32)
        pooled = wsum / cnt - u[:, sl]
        mixed = jnp.dot(pooled.astype(BF16), wpool_ref[g], preferred_element_type=F32)
        pool_ref[0, rows, sl] = (mixed * pscale_ref[:, sl]).astype(BF16)


def _inproj(pos3, x, scale1, shift1, g_mix, w_in_p, segsum, gq_t, gk_e, gkidx_e, invf, w_pool, pscale):
    b, s, d = x.shape
    tm = TM_PROJ
    d_attn = N_HEADS * HEAD_DIM
    d_qidx = N_IDX_HEADS * IDX_DIM
    d_pool = LANES * len(POOL_WINDOWS)
    assert tm % Q_BLK == 0 and Q_BLK == K_BLK and HEAD_DIM == IDX_DIM and 2 * HEAD_DIM == LANES
    n_tiles = s // tm
    ahead = lambda si: jnp.minimum(si, n_tiles - 1)
    behind = lambda si: jnp.maximum(si - 1, 0)
    tok = lambda w: pl.BlockSpec((1, tm, w), lambda bi, si: (bi, behind(si), 0))
    blk = lambda n, r, c: pl.BlockSpec((1, tm // n, r, c), lambda bi, si: (bi, behind(si), 0, 0))
    per_b = pl.BlockSpec((1, 1, d), lambda bi, si: (bi, 0, 0))
    full = lambda a: pl.BlockSpec(a.shape, lambda bi, si: (0,) * a.ndim)
    nqb, nkb = s // Q_BLK, s // K_BLK
    proj_buf = pltpu.VMEM((tm // Q_BLK, Q_BLK, w_in_p.shape[1]), F32)
    return pl.pallas_call(
        _inproj_kernel,
        out_shape=(jax.ShapeDtypeStruct((b, nqb, HEAD_DIM, N_HEADS * Q_BLK), BF16),
                   jax.ShapeDtypeStruct((b, s, LANES), BF16),
                   jax.ShapeDtypeStruct((b, nqb, IDX_DIM, N_IDX_HEADS * Q_BLK), BF16),
                   jax.ShapeDtypeStruct((b, s, LANES), BF16),
                   jax.ShapeDtypeStruct((b, nqb, N_IDX_HEADS, Q_BLK), F32),
                   jax.ShapeDtypeStruct((b, nkb, HEAD_DIM + SUBLANES, K_BLK), BF16),
                   jax.ShapeDtypeStruct((b, s, d_pool), BF16)),
        grid=(b, n_tiles + 1),
        in_specs=[tok(1), pl.BlockSpec((1, tm, d), lambda bi, si: (bi, ahead(si), 0)), per_b, per_b,
                  full(g_mix), full(w_in_p), full(segsum),
                  full(gq_t), full(gk_e), full(gkidx_e), full(invf), full(w_pool), full(pscale)],
        out_specs=(blk(Q_BLK, HEAD_DIM, N_HEADS * Q_BLK), tok(LANES),
                   blk(Q_BLK, IDX_DIM, N_IDX_HEADS * Q_BLK), tok(LANES),
                   blk(Q_BLK, N_IDX_HEADS, Q_BLK), blk(K_BLK, HEAD_DIM + SUBLANES, K_BLK), tok(d_pool)),
        scratch_shapes=[pltpu.VMEM((len(POOL_WINDOWS), tm + MAX_WIN, d_pool), F32),
                        proj_buf, proj_buf],
        compiler_params=_cparams(("arbitrary", "arbitrary")),
        name="inproj",
    )(pos3, x, scale1, shift1, g_mix, w_in_p, segsum, gq_t, gk_e, gkidx_e, invf, w_pool, pscale)


def _dsa_kernel(qt_ref, qit_ref, w_ref, kv_ref, ki_ref, vt_ref, o_ref,
                sc_ref, qe_ref, qie_ref, m_ref, mx_ref, st_ref, acc_ref, lg_ref, p_ref):
    topk = float(min(TOPK_MAX, (sc_ref.shape[0] * CNT_BLK) // 4))
    qb = pl.program_id(1)
    n_cols = qt_ref.shape[3]
    n_chunks = n_cols // COL_BLK
    sub = CNT_BLK // K_BLK
    nch = ((qb + 1) * Q_BLK + CNT_BLK - 1) // CNT_BLK
    kgrp = K_BLK // SUBLANES
    sub_rows = [slice(j * K_BLK, (j + 1) * K_BLK) for j in range(sub)]

    zeros_half = jnp.zeros((LANES - HEAD_DIM, n_cols), BF16)
    qe_ref[0:HEAD_DIM, :] = qt_ref[0, 0]
    qe_ref[HEAD_DIM:LANES, :] = zeros_half
    qie_ref[0:IDX_DIM, :] = qit_ref[0, 0]
    qie_ref[IDX_DIM:LANES, :] = zeros_half

    q_pos = qb * Q_BLK + lax.broadcasted_iota(jnp.int32, (K_BLK, Q_BLK), 1)
    key_off = lax.broadcasted_iota(jnp.int32, (K_BLK, Q_BLK), 0)

    def score_step(ch, carry):
        rmax, rmin = carry
        for j in range(sub):
            ki_blk = ki_ref[0, ch, sub_rows[j], :]
            score = None
            for cc in range(n_chunks):
                cs = slice(cc * COL_BLK, (cc + 1) * COL_BLK)
                s_h = jnp.dot(ki_blk, qie_ref[:, cs], preferred_element_type=F32)
                s_h = jnp.maximum(s_h, 0.0)
                for hh in range(COL_BLK // Q_BLK):
                    head = cc * (COL_BLK // Q_BLK) + hh
                    part = s_h[:, hh * Q_BLK:(hh + 1) * Q_BLK] * w_ref[0, 0, head:head + 1, :]
                    score = part if score is None else score + part
            causal = (ch * CNT_BLK + j * K_BLK + key_off) <= q_pos
            masked = jnp.where(causal, score, -jnp.inf)
            sc_ref[ch, sub_rows[j], :] = masked
            hi_part = masked.reshape(kgrp, SUBLANES, Q_BLK).max(axis=0)
            lo_part = jnp.where(causal, score, jnp.inf).reshape(kgrp, SUBLANES, Q_BLK).min(axis=0)
            rmax, rmin = jnp.maximum(rmax, hi_part), jnp.minimum(rmin, lo_part)
        return rmax, rmin

    def score_body(i, carry):
        return score_step(2 * i + 1, score_step(2 * i, carry))

    stats = lax.fori_loop(
        0, nch // 2, score_body,
        (jnp.full((SUBLANES, Q_BLK), -jnp.inf, F32), jnp.full((SUBLANES, Q_BLK), jnp.inf, F32)))
    rmax8, rmin8 = lax.cond(nch % 2 == 1, lambda c: score_step(nch - 1, c), lambda c: c, stats)
    rowmax = jnp.max(rmax8, axis=0, keepdims=True)
    rowmin = jnp.min(rmin8, axis=0, keepdims=True)

    n_causal = (qb * Q_BLK + 1 + lax.broadcasted_iota(jnp.int32, (1, Q_BLK), 1)).astype(F32)
    kt = jnp.minimum(n_causal, topk)

    cgrp = CNT_BLK // CNT_ROWS

    def count_ge(t):
        def body(ch, acc):
            for r in range(cgrp):
                rows = sc_ref[ch, r * CNT_ROWS:(r + 1) * CNT_ROWS, :]
                acc = acc + jnp.where(rows >= t, 1.0, 0.0)
            return acc
        acc = lax.fori_loop(0, nch, body, jnp.zeros((CNT_ROWS, Q_BLK), F32))
        return jnp.sum(acc, axis=0, keepdims=True)

    def bisect_pass(state):
        lo, hi, top, c_lo, c_hi, thr, done = state
        cap = jnp.minimum(hi, top)
        mid = lo + 0.5 * (cap - lo)
        mid = jnp.where(mid <= lo, cap, mid)
        c = count_ge(mid)
        hit = jnp.logical_and(done == 0.0, c == kt)
        thr = jnp.where(hit, mid, thr)
        done = jnp.where(hit, 1.0, done)
        active = done == 0.0
        up = jnp.logical_and(active, c >= kt)
        down = jnp.logical_and(active, c < kt)
        return (jnp.where(up, mid, lo), jnp.where(down, mid, hi), jnp.where(down, jnp.inf, top),
                jnp.where(up, c, c_lo), jnp.where(down, c, c_hi), thr, done)

    def snap_pass(state):
        lo, hi, top, c_lo, c_hi, thr, done = state

        def body(ch, carry):
            a8, b8 = carry
            for r in range(cgrp):
                s = sc_ref[ch, r * CNT_ROWS:(r + 1) * CNT_ROWS, :]
                a8 = jnp.minimum(a8, jnp.where(s >= lo, s, jnp.inf))
                b8 = jnp.maximum(b8, jnp.where(s < hi, s, -jnp.inf))
            return a8, b8

        a8, b8 = lax.fori_loop(
            0, nch, body,
            (jnp.full((CNT_ROWS, Q_BLK), jnp.inf, F32), jnp.full((CNT_ROWS, Q_BLK), -jnp.inf, F32)))
        a = jnp.min(a8, axis=0, keepdims=True)
        b = jnp.max(b8, axis=0, keepdims=True)
        active = done == 0.0
        hit = jnp.logical_and(active, jnp.logical_or(a == b, kt - c_hi == 1.0))
        thr = jnp.where(hit, b, thr)
        done = jnp.where(hit, 2.0, done)
        c_lo = jnp.where(jnp.logical_and(hit, a != b), kt + 1.0, c_lo)
        return jnp.where(active, a, lo), hi, jnp.where(active, b, top), c_lo, c_hi, thr, done

    few = n_causal <= topk
    state0 = (rowmin, jnp.full((1, Q_BLK), jnp.inf, F32), rowmax, n_causal,
              jnp.zeros((1, Q_BLK), F32), jnp.where(few, F32_LOWEST, 0.0), jnp.where(few, 1.0, 0.0))

    def outer_cond(carry):
        return carry[1] > 0.0

    def outer_body(carry):
        state, _ = carry
        state = lax.fori_loop(0, SEARCH_PERIOD, lambda i, st: bisect_pass(st), state)
        state = snap_pass(state)
        pending = jnp.max(jnp.where(state[6] == 0.0, 1.0, 0.0))
        return state, pending

    state1 = lax.fori_loop(0, SEARCH_FIRST, lambda i, st: bisect_pass(st), state0)
    state1 = snap_pass(state1)
    pending1 = jnp.max(jnp.where(state1[6] == 0.0, 1.0, 0.0))
    (lo, hi, _, c_lo, c_hi, thr, done), _ = lax.while_loop(outer_cond, outer_body, (state1, pending1))

    excess = jnp.where(done == 2.0, c_lo - kt, 0.0)
    need = kt - c_hi

    @pl.when(jnp.max(excess) > 0.0)
    def _():
        tri = (lax.broadcasted_iota(jnp.int32, (K_BLK, K_BLK), 0)
               >= lax.broadcasted_iota(jnp.int32, (K_BLK, K_BLK), 1)).astype(BF16)
        has_excess = excess > 0.0

        def drop_step(ch, run):
            for j in range(sub):
                s = sc_ref[ch, sub_rows[j], :]
                tied = jnp.logical_and(s == thr, has_excess)
                prefix = jnp.dot(tri, jnp.where(tied, 1.0, 0.0).astype(BF16), preferred_element_type=F32)
                drop = jnp.logical_and(tied, run + prefix > need)
                sc_ref[ch, sub_rows[j], :] = jnp.where(drop, -jnp.inf, s)
                run = run + jnp.max(prefix, axis=0, keepdims=True)
            return run

        run = lax.fori_loop(0, nch // 2, lambda i, r: drop_step(2 * i + 1, drop_step(2 * i, r)),
                            jnp.zeros((1, Q_BLK), F32))

        @pl.when(nch % 2 == 1)
        def _():
            drop_step(nch - 1, run)

    m_ref[...] = jnp.full(m_ref.shape, M_INIT, F32)
    acc_ref[...] = jnp.zeros(acc_ref.shape, F32)
    row_m = lambda j: slice(j, j + 1)
    row_a = lambda j: slice(sub + j, sub + j + 1)

    def logits_stage(ch, j):
        kv_blk = kv_ref[0, ch, sub_rows[j], :]
        bias = jnp.where(sc_ref[ch, sub_rows[j], :] >= thr, 0.0, MASKED)
        for cc in range(n_chunks):
            logits = jnp.dot(kv_blk, qe_ref[:, cc * COL_BLK:(cc + 1) * COL_BLK],
                             preferred_element_type=F32)
            for hh in range(COL_BLK // Q_BLK):
                cs = slice(cc * COL_BLK + hh * Q_BLK, cc * COL_BLK + (hh + 1) * Q_BLK)
                lg = logits[:, hh * Q_BLK:(hh + 1) * Q_BLK] + bias
                lg_ref[j, :, cs] = lg
                mx_ref[:, cs] = lg.reshape(kgrp, SUBLANES, Q_BLK).max(axis=0)
        m_old = m_ref[...]
        m_new = jnp.maximum(m_old, jnp.max(mx_ref[...], axis=0, keepdims=True))
        st_ref[row_m(j), :] = m_new
        st_ref[row_a(j), :] = jnp.exp2(m_old - m_new)
        m_ref[...] = m_new

    def probs_stage(j):
        p_ref[j] = jnp.exp2(lg_ref[j] - st_ref[row_m(j), :]).astype(BF16)

    def value_stage(kb, j, alpha):
        acc_ref[...] = acc_ref[...] * alpha + jnp.dot(
            vt_ref[0, kb], p_ref[j], preferred_element_type=F32)

    p_ref[sub - 1] = jnp.zeros(p_ref.shape[1:], BF16)
    st_ref[row_a(sub - 1), :] = jnp.ones((1, n_cols), F32)
    logits_stage(0, 0)

    def attn_body(ch, _):
        alpha_prev = st_ref[row_a(1), :]
        logits_stage(ch, 1)
        value_stage(jnp.maximum(ch * sub - 1, 0), 1, alpha_prev)
        probs_stage(0)
        alpha_cur = st_ref[row_a(0), :]
        logits_stage(jnp.minimum(ch + 1, nch - 1), 0)
        value_stage(ch * sub, 0, alpha_cur)
        probs_stage(1)
        return 0

    lax.fori_loop(0, nch, attn_body, 0)
    value_stage(nch * sub - 1, 1, st_ref[row_a(1), :])
    dh = qt_ref.shape[2]
    inv_l = 1.0 / acc_ref[dh:dh + 1, :]
    for j in range(n_cols // Q_BLK // 2):
        pair = [acc_ref[0:dh, (2 * j + hh) * Q_BLK:(2 * j + hh + 1) * Q_BLK]
                * inv_l[:, (2 * j + hh) * Q_BLK:(2 * j + hh + 1) * Q_BLK] for hh in range(2)]
        o_ref[0, :, j * 2 * dh:(j + 1) * 2 * dh] = jnp.concatenate(pair, axis=0).T.astype(o_ref.dtype)


def _dsa(qt, qit, w_t, kv4, ki4, vt4):
    b, nqb, dh, n_cols = qt.shape
    n_steps = kv4.shape[1]
    assert kv4.shape[2] == CNT_BLK and CNT_BLK == 2 * K_BLK and n_cols % COL_BLK == 0
    assert vt4.shape[1] * K_BLK == n_steps * CNT_BLK and vt4.shape[2] == dh + SUBLANES
    per_q = lambda a: pl.BlockSpec((1, 1) + a.shape[2:], lambda bi, qi: (bi, qi, 0, 0))
    per_b = lambda a: pl.BlockSpec((1,) + a.shape[1:], lambda bi, qi: (bi, 0, 0, 0))
    return pl.pallas_call(
        _dsa_kernel,
        out_shape=jax.ShapeDtypeStruct((b, nqb * Q_BLK, (n_cols // Q_BLK) * dh), BF16),
        grid=(b, nqb),
        in_specs=[per_q(qt), per_q(qit), per_q(w_t), per_b(kv4), per_b(ki4), per_b(vt4)],
        out_specs=pl.BlockSpec((1, Q_BLK, (n_cols // Q_BLK) * dh), lambda bi, qi: (bi, qi, 0)),
        scratch_shapes=[pltpu.VMEM((n_steps, CNT_BLK, Q_BLK), F32),
                        pltpu.VMEM((LANES, n_cols), BF16),
                        pltpu.VMEM((LANES, n_cols), BF16),
                        pltpu.VMEM((1, n_cols), F32),
                        pltpu.VMEM((SUBLANES, n_cols), F32),
                        pltpu.VMEM((SUBLANES, n_cols), F32),
                        pltpu.VMEM((dh + SUBLANES, n_cols), F32),
                        pltpu.VMEM((CNT_BLK // K_BLK, K_BLK, n_cols), F32),
                        pltpu.VMEM((CNT_BLK // K_BLK, K_BLK, n_cols), BF16)],
        compiler_params=_cparams(("arbitrary", "arbitrary")),
        name="dsa",
    )(qt, qit, w_t, kv4, ki4, vt4)


def _outproj_kernel(x_ref, attn_ref, pool_ref, woa_ref, wop_ref, gate1_ref, gffn_ref,
                    scale2_ref, shift2_ref, wr_ref, br_ref, x1_ref, h2_ref, gates_ref):
    tm = x_ref.shape[1]
    mix = (jnp.dot(attn_ref[0], woa_ref[...], preferred_element_type=F32)
           + jnp.dot(pool_ref[0], wop_ref[...], preferred_element_type=F32))
    x1 = x_ref[0] + gate1_ref[0] * mix
    x1_ref[0] = x1
    ms = jnp.mean(x1 * x1, axis=-1, keepdims=True)
    h2 = (x1 * lax.rsqrt(ms + EPS) * gffn_ref[...]) * (1.0 + scale2_ref[0]) + shift2_ref[0]
    h2_hi = h2.astype(BF16)
    h2_ref[0] = h2_hi

    h2_lo = (h2 - h2_hi.astype(F32)).astype(BF16)
    wr = wr_ref[...]
    wr_hi = wr.astype(BF16)
    wr_lo = (wr - wr_hi.astype(F32)).astype(BF16)
    logits = (jnp.dot(h2_hi, wr_hi, preferred_element_type=F32)
              + jnp.dot(h2_lo, wr_hi, preferred_element_type=F32)
              + jnp.dot(h2_hi, wr_lo, preferred_element_type=F32)) + br_ref[...]

    lane = lax.broadcasted_iota(jnp.int32, (tm, LANES), 1)
    big = jnp.int32(LANES)
    is_g = jnp.logical_and(lane >= N_EXPERTS, lane < N_EXPERTS + N_GROUPS)
    glog = jnp.where(is_g, logits, -jnp.inf)
    gmax = jnp.max(glog, axis=-1, keepdims=True)
    gsum = jnp.sum(jnp.exp(glog - gmax), axis=-1, keepdims=True)
    p_g = 1.0 / gsum
    g_sel = jnp.min(jnp.where(glog == gmax, lane, big), axis=-1, keepdims=True) - N_EXPERTS
    in_grp = jnp.logical_and(lane < N_EXPERTS,
                             jnp.right_shift(lane, EXPERTS_PER_GROUP.bit_length() - 1) == g_sel)
    elog = jnp.where(in_grp, logits, -jnp.inf)
    emax = jnp.max(elog, axis=-1, keepdims=True)
    eexp = jnp.exp(elog - emax)
    esum = jnp.sum(eexp, axis=-1, keepdims=True)
    p_e = jnp.where(in_grp, eexp / esum, -1.0)
    p1 = jnp.max(p_e, axis=-1, keepdims=True)
    i1 = jnp.min(jnp.where(p_e == p1, lane, big), axis=-1, keepdims=True)
    p_e2 = jnp.where(lane == i1, -1.0, p_e)
    p2 = jnp.max(p_e2, axis=-1, keepdims=True)
    i2 = jnp.min(jnp.where(p_e2 == p2, lane, big), axis=-1, keepdims=True)
    tot = p1 + p2
    gates_ref[0] = (jnp.where(lane == i1, p_g * (p1 / tot), 0.0)
                    + jnp.where(lane == i2, p_g * (p2 / tot), 0.0)
                    + jnp.where(lane == N_EXPERTS, g_sel.astype(F32), 0.0))


def _outproj(x, attn, pool, wo_a, wo_p, gate1, g_ffn, scale2, shift2, w_r, b_r):
    b, s, d = x.shape
    tm = TM_PROJ
    tok = lambda w: pl.BlockSpec((1, tm, w), lambda bi, si: (bi, si, 0))
    per_b = pl.BlockSpec((1, 1, d), lambda bi, si: (bi, 0, 0))
    full = lambda a: pl.BlockSpec(a.shape, lambda bi, si: (0,) * a.ndim)
    return pl.pallas_call(
        _outproj_kernel,
        out_shape=(jax.ShapeDtypeStruct((b, s, d), F32),
                   jax.ShapeDtypeStruct((b, s, d), BF16),
                   jax.ShapeDtypeStruct((b, s, LANES), F32)),
        grid=(b, s // tm),
        in_specs=[tok(d), tok(attn.shape[2]), tok(pool.shape[2]), full(wo_a), full(wo_p), per_b,
                  full(g_ffn), per_b, per_b, full(w_r), full(b_r)],
        out_specs=(tok(d), tok(d), tok(LANES)),
        compiler_params=_cparams(("arbitrary", "arbitrary")),
        name="outproj",
    )(x, attn, pool, wo_a, wo_p, gate1, g_ffn, scale2, shift2, w_r, b_r)


def _moe_kernel(x1_ref, h2_ref, gates_ref, gate2_ref, *refs):
    wgu_parts, wd_parts = refs[:MOE_W_PARTS], refs[MOE_W_PARTS:2 * MOE_W_PARTS]
    o_ref, xe_ref, rank_ref, rank_t_ref = refs[2 * MOE_W_PARTS:]
    g = pl.program_id(2)
    tm, d = h2_ref.shape[1], h2_ref.shape[2]
    per_part = wd_parts[0].shape[0]
    n_e, d_exp = per_part * MOE_W_PARTS, wd_parts[0].shape[1]
    lane = lax.broadcasted_iota(jnp.int32, (tm, LANES), 1)
    gf = g.astype(F32)

    @pl.when(g == 0)
    def _():
        tri = (lax.broadcasted_iota(jnp.int32, (RANK_BLK, RANK_BLK), 0)
               >= lax.broadcasted_iota(jnp.int32, (RANK_BLK, RANK_BLK), 1)).astype(BF16)
        lane_b = lax.broadcasted_iota(jnp.int32, (RANK_BLK, LANES), 1)
        run = jnp.zeros((1, LANES), F32)
        for sb in range(tm // RANK_BLK):
            rows = slice(sb * RANK_BLK, (sb + 1) * RANK_BLK)
            gts_b = gates_ref[0, rows, :]
            grp = jnp.sum(jnp.where(lane_b == N_EXPERTS, gts_b, 0.0), axis=-1, keepdims=True)
            member = jnp.where(jnp.logical_and(lane_b < N_GROUPS, lane_b.astype(F32) == grp), 1.0, 0.0)
            pre = jnp.dot(tri, member.astype(BF16), preferred_element_type=F32) + run
            rank_ref[rows, :] = jnp.where(lane_b == N_GROUPS, grp, pre)
            run = jnp.max(pre, axis=0, keepdims=True)
        rank_t_ref[...] = rank_ref[...].T
        gts = gates_ref[0]
        g_hi = gts.astype(BF16)
        xe_ref[:, :d] = h2_ref[0]
        xe_ref[:, d:d + LANES] = g_hi
        xe_ref[:, d + LANES:d + 2 * LANES] = (gts - g_hi.astype(F32)).astype(BF16)
        o_ref[0] = jnp.zeros((tm, d), F32)

    rank_row = rank_t_ref[pl.ds(g, 1), :]
    pos_row = jnp.where(rank_t_ref[N_GROUPS:N_GROUPS + 1, :] == gf, rank_row - 1.0, -1.0)
    rk = rank_ref[...]
    rank_col = jnp.sum(jnp.where(lane == g, rk, 0.0), axis=-1, keepdims=True)
    grp_col = jnp.sum(jnp.where(lane == N_GROUPS, rk, 0.0), axis=-1, keepdims=True)
    pos_col = jnp.where(grp_col == gf, rank_col - 1.0, -1.0)
    n_rows = jnp.max(rank_row).astype(jnp.int32)
    row_id = lax.broadcasted_iota(jnp.int32, (MOE_CH, tm), 0).astype(F32)
    col_id = lax.broadcasted_iota(jnp.int32, (tm, MOE_CH), 1).astype(F32)
    lane_c = lax.broadcasted_iota(jnp.int32, (MOE_CH, LANES), 1)

    def chunk_body(c, _):
        r0 = (c * MOE_CH).astype(F32)
        gather = jnp.where(pos_row - r0 == row_id, 1.0, 0.0).astype(BF16)
        xg = jnp.dot(gather, xe_ref[...], preferred_element_type=F32)
        xb = xg[:, :d].astype(BF16)
        gates_c = xg[:, d:d + LANES] + xg[:, d + LANES:d + 2 * LANES]
        ya = None
        for e in range(n_e):
            w_gu_e = wgu_parts[e // per_part][e % per_part]
            w_d_e = wd_parts[e // per_part][e % per_part]
            gu = jnp.dot(xb, w_gu_e, preferred_element_type=F32)
            gt = gu[:, :d_exp]
            a = (gt * jax.nn.sigmoid(gt)) * gu[:, d_exp:]
            gate_e = jnp.sum(jnp.where(lane_c == g * n_e + e, gates_c, 0.0), axis=-1, keepdims=True)
            y = jnp.dot((a * gate_e).astype(BF16), w_d_e, preferred_element_type=F32)
            ya = y if ya is None else ya + y
        scatter = jnp.where(pos_col - r0 == col_id, 1.0, 0.0).astype(BF16)
        o_ref[0] += jnp.dot(scatter, ya.astype(BF16), preferred_element_type=F32)
        return 0

    lax.fori_loop(0, (n_rows + MOE_CH - 1) // MOE_CH, chunk_body, 0)

    @pl.when(g == pl.num_programs(2) - 1)
    def _():
        o_ref[0] = x1_ref[0] + gate2_ref[0] * o_ref[0]


def _moe(x1, h2, gates, gate2, w_gu, w_d):
    b, s, d = x1.shape
    tm = TM_MOE
    n_e = EXPERTS_PER_GROUP
    assert w_gu.shape[0] == N_GROUPS * n_e and tm % RANK_BLK == 0 and n_e % MOE_W_PARTS == 0
    per_part = n_e // MOE_W_PARTS
    tok = lambda w: pl.BlockSpec((1, tm, w), lambda bi, si, g: (bi, si, 0))

    def slab(w, k):
        return pl.BlockSpec((per_part,) + w.shape[1:], lambda bi, si, g: (g * MOE_W_PARTS + k, 0, 0))

    return pl.pallas_call(
        _moe_kernel,
        out_shape=jax.ShapeDtypeStruct((b, s, d), F32),
        grid=(b, s // tm, N_GROUPS),
        in_specs=([tok(d), tok(d), tok(LANES), pl.BlockSpec((1, 1, d), lambda bi, si, g: (bi, 0, 0))]
                  + [slab(w_gu, k) for k in range(MOE_W_PARTS)]
                  + [slab(w_d, k) for k in range(MOE_W_PARTS)]),
        out_specs=tok(d),
        scratch_shapes=[pltpu.VMEM((tm, d + 2 * LANES), BF16),
                        pltpu.VMEM((tm, LANES), F32),
                        pltpu.VMEM((LANES, tm), F32)],
        compiler_params=pltpu.CompilerParams(
            dimension_semantics=("arbitrary", "arbitrary", "arbitrary"),
            vmem_limit_bytes=VMEM_LIMIT_MOE_BYTES),
        name="moe",
    )(x1, h2, gates, gate2, *([w_gu] * MOE_W_PARTS), *([w_d] * MOE_W_PARTS))


def _layer(x, mod, pos3, g_mix, g_ffn, w_in, g_q, g_k, g_kidx, w_pool, pool_scale, w_out,
           w_rg, b_rg, w_re, b_re, w_gate, w_up, w_down):
    b, s, d = x.shape
    d_attn = N_HEADS * HEAD_DIM
    nqb = s // Q_BLK
    nkb = s // K_BLK
    shift1, scale1, gate1, shift2, scale2, gate2 = [m[:, None, :] for m in jnp.split(mod, 6, axis=-1)]

    n_front = d_attn + 2 * HEAD_DIM + N_IDX_HEADS * IDX_DIM + IDX_DIM + N_IDX_HEADS
    pad = (-n_front) % LANES
    w_in_p = jnp.concatenate([w_in[:, :n_front], jnp.zeros((d, pad), w_in.dtype), w_in[:, n_front:]],
                             axis=1).astype(BF16)
    seg_id = jnp.arange(d_attn) // HEAD_DIM
    segsum = (seg_id[:, None] == seg_id[None, :]).astype(BF16)
    ones_half = jnp.ones((LANES - HEAD_DIM,), F32)
    gq_t = (jnp.tile(g_q, N_HEADS) * (LOG2_E * HEAD_DIM ** -0.5))[None, :]
    gk_e = jnp.concatenate([g_k, ones_half])[None, :]
    gkidx_e = jnp.concatenate([g_kidx, ones_half])[None, :]
    half = HEAD_DIM // 2
    inv_freq = ROPE_THETA ** (-jnp.arange(0, HEAD_DIM, 2, dtype=F32) / HEAD_DIM)
    invf = jnp.tile(inv_freq, LANES // half)[None, :]

    qt, kv, qit, ki, w_t, vt4, pool = _inproj(pos3, x, scale1, shift1, g_mix[None, :], w_in_p, segsum,
                                              gq_t, gk_e, gkidx_e, invf, w_pool.astype(BF16),
                                              pool_scale[None, :])
    kv4 = kv.reshape(b, s // CNT_BLK, CNT_BLK, LANES)
    ki4 = ki.reshape(b, s // CNT_BLK, CNT_BLK, LANES)
    attn = _dsa(qt, qit, w_t, kv4, ki4, vt4)

    w_out_b = w_out.astype(BF16)
    w_r = jnp.concatenate([w_re, w_rg, jnp.zeros((d, LANES - N_EXPERTS - N_GROUPS), F32)], axis=1)
    b_r = jnp.concatenate([b_re, b_rg, jnp.zeros((LANES - N_EXPERTS - N_GROUPS,), F32)])[None, :]
    x1, h2, gates = _outproj(x, attn, pool, w_out_b[:d_attn], w_out_b[d_attn:], gate1,
                             g_ffn[None, :], scale2, shift2, w_r, b_r)

    w_gu = jnp.concatenate([w_gate, w_up], axis=-1).astype(BF16)
    return _moe(x1, h2, gates, gate2, w_gu, w_down.astype(BF16))


def kernel(x, c, positions, w_ada, b_ada, g_norm_mix, g_norm_ffn, w_in, g_q, g_k, g_kidx, w_pool,
           pool_scale, w_out, w_router_group, b_router_group, w_router_expert, b_router_expert,
           w_gate, w_up, w_down):
    b, s, d = x.shape
    depth = w_ada.shape[0]
    assert s % TM_MOE == 0 and s % K_BLK == 0 and d % LANES == 0
    pos3 = positions[:, :, None]
    c_pad = jnp.concatenate([c, jnp.zeros((-b % SUBLANES, d), c.dtype)], axis=0)
    for l in range(depth):
        mod = _adaln(c_pad, w_ada[l], b_ada[l][None, :])[:b]
        x = _layer(x, mod, pos3, g_norm_mix[l], g_norm_ffn[l], w_in[l], g_q[l], g_k[l], g_kidx[l],
                   w_pool[l], pool_scale[l], w_out[l], w_router_group[l], b_router_group[l],
                   w_router_expert[l], b_router_expert[l], w_gate[l], w_up[l], w_down[l])
    return x
```

```python
import functools

import jax
import jax.numpy as jnp
from jax import lax
from jax.experimental import pallas as pl
from jax.experimental.pallas import tpu as pltpu

N_HEADS = 8
HEAD_DIM = 64
N_IDX_HEADS = 8
IDX_DIM = 64
TOPK_MAX = 256
ROPE_THETA = 10000.0
POOL_WINDOWS = (2, 4, 8, 16)
N_GROUPS = 4
EXPERTS_PER_GROUP = 8
N_EXPERTS = N_GROUPS * EXPERTS_PER_GROUP
EPS = 1e-6
N_MOD = 6
assert EXPERTS_PER_GROUP & (EXPERTS_PER_GROUP - 1) == 0

LANES = 128
SUBLANES = 8
VMEM_LIMIT_BYTES = 56 * 1024 * 1024
VMEM_LIMIT_MOE_BYTES = 60 * 1024 * 1024

Q_BLK = 256
K_BLK = 256
COL_BLK = 256
CNT_BLK = 512
CNT_ROWS = 32
SEARCH_FIRST = 15
SEARCH_PERIOD = 1
TM_PROJ = 512
TM_MOE = 1024
MOE_CH = 256
MOE_CH_TAIL = 128
RANK_BLK = 256
MOE_W_PARTS = 4
MAX_WIN = max(POOL_WINDOWS)
assert all(w == 2 ** (g + 1) for g, w in enumerate(POOL_WINDOWS))
M_INIT = -1e29
MASKED = -1e30
F32_LOWEST = -3.0e38
LOG2_E = 1.4426950408889634

BF16 = jnp.bfloat16
F32 = jnp.float32


def _cparams(sem):
    return pltpu.CompilerParams(dimension_semantics=sem, vmem_limit_bytes=VMEM_LIMIT_BYTES)


def _adaln_kernel(c_ref, w_ref, b_ref, o_ref):
    c = c_ref[...]
    c_act = c * jax.nn.sigmoid(c)
    o_ref[...] = jnp.dot(c_act, w_ref[...], preferred_element_type=F32) + b_ref[...]


def _adaln(c_pad, w_ada, b_ada):
    rows, d = c_pad.shape
    n = w_ada.shape[1]
    tn = n // N_MOD
    return pl.pallas_call(
        _adaln_kernel,
        out_shape=jax.ShapeDtypeStruct((rows, n), F32),
        grid=(n // tn,),
        in_specs=[pl.BlockSpec((rows, d), lambda j: (0, 0)),
                  pl.BlockSpec((d, tn), lambda j: (0, j)),
                  pl.BlockSpec((1, tn), lambda j: (0, j))],
        out_specs=pl.BlockSpec((rows, tn), lambda j: (0, j)),
        compiler_params=_cparams(("arbitrary",)),
        name="adaln",
    )(c_pad, w_ada, b_ada)


_PIO2_HI, _PIO2_MID, _PIO2_LO = 1.5703125, 4.837512969970703125e-4, 7.54978995489188e-8
_SIN_COEF = (-1.9515295891e-4, 8.3321608736e-3, -1.6666654611e-1)
_COS_COEF = (2.443315711809948e-5, -1.388731625493765e-3, 4.166664568298827e-2)


def _sincos(x):
    k = jnp.floor(x * (2.0 / jnp.pi) + 0.5)
    r = ((x - k * _PIO2_HI) - k * _PIO2_MID) - k * _PIO2_LO
    z = r * r
    s = r + r * z * (_SIN_COEF[2] + z * (_SIN_COEF[1] + z * _SIN_COEF[0]))
    c = 1.0 - 0.5 * z + z * z * (_COS_COEF[2] + z * (_COS_COEF[1] + z * _COS_COEF[0]))
    q = k - 4.0 * jnp.floor(k * 0.25)
    odd = jnp.logical_or(q == 1.0, q == 3.0)
    sin_b = jnp.where(odd, c, s)
    cos_b = jnp.where(odd, s, c)
    return (jnp.where(q >= 2.0, -sin_b, sin_b),
            jnp.where(jnp.logical_or(q == 1.0, q == 2.0), -cos_b, cos_b))


def _rope_chunk(y, cos, sin_signed, first_half):
    from_hi = pltpu.roll(y, LANES - HEAD_DIM // 2, 1)
    from_lo = pltpu.roll(y, HEAD_DIM // 2, 1)
    return y * cos + jnp.where(first_half, from_hi, from_lo) * sin_signed


def _inproj_kernel(pos_ref, x_ref, scale_ref, shift_ref, gmix_ref, win_ref, segsum_ref,
                   gq_ref, gk_ref, gkidx_ref, invf_ref, wpool_ref, pscale_ref,
                   qt_ref, kv_ref, qit_ref, ki_ref, wt_ref, vt_ref, pool_ref,
                   ubuf_ref, proj_a_ref, proj_b_ref):
    tm = x_ref.shape[1]
    sb = Q_BLK
    step = pl.program_id(1)
    tile = jnp.maximum(step - 1, 0)

    @pl.when(step == 0)
    def _():
        proj_b_ref[...] = jnp.zeros(proj_b_ref.shape, F32)

    @pl.when(step <= 1)
    def _():
        for lvl in range(len(POOL_WINDOWS)):
            ubuf_ref[lvl, 0:MAX_WIN, lvl * LANES:] = jnp.zeros(
                (MAX_WIN, ubuf_ref.shape[2] - lvl * LANES), F32)

    @pl.when(step > 1)
    def _():
        for lvl in range(len(POOL_WINDOWS)):
            ubuf_ref[lvl, 0:MAX_WIN, lvl * LANES:] = ubuf_ref[lvl, tm:tm + MAX_WIN, lvl * LANES:]

    def run(write_ref, read_ref):
        for t in range(tm // sb):
            _inproj_post(t, read_ref[t], tile * tm + t * sb, pos_ref, segsum_ref, gq_ref, gk_ref,
                         gkidx_ref, invf_ref, wpool_ref, pscale_ref, qt_ref, kv_ref, qit_ref, ki_ref,
                         wt_ref, vt_ref, pool_ref, ubuf_ref)
        gain = gmix_ref[...] * (1.0 + scale_ref[0])
        for t in range(tm // sb):
            x = x_ref[0, t * sb:(t + 1) * sb, :]
            ms = jnp.mean(x * x, axis=-1, keepdims=True)
            h = (x * lax.rsqrt(ms + EPS) * gain + shift_ref[0]).astype(BF16)
            write_ref[t] = jnp.dot(h, win_ref[...], preferred_element_type=F32)

    @pl.when(step % 2 == 0)
    def _():
        run(proj_a_ref, proj_b_ref)

    @pl.when(step % 2 == 1)
    def _():
        run(proj_b_ref, proj_a_ref)


def _inproj_post(t, proj, t0, pos_ref, segsum_ref, gq_ref, gk_ref, gkidx_ref, invf_ref, wpool_ref,
                 pscale_ref, qt_ref, kv_ref, qit_ref, ki_ref, wt_ref, vt_ref, pool_ref, ubuf_ref):
    sb = Q_BLK
    rows = slice(t * sb, (t + 1) * sb)
    d_attn = N_HEADS * HEAD_DIM
    d_qidx = N_IDX_HEADS * IDX_DIM

    def store_cols(dst_ref, chunk, j):
        ct = chunk.T
        for hh in range(2):
            col = (2 * j + hh) * Q_BLK
            dst_ref[0, t, :, col:col + Q_BLK] = ct[hh * HEAD_DIM:(hh + 1) * HEAD_DIM, :].astype(dst_ref.dtype)

    lane = lax.broadcasted_iota(jnp.int32, (sb, LANES), 1)
    first_half = (lane & (HEAD_DIM - 1)) < (HEAD_DIM // 2)
    ang = pos_ref[0, rows, :].astype(F32) * invf_ref[...]
    sin, cos = _sincos(ang)
    sin_signed = jnp.where(first_half, -sin, sin)
    rope = functools.partial(_rope_chunk, cos=cos, sin_signed=sin_signed, first_half=first_half)

    qf = proj[:, :d_attn]
    qsq = qf * qf
    qsq_hi = qsq.astype(BF16)
    qsq_lo = (qsq - qsq_hi.astype(F32)).astype(BF16)
    seg = segsum_ref[...]
    ssq = (jnp.dot(qsq_hi, seg, preferred_element_type=F32)
           + jnp.dot(qsq_lo, seg, preferred_element_type=F32))
    qn = qf * lax.rsqrt(ssq * (1.0 / HEAD_DIM) + EPS) * gq_ref[...]
    for j in range(d_attn // LANES):
        sl = slice(j * LANES, (j + 1) * LANES)
        store_cols(qt_ref, rope(qn[:, sl]), j)

    kvc = proj[:, d_attn:d_attn + LANES]
    is_k = lane < HEAD_DIM
    ksq = jnp.sum(jnp.where(is_k, kvc * kvc, 0.0), axis=-1, keepdims=True)
    kn = kvc * lax.rsqrt(ksq * (1.0 / HEAD_DIM) + EPS) * gk_ref[...]
    kv_ref[0, rows, :] = jnp.where(is_k, rope(kn), kvc).astype(BF16)
    row8 = lax.broadcasted_iota(jnp.int32, (SUBLANES, K_BLK), 0)
    vt_ref[0, t, 0:HEAD_DIM, :] = kvc.T[HEAD_DIM:, :].astype(BF16)
    vt_ref[0, t, HEAD_DIM:HEAD_DIM + SUBLANES, :] = jnp.where(row8 == 0, 1.0, 0.0).astype(BF16)

    o_qi = d_attn + LANES
    for j in range(d_qidx // LANES):
        store_cols(qit_ref, rope(proj[:, o_qi + j * LANES:o_qi + (j + 1) * LANES]), j)

    o_ki = o_qi + d_qidx
    kic = proj[:, o_ki:o_ki + LANES]
    kisq = jnp.sum(jnp.where(is_k, kic * kic, 0.0), axis=-1, keepdims=True)
    kin = kic * lax.rsqrt(kisq * (1.0 / IDX_DIM) + EPS) * gkidx_ref[...]
    ki_ref[0, rows, :] = jnp.where(is_k, rope(kin), 0.0).astype(BF16)
    wt_ref[0, t] = kic.T[IDX_DIM:IDX_DIM + N_IDX_HEADS, :] * (N_IDX_HEADS ** -0.5 * IDX_DIM ** -0.5)

    o_u = o_ki + LANES
    u = proj[:, o_u:o_u + LANES * len(POOL_WINDOWS)]

    base = MAX_WIN + t * sb
    t_idx = t0 + lax.broadcasted_iota(jnp.int32, (sb, 1), 0)
    level = u
    for g, win in enumerate(POOL_WINDOWS):
        sl = slice(g * LANES, (g + 1) * LANES)
        shift = win // 2
        ubuf_ref[g, base:base + sb, g * LANES:] = level
        level = level + ubuf_ref[g, base - shift:base - shift + sb, g * LANES:]
        wsum = level[:, :LANES]
        if g + 1 < len(POOL_WINDOWS):
            level = level[:, LANES:]
        cnt = jnp.minimum(t_idx + 1, win).astype(F32)
        pooled = wsum / cnt - u[:, sl]
        mixed = jnp.dot(pooled.astype(BF16), wpool_ref[g], preferred_element_type=F32)
        pool_ref[0, rows, sl] = (mixed * pscale_ref[:, sl]).astype(BF16)


def _inproj(pos3, x, scale1, shift1, g_mix, w_in_p, segsum, gq_t, gk_e, gkidx_e, invf, w_pool, pscale):
    b, s, d = x.shape
    tm = TM_PROJ
    d_attn = N_HEADS * HEAD_DIM
    d_qidx = N_IDX_HEADS * IDX_DIM
    d_pool = LANES * len(POOL_WINDOWS)
    assert tm % Q_BLK == 0 and Q_BLK == K_BLK and HEAD_DIM == IDX_DIM and 2 * HEAD_DIM == LANES
    n_tiles = s // tm
    ahead = lambda si: jnp.minimum(si, n_tiles - 1)
    behind = lambda si: jnp.maximum(si - 1, 0)
    tok = lambda w: pl.BlockSpec((1, tm, w), lambda bi, si: (bi, behind(si), 0))
    blk = lambda n, r, c: pl.BlockSpec((1, tm // n, r, c), lambda bi, si: (bi, behind(si), 0, 0))
    per_b = pl.BlockSpec((1, 1, d), lambda bi, si: (bi, 0, 0))
    full = lambda a: pl.BlockSpec(a.shape, lambda bi, si: (0,) * a.ndim)
    nqb, nkb = s // Q_BLK, s // K_BLK
    proj_buf = pltpu.VMEM((tm // Q_BLK, Q_BLK, w_in_p.shape[1]), F32)
    return pl.pallas_call(
        _inproj_kernel,
        out_shape=(jax.ShapeDtypeStruct((b, nqb, HEAD_DIM, N_HEADS * Q_BLK), BF16),
                   jax.ShapeDtypeStruct((b, s, LANES), BF16),
                   jax.ShapeDtypeStruct((b, nqb, IDX_DIM, N_IDX_HEADS * Q_BLK), BF16),
                   jax.ShapeDtypeStruct((b, s, LANES), BF16),
                   jax.ShapeDtypeStruct((b, nqb, N_IDX_HEADS, Q_BLK), F32),
                   jax.ShapeDtypeStruct((b, nkb, HEAD_DIM + SUBLANES, K_BLK), BF16),
                   jax.ShapeDtypeStruct((b, s, d_pool), BF16)),
        grid=(b, n_tiles + 1),
        in_specs=[tok(1), pl.BlockSpec((1, tm, d), lambda bi, si: (bi, ahead(si), 0)), per_b, per_b,
                  full(g_mix), full(w_in_p), full(segsum),
                  full(gq_t), full(gk_e), full(gkidx_e), full(invf), full(w_pool), full(pscale)],
        out_specs=(blk(Q_BLK, HEAD_DIM, N_HEADS * Q_BLK), tok(LANES),
                   blk(Q_BLK, IDX_DIM, N_IDX_HEADS * Q_BLK), tok(LANES),
                   blk(Q_BLK, N_IDX_HEADS, Q_BLK), blk(K_BLK, HEAD_DIM + SUBLANES, K_BLK), tok(d_pool)),
        scratch_shapes=[pltpu.VMEM((len(POOL_WINDOWS), tm + MAX_WIN, d_pool), F32),
                        proj_buf, proj_buf],
        compiler_params=_cparams(("arbitrary", "arbitrary")),
        name="inproj",
    )(pos3, x, scale1, shift1, g_mix, w_in_p, segsum, gq_t, gk_e, gkidx_e, invf, w_pool, pscale)


def _dsa_kernel(qt_ref, qit_ref, w_ref, kv_ref, ki_ref, vt_ref, o_ref,
                sc_ref, qe_ref, qie_ref, m_ref, mx_ref, st_ref, acc_ref, lg_ref, p_ref):
    topk = float(min(TOPK_MAX, (sc_ref.shape[0] * CNT_BLK) // 4))
    qb = pl.program_id(1)
    n_cols = qt_ref.shape[3]
    n_chunks = n_cols // COL_BLK
    sub = CNT_BLK // K_BLK
    nch = ((qb + 1) * Q_BLK + CNT_BLK - 1) // CNT_BLK
    kgrp = K_BLK // SUBLANES
    sub_rows = [slice(j * K_BLK, (j + 1) * K_BLK) for j in range(sub)]

    zeros_half = jnp.zeros((LANES - HEAD_DIM, n_cols), BF16)
    qe_ref[0:HEAD_DIM, :] = qt_ref[0, 0]
    qe_ref[HEAD_DIM:LANES, :] = zeros_half
    qie_ref[0:IDX_DIM, :] = qit_ref[0, 0]
    qie_ref[IDX_DIM:LANES, :] = zeros_half

    q_pos = qb * Q_BLK + lax.broadcasted_iota(jnp.int32, (K_BLK, Q_BLK), 1)
    key_off = lax.broadcasted_iota(jnp.int32, (K_BLK, Q_BLK), 0)

    def score_step(ch, carry):
        rmax, rmin = carry
        for j in range(sub):
            ki_blk = ki_ref[0, ch, sub_rows[j], :]
            score = None
            for cc in range(n_chunks):
                cs = slice(cc * COL_BLK, (cc + 1) * COL_BLK)
                s_h = jnp.dot(ki_blk, qie_ref[:, cs], preferred_element_type=F32)
                s_h = jnp.maximum(s_h, 0.0)
                for hh in range(COL_BLK // Q_BLK):
                    head = cc * (COL_BLK // Q_BLK) + hh
                    part = s_h[:, hh * Q_BLK:(hh + 1) * Q_BLK] * w_ref[0, 0, head:head + 1, :]
                    score = part if score is None else score + part
            causal = (ch * CNT_BLK + j * K_BLK + key_off) <= q_pos
            masked = jnp.where(causal, score, -jnp.inf)
            sc_ref[ch, sub_rows[j], :] = masked
            hi_part = masked.reshape(kgrp, SUBLANES, Q_BLK).max(axis=0)
            lo_part = jnp.where(causal, score, jnp.inf).reshape(kgrp, SUBLANES, Q_BLK).min(axis=0)
            rmax, rmin = jnp.maximum(rmax, hi_part), jnp.minimum(rmin, lo_part)
        return rmax, rmin

    def score_body(i, carry):
        return score_step(2 * i + 1, score_step(2 * i, carry))

    stats = lax.fori_loop(
        0, nch // 2, score_body,
        (jnp.full((SUBLANES, Q_BLK), -jnp.inf, F32), jnp.full((SUBLANES, Q_BLK), jnp.inf, F32)))
    rmax8, rmin8 = lax.cond(nch % 2 == 1, lambda c: score_step(nch - 1, c), lambda c: c, stats)
    rowmax = jnp.max(rmax8, axis=0, keepdims=True)
    rowmin = jnp.min(rmin8, axis=0, keepdims=True)

    n_causal = (qb * Q_BLK + 1 + lax.broadcasted_iota(jnp.int32, (1, Q_BLK), 1)).astype(F32)
    kt = jnp.minimum(n_causal, topk)

    cgrp = CNT_BLK // CNT_ROWS

    def count_ge(t):
        def body(ch, acc):
            for r in range(cgrp):
                rows = sc_ref[ch, r * CNT_ROWS:(r + 1) * CNT_ROWS, :]
                acc = acc + jnp.where(rows >= t, 1.0, 0.0)
            return acc
        acc = lax.fori_loop(0, nch, body, jnp.zeros((CNT_ROWS, Q_BLK), F32))
        return jnp.sum(acc, axis=0, keepdims=True)

    def bisect_pass(state):
        lo, hi, top, c_lo, c_hi, thr, done = state
        cap = jnp.minimum(hi, top)
        mid = lo + 0.5 * (cap - lo)
        mid = jnp.where(mid <= lo, cap, mid)
        c = count_ge(mid)
        hit = jnp.logical_and(done == 0.0, c == kt)
        thr = jnp.where(hit, mid, thr)
        done = jnp.where(hit, 1.0, done)
        active = done == 0.0
        up = jnp.logical_and(active, c >= kt)
        down = jnp.logical_and(active, c < kt)
        return (jnp.where(up, mid, lo), jnp.where(down, mid, hi), jnp.where(down, jnp.inf, top),
                jnp.where(up, c, c_lo), jnp.where(down, c, c_hi), thr, done)

    def snap_pass(state):
        lo, hi, top, c_lo, c_hi, thr, done = state

        def body(ch, carry):
            a8, b8 = carry
            for r in range(cgrp):
                s = sc_ref[ch, r * CNT_ROWS:(r + 1) * CNT_ROWS, :]
                a8 = jnp.minimum(a8, jnp.where(s >= lo, s, jnp.inf))
                b8 = jnp.maximum(b8, jnp.where(s < hi, s, -jnp.inf))
            return a8, b8

        a8, b8 = lax.fori_loop(
            0, nch, body,
            (jnp.full((CNT_ROWS, Q_BLK), jnp.inf, F32), jnp.full((CNT_ROWS, Q_BLK), -jnp.inf, F32)))
        a = jnp.min(a8, axis=0, keepdims=True)
        b = jnp.max(b8, axis=0, keepdims=True)
        active = done == 0.0
        hit = jnp.logical_and(active, jnp.logical_or(a == b, kt - c_hi == 1.0))
        thr = jnp.where(hit, b, thr)
        done = jnp.where(hit, 2.0, done)
        c_lo = jnp.where(jnp.logical_and(hit, a != b), kt + 1.0, c_lo)
        return jnp.where(active, a, lo), hi, jnp.where(active, b, top), c_lo, c_hi, thr, done

    few = n_causal <= topk
    state0 = (rowmin, jnp.full((1, Q_BLK), jnp.inf, F32), rowmax, n_causal,
              jnp.zeros((1, Q_BLK), F32), jnp.where(few, F32_LOWEST, 0.0), jnp.where(few, 1.0, 0.0))

    def outer_cond(carry):
        return carry[1] > 0.0

    def outer_body(carry):
        state, _ = carry
        state = lax.fori_loop(0, SEARCH_PERIOD, lambda i, st: bisect_pass(st), state)
        state = snap_pass(state)
        pending = jnp.max(jnp.where(state[6] == 0.0, 1.0, 0.0))
        return state, pending

    state1 = lax.fori_loop(0, SEARCH_FIRST, lambda i, st: bisect_pass(st), state0)
    state1 = snap_pass(state1)
    pending1 = jnp.max(jnp.where(state1[6] == 0.0, 1.0, 0.0))
    (lo, hi, _, c_lo, c_hi, thr, done), _ = lax.while_loop(outer_cond, outer_body, (state1, pending1))

    excess = jnp.where(done == 2.0, c_lo - kt, 0.0)
    need = kt - c_hi

    @pl.when(jnp.max(excess) > 0.0)
    def _():
        tri = (lax.broadcasted_iota(jnp.int32, (K_BLK, K_BLK), 0)
               >= lax.broadcasted_iota(jnp.int32, (K_BLK, K_BLK), 1)).astype(BF16)
        has_excess = excess > 0.0

        def drop_step(ch, run):
            for j in range(sub):
                s = sc_ref[ch, sub_rows[j], :]
                tied = jnp.logical_and(s == thr, has_excess)
                prefix = jnp.dot(tri, jnp.where(tied, 1.0, 0.0).astype(BF16), preferred_element_type=F32)
                drop = jnp.logical_and(tied, run + prefix > need)
                sc_ref[ch, sub_rows[j], :] = jnp.where(drop, -jnp.inf, s)
                run = run + jnp.max(prefix, axis=0, keepdims=True)
            return run

        run = lax.fori_loop(0, nch // 2, lambda i, r: drop_step(2 * i + 1, drop_step(2 * i, r)),
                            jnp.zeros((1, Q_BLK), F32))

        @pl.when(nch % 2 == 1)
        def _():
            drop_step(nch - 1, run)

    m_ref[...] = jnp.full(m_ref.shape, M_INIT, F32)
    acc_ref[...] = jnp.zeros(acc_ref.shape, F32)
    row_m = lambda j: slice(j, j + 1)
    row_a = lambda j: slice(sub + j, sub + j + 1)

    def logits_stage(ch, j):
        kv_blk = kv_ref[0, ch, sub_rows[j], :]
        bias = jnp.where(sc_ref[ch, sub_rows[j], :] >= thr, 0.0, MASKED)
        for cc in range(n_chunks):
            logits = jnp.dot(kv_blk, qe_ref[:, cc * COL_BLK:(cc + 1) * COL_BLK],
                             preferred_element_type=F32)
            for hh in range(COL_BLK // Q_BLK):
                cs = slice(cc * COL_BLK + hh * Q_BLK, cc * COL_BLK + (hh + 1) * Q_BLK)
                lg = logits[:, hh * Q_BLK:(hh + 1) * Q_BLK] + bias
                lg_ref[j, :, cs] = lg
                mx_ref[:, cs] = lg.reshape(kgrp, SUBLANES, Q_BLK).max(axis=0)
        m_old = m_ref[...]
        m_new = jnp.maximum(m_old, jnp.max(mx_ref[...], axis=0, keepdims=True))
        st_ref[row_m(j), :] = m_new
        st_ref[row_a(j), :] = jnp.exp2(m_old - m_new)
        m_ref[...] = m_new

    def probs_stage(j):
        p_ref[j] = jnp.exp2(lg_ref[j] - st_ref[row_m(j), :]).astype(BF16)

    def value_stage(kb, j, alpha):
        acc_ref[...] = acc_ref[...] * alpha + jnp.dot(
            vt_ref[0, kb], p_ref[j], preferred_element_type=F32)

    p_ref[sub - 1] = jnp.zeros(p_ref.shape[1:], BF16)
    st_ref[row_a(sub - 1), :] = jnp.ones((1, n_cols), F32)
    logits_stage(0, 0)

    def attn_body(ch, _):
        alpha_prev = st_ref[row_a(1), :]
        logits_stage(ch, 1)
        value_stage(jnp.maximum(ch * sub - 1, 0), 1, alpha_prev)
        probs_stage(0)
        alpha_cur = st_ref[row_a(0), :]
        logits_stage(jnp.minimum(ch + 1, nch - 1), 0)
        value_stage(ch * sub, 0, alpha_cur)
        probs_stage(1)
        return 0

    lax.fori_loop(0, nch, attn_body, 0)
    value_stage(nch * sub - 1, 1, st_ref[row_a(1), :])
    dh = qt_ref.shape[2]
    inv_l = 1.0 / acc_ref[dh:dh + 1, :]
    for j in range(n_cols // Q_BLK // 2):
        pair = [acc_ref[0:dh, (2 * j + hh) * Q_BLK:(2 * j + hh + 1) * Q_BLK]
                * inv_l[:, (2 * j + hh) * Q_BLK:(2 * j + hh + 1) * Q_BLK] for hh in range(2)]
        o_ref[0, :, j * 2 * dh:(j + 1) * 2 * dh] = jnp.concatenate(pair, axis=0).T.astype(o_ref.dtype)


def _dsa(qt, qit, w_t, kv4, ki4, vt4):
    b, nqb, dh, n_cols = qt.shape
    n_steps = kv4.shape[1]
    assert kv4.shape[2] == CNT_BLK and CNT_BLK == 2 * K_BLK and n_cols % COL_BLK == 0
    assert vt4.shape[1] * K_BLK == n_steps * CNT_BLK and vt4.shape[2] == dh + SUBLANES
    per_q = lambda a: pl.BlockSpec((1, 1) + a.shape[2:], lambda bi, qi: (bi, qi, 0, 0))
    per_b = lambda a: pl.BlockSpec((1,) + a.shape[1:], lambda bi, qi: (bi, 0, 0, 0))
    return pl.pallas_call(
        _dsa_kernel,
        out_shape=jax.ShapeDtypeStruct((b, nqb * Q_BLK, (n_cols // Q_BLK) * dh), BF16),
        grid=(b, nqb),
        in_specs=[per_q(qt), per_q(qit), per_q(w_t), per_b(kv4), per_b(ki4), per_b(vt4)],
        out_specs=pl.BlockSpec((1, Q_BLK, (n_cols // Q_BLK) * dh), lambda bi, qi: (bi, qi, 0)),
        scratch_shapes=[pltpu.VMEM((n_steps, CNT_BLK, Q_BLK), F32),
                        pltpu.VMEM((LANES, n_cols), BF16),
                        pltpu.VMEM((LANES, n_cols), BF16),
                        pltpu.VMEM((1, n_cols), F32),
                        pltpu.VMEM((SUBLANES, n_cols), F32),
                        pltpu.VMEM((SUBLANES, n_cols), F32),
                        pltpu.VMEM((dh + SUBLANES, n_cols), F32),
                        pltpu.VMEM((CNT_BLK // K_BLK, K_BLK, n_cols), F32),
                        pltpu.VMEM((CNT_BLK // K_BLK, K_BLK, n_cols), BF16)],
        compiler_params=_cparams(("arbitrary", "arbitrary")),
        name="dsa",
    )(qt, qit, w_t, kv4, ki4, vt4)


def _outproj_kernel(x_ref, attn_ref, pool_ref, woa_ref, wop_ref, gate1_ref, gffn_ref,
                    scale2_ref, shift2_ref, wr_ref, br_ref, x1_ref, h2_ref, gates_ref):
    tm = x_ref.shape[1]
    mix = (jnp.dot(attn_ref[0], woa_ref[...], preferred_element_type=F32)
           + jnp.dot(pool_ref[0], wop_ref[...], preferred_element_type=F32))
    x1 = x_ref[0] + gate1_ref[0] * mix
    x1_ref[0] = x1
    ms = jnp.mean(x1 * x1, axis=-1, keepdims=True)
    h2 = (x1 * lax.rsqrt(ms + EPS) * gffn_ref[...]) * (1.0 + scale2_ref[0]) + shift2_ref[0]
    h2_hi = h2.astype(BF16)
    h2_ref[0] = h2_hi

    h2_lo = (h2 - h2_hi.astype(F32)).astype(BF16)
    wr = wr_ref[...]
    wr_hi = wr.astype(BF16)
    wr_lo = (wr - wr_hi.astype(F32)).astype(BF16)
    logits = (jnp.dot(h2_hi, wr_hi, preferred_element_type=F32)
              + jnp.dot(h2_lo, wr_hi, preferred_element_type=F32)
              + jnp.dot(h2_hi, wr_lo, preferred_element_type=F32)) + br_ref[...]

    lt = logits.T
    n_e = EXPERTS_PER_GROUP
    sub_id = lax.broadcasted_iota(jnp.int32, (SUBLANES, tm), 0)
    big = jnp.int32(LANES)
    glog = jnp.where(sub_id < N_GROUPS, lt[N_EXPERTS:N_EXPERTS + SUBLANES, :], -jnp.inf)
    gmax = jnp.max(glog, axis=0, keepdims=True)
    gsum = jnp.sum(jnp.exp(glog - gmax), axis=0, keepdims=True)
    p_g = 1.0 / gsum
    g_sel = jnp.min(jnp.where(glog == gmax, sub_id, big), axis=0, keepdims=True)
    elog = lt[0:n_e, :]
    for gi in range(1, N_GROUPS):
        elog = jnp.where(g_sel == gi, lt[gi * n_e:(gi + 1) * n_e, :], elog)
    emax = jnp.max(elog, axis=0, keepdims=True)
    eexp = jnp.exp(elog - emax)
    p_e = eexp / jnp.sum(eexp, axis=0, keepdims=True)
    p1 = jnp.max(p_e, axis=0, keepdims=True)
    i1 = jnp.min(jnp.where(p_e == p1, sub_id, big), axis=0, keepdims=True)
    p_e2 = jnp.where(sub_id == i1, -1.0, p_e)
    p2 = jnp.max(p_e2, axis=0, keepdims=True)
    i2 = jnp.min(jnp.where(p_e2 == p2, sub_id, big), axis=0, keepdims=True)
    tot = p1 + p2
    in_grp = (jnp.where(sub_id == i1, p_g * (p1 / tot), 0.0)
              + jnp.where(sub_id == i2, p_g * (p2 / tot), 0.0))
    rows = [jnp.where(g_sel == gi, in_grp, 0.0) for gi in range(N_GROUPS)]
    rows.append(jnp.where(sub_id == 0, g_sel.astype(F32), 0.0))
    rows.append(jnp.zeros((LANES - N_EXPERTS - SUBLANES, tm), F32))
    gates_ref[0] = jnp.concatenate(rows, axis=0).T


def _outproj(x, attn, pool, wo_a, wo_p, gate1, g_ffn, scale2, shift2, w_r, b_r):
    b, s, d = x.shape
    tm = TM_PROJ
    tok = lambda w: pl.BlockSpec((1, tm, w), lambda bi, si: (bi, si, 0))
    per_b = pl.BlockSpec((1, 1, d), lambda bi, si: (bi, 0, 0))
    full = lambda a: pl.BlockSpec(a.shape, lambda bi, si: (0,) * a.ndim)
    return pl.pallas_call(
        _outproj_kernel,
        out_shape=(jax.ShapeDtypeStruct((b, s, d), F32),
                   jax.ShapeDtypeStruct((b, s, d), BF16),
                   jax.ShapeDtypeStruct((b, s, LANES), F32)),
        grid=(b, s // tm),
        in_specs=[tok(d), tok(attn.shape[2]), tok(pool.shape[2]), full(wo_a), full(wo_p), per_b,
                  full(g_ffn), per_b, per_b, full(w_r), full(b_r)],
        out_specs=(tok(d), tok(d), tok(LANES)),
        compiler_params=_cparams(("arbitrary", "arbitrary")),
        name="outproj",
    )(x, attn, pool, wo_a, wo_p, gate1, g_ffn, scale2, shift2, w_r, b_r)


def _moe_kernel(x1_ref, h2_ref, gates_ref, gate2_ref, *refs):
    wgu_parts, wd_parts = refs[:MOE_W_PARTS], refs[MOE_W_PARTS:2 * MOE_W_PARTS]
    o_ref, xe_ref, rank_ref, rank_t_ref = refs[2 * MOE_W_PARTS:]
    g = pl.program_id(2)
    tm, d = h2_ref.shape[1], h2_ref.shape[2]
    per_part = wd_parts[0].shape[0]
    n_e, d_exp = per_part * MOE_W_PARTS, wd_parts[0].shape[1]
    lane = lax.broadcasted_iota(jnp.int32, (tm, LANES), 1)
    gf = g.astype(F32)

    @pl.when(g == 0)
    def _():
        tri = (lax.broadcasted_iota(jnp.int32, (RANK_BLK, RANK_BLK), 0)
               >= lax.broadcasted_iota(jnp.int32, (RANK_BLK, RANK_BLK), 1)).astype(BF16)
        lane_b = lax.broadcasted_iota(jnp.int32, (RANK_BLK, LANES), 1)
        run = jnp.zeros((1, LANES), F32)
        for sb in range(tm // RANK_BLK):
            rows = slice(sb * RANK_BLK, (sb + 1) * RANK_BLK)
            gts_b = gates_ref[0, rows, :]
            grp = jnp.sum(jnp.where(lane_b == N_EXPERTS, gts_b, 0.0), axis=-1, keepdims=True)
            member = jnp.where(jnp.logical_and(lane_b < N_GROUPS, lane_b.astype(F32) == grp), 1.0, 0.0)
            pre = jnp.dot(tri, member.astype(BF16), preferred_element_type=F32) + run
            rank_ref[rows, :] = jnp.where(lane_b == N_GROUPS, grp, pre)
            run = jnp.max(pre, axis=0, keepdims=True)
        rank_t_ref[...] = rank_ref[...].T
        gts = gates_ref[0]
        g_hi = gts.astype(BF16)
        xe_ref[:, :d] = h2_ref[0]
        xe_ref[:, d:d + LANES] = g_hi
        xe_ref[:, d + LANES:d + 2 * LANES] = (gts - g_hi.astype(F32)).astype(BF16)
        o_ref[0] = jnp.zeros((tm, d), F32)

    rank_row = rank_t_ref[pl.ds(g, 1), :]
    pos_row = jnp.where(rank_t_ref[N_GROUPS:N_GROUPS + 1, :] == gf, rank_row - 1.0, -1.0)
    rk = rank_ref[...]
    rank_col = jnp.sum(jnp.where(lane == g, rk, 0.0), axis=-1, keepdims=True)
    grp_col = jnp.sum(jnp.where(lane == N_GROUPS, rk, 0.0), axis=-1, keepdims=True)
    pos_col = jnp.where(grp_col == gf, rank_col - 1.0, -1.0)
    n_rows = jnp.max(rank_row).astype(jnp.int32)

    def expert_pass(first_row, n_ch):
        row_id = lax.broadcasted_iota(jnp.int32, (n_ch, tm), 0).astype(F32)
        col_id = lax.broadcasted_iota(jnp.int32, (tm, n_ch), 1).astype(F32)
        lane_c = lax.broadcasted_iota(jnp.int32, (n_ch, LANES), 1)
        r0 = first_row.astype(F32)
        gather = jnp.where(pos_row - r0 == row_id, 1.0, 0.0).astype(BF16)
        xg = jnp.dot(gather, xe_ref[...], preferred_element_type=F32)
        xb = xg[:, :d].astype(BF16)
        gates_c = xg[:, d:d + LANES] + xg[:, d + LANES:d + 2 * LANES]
        ya = None
        for e in range(n_e):
            w_gu_e = wgu_parts[e // per_part][e % per_part]
            w_d_e = wd_parts[e // per_part][e % per_part]
            gu = jnp.dot(xb, w_gu_e, preferred_element_type=F32)
            gt = gu[:, :d_exp]
            a = (gt * jax.nn.sigmoid(gt)) * gu[:, d_exp:]
            gate_e = jnp.sum(jnp.where(lane_c == g * n_e + e, gates_c, 0.0), axis=-1, keepdims=True)
            y = jnp.dot((a * gate_e).astype(BF16), w_d_e, preferred_element_type=F32)
            ya = y if ya is None else ya + y
        scatter = jnp.where(pos_col - r0 == col_id, 1.0, 0.0).astype(BF16)
        o_ref[0] += jnp.dot(scatter, ya.astype(BF16), preferred_element_type=F32)

    n_full = (n_rows + MOE_CH - MOE_CH_TAIL - 1) // MOE_CH

    def full_pass(c, _):
        expert_pass(c * MOE_CH, MOE_CH)
        return 0

    lax.fori_loop(0, n_full, full_pass, 0)

    @pl.when(n_rows > n_full * MOE_CH)
    def _():
        expert_pass(n_full * MOE_CH, MOE_CH_TAIL)

    @pl.when(g == pl.num_programs(2) - 1)
    def _():
        o_ref[0] = x1_ref[0] + gate2_ref[0] * o_ref[0]


def _moe(x1, h2, gates, gate2, w_gu, w_d):
    b, s, d = x1.shape
    tm = TM_MOE
    n_e = EXPERTS_PER_GROUP
    assert w_gu.shape[0] == N_GROUPS * n_e and tm % RANK_BLK == 0 and n_e % MOE_W_PARTS == 0
    per_part = n_e // MOE_W_PARTS
    tok = lambda w: pl.BlockSpec((1, tm, w), lambda bi, si, g: (bi, si, 0))

    def slab(w, k):
        return pl.BlockSpec((per_part,) + w.shape[1:], lambda bi, si, g: (g * MOE_W_PARTS + k, 0, 0))

    return pl.pallas_call(
        _moe_kernel,
        out_shape=jax.ShapeDtypeStruct((b, s, d), F32),
        grid=(b, s // tm, N_GROUPS),
        in_specs=([tok(d), tok(d), tok(LANES), pl.BlockSpec((1, 1, d), lambda bi, si, g: (bi, 0, 0))]
                  + [slab(w_gu, k) for k in range(MOE_W_PARTS)]
                  + [slab(w_d, k) for k in range(MOE_W_PARTS)]),
        out_specs=tok(d),
        scratch_shapes=[pltpu.VMEM((tm, d + 2 * LANES), BF16),
                        pltpu.VMEM((tm, LANES), F32),
                        pltpu.VMEM((LANES, tm), F32)],
        compiler_params=pltpu.CompilerParams(
            dimension_semantics=("arbitrary", "arbitrary", "arbitrary"),
            vmem_limit_bytes=VMEM_LIMIT_MOE_BYTES),
        name="moe",
    )(x1, h2, gates, gate2, *([w_gu] * MOE_W_PARTS), *([w_d] * MOE_W_PARTS))


def _layer(x, mod, pos3, g_mix, g_ffn, w_in, g_q, g_k, g_kidx, w_pool, pool_scale, w_out,
           w_rg, b_rg, w_re, b_re, w_gate, w_up, w_down):
    b, s, d = x.shape
    d_attn = N_HEADS * HEAD_DIM
    nqb = s // Q_BLK
    nkb = s // K_BLK
    shift1, scale1, gate1, shift2, scale2, gate2 = [m[:, None, :] for m in jnp.split(mod, 6, axis=-1)]

    n_front = d_attn + 2 * HEAD_DIM + N_IDX_HEADS * IDX_DIM + IDX_DIM + N_IDX_HEADS
    pad = (-n_front) % LANES
    w_in_p = jnp.concatenate([w_in[:, :n_front], jnp.zeros((d, pad), w_in.dtype), w_in[:, n_front:]],
                             axis=1).astype(BF16)
    seg_id = jnp.arange(d_attn) // HEAD_DIM
    segsum = (seg_id[:, None] == seg_id[None, :]).astype(BF16)
    ones_half = jnp.ones((LANES - HEAD_DIM,), F32)
    gq_t = (jnp.tile(g_q, N_HEADS) * (LOG2_E * HEAD_DIM ** -0.5))[None, :]
    gk_e = jnp.concatenate([g_k, ones_half])[None, :]
    gkidx_e = jnp.concatenate([g_kidx, ones_half])[None, :]
    half = HEAD_DIM // 2
    inv_freq = ROPE_THETA ** (-jnp.arange(0, HEAD_DIM, 2, dtype=F32) / HEAD_DIM)
    invf = jnp.tile(inv_freq, LANES // half)[None, :]

    qt, kv, qit, ki, w_t, vt4, pool = _inproj(pos3, x, scale1, shift1, g_mix[None, :], w_in_p, segsum,
                                              gq_t, gk_e, gkidx_e, invf, w_pool.astype(BF16),
                                              pool_scale[None, :])
    kv4 = kv.reshape(b, s // CNT_BLK, CNT_BLK, LANES)
    ki4 = ki.reshape(b, s // CNT_BLK, CNT_BLK, LANES)
    attn = _dsa(qt, qit, w_t, kv4, ki4, vt4)

    w_out_b = w_out.astype(BF16)
    w_r = jnp.concatenate([w_re, w_rg, jnp.zeros((d, LANES - N_EXPERTS - N_GROUPS), F32)], axis=1)
    b_r = jnp.concatenate([b_re, b_rg, jnp.zeros((LANES - N_EXPERTS - N_GROUPS,), F32)])[None, :]
    x1, h2, gates = _outproj(x, attn, pool, w_out_b[:d_attn], w_out_b[d_attn:], gate1,
                             g_ffn[None, :], scale2, shift2, w_r, b_r)

    w_gu = jnp.concatenate([w_gate, w_up], axis=-1).astype(BF16)
    return _moe(x1, h2, gates, gate2, w_gu, w_down.astype(BF16))


def kernel(x, c, positions, w_ada, b_ada, g_norm_mix, g_norm_ffn, w_in, g_q, g_k, g_kidx, w_pool,
           pool_scale, w_out, w_router_group, b_router_group, w_router_expert, b_router_expert,
           w_gate, w_up, w_down):
    b, s, d = x.shape
    depth = w_ada.shape[0]
    assert s % TM_MOE == 0 and s % K_BLK == 0 and d % LANES == 0
    pos3 = positions[:, :, None]
    c_pad = jnp.concatenate([c, jnp.zeros((-b % SUBLANES, d), c.dtype)], axis=0)
    for l in range(depth):
        mod = _adaln(c_pad, w_ada[l], b_ada[l][None, :])[:b]
        x = _layer(x, mod, pos3, g_norm_mix[l], g_norm_ffn[l], w_in[l], g_q[l], g_k[l], g_kidx[l],
                   w_pool[l], pool_scale[l], w_out[l], w_router_group[l], b_router_group[l],
                   w_router_expert[l], b_router_expert[l], w_gate[l], w_up[l], w_down[l])
    return x
```

```python
import functools

import jax
import jax.numpy as jnp
from jax import lax
from jax.experimental import pallas as pl
from jax.experimental.pallas import tpu as pltpu

N_HEADS = 8
HEAD_DIM = 64
N_IDX_HEADS = 8
IDX_DIM = 64
TOPK_MAX = 256
ROPE_THETA = 10000.0
POOL_WINDOWS = (2, 4, 8, 16)
N_GROUPS = 4
EXPERTS_PER_GROUP = 8
N_EXPERTS = N_GROUPS * EXPERTS_PER_GROUP
EPS = 1e-6
N_MOD = 6
assert EXPERTS_PER_GROUP & (EXPERTS_PER_GROUP - 1) == 0

LANES = 128
SUBLANES = 8
VMEM_LIMIT_BYTES = 56 * 1024 * 1024
VMEM_LIMIT_MOE_BYTES = 60 * 1024 * 1024

Q_BLK = 256
K_BLK = 256
COL_BLK = 256
CNT_BLK = 512
CNT_ROWS = 32
SEARCH_FIRST = 15
SEARCH_PERIOD = 1
TM_PROJ = 512
TM_MOE = 1024
MOE_CH = 256
MOE_CH_TAIL = 128
RANK_BLK = 256
MOE_W_PARTS = 4
MAX_WIN = max(POOL_WINDOWS)
assert all(w == 2 ** (g + 1) for g, w in enumerate(POOL_WINDOWS))
M_INIT = -1e29
MASKED = -1e30
F32_LOWEST = -3.0e38
LOG2_E = 1.4426950408889634

BF16 = jnp.bfloat16
F32 = jnp.float32


def _cparams(sem):
    return pltpu.CompilerParams(dimension_semantics=sem, vmem_limit_bytes=VMEM_LIMIT_BYTES)


def _adaln_kernel(c_ref, w_ref, b_ref, o_ref):
    c = c_ref[...]
    c_act = c * jax.nn.sigmoid(c)
    o_ref[...] = jnp.dot(c_act, w_ref[...], preferred_element_type=F32) + b_ref[...]


def _adaln(c_pad, w_ada, b_ada):
    rows, d = c_pad.shape
    n = w_ada.shape[1]
    tn = n // N_MOD
    return pl.pallas_call(
        _adaln_kernel,
        out_shape=jax.ShapeDtypeStruct((rows, n), F32),
        grid=(n // tn,),
        in_specs=[pl.BlockSpec((rows, d), lambda j: (0, 0)),
                  pl.BlockSpec((d, tn), lambda j: (0, j)),
                  pl.BlockSpec((1, tn), lambda j: (0, j))],
        out_specs=pl.BlockSpec((rows, tn), lambda j: (0, j)),
        compiler_params=_cparams(("arbitrary",)),
        name="adaln",
    )(c_pad, w_ada, b_ada)


_PIO2_HI, _PIO2_MID, _PIO2_LO = 1.5703125, 4.837512969970703125e-4, 7.54978995489188e-8
_SIN_COEF = (-1.9515295891e-4, 8.3321608736e-3, -1.6666654611e-1)
_COS_COEF = (2.443315711809948e-5, -1.388731625493765e-3, 4.166664568298827e-2)


def _sincos(x):
    k = jnp.floor(x * (2.0 / jnp.pi) + 0.5)
    r = ((x - k * _PIO2_HI) - k * _PIO2_MID) - k * _PIO2_LO
    z = r * r
    s = r + r * z * (_SIN_COEF[2] + z * (_SIN_COEF[1] + z * _SIN_COEF[0]))
    c = 1.0 - 0.5 * z + z * z * (_COS_COEF[2] + z * (_COS_COEF[1] + z * _COS_COEF[0]))
    q = k - 4.0 * jnp.floor(k * 0.25)
    odd = jnp.logical_or(q == 1.0, q == 3.0)
    sin_b = jnp.where(odd, c, s)
    cos_b = jnp.where(odd, s, c)
    return (jnp.where(q >= 2.0, -sin_b, sin_b),
            jnp.where(jnp.logical_or(q == 1.0, q == 2.0), -cos_b, cos_b))


def _rope_chunk(y, cos, sin_signed, first_half):
    from_hi = pltpu.roll(y, LANES - HEAD_DIM // 2, 1)
    from_lo = pltpu.roll(y, HEAD_DIM // 2, 1)
    return y * cos + jnp.where(first_half, from_hi, from_lo) * sin_signed


def _inproj_kernel(pos_ref, x_ref, scale_ref, shift_ref, gmix_ref, win_ref, segsum_ref,
                   gq_ref, gk_ref, gkidx_ref, invf_ref, wpool_ref, pscale_ref,
                   qt_ref, kv_ref, qit_ref, ki_ref, wt_ref, vt_ref, pool_ref,
                   ubuf_ref, proj_a_ref, proj_b_ref):
    tm = x_ref.shape[1]
    sb = Q_BLK
    step = pl.program_id(1)
    tile = jnp.maximum(step - 1, 0)

    @pl.when(step == 0)
    def _():
        proj_b_ref[...] = jnp.zeros(proj_b_ref.shape, F32)

    @pl.when(step <= 1)
    def _():
        for lvl in range(len(POOL_WINDOWS)):
            ubuf_ref[lvl, 0:MAX_WIN, lvl * LANES:] = jnp.zeros(
                (MAX_WIN, ubuf_ref.shape[2] - lvl * LANES), F32)

    @pl.when(step > 1)
    def _():
        for lvl in range(len(POOL_WINDOWS)):
            ubuf_ref[lvl, 0:MAX_WIN, lvl * LANES:] = ubuf_ref[lvl, tm:tm + MAX_WIN, lvl * LANES:]

    def run(write_ref, read_ref):
        for t in range(tm // sb):
            _inproj_post(t, read_ref[t], tile * tm + t * sb, pos_ref, segsum_ref, gq_ref, gk_ref,
                         gkidx_ref, invf_ref, wpool_ref, pscale_ref, qt_ref, kv_ref, qit_ref, ki_ref,
                         wt_ref, vt_ref, pool_ref, ubuf_ref)
        gain = gmix_ref[...] * (1.0 + scale_ref[0])
        for t in range(tm // sb):
            x = x_ref[0, t * sb:(t + 1) * sb, :]
            ms = jnp.mean(x * x, axis=-1, keepdims=True)
            h = (x * lax.rsqrt(ms + EPS) * gain + shift_ref[0]).astype(BF16)
            write_ref[t] = jnp.dot(h, win_ref[...], preferred_element_type=F32)

    @pl.when(step % 2 == 0)
    def _():
        run(proj_a_ref, proj_b_ref)

    @pl.when(step % 2 == 1)
    def _():
        run(proj_b_ref, proj_a_ref)


def _inproj_post(t, proj, t0, pos_ref, segsum_ref, gq_ref, gk_ref, gkidx_ref, invf_ref, wpool_ref,
                 pscale_ref, qt_ref, kv_ref, qit_ref, ki_ref, wt_ref, vt_ref, pool_ref, ubuf_ref):
    sb = Q_BLK
    rows = slice(t * sb, (t + 1) * sb)
    d_attn = N_HEADS * HEAD_DIM
    d_qidx = N_IDX_HEADS * IDX_DIM

    def store_cols(dst_ref, chunk, j):
        ct = chunk.T
        for hh in range(2):
            col = (2 * j + hh) * Q_BLK
            dst_ref[0, t, :, col:col + Q_BLK] = ct[hh * HEAD_DIM:(hh + 1) * HEAD_DIM, :].astype(dst_ref.dtype)

    lane = lax.broadcasted_iota(jnp.int32, (sb, LANES), 1)
    first_half = (lane & (HEAD_DIM - 1)) < (HEAD_DIM // 2)
    ang = pos_ref[0, rows, :].astype(F32) * invf_ref[...]
    sin, cos = _sincos(ang)
    sin_signed = jnp.where(first_half, -sin, sin)
    rope = functools.partial(_rope_chunk, cos=cos, sin_signed=sin_signed, first_half=first_half)

    qf = proj[:, :d_attn]
    qsq = qf * qf
    qsq_hi = qsq.astype(BF16)
    qsq_lo = (qsq - qsq_hi.astype(F32)).astype(BF16)
    seg = segsum_ref[...]
    ssq = (jnp.dot(qsq_hi, seg, preferred_element_type=F32)
           + jnp.dot(qsq_lo, seg, preferred_element_type=F32))
    qn = qf * lax.rsqrt(ssq * (1.0 / HEAD_DIM) + EPS) * gq_ref[...]
    for j in range(d_attn // LANES):
        sl = slice(j * LANES, (j + 1) * LANES)
        store_cols(qt_ref, rope(qn[:, sl]), j)

    kvc = proj[:, d_attn:d_attn + LANES]
    is_k = lane < HEAD_DIM
    ksq = jnp.sum(jnp.where(is_k, kvc * kvc, 0.0), axis=-1, keepdims=True)
    kn = kvc * lax.rsqrt(ksq * (1.0 / HEAD_DIM) + EPS) * gk_ref[...]
    kv_ref[0, rows, :] = jnp.where(is_k, rope(kn), kvc).astype(BF16)
    row8 = lax.broadcasted_iota(jnp.int32, (SUBLANES, K_BLK), 0)
    vt_ref[0, t, 0:HEAD_DIM, :] = kvc.T[HEAD_DIM:, :].astype(BF16)
    vt_ref[0, t, HEAD_DIM:HEAD_DIM + SUBLANES, :] = jnp.where(row8 == 0, 1.0, 0.0).astype(BF16)

    o_qi = d_attn + LANES
    for j in range(d_qidx // LANES):
        store_cols(qit_ref, rope(proj[:, o_qi + j * LANES:o_qi + (j + 1) * LANES]), j)

    o_ki = o_qi + d_qidx
    kic = proj[:, o_ki:o_ki + LANES]
    kisq = jnp.sum(jnp.where(is_k, kic * kic, 0.0), axis=-1, keepdims=True)
    kin = kic * lax.rsqrt(kisq * (1.0 / IDX_DIM) + EPS) * gkidx_ref[...]
    ki_ref[0, rows, :] = jnp.where(is_k, rope(kin), 0.0).astype(BF16)
    wt_ref[0, t] = kic.T[IDX_DIM:IDX_DIM + N_IDX_HEADS, :] * (N_IDX_HEADS ** -0.5 * IDX_DIM ** -0.5)

    o_u = o_ki + LANES
    u = proj[:, o_u:o_u + LANES * len(POOL_WINDOWS)]

    base = MAX_WIN + t * sb
    t_idx = t0 + lax.broadcasted_iota(jnp.int32, (sb, 1), 0)
    level = u
    for g, win in enumerate(POOL_WINDOWS):
        sl = slice(g * LANES, (g + 1) * LANES)
        shift = win // 2
        ubuf_ref[g, base:base + sb, g * LANES:] = level
        level = level + ubuf_ref[g, base - shift:base - shift + sb, g * LANES:]
        wsum = level[:, :LANES]
        if g + 1 < len(POOL_WINDOWS):
            level = level[:, LANES:]
        cnt = jnp.minimum(t_idx + 1, win).astype(F32)
        pooled = wsum / cnt - u[:, sl]
        mixed = jnp.dot(pooled.astype(BF16), wpool_ref[g], preferred_element_type=F32)
        pool_ref[0, rows, sl] = (mixed * pscale_ref[:, sl]).astype(BF16)


def _inproj(pos3, x, scale1, shift1, g_mix, w_in_p, segsum, gq_t, gk_e, gkidx_e, invf, w_pool, pscale):
    b, s, d = x.shape
    tm = TM_PROJ
    d_attn = N_HEADS * HEAD_DIM
    d_qidx = N_IDX_HEADS * IDX_DIM
    d_pool = LANES * len(POOL_WINDOWS)
    assert tm % Q_BLK == 0 and Q_BLK == K_BLK and HEAD_DIM == IDX_DIM and 2 * HEAD_DIM == LANES
    n_tiles = s // tm
    ahead = lambda si: jnp.minimum(si, n_tiles - 1)
    behind = lambda si: jnp.maximum(si - 1, 0)
    tok = lambda w: pl.BlockSpec((1, tm, w), lambda bi, si: (bi, behind(si), 0))
    blk = lambda n, r, c: pl.BlockSpec((1, tm // n, r, c), lambda bi, si: (bi, behind(si), 0, 0))
    per_b = pl.BlockSpec((1, 1, d), lambda bi, si: (bi, 0, 0))
    full = lambda a: pl.BlockSpec(a.shape, lambda bi, si: (0,) * a.ndim)
    nqb, nkb = s // Q_BLK, s // K_BLK
    proj_buf = pltpu.VMEM((tm // Q_BLK, Q_BLK, w_in_p.shape[1]), F32)
    return pl.pallas_call(
        _inproj_kernel,
        out_shape=(jax.ShapeDtypeStruct((b, nqb, HEAD_DIM, N_HEADS * Q_BLK), BF16),
                   jax.ShapeDtypeStruct((b, s, LANES), BF16),
                   jax.ShapeDtypeStruct((b, nqb, IDX_DIM, N_IDX_HEADS * Q_BLK), BF16),
                   jax.ShapeDtypeStruct((b, s, LANES), BF16),
                   jax.ShapeDtypeStruct((b, nqb, N_IDX_HEADS, Q_BLK), F32),
                   jax.ShapeDtypeStruct((b, nkb, HEAD_DIM + SUBLANES, K_BLK), BF16),
                   jax.ShapeDtypeStruct((b, s, d_pool), BF16)),
        grid=(b, n_tiles + 1),
        in_specs=[tok(1), pl.BlockSpec((1, tm, d), lambda bi, si: (bi, ahead(si), 0)), per_b, per_b,
                  full(g_mix), full(w_in_p), full(segsum),
                  full(gq_t), full(gk_e), full(gkidx_e), full(invf), full(w_pool), full(pscale)],
        out_specs=(blk(Q_BLK, HEAD_DIM, N_HEADS * Q_BLK), tok(LANES),
                   blk(Q_BLK, IDX_DIM, N_IDX_HEADS * Q_BLK), tok(LANES),
                   blk(Q_BLK, N_IDX_HEADS, Q_BLK), blk(K_BLK, HEAD_DIM + SUBLANES, K_BLK), tok(d_pool)),
        scratch_shapes=[pltpu.VMEM((len(POOL_WINDOWS), tm + MAX_WIN, d_pool), F32),
                        proj_buf, proj_buf],
        compiler_params=_cparams(("arbitrary", "arbitrary")),
        name="inproj",
    )(pos3, x, scale1, shift1, g_mix, w_in_p, segsum, gq_t, gk_e, gkidx_e, invf, w_pool, pscale)


def _dsa_kernel(qt_ref, qit_ref, w_ref, kv_ref, ki_ref, vt_ref, o_ref,
                sc_ref, qe_ref, qie_ref, m_ref, mx_ref, st_ref, acc_ref, lg_ref, p_ref):
    topk = float(min(TOPK_MAX, (sc_ref.shape[0] * CNT_BLK) // 4))
    qb = pl.program_id(1)
    n_cols = qt_ref.shape[3]
    n_chunks = n_cols // COL_BLK
    sub = CNT_BLK // K_BLK
    nch = ((qb + 1) * Q_BLK + CNT_BLK - 1) // CNT_BLK
    kgrp = K_BLK // SUBLANES
    sub_rows = [slice(j * K_BLK, (j + 1) * K_BLK) for j in range(sub)]

    zeros_half = jnp.zeros((LANES - HEAD_DIM, n_cols), BF16)
    qe_ref[0:HEAD_DIM, :] = qt_ref[0, 0]
    qe_ref[HEAD_DIM:LANES, :] = zeros_half
    qie_ref[0:IDX_DIM, :] = qit_ref[0, 0]
    qie_ref[IDX_DIM:LANES, :] = zeros_half

    q_pos = qb * Q_BLK + lax.broadcasted_iota(jnp.int32, (K_BLK, Q_BLK), 1)
    key_off = lax.broadcasted_iota(jnp.int32, (K_BLK, Q_BLK), 0)

    def score_step(ch, carry):
        rmax, rmin = carry
        for j in range(sub):
            ki_blk = ki_ref[0, ch, sub_rows[j], :]
            score = None
            for cc in range(n_chunks):
                cs = slice(cc * COL_BLK, (cc + 1) * COL_BLK)
                s_h = jnp.dot(ki_blk, qie_ref[:, cs], preferred_element_type=F32)
                s_h = jnp.maximum(s_h, 0.0)
                for hh in range(COL_BLK // Q_BLK):
                    head = cc * (COL_BLK // Q_BLK) + hh
                    part = s_h[:, hh * Q_BLK:(hh + 1) * Q_BLK] * w_ref[0, 0, head:head + 1, :]
                    score = part if score is None else score + part
            causal = (ch * CNT_BLK + j * K_BLK + key_off) <= q_pos
            masked = jnp.where(causal, score, -jnp.inf)
            sc_ref[ch, sub_rows[j], :] = masked
            hi_part = masked.reshape(kgrp, SUBLANES, Q_BLK).max(axis=0)
            lo_part = jnp.where(causal, score, jnp.inf).reshape(kgrp, SUBLANES, Q_BLK).min(axis=0)
            rmax, rmin = jnp.maximum(rmax, hi_part), jnp.minimum(rmin, lo_part)
        return rmax, rmin

    def score_body(i, carry):
        return score_step(2 * i + 1, score_step(2 * i, carry))

    stats = lax.fori_loop(
        0, nch // 2, score_body,
        (jnp.full((SUBLANES, Q_BLK), -jnp.inf, F32), jnp.full((SUBLANES, Q_BLK), jnp.inf, F32)))
    rmax8, rmin8 = lax.cond(nch % 2 == 1, lambda c: score_step(nch - 1, c), lambda c: c, stats)
    rowmax = jnp.max(rmax8, axis=0, keepdims=True)
    rowmin = jnp.min(rmin8, axis=0, keepdims=True)

    n_causal = (qb * Q_BLK + 1 + lax.broadcasted_iota(jnp.int32, (1, Q_BLK), 1)).astype(F32)
    kt = jnp.minimum(n_causal, topk)

    cgrp = CNT_BLK // CNT_ROWS

    def count_ge(t):
        def body(ch, acc):
            for r in range(cgrp):
                rows = sc_ref[ch, r * CNT_ROWS:(r + 1) * CNT_ROWS, :]
                acc = acc + jnp.where(rows >= t, 1.0, 0.0)
            return acc
        acc = lax.fori_loop(0, nch, body, jnp.zeros((CNT_ROWS, Q_BLK), F32))
        return jnp.sum(acc, axis=0, keepdims=True)

    def bisect_pass(state):
        lo, hi, top, c_lo, c_hi, thr, done = state
        cap = jnp.minimum(hi, top)
        mid = lo + 0.5 * (cap - lo)
        mid = jnp.where(mid <= lo, cap, mid)
        c = count_ge(mid)
        hit = jnp.logical_and(done == 0.0, c == kt)
        thr = jnp.where(hit, mid, thr)
        done = jnp.where(hit, 1.0, done)
        active = done == 0.0
        up = jnp.logical_and(active, c >= kt)
        down = jnp.logical_and(active, c < kt)
        return (jnp.where(up, mid, lo), jnp.where(down, mid, hi), jnp.where(down, jnp.inf, top),
                jnp.where(up, c, c_lo), jnp.where(down, c, c_hi), thr, done)

    def snap_pass(state):
        lo, hi, top, c_lo, c_hi, thr, done = state

        def body(ch, carry):
            a8, b8 = carry
            for r in range(cgrp):
                s = sc_ref[ch, r * CNT_ROWS:(r + 1) * CNT_ROWS, :]
                a8 = jnp.minimum(a8, jnp.where(s >= lo, s, jnp.inf))
                b8 = jnp.maximum(b8, jnp.where(s < hi, s, -jnp.inf))
            return a8, b8

        a8, b8 = lax.fori_loop(
            0, nch, body,
            (jnp.full((CNT_ROWS, Q_BLK), jnp.inf, F32), jnp.full((CNT_ROWS, Q_BLK), -jnp.inf, F32)))
        a = jnp.min(a8, axis=0, keepdims=True)
        b = jnp.max(b8, axis=0, keepdims=True)
        active = done == 0.0
        hit = jnp.logical_and(active, jnp.logical_or(a == b, kt - c_hi == 1.0))
        thr = jnp.where(hit, b, thr)
        done = jnp.where(hit, 2.0, done)
        c_lo = jnp.where(jnp.logical_and(hit, a != b), kt + 1.0, c_lo)
        return jnp.where(active, a, lo), hi, jnp.where(active, b, top), c_lo, c_hi, thr, done

    few = n_causal <= topk
    state0 = (rowmin, jnp.full((1, Q_BLK), jnp.inf, F32), rowmax, n_causal,
              jnp.zeros((1, Q_BLK), F32), jnp.where(few, F32_LOWEST, 0.0), jnp.where(few, 1.0, 0.0))

    def outer_cond(carry):
        return carry[1] > 0.0

    def outer_body(carry):
        state, _ = carry
        state = lax.fori_loop(0, SEARCH_PERIOD, lambda i, st: bisect_pass(st), state)
        state = snap_pass(state)
        pending = jnp.max(jnp.where(state[6] == 0.0, 1.0, 0.0))
        return state, pending

    state1 = lax.fori_loop(0, SEARCH_FIRST, lambda i, st: bisect_pass(st), state0)
    state1 = snap_pass(state1)
    pending1 = jnp.max(jnp.where(state1[6] == 0.0, 1.0, 0.0))
    (lo, hi, _, c_lo, c_hi, thr, done), _ = lax.while_loop(outer_cond, outer_body, (state1, pending1))

    excess = jnp.where(done == 2.0, c_lo - kt, 0.0)
    need = kt - c_hi

    @pl.when(jnp.max(excess) > 0.0)
    def _():
        tri = (lax.broadcasted_iota(jnp.int32, (K_BLK, K_BLK), 0)
               >= lax.broadcasted_iota(jnp.int32, (K_BLK, K_BLK), 1)).astype(BF16)
        has_excess = excess > 0.0

        def drop_step(ch, run):
            for j in range(sub):
                s = sc_ref[ch, sub_rows[j], :]
                tied = jnp.logical_and(s == thr, has_excess)
                prefix = jnp.dot(tri, jnp.where(tied, 1.0, 0.0).astype(BF16), preferred_element_type=F32)
                drop = jnp.logical_and(tied, run + prefix > need)
                sc_ref[ch, sub_rows[j], :] = jnp.where(drop, -jnp.inf, s)
                run = run + jnp.max(prefix, axis=0, keepdims=True)
            return run

        run = lax.fori_loop(0, nch // 2, lambda i, r: drop_step(2 * i + 1, drop_step(2 * i, r)),
                            jnp.zeros((1, Q_BLK), F32))

        @pl.when(nch % 2 == 1)
        def _():
            drop_step(nch - 1, run)

    m_ref[...] = jnp.full(m_ref.shape, M_INIT, F32)
    acc_ref[...] = jnp.zeros(acc_ref.shape, F32)
    row_m = lambda j: slice(j, j + 1)
    row_a = lambda j: slice(sub + j, sub + j + 1)

    def logits_stage(ch, j):
        kv_blk = kv_ref[0, ch, sub_rows[j], :]
        bias = jnp.where(sc_ref[ch, sub_rows[j], :] >= thr, 0.0, MASKED)
        for cc in range(n_chunks):
            logits = jnp.dot(kv_blk, qe_ref[:, cc * COL_BLK:(cc + 1) * COL_BLK],
                             preferred_element_type=F32)
            for hh in range(COL_BLK // Q_BLK):
                cs = slice(cc * COL_BLK + hh * Q_BLK, cc * COL_BLK + (hh + 1) * Q_BLK)
                lg = logits[:, hh * Q_BLK:(hh + 1) * Q_BLK] + bias
                lg_ref[j, :, cs] = lg
                mx_ref[:, cs] = lg.reshape(kgrp, SUBLANES, Q_BLK).max(axis=0)
        m_old = m_ref[...]
        m_new = jnp.maximum(m_old, jnp.max(mx_ref[...], axis=0, keepdims=True))
        st_ref[row_m(j), :] = m_new
        st_ref[row_a(j), :] = jnp.exp2(m_old - m_new)
        m_ref[...] = m_new

    def probs_stage(j):
        p_ref[j] = jnp.exp2(lg_ref[j] - st_ref[row_m(j), :]).astype(BF16)

    def value_stage(kb, j, alpha):
        acc_ref[...] = acc_ref[...] * alpha + jnp.dot(
            vt_ref[0, kb], p_ref[j], preferred_element_type=F32)

    p_ref[sub - 1] = jnp.zeros(p_ref.shape[1:], BF16)
    st_ref[row_a(sub - 1), :] = jnp.ones((1, n_cols), F32)
    logits_stage(0, 0)

    def attn_body(ch, _):
        alpha_prev = st_ref[row_a(1), :]
        logits_stage(ch, 1)
        value_stage(jnp.maximum(ch * sub - 1, 0), 1, alpha_prev)
        probs_stage(0)
        alpha_cur = st_ref[row_a(0), :]
        logits_stage(jnp.minimum(ch + 1, nch - 1), 0)
        value_stage(ch * sub, 0, alpha_cur)
        probs_stage(1)
        return 0

    lax.fori_loop(0, nch, attn_body, 0)
    value_stage(nch * sub - 1, 1, st_ref[row_a(1), :])
    dh = qt_ref.shape[2]
    inv_l = 1.0 / acc_ref[dh:dh + 1, :]
    for j in range(n_cols // Q_BLK // 2):
        pair = [acc_ref[0:dh, (2 * j + hh) * Q_BLK:(2 * j + hh + 1) * Q_BLK]
                * inv_l[:, (2 * j + hh) * Q_BLK:(2 * j + hh + 1) * Q_BLK] for hh in range(2)]
        o_ref[0, :, j * 2 * dh:(j + 1) * 2 * dh] = jnp.concatenate(pair, axis=0).T.astype(o_ref.dtype)


def _dsa(qt, qit, w_t, kv4, ki4, vt4):
    b, nqb, dh, n_cols = qt.shape
    n_steps = kv4.shape[1]
    assert kv4.shape[2] == CNT_BLK and CNT_BLK == 2 * K_BLK and n_cols % COL_BLK == 0
    assert vt4.shape[1] * K_BLK == n_steps * CNT_BLK and vt4.shape[2] == dh + SUBLANES
    per_q = lambda a: pl.BlockSpec((1, 1) + a.shape[2:], lambda bi, qi: (bi, qi, 0, 0))
    per_b = lambda a: pl.BlockSpec((1,) + a.shape[1:], lambda bi, qi: (bi, 0, 0, 0))
    return pl.pallas_call(
        _dsa_kernel,
        out_shape=jax.ShapeDtypeStruct((b, nqb * Q_BLK, (n_cols // Q_BLK) * dh), BF16),
        grid=(b, nqb),
        in_specs=[per_q(qt), per_q(qit), per_q(w_t), per_b(kv4), per_b(ki4), per_b(vt4)],
        out_specs=pl.BlockSpec((1, Q_BLK, (n_cols // Q_BLK) * dh), lambda bi, qi: (bi, qi, 0)),
        scratch_shapes=[pltpu.VMEM((n_steps, CNT_BLK, Q_BLK), F32),
                        pltpu.VMEM((LANES, n_cols), BF16),
                        pltpu.VMEM((LANES, n_cols), BF16),
                        pltpu.VMEM((1, n_cols), F32),
                        pltpu.VMEM((SUBLANES, n_cols), F32),
                        pltpu.VMEM((SUBLANES, n_cols), F32),
                        pltpu.VMEM((dh + SUBLANES, n_cols), F32),
                        pltpu.VMEM((CNT_BLK // K_BLK, K_BLK, n_cols), F32),
                        pltpu.VMEM((CNT_BLK // K_BLK, K_BLK, n_cols), BF16)],
        compiler_params=_cparams(("arbitrary", "arbitrary")),
        name="dsa",
    )(qt, qit, w_t, kv4, ki4, vt4)


def _outproj_kernel(x_ref, attn_ref, pool_ref, woa_ref, wop_ref, gate1_ref, gffn_ref,
                    scale2_ref, shift2_ref, wr_ref, br_ref, x1_ref, h2_ref, gates_ref):
    tm = x_ref.shape[1]
    mix = (jnp.dot(attn_ref[0], woa_ref[...], preferred_element_type=F32)
           + jnp.dot(pool_ref[0], wop_ref[...], preferred_element_type=F32))
    x1 = x_ref[0] + gate1_ref[0] * mix
    x1_ref[0] = x1
    ms = jnp.mean(x1 * x1, axis=-1, keepdims=True)
    h2 = (x1 * lax.rsqrt(ms + EPS) * gffn_ref[...]) * (1.0 + scale2_ref[0]) + shift2_ref[0]
    h2_hi = h2.astype(BF16)
    h2_ref[0] = h2_hi

    h2_lo = (h2 - h2_hi.astype(F32)).astype(BF16)
    wr = wr_ref[...]
    wr_hi = wr.astype(BF16)
    wr_lo = (wr - wr_hi.astype(F32)).astype(BF16)
    logits = (jnp.dot(h2_hi, wr_hi, preferred_element_type=F32)
              + jnp.dot(h2_lo, wr_hi, preferred_element_type=F32)
              + jnp.dot(h2_hi, wr_lo, preferred_element_type=F32)) + br_ref[...]

    lt = logits.T
    n_e = EXPERTS_PER_GROUP
    sub_id = lax.broadcasted_iota(jnp.int32, (SUBLANES, tm), 0)
    big = jnp.int32(LANES)
    glog = jnp.where(sub_id < N_GROUPS, lt[N_EXPERTS:N_EXPERTS + SUBLANES, :], -jnp.inf)
    gmax = jnp.max(glog, axis=0, keepdims=True)
    gsum = jnp.sum(jnp.exp(glog - gmax), axis=0, keepdims=True)
    p_g = 1.0 / gsum
    g_sel = jnp.min(jnp.where(glog == gmax, sub_id, big), axis=0, keepdims=True)
    elog = lt[0:n_e, :]
    for gi in range(1, N_GROUPS):
        elog = jnp.where(g_sel == gi, lt[gi * n_e:(gi + 1) * n_e, :], elog)
    emax = jnp.max(elog, axis=0, keepdims=True)
    eexp = jnp.exp(elog - emax)
    p_e = eexp / jnp.sum(eexp, axis=0, keepdims=True)
    p1 = jnp.max(p_e, axis=0, keepdims=True)
    i1 = jnp.min(jnp.where(p_e == p1, sub_id, big), axis=0, keepdims=True)
    p_e2 = jnp.where(sub_id == i1, -1.0, p_e)
    p2 = jnp.max(p_e2, axis=0, keepdims=True)
    i2 = jnp.min(jnp.where(p_e2 == p2, sub_id, big), axis=0, keepdims=True)
    tot = p1 + p2
    in_grp = (jnp.where(sub_id == i1, p_g * (p1 / tot), 0.0)
              + jnp.where(sub_id == i2, p_g * (p2 / tot), 0.0))
    rows = [jnp.where(g_sel == gi, in_grp, 0.0) for gi in range(N_GROUPS)]
    rows.append(jnp.where(sub_id == 0, g_sel.astype(F32), 0.0))
    rows.append(jnp.zeros((LANES - N_EXPERTS - SUBLANES, tm), F32))
    gates_ref[0] = jnp.concatenate(rows, axis=0).T


def _outproj(x, attn, pool, wo_a, wo_p, gate1, g_ffn, scale2, shift2, w_r, b_r):
    b, s, d = x.shape
    tm = TM_PROJ
    tok = lambda w: pl.BlockSpec((1, tm, w), lambda bi, si: (bi, si, 0))
    per_b = pl.BlockSpec((1, 1, d), lambda bi, si: (bi, 0, 0))
    full = lambda a: pl.BlockSpec(a.shape, lambda bi, si: (0,) * a.ndim)
    return pl.pallas_call(
        _outproj_kernel,
        out_shape=(jax.ShapeDtypeStruct((b, s, d), F32),
                   jax.ShapeDtypeStruct((b, s, d), BF16),
                   jax.ShapeDtypeStruct((b, s, LANES), F32)),
        grid=(b, s // tm),
        in_specs=[tok(d), tok(attn.shape[2]), tok(pool.shape[2]), full(wo_a), full(wo_p), per_b,
                  full(g_ffn), per_b, per_b, full(w_r), full(b_r)],
        out_specs=(tok(d), tok(d), tok(LANES)),
        compiler_params=_cparams(("arbitrary", "arbitrary")),
        name="outproj",
    )(x, attn, pool, wo_a, wo_p, gate1, g_ffn, scale2, shift2, w_r, b_r)


def _moe_kernel(x1_ref, h2_ref, gates_ref, gate2_ref, *refs):
    wgu_parts, wd_parts = refs[:MOE_W_PARTS], refs[MOE_W_PARTS:2 * MOE_W_PARTS]
    o_ref, xe_ref, rank_ref, rank_t_ref, own_ref, grpb_ref = refs[2 * MOE_W_PARTS:]
    g = pl.program_id(2)
    tm, d = h2_ref.shape[1], h2_ref.shape[2]
    per_part = wd_parts[0].shape[0]
    n_e, d_exp = per_part * MOE_W_PARTS, wd_parts[0].shape[1]
    gf = g.astype(F32)

    @pl.when(g == 0)
    def _():
        tri = (lax.broadcasted_iota(jnp.int32, (RANK_BLK, RANK_BLK), 0)
               >= lax.broadcasted_iota(jnp.int32, (RANK_BLK, RANK_BLK), 1)).astype(BF16)
        lane_b = lax.broadcasted_iota(jnp.int32, (RANK_BLK, LANES), 1)
        run = jnp.zeros((1, LANES), F32)
        for sb in range(tm // RANK_BLK):
            rows = slice(sb * RANK_BLK, (sb + 1) * RANK_BLK)
            gts_b = gates_ref[0, rows, :]
            grp = jnp.sum(jnp.where(lane_b == N_EXPERTS, gts_b, 0.0), axis=-1, keepdims=True)
            member = jnp.where(jnp.logical_and(lane_b < N_GROUPS, lane_b.astype(F32) == grp), 1.0, 0.0)
            pre = jnp.dot(tri, member.astype(BF16), preferred_element_type=F32) + run
            rank_ref[rows, :] = jnp.where(lane_b == N_GROUPS, grp, pre)
            own = jnp.sum(member * pre, axis=-1, keepdims=True) - 1.0
            own_ref[rows, :] = jnp.broadcast_to(own, (RANK_BLK, LANES))
            grpb_ref[rows, :] = jnp.broadcast_to(grp, (RANK_BLK, LANES))
            run = jnp.max(pre, axis=0, keepdims=True)
        rank_t_ref[...] = rank_ref[...].T
        gts = gates_ref[0]
        g_hi = gts.astype(BF16)
        xe_ref[:, :d] = h2_ref[0]
        xe_ref[:, d:d + LANES] = g_hi
        xe_ref[:, d + LANES:d + 2 * LANES] = (gts - g_hi.astype(F32)).astype(BF16)
        o_ref[0] = jnp.zeros((tm, d), F32)

    rank_row = rank_t_ref[pl.ds(g, 1), :]
    pos_row = jnp.where(rank_t_ref[N_GROUPS:N_GROUPS + 1, :] == gf, rank_row - 1.0, -1.0)
    pos_col = jnp.where(grpb_ref[...] == gf, own_ref[...], -1.0)
    n_rows = jnp.max(rank_row).astype(jnp.int32)

    def expert_pass(first_row, n_ch):
        row_id = lax.broadcasted_iota(jnp.int32, (n_ch, tm), 0).astype(F32)
        col_id = lax.broadcasted_iota(jnp.int32, (tm, n_ch), 1).astype(F32)
        lane_c = lax.broadcasted_iota(jnp.int32, (n_ch, LANES), 1)
        r0 = first_row.astype(F32)
        gather = jnp.where(pos_row - r0 == row_id, 1.0, 0.0).astype(BF16)
        xg = jnp.dot(gather, xe_ref[...], preferred_element_type=F32)
        xb = xg[:, :d].astype(BF16)
        gates_c = xg[:, d:d + LANES] + xg[:, d + LANES:d + 2 * LANES]
        ya = None
        for e in range(n_e):
            w_gu_e = wgu_parts[e // per_part][e % per_part]
            w_d_e = wd_parts[e // per_part][e % per_part]
            gu = jnp.dot(xb, w_gu_e, preferred_element_type=F32)
            gt = gu[:, :d_exp]
            a = (gt * jax.nn.sigmoid(gt)) * gu[:, d_exp:]
            gate_e = jnp.sum(jnp.where(lane_c == g * n_e + e, gates_c, 0.0), axis=-1, keepdims=True)
            y = jnp.dot((a * gate_e).astype(BF16), w_d_e, preferred_element_type=F32)
            ya = y if ya is None else ya + y
        pos_wide = jnp.concatenate([pos_col] * (n_ch // LANES), axis=1)
        scatter = jnp.where(pos_wide - r0 == col_id, 1.0, 0.0).astype(BF16)
        o_ref[0] += jnp.dot(scatter, ya.astype(BF16), preferred_element_type=F32)

    n_full = (n_rows + MOE_CH - MOE_CH_TAIL - 1) // MOE_CH

    def full_pass(c, _):
        expert_pass(c * MOE_CH, MOE_CH)
        return 0

    lax.fori_loop(0, n_full, full_pass, 0)

    @pl.when(n_rows > n_full * MOE_CH)
    def _():
        expert_pass(n_full * MOE_CH, MOE_CH_TAIL)

    @pl.when(g == pl.num_programs(2) - 1)
    def _():
        o_ref[0] = x1_ref[0] + gate2_ref[0] * o_ref[0]


def _moe(x1, h2, gates, gate2, w_gu, w_d):
    b, s, d = x1.shape
    tm = TM_MOE
    n_e = EXPERTS_PER_GROUP
    assert w_gu.shape[0] == N_GROUPS * n_e and tm % RANK_BLK == 0 and n_e % MOE_W_PARTS == 0
    assert MOE_CH % LANES == 0 and MOE_CH_TAIL % LANES == 0 and MOE_CH_TAIL <= MOE_CH
    per_part = n_e // MOE_W_PARTS
    tok = lambda w: pl.BlockSpec((1, tm, w), lambda bi, si, g: (bi, si, 0))

    def slab(w, k):
        return pl.BlockSpec((per_part,) + w.shape[1:], lambda bi, si, g: (g * MOE_W_PARTS + k, 0, 0))

    return pl.pallas_call(
        _moe_kernel,
        out_shape=jax.ShapeDtypeStruct((b, s, d), F32),
        grid=(b, s // tm, N_GROUPS),
        in_specs=([tok(d), tok(d), tok(LANES), pl.BlockSpec((1, 1, d), lambda bi, si, g: (bi, 0, 0))]
                  + [slab(w_gu, k) for k in range(MOE_W_PARTS)]
                  + [slab(w_d, k) for k in range(MOE_W_PARTS)]),
        out_specs=tok(d),
        scratch_shapes=[pltpu.VMEM((tm, d + 2 * LANES), BF16),
                        pltpu.VMEM((tm, LANES), F32),
                        pltpu.VMEM((LANES, tm), F32),
                        pltpu.VMEM((tm, LANES), F32),
                        pltpu.VMEM((tm, LANES), F32)],
        compiler_params=pltpu.CompilerParams(
            dimension_semantics=("arbitrary", "arbitrary", "arbitrary"),
            vmem_limit_bytes=VMEM_LIMIT_MOE_BYTES),
        name="moe",
    )(x1, h2, gates, gate2, *([w_gu] * MOE_W_PARTS), *([w_d] * MOE_W_PARTS))


def _layer(x, mod, pos3, g_mix, g_ffn, w_in, g_q, g_k, g_kidx, w_pool, pool_scale, w_out,
           w_rg, b_rg, w_re, b_re, w_gate, w_up, w_down):
    b, s, d = x.shape
    d_attn = N_HEADS * HEAD_DIM
    nqb = s // Q_BLK
    nkb = s // K_BLK
    shift1, scale1, gate1, shift2, scale2, gate2 = [m[:, None, :] for m in jnp.split(mod, 6, axis=-1)]

    n_front = d_attn + 2 * HEAD_DIM + N_IDX_HEADS * IDX_DIM + IDX_DIM + N_IDX_HEADS
    pad = (-n_front) % LANES
    w_in_p = jnp.concatenate([w_in[:, :n_front], jnp.zeros((d, pad), w_in.dtype), w_in[:, n_front:]],
                             axis=1).astype(BF16)
    seg_id = jnp.arange(d_attn) // HEAD_DIM
    segsum = (seg_id[:, None] == seg_id[None, :]).astype(BF16)
    ones_half = jnp.ones((LANES - HEAD_DIM,), F32)
    gq_t = (jnp.tile(g_q, N_HEADS) * (LOG2_E * HEAD_DIM ** -0.5))[None, :]
    gk_e = jnp.concatenate([g_k, ones_half])[None, :]
    gkidx_e = jnp.concatenate([g_kidx, ones_half])[None, :]
    half = HEAD_DIM // 2
    inv_freq = ROPE_THETA ** (-jnp.arange(0, HEAD_DIM, 2, dtype=F32) / HEAD_DIM)
    invf = jnp.tile(inv_freq, LANES // half)[None, :]

    qt, kv, qit, ki, w_t, vt4, pool = _inproj(pos3, x, scale1, shift1, g_mix[None, :], w_in_p, segsum,
                                              gq_t, gk_e, gkidx_e, invf, w_pool.astype(BF16),
                                              pool_scale[None, :])
    kv4 = kv.reshape(b, s // CNT_BLK, CNT_BLK, LANES)
    ki4 = ki.reshape(b, s // CNT_BLK, CNT_BLK, LANES)
    attn = _dsa(qt, qit, w_t, kv4, ki4, vt4)

    w_out_b = w_out.astype(BF16)
    w_r = jnp.concatenate([w_re, w_rg, jnp.zeros((d, LANES - N_EXPERTS - N_GROUPS), F32)], axis=1)
    b_r = jnp.concatenate([b_re, b_rg, jnp.zeros((LANES - N_EXPERTS - N_GROUPS,), F32)])[None, :]
    x1, h2, gates = _outproj(x, attn, pool, w_out_b[:d_attn], w_out_b[d_attn:], gate1,
                             g_ffn[None, :], scale2, shift2, w_r, b_r)

    w_gu = jnp.concatenate([w_gate, w_up], axis=-1).astype(BF16)
    return _moe(x1, h2, gates, gate2, w_gu, w_down.astype(BF16))


def kernel(x, c, positions, w_ada, b_ada, g_norm_mix, g_norm_ffn, w_in, g_q, g_k, g_kidx, w_pool,
           pool_scale, w_out, w_router_group, b_router_group, w_router_expert, b_router_expert,
           w_gate, w_up, w_down):
    b, s, d = x.shape
    depth = w_ada.shape[0]
    assert s % TM_MOE == 0 and s % K_BLK == 0 and d % LANES == 0
    pos3 = positions[:, :, None]
    c_pad = jnp.concatenate([c, jnp.zeros((-b % SUBLANES, d), c.dtype)], axis=0)
    for l in range(depth):
        mod = _adaln(c_pad, w_ada[l], b_ada[l][None, :])[:b]
        x = _layer(x, mod, pos3, g_norm_mix[l], g_norm_ffn[l], w_in[l], g_q[l], g_k[l], g_kidx[l],
                   w_pool[l], pool_scale[l], w_out[l], w_router_group[l], b_router_group[l],
                   w_router_expert[l], b_router_expert[l], w_gate[l], w_up[l], w_down[l])
    return x
```

```python
import functools

import jax
import jax.numpy as jnp
from jax import lax
from jax.experimental import pallas as pl
from jax.experimental.pallas import tpu as pltpu

N_HEADS = 8
HEAD_DIM = 64
N_IDX_HEADS = 8
IDX_DIM = 64
TOPK_MAX = 256
ROPE_THETA = 10000.0
POOL_WINDOWS = (2, 4, 8, 16)
N_GROUPS = 4
EXPERTS_PER_GROUP = 8
N_EXPERTS = N_GROUPS * EXPERTS_PER_GROUP
EPS = 1e-6
N_MOD = 6

LANES = 128
SUBLANES = 8
assert EXPERTS_PER_GROUP == SUBLANES and N_GROUPS <= SUBLANES
VMEM_LIMIT_BYTES = 56 * 1024 * 1024
VMEM_LIMIT_MOE_BYTES = 60 * 1024 * 1024

Q_BLK = 256
K_BLK = 256
COL_BLK = 256
CNT_BLK = 512
CNT_ROWS = 32
SEARCH_FIRST = 15
SEARCH_PERIOD = 1
TM_PROJ = 512
TM_MOE = 1024
MOE_CH = 256
MOE_CH_TAIL = 128
RANK_BLK = 256
MOE_W_PARTS = 4
MAX_WIN = max(POOL_WINDOWS)
assert all(w == 2 ** (g + 1) for g, w in enumerate(POOL_WINDOWS))
M_INIT = -1e29
MASKED = -1e30
F32_LOWEST = -3.0e38
LOG2_E = 1.4426950408889634

BF16 = jnp.bfloat16
F32 = jnp.float32


def _cparams(sem):
    return pltpu.CompilerParams(dimension_semantics=sem, vmem_limit_bytes=VMEM_LIMIT_BYTES)


def _adaln_kernel(c_ref, w_ref, b_ref, o_ref):
    c = c_ref[...]
    c_act = c * jax.nn.sigmoid(c)
    o_ref[...] = jnp.dot(c_act, w_ref[...], preferred_element_type=F32) + b_ref[...]


def _adaln(c_pad, w_ada, b_ada):
    rows, d = c_pad.shape
    n = w_ada.shape[1]
    tn = n // N_MOD
    return pl.pallas_call(
        _adaln_kernel,
        out_shape=jax.ShapeDtypeStruct((rows, n), F32),
        grid=(n // tn,),
        in_specs=[pl.BlockSpec((rows, d), lambda j: (0, 0)),
                  pl.BlockSpec((d, tn), lambda j: (0, j)),
                  pl.BlockSpec((1, tn), lambda j: (0, j))],
        out_specs=pl.BlockSpec((rows, tn), lambda j: (0, j)),
        compiler_params=_cparams(("arbitrary",)),
        name="adaln",
    )(c_pad, w_ada, b_ada)


_PIO2_HI, _PIO2_MID, _PIO2_LO = 1.5703125, 4.837512969970703125e-4, 7.54978995489188e-8
_SIN_COEF = (-1.9515295891e-4, 8.3321608736e-3, -1.6666654611e-1)
_COS_COEF = (2.443315711809948e-5, -1.388731625493765e-3, 4.166664568298827e-2)


def _sincos(x):
    k = jnp.floor(x * (2.0 / jnp.pi) + 0.5)
    r = ((x - k * _PIO2_HI) - k * _PIO2_MID) - k * _PIO2_LO
    z = r * r
    s = r + r * z * (_SIN_COEF[2] + z * (_SIN_COEF[1] + z * _SIN_COEF[0]))
    c = 1.0 - 0.5 * z + z * z * (_COS_COEF[2] + z * (_COS_COEF[1] + z * _COS_COEF[0]))
    q = k - 4.0 * jnp.floor(k * 0.25)
    odd = jnp.logical_or(q == 1.0, q == 3.0)
    sin_b = jnp.where(odd, c, s)
    cos_b = jnp.where(odd, s, c)
    return (jnp.where(q >= 2.0, -sin_b, sin_b),
            jnp.where(jnp.logical_or(q == 1.0, q == 2.0), -cos_b, cos_b))


def _rope_chunk(y, cos, sin_signed, first_half):
    from_hi = pltpu.roll(y, LANES - HEAD_DIM // 2, 1)
    from_lo = pltpu.roll(y, HEAD_DIM // 2, 1)
    return y * cos + jnp.where(first_half, from_hi, from_lo) * sin_signed


def _inproj_kernel(pos_ref, x_ref, scale_ref, shift_ref, gmix_ref, win_ref, segsum_ref,
                   gq_ref, gk_ref, gkidx_ref, invf_ref, wpool_ref, pscale_ref,
                   qt_ref, kv_ref, qit_ref, ki_ref, wt_ref, vt_ref, pool_ref,
                   ubuf_ref, proj_a_ref, proj_b_ref):
    tm = x_ref.shape[1]
    sb = Q_BLK
    step = pl.program_id(1)
    tile = jnp.maximum(step - 1, 0)

    @pl.when(step == 0)
    def _():
        proj_b_ref[...] = jnp.zeros(proj_b_ref.shape, F32)

    @pl.when(step <= 1)
    def _():
        for lvl in range(len(POOL_WINDOWS)):
            ubuf_ref[lvl, 0:MAX_WIN, lvl * LANES:] = jnp.zeros(
                (MAX_WIN, ubuf_ref.shape[2] - lvl * LANES), F32)

    @pl.when(step > 1)
    def _():
        for lvl in range(len(POOL_WINDOWS)):
            ubuf_ref[lvl, 0:MAX_WIN, lvl * LANES:] = ubuf_ref[lvl, tm:tm + MAX_WIN, lvl * LANES:]

    def run(write_ref, read_ref):
        for t in range(tm // sb):
            _inproj_post(t, read_ref[t], tile * tm + t * sb, pos_ref, segsum_ref, gq_ref, gk_ref,
                         gkidx_ref, invf_ref, wpool_ref, pscale_ref, qt_ref, kv_ref, qit_ref, ki_ref,
                         wt_ref, vt_ref, pool_ref, ubuf_ref)
        gain = gmix_ref[...] * (1.0 + scale_ref[0])
        for t in range(tm // sb):
            x = x_ref[0, t * sb:(t + 1) * sb, :]
            ms = jnp.mean(x * x, axis=-1, keepdims=True)
            h = (x * lax.rsqrt(ms + EPS) * gain + shift_ref[0]).astype(BF16)
            write_ref[t] = jnp.dot(h, win_ref[...], preferred_element_type=F32)

    @pl.when(step % 2 == 0)
    def _():
        run(proj_a_ref, proj_b_ref)

    @pl.when(step % 2 == 1)
    def _():
        run(proj_b_ref, proj_a_ref)


def _inproj_post(t, proj, t0, pos_ref, segsum_ref, gq_ref, gk_ref, gkidx_ref, invf_ref, wpool_ref,
                 pscale_ref, qt_ref, kv_ref, qit_ref, ki_ref, wt_ref, vt_ref, pool_ref, ubuf_ref):
    sb = Q_BLK
    rows = slice(t * sb, (t + 1) * sb)
    d_attn = N_HEADS * HEAD_DIM
    d_qidx = N_IDX_HEADS * IDX_DIM

    def store_cols(dst_ref, chunk, j):
        ct = chunk.T
        for hh in range(2):
            col = (2 * j + hh) * Q_BLK
            dst_ref[0, t, :, col:col + Q_BLK] = ct[hh * HEAD_DIM:(hh + 1) * HEAD_DIM, :].astype(dst_ref.dtype)

    lane = lax.broadcasted_iota(jnp.int32, (sb, LANES), 1)
    first_half = (lane & (HEAD_DIM - 1)) < (HEAD_DIM // 2)
    ang = pos_ref[0, rows, :].astype(F32) * invf_ref[...]
    sin, cos = _sincos(ang)
    sin_signed = jnp.where(first_half, -sin, sin)
    rope = functools.partial(_rope_chunk, cos=cos, sin_signed=sin_signed, first_half=first_half)

    qf = proj[:, :d_attn]
    qsq = qf * qf
    qsq_hi = qsq.astype(BF16)
    qsq_lo = (qsq - qsq_hi.astype(F32)).astype(BF16)
    seg = segsum_ref[...]
    ssq = (jnp.dot(qsq_hi, seg, preferred_element_type=F32)
           + jnp.dot(qsq_lo, seg, preferred_element_type=F32))
    qn = qf * lax.rsqrt(ssq * (1.0 / HEAD_DIM) + EPS) * gq_ref[...]
    for j in range(d_attn // LANES):
        sl = slice(j * LANES, (j + 1) * LANES)
        store_cols(qt_ref, rope(qn[:, sl]), j)

    kvc = proj[:, d_attn:d_attn + LANES]
    is_k = lane < HEAD_DIM
    ksq = jnp.sum(jnp.where(is_k, kvc * kvc, 0.0), axis=-1, keepdims=True)
    kn = kvc * lax.rsqrt(ksq * (1.0 / HEAD_DIM) + EPS) * gk_ref[...]
    kv_ref[0, rows, :] = jnp.where(is_k, rope(kn), kvc).astype(BF16)
    row8 = lax.broadcasted_iota(jnp.int32, (SUBLANES, K_BLK), 0)
    vt_ref[0, t, 0:HEAD_DIM, :] = kvc.T[HEAD_DIM:, :].astype(BF16)
    vt_ref[0, t, HEAD_DIM:HEAD_DIM + SUBLANES, :] = jnp.where(row8 == 0, 1.0, 0.0).astype(BF16)

    o_qi = d_attn + LANES
    for j in range(d_qidx // LANES):
        store_cols(qit_ref, rope(proj[:, o_qi + j * LANES:o_qi + (j + 1) * LANES]), j)

    o_ki = o_qi + d_qidx
    kic = proj[:, o_ki:o_ki + LANES]
    kisq = jnp.sum(jnp.where(is_k, kic * kic, 0.0), axis=-1, keepdims=True)
    kin = kic * lax.rsqrt(kisq * (1.0 / IDX_DIM) + EPS) * gkidx_ref[...]
    ki_ref[0, rows, :] = jnp.where(is_k, rope(kin), 0.0).astype(BF16)
    wt_ref[0, t] = kic.T[IDX_DIM:IDX_DIM + N_IDX_HEADS, :] * (N_IDX_HEADS ** -0.5 * IDX_DIM ** -0.5)

    o_u = o_ki + LANES
    u = proj[:, o_u:o_u + LANES * len(POOL_WINDOWS)]

    base = MAX_WIN + t * sb
    t_idx = t0 + lax.broadcasted_iota(jnp.int32, (sb, 1), 0)
    level = u
    for g, win in enumerate(POOL_WINDOWS):
        sl = slice(g * LANES, (g + 1) * LANES)
        shift = win // 2
        ubuf_ref[g, base:base + sb, g * LANES:] = level
        level = level + ubuf_ref[g, base - shift:base - shift + sb, g * LANES:]
        wsum = level[:, :LANES]
        if g + 1 < len(POOL_WINDOWS):
            level = level[:, LANES:]
        cnt = jnp.minimum(t_idx + 1, win).astype(F32)
        pooled = wsum / cnt - u[:, sl]
        mixed = jnp.dot(pooled.astype(BF16), wpool_ref[g], preferred_element_type=F32)
        pool_ref[0, rows, sl] = (mixed * pscale_ref[:, sl]).astype(BF16)


def _inproj(pos3, x, scale1, shift1, g_mix, w_in_p, segsum, gq_t, gk_e, gkidx_e, invf, w_pool, pscale):
    b, s, d = x.shape
    tm = TM_PROJ
    d_attn = N_HEADS * HEAD_DIM
    d_qidx = N_IDX_HEADS * IDX_DIM
    d_pool = LANES * len(POOL_WINDOWS)
    assert tm % Q_BLK == 0 and Q_BLK == K_BLK and HEAD_DIM == IDX_DIM and 2 * HEAD_DIM == LANES
    n_tiles = s // tm
    ahead = lambda si: jnp.minimum(si, n_tiles - 1)
    behind = lambda si: jnp.maximum(si - 1, 0)
    tok = lambda w: pl.BlockSpec((1, tm, w), lambda bi, si: (bi, behind(si), 0))
    blk = lambda n, r, c: pl.BlockSpec((1, tm // n, r, c), lambda bi, si: (bi, behind(si), 0, 0))
    per_b = pl.BlockSpec((1, 1, d), lambda bi, si: (bi, 0, 0))
    full = lambda a: pl.BlockSpec(a.shape, lambda bi, si: (0,) * a.ndim)
    nqb, nkb = s // Q_BLK, s // K_BLK
    proj_buf = pltpu.VMEM((tm // Q_BLK, Q_BLK, w_in_p.shape[1]), F32)
    return pl.pallas_call(
        _inproj_kernel,
        out_shape=(jax.ShapeDtypeStruct((b, nqb, HEAD_DIM, N_HEADS * Q_BLK), BF16),
                   jax.ShapeDtypeStruct((b, s, LANES), BF16),
                   jax.ShapeDtypeStruct((b, nqb, IDX_DIM, N_IDX_HEADS * Q_BLK), BF16),
                   jax.ShapeDtypeStruct((b, s, LANES), BF16),
                   jax.ShapeDtypeStruct((b, nqb, N_IDX_HEADS, Q_BLK), F32),
                   jax.ShapeDtypeStruct((b, nkb, HEAD_DIM + SUBLANES, K_BLK), BF16),
                   jax.ShapeDtypeStruct((b, s, d_pool), BF16)),
        grid=(b, n_tiles + 1),
        in_specs=[tok(1), pl.BlockSpec((1, tm, d), lambda bi, si: (bi, ahead(si), 0)), per_b, per_b,
                  full(g_mix), full(w_in_p), full(segsum),
                  full(gq_t), full(gk_e), full(gkidx_e), full(invf), full(w_pool), full(pscale)],
        out_specs=(blk(Q_BLK, HEAD_DIM, N_HEADS * Q_BLK), tok(LANES),
                   blk(Q_BLK, IDX_DIM, N_IDX_HEADS * Q_BLK), tok(LANES),
                   blk(Q_BLK, N_IDX_HEADS, Q_BLK), blk(K_BLK, HEAD_DIM + SUBLANES, K_BLK), tok(d_pool)),
        scratch_shapes=[pltpu.VMEM((len(POOL_WINDOWS), tm + MAX_WIN, d_pool), F32),
                        proj_buf, proj_buf],
        compiler_params=_cparams(("arbitrary", "arbitrary")),
        name="inproj",
    )(pos3, x, scale1, shift1, g_mix, w_in_p, segsum, gq_t, gk_e, gkidx_e, invf, w_pool, pscale)


def _dsa_kernel(qt_ref, qit_ref, w_ref, kv_ref, ki_ref, vt_ref, o_ref,
                sc_ref, qe_ref, qie_ref, m_ref, mx_ref, st_ref, acc_ref, lg_ref, p_ref):
    topk = float(min(TOPK_MAX, (sc_ref.shape[0] * CNT_BLK) // 4))
    qb = pl.program_id(1)
    n_cols = qt_ref.shape[3]
    n_chunks = n_cols // COL_BLK
    sub = CNT_BLK // K_BLK
    nch = ((qb + 1) * Q_BLK + CNT_BLK - 1) // CNT_BLK
    kgrp = K_BLK // SUBLANES
    sub_rows = [slice(j * K_BLK, (j + 1) * K_BLK) for j in range(sub)]

    zeros_half = jnp.zeros((LANES - HEAD_DIM, n_cols), BF16)
    qe_ref[0:HEAD_DIM, :] = qt_ref[0, 0]
    qe_ref[HEAD_DIM:LANES, :] = zeros_half
    qie_ref[0:IDX_DIM, :] = qit_ref[0, 0]
    qie_ref[IDX_DIM:LANES, :] = zeros_half

    q_pos = qb * Q_BLK + lax.broadcasted_iota(jnp.int32, (K_BLK, Q_BLK), 1)
    key_off = lax.broadcasted_iota(jnp.int32, (K_BLK, Q_BLK), 0)

    def score_step(ch, carry):
        rmax, rmin = carry
        for j in range(sub):
            ki_blk = ki_ref[0, ch, sub_rows[j], :]
            score = None
            for cc in range(n_chunks):
                cs = slice(cc * COL_BLK, (cc + 1) * COL_BLK)
                s_h = jnp.dot(ki_blk, qie_ref[:, cs], preferred_element_type=F32)
                s_h = jnp.maximum(s_h, 0.0)
                for hh in range(COL_BLK // Q_BLK):
                    head = cc * (COL_BLK // Q_BLK) + hh
                    part = s_h[:, hh * Q_BLK:(hh + 1) * Q_BLK] * w_ref[0, 0, head:head + 1, :]
                    score = part if score is None else score + part
            causal = (ch * CNT_BLK + j * K_BLK + key_off) <= q_pos
            masked = jnp.where(causal, score, -jnp.inf)
            sc_ref[ch, sub_rows[j], :] = masked
            hi_part = masked.reshape(kgrp, SUBLANES, Q_BLK).max(axis=0)
            lo_part = jnp.where(causal, score, jnp.inf).reshape(kgrp, SUBLANES, Q_BLK).min(axis=0)
            rmax, rmin = jnp.maximum(rmax, hi_part), jnp.minimum(rmin, lo_part)
        return rmax, rmin

    def score_body(i, carry):
        return score_step(2 * i + 1, score_step(2 * i, carry))

    stats = lax.fori_loop(
        0, nch // 2, score_body,
        (jnp.full((SUBLANES, Q_BLK), -jnp.inf, F32), jnp.full((SUBLANES, Q_BLK), jnp.inf, F32)))
    rmax8, rmin8 = lax.cond(nch % 2 == 1, lambda c: score_step(nch - 1, c), lambda c: c, stats)
    rowmax = jnp.max(rmax8, axis=0, keepdims=True)
    rowmin = jnp.min(rmin8, axis=0, keepdims=True)

    n_causal = (qb * Q_BLK + 1 + lax.broadcasted_iota(jnp.int32, (1, Q_BLK), 1)).astype(F32)
    kt = jnp.minimum(n_causal, topk)

    cgrp = CNT_BLK // CNT_ROWS

    def count_ge(t):
        def body(ch, acc):
            for r in range(cgrp):
                rows = sc_ref[ch, r * CNT_ROWS:(r + 1) * CNT_ROWS, :]
                acc = acc + jnp.where(rows >= t, 1.0, 0.0)
            return acc
        acc = lax.fori_loop(0, nch, body, jnp.zeros((CNT_ROWS, Q_BLK), F32))
        return jnp.sum(acc, axis=0, keepdims=True)

    def bisect_pass(state):
        lo, hi, top, c_lo, c_hi, thr, done = state
        cap = jnp.minimum(hi, top)
        mid = lo + 0.5 * (cap - lo)
        mid = jnp.where(mid <= lo, cap, mid)
        c = count_ge(mid)
        hit = jnp.logical_and(done == 0.0, c == kt)
        thr = jnp.where(hit, mid, thr)
        done = jnp.where(hit, 1.0, done)
        active = done == 0.0
        up = jnp.logical_and(active, c >= kt)
        down = jnp.logical_and(active, c < kt)
        return (jnp.where(up, mid, lo), jnp.where(down, mid, hi), jnp.where(down, jnp.inf, top),
                jnp.where(up, c, c_lo), jnp.where(down, c, c_hi), thr, done)

    def snap_pass(state):
        lo, hi, top, c_lo, c_hi, thr, done = state

        def body(ch, carry):
            a8, b8 = carry
            for r in range(cgrp):
                s = sc_ref[ch, r * CNT_ROWS:(r + 1) * CNT_ROWS, :]
                a8 = jnp.minimum(a8, jnp.where(s >= lo, s, jnp.inf))
                b8 = jnp.maximum(b8, jnp.where(s < hi, s, -jnp.inf))
            return a8, b8

        a8, b8 = lax.fori_loop(
            0, nch, body,
            (jnp.full((CNT_ROWS, Q_BLK), jnp.inf, F32), jnp.full((CNT_ROWS, Q_BLK), -jnp.inf, F32)))
        a = jnp.min(a8, axis=0, keepdims=True)
        b = jnp.max(b8, axis=0, keepdims=True)
        active = done == 0.0
        hit = jnp.logical_and(active, jnp.logical_or(a == b, kt - c_hi == 1.0))
        thr = jnp.where(hit, b, thr)
        done = jnp.where(hit, 2.0, done)
        c_lo = jnp.where(jnp.logical_and(hit, a != b), kt + 1.0, c_lo)
        return jnp.where(active, a, lo), hi, jnp.where(active, b, top), c_lo, c_hi, thr, done

    few = n_causal <= topk
    state0 = (rowmin, jnp.full((1, Q_BLK), jnp.inf, F32), rowmax, n_causal,
              jnp.zeros((1, Q_BLK), F32), jnp.where(few, F32_LOWEST, 0.0), jnp.where(few, 1.0, 0.0))

    def outer_cond(carry):
        return carry[1] > 0.0

    def outer_body(carry):
        state, _ = carry
        state = lax.fori_loop(0, SEARCH_PERIOD, lambda i, st: bisect_pass(st), state)
        state = snap_pass(state)
        pending = jnp.max(jnp.where(state[6] == 0.0, 1.0, 0.0))
        return state, pending

    state1 = lax.fori_loop(0, SEARCH_FIRST, lambda i, st: bisect_pass(st), state0)
    state1 = snap_pass(state1)
    pending1 = jnp.max(jnp.where(state1[6] == 0.0, 1.0, 0.0))
    (lo, hi, _, c_lo, c_hi, thr, done), _ = lax.while_loop(outer_cond, outer_body, (state1, pending1))

    excess = jnp.where(done == 2.0, c_lo - kt, 0.0)
    need = kt - c_hi

    @pl.when(jnp.max(excess) > 0.0)
    def _():
        tri = (lax.broadcasted_iota(jnp.int32, (K_BLK, K_BLK), 0)
               >= lax.broadcasted_iota(jnp.int32, (K_BLK, K_BLK), 1)).astype(BF16)
        has_excess = excess > 0.0

        def drop_step(ch, run):
            for j in range(sub):
                s = sc_ref[ch, sub_rows[j], :]
                tied = jnp.logical_and(s == thr, has_excess)
                prefix = jnp.dot(tri, jnp.where(tied, 1.0, 0.0).astype(BF16), preferred_element_type=F32)
                drop = jnp.logical_and(tied, run + prefix > need)
                sc_ref[ch, sub_rows[j], :] = jnp.where(drop, -jnp.inf, s)
                run = run + jnp.max(prefix, axis=0, keepdims=True)
            return run

        run = lax.fori_loop(0, nch // 2, lambda i, r: drop_step(2 * i + 1, drop_step(2 * i, r)),
                            jnp.zeros((1, Q_BLK), F32))

        @pl.when(nch % 2 == 1)
        def _():
            drop_step(nch - 1, run)

    m_ref[...] = jnp.full(m_ref.shape, M_INIT, F32)
    acc_ref[...] = jnp.zeros(acc_ref.shape, F32)
    row_m = lambda j: slice(j, j + 1)
    row_a = lambda j: slice(sub + j, sub + j + 1)

    def logits_stage(ch, j):
        kv_blk = kv_ref[0, ch, sub_rows[j], :]
        bias = jnp.where(sc_ref[ch, sub_rows[j], :] >= thr, 0.0, MASKED)
        for cc in range(n_chunks):
            logits = jnp.dot(kv_blk, qe_ref[:, cc * COL_BLK:(cc + 1) * COL_BLK],
                             preferred_element_type=F32)
            for hh in range(COL_BLK // Q_BLK):
                cs = slice(cc * COL_BLK + hh * Q_BLK, cc * COL_BLK + (hh + 1) * Q_BLK)
                lg = logits[:, hh * Q_BLK:(hh + 1) * Q_BLK] + bias
                lg_ref[j, :, cs] = lg
                mx_ref[:, cs] = lg.reshape(kgrp, SUBLANES, Q_BLK).max(axis=0)
        m_old = m_ref[...]
        m_new = jnp.maximum(m_old, jnp.max(mx_ref[...], axis=0, keepdims=True))
        st_ref[row_m(j), :] = m_new
        st_ref[row_a(j), :] = jnp.exp2(m_old - m_new)
        m_ref[...] = m_new

    def probs_stage(j):
        p_ref[j] = jnp.exp2(lg_ref[j] - st_ref[row_m(j), :]).astype(BF16)

    def value_stage(kb, j, alpha):
        acc_ref[...] = acc_ref[...] * alpha + jnp.dot(
            vt_ref[0, kb], p_ref[j], preferred_element_type=F32)

    p_ref[sub - 1] = jnp.zeros(p_ref.shape[1:], BF16)
    st_ref[row_a(sub - 1), :] = jnp.ones((1, n_cols), F32)
    logits_stage(0, 0)

    def attn_body(ch, _):
        alpha_prev = st_ref[row_a(1), :]
        logits_stage(ch, 1)
        value_stage(jnp.maximum(ch * sub - 1, 0), 1, alpha_prev)
        probs_stage(0)
        alpha_cur = st_ref[row_a(0), :]
        logits_stage(jnp.minimum(ch + 1, nch - 1), 0)
        value_stage(ch * sub, 0, alpha_cur)
        probs_stage(1)
        return 0

    lax.fori_loop(0, nch, attn_body, 0)
    value_stage(nch * sub - 1, 1, st_ref[row_a(1), :])
    dh = qt_ref.shape[2]
    inv_l = 1.0 / acc_ref[dh:dh + 1, :]
    for j in range(n_cols // Q_BLK // 2):
        pair = [acc_ref[0:dh, (2 * j + hh) * Q_BLK:(2 * j + hh + 1) * Q_BLK]
                * inv_l[:, (2 * j + hh) * Q_BLK:(2 * j + hh + 1) * Q_BLK] for hh in range(2)]
        o_ref[0, :, j * 2 * dh:(j + 1) * 2 * dh] = jnp.concatenate(pair, axis=0).T.astype(o_ref.dtype)


def _dsa(qt, qit, w_t, kv4, ki4, vt4):
    b, nqb, dh, n_cols = qt.shape
    n_steps = kv4.shape[1]
    assert kv4.shape[2] == CNT_BLK and CNT_BLK == 2 * K_BLK and n_cols % COL_BLK == 0
    assert vt4.shape[1] * K_BLK == n_steps * CNT_BLK and vt4.shape[2] == dh + SUBLANES
    per_q = lambda a: pl.BlockSpec((1, 1) + a.shape[2:], lambda bi, qi: (bi, qi, 0, 0))
    per_b = lambda a: pl.BlockSpec((1,) + a.shape[1:], lambda bi, qi: (bi, 0, 0, 0))
    return pl.pallas_call(
        _dsa_kernel,
        out_shape=jax.ShapeDtypeStruct((b, nqb * Q_BLK, (n_cols // Q_BLK) * dh), BF16),
        grid=(b, nqb),
        in_specs=[per_q(qt), per_q(qit), per_q(w_t), per_b(kv4), per_b(ki4), per_b(vt4)],
        out_specs=pl.BlockSpec((1, Q_BLK, (n_cols // Q_BLK) * dh), lambda bi, qi: (bi, qi, 0)),
        scratch_shapes=[pltpu.VMEM((n_steps, CNT_BLK, Q_BLK), F32),
                        pltpu.VMEM((LANES, n_cols), BF16),
                        pltpu.VMEM((LANES, n_cols), BF16),
                        pltpu.VMEM((1, n_cols), F32),
                        pltpu.VMEM((SUBLANES, n_cols), F32),
                        pltpu.VMEM((SUBLANES, n_cols), F32),
                        pltpu.VMEM((dh + SUBLANES, n_cols), F32),
                        pltpu.VMEM((CNT_BLK // K_BLK, K_BLK, n_cols), F32),
                        pltpu.VMEM((CNT_BLK // K_BLK, K_BLK, n_cols), BF16)],
        compiler_params=_cparams(("arbitrary", "arbitrary")),
        name="dsa",
    )(qt, qit, w_t, kv4, ki4, vt4)


def _outproj_kernel(x_ref, attn_ref, pool_ref, woa_ref, wop_ref, gate1_ref, gffn_ref,
                    scale2_ref, shift2_ref, wr_ref, br_ref, x1_ref, h2_ref, gates_ref):
    tm = x_ref.shape[1]
    mix = (jnp.dot(attn_ref[0], woa_ref[...], preferred_element_type=F32)
           + jnp.dot(pool_ref[0], wop_ref[...], preferred_element_type=F32))
    x1 = x_ref[0] + gate1_ref[0] * mix
    x1_ref[0] = x1
    ms = jnp.mean(x1 * x1, axis=-1, keepdims=True)
    h2 = (x1 * lax.rsqrt(ms + EPS) * gffn_ref[...]) * (1.0 + scale2_ref[0]) + shift2_ref[0]
    h2_hi = h2.astype(BF16)
    h2_ref[0] = h2_hi

    h2_lo = (h2 - h2_hi.astype(F32)).astype(BF16)
    wr = wr_ref[...]
    wr_hi = wr.astype(BF16)
    wr_lo = (wr - wr_hi.astype(F32)).astype(BF16)
    logits = (jnp.dot(h2_hi, wr_hi, preferred_element_type=F32)
              + jnp.dot(h2_lo, wr_hi, preferred_element_type=F32)
              + jnp.dot(h2_hi, wr_lo, preferred_element_type=F32)) + br_ref[...]

    lt = logits.T
    n_e = EXPERTS_PER_GROUP
    sub_id = lax.broadcasted_iota(jnp.int32, (SUBLANES, tm), 0)
    big = jnp.int32(LANES)
    glog = jnp.where(sub_id < N_GROUPS, lt[N_EXPERTS:N_EXPERTS + SUBLANES, :], -jnp.inf)
    gmax = jnp.max(glog, axis=0, keepdims=True)
    gsum = jnp.sum(jnp.exp(glog - gmax), axis=0, keepdims=True)
    p_g = 1.0 / gsum
    g_sel = jnp.min(jnp.where(glog == gmax, sub_id, big), axis=0, keepdims=True)
    elog = lt[0:n_e, :]
    for gi in range(1, N_GROUPS):
        elog = jnp.where(g_sel == gi, lt[gi * n_e:(gi + 1) * n_e, :], elog)
    emax = jnp.max(elog, axis=0, keepdims=True)
    eexp = jnp.exp(elog - emax)
    p_e = eexp / jnp.sum(eexp, axis=0, keepdims=True)
    p1 = jnp.max(p_e, axis=0, keepdims=True)
    i1 = jnp.min(jnp.where(p_e == p1, sub_id, big), axis=0, keepdims=True)
    p_e2 = jnp.where(sub_id == i1, -1.0, p_e)
    p2 = jnp.max(p_e2, axis=0, keepdims=True)
    i2 = jnp.min(jnp.where(p_e2 == p2, sub_id, big), axis=0, keepdims=True)
    tot = p1 + p2
    in_grp = (jnp.where(sub_id == i1, p_g * (p1 / tot), 0.0)
              + jnp.where(sub_id == i2, p_g * (p2 / tot), 0.0))
    rows = [jnp.where(g_sel == gi, in_grp, 0.0) for gi in range(N_GROUPS)]
    rows.append(jnp.where(sub_id == 0, g_sel.astype(F32), 0.0))
    rows.append(jnp.zeros((LANES - N_EXPERTS - SUBLANES, tm), F32))
    gates_ref[0] = jnp.concatenate(rows, axis=0).T


def _outproj(x, attn, pool, wo_a, wo_p, gate1, g_ffn, scale2, shift2, w_r, b_r):
    b, s, d = x.shape
    tm = TM_PROJ
    tok = lambda w: pl.BlockSpec((1, tm, w), lambda bi, si: (bi, si, 0))
    per_b = pl.BlockSpec((1, 1, d), lambda bi, si: (bi, 0, 0))
    full = lambda a: pl.BlockSpec(a.shape, lambda bi, si: (0,) * a.ndim)
    return pl.pallas_call(
        _outproj_kernel,
        out_shape=(jax.ShapeDtypeStruct((b, s, d), F32),
                   jax.ShapeDtypeStruct((b, s, d), BF16),
                   jax.ShapeDtypeStruct((b, s, LANES), F32)),
        grid=(b, s // tm),
        in_specs=[tok(d), tok(attn.shape[2]), tok(pool.shape[2]), full(wo_a), full(wo_p), per_b,
                  full(g_ffn), per_b, per_b, full(w_r), full(b_r)],
        out_specs=(tok(d), tok(d), tok(LANES)),
        compiler_params=_cparams(("arbitrary", "arbitrary")),
        name="outproj",
    )(x, attn, pool, wo_a, wo_p, gate1, g_ffn, scale2, shift2, w_r, b_r)


def _moe_kernel(x1_ref, h2_ref, gates_ref, gate2_ref, *refs):
    wgu_parts, wd_parts = refs[:MOE_W_PARTS], refs[MOE_W_PARTS:2 * MOE_W_PARTS]
    o_ref, xe_ref, rank_ref, rank_t_ref, own_ref, grpb_ref = refs[2 * MOE_W_PARTS:]
    g = pl.program_id(2)
    tm, d = h2_ref.shape[1], h2_ref.shape[2]
    per_part = wd_parts[0].shape[0]
    n_e, d_exp = per_part * MOE_W_PARTS, wd_parts[0].shape[1]
    gf = g.astype(F32)

    @pl.when(g == 0)
    def _():
        tri = (lax.broadcasted_iota(jnp.int32, (RANK_BLK, RANK_BLK), 0)
               >= lax.broadcasted_iota(jnp.int32, (RANK_BLK, RANK_BLK), 1)).astype(BF16)
        lane_b = lax.broadcasted_iota(jnp.int32, (RANK_BLK, LANES), 1)
        run = jnp.zeros((1, LANES), F32)
        for sb in range(tm // RANK_BLK):
            rows = slice(sb * RANK_BLK, (sb + 1) * RANK_BLK)
            gts_b = gates_ref[0, rows, :]
            grp = jnp.sum(jnp.where(lane_b == N_EXPERTS, gts_b, 0.0), axis=-1, keepdims=True)
            member = jnp.where(jnp.logical_and(lane_b < N_GROUPS, lane_b.astype(F32) == grp), 1.0, 0.0)
            pre = jnp.dot(tri, member.astype(BF16), preferred_element_type=F32) + run
            rank_ref[rows, :] = jnp.where(lane_b == N_GROUPS, grp, pre)
            own = jnp.sum(member * pre, axis=-1, keepdims=True) - 1.0
            own_ref[rows, :] = jnp.broadcast_to(own, (RANK_BLK, LANES))
            grpb_ref[rows, :] = jnp.broadcast_to(grp, (RANK_BLK, LANES))
            run = jnp.max(pre, axis=0, keepdims=True)
        rank_t_ref[...] = rank_ref[...].T
        gts = gates_ref[0]
        g_hi = gts.astype(BF16)
        xe_ref[:, :d] = h2_ref[0]
        xe_ref[:, d:d + LANES] = g_hi
        xe_ref[:, d + LANES:d + 2 * LANES] = (gts - g_hi.astype(F32)).astype(BF16)
        o_ref[0] = jnp.zeros((tm, d), F32)

    rank_row = rank_t_ref[pl.ds(g, 1), :]
    pos_row = jnp.where(rank_t_ref[N_GROUPS:N_GROUPS + 1, :] == gf, rank_row - 1.0, -1.0)
    pos_col = jnp.where(grpb_ref[...] == gf, own_ref[...], -1.0)
    n_rows = jnp.max(rank_row).astype(jnp.int32)

    def expert_pass(first_row, n_ch):
        row_id = lax.broadcasted_iota(jnp.int32, (n_ch, tm), 0).astype(F32)
        col_id = lax.broadcasted_iota(jnp.int32, (tm, n_ch), 1).astype(F32)
        lane_c = lax.broadcasted_iota(jnp.int32, (n_ch, LANES), 1)
        r0 = first_row.astype(F32)
        gather = jnp.where(pos_row - r0 == row_id, 1.0, 0.0).astype(BF16)
        xg = jnp.dot(gather, xe_ref[...], preferred_element_type=F32)
        xb = xg[:, :d].astype(BF16)
        gates_c = xg[:, d:d + LANES] + xg[:, d + LANES:d + 2 * LANES]
        ya = None
        for e in range(n_e):
            w_gu_e = wgu_parts[e // per_part][e % per_part]
            w_d_e = wd_parts[e // per_part][e % per_part]
            gu = jnp.dot(xb, w_gu_e, preferred_element_type=F32)
            gt = gu[:, :d_exp]
            a = (gt * jax.nn.sigmoid(gt)) * gu[:, d_exp:]
            gate_e = jnp.sum(jnp.where(lane_c == g * n_e + e, gates_c, 0.0), axis=-1, keepdims=True)
            y = jnp.dot((a * gate_e).astype(BF16), w_d_e, preferred_element_type=F32)
            ya = y if ya is None else ya + y
        pos_wide = jnp.concatenate([pos_col] * (n_ch // LANES), axis=1)
        scatter = jnp.where(pos_wide - r0 == col_id, 1.0, 0.0).astype(BF16)
        o_ref[0] += jnp.dot(scatter, ya.astype(BF16), preferred_element_type=F32)

    n_full = (n_rows + MOE_CH - MOE_CH_TAIL - 1) // MOE_CH

    def full_pass(c, _):
        expert_pass(c * MOE_CH, MOE_CH)
        return 0

    lax.fori_loop(0, n_full, full_pass, 0)

    @pl.when(n_rows > n_full * MOE_CH)
    def _():
        expert_pass(n_full * MOE_CH, MOE_CH_TAIL)

    @pl.when(g == pl.num_programs(2) - 1)
    def _():
        o_ref[0] = x1_ref[0] + gate2_ref[0] * o_ref[0]


def _moe(x1, h2, gates, gate2, w_gu, w_d):
    b, s, d = x1.shape
    tm = TM_MOE
    n_e = EXPERTS_PER_GROUP
    assert w_gu.shape[0] == N_GROUPS * n_e and tm % RANK_BLK == 0 and n_e % MOE_W_PARTS == 0
    assert MOE_CH % LANES == 0 and MOE_CH_TAIL % LANES == 0 and MOE_CH_TAIL <= MOE_CH
    per_part = n_e // MOE_W_PARTS
    tok = lambda w: pl.BlockSpec((1, tm, w), lambda bi, si, g: (bi, si, 0))

    def slab(w, k):
        return pl.BlockSpec((per_part,) + w.shape[1:], lambda bi, si, g: (g * MOE_W_PARTS + k, 0, 0))

    return pl.pallas_call(
        _moe_kernel,
        out_shape=jax.ShapeDtypeStruct((b, s, d), F32),
        grid=(b, s // tm, N_GROUPS),
        in_specs=([tok(d), tok(d), tok(LANES), pl.BlockSpec((1, 1, d), lambda bi, si, g: (bi, 0, 0))]
                  + [slab(w_gu, k) for k in range(MOE_W_PARTS)]
                  + [slab(w_d, k) for k in range(MOE_W_PARTS)]),
        out_specs=tok(d),
        scratch_shapes=[pltpu.VMEM((tm, d + 2 * LANES), BF16),
                        pltpu.VMEM((tm, LANES), F32),
                        pltpu.VMEM((LANES, tm), F32),
                        pltpu.VMEM((tm, LANES), F32),
                        pltpu.VMEM((tm, LANES), F32)],
        compiler_params=pltpu.CompilerParams(
            dimension_semantics=("arbitrary", "arbitrary", "arbitrary"),
            vmem_limit_bytes=VMEM_LIMIT_MOE_BYTES),
        name="moe",
    )(x1, h2, gates, gate2, *([w_gu] * MOE_W_PARTS), *([w_d] * MOE_W_PARTS))


def _layer(x, mod, pos3, g_mix, g_ffn, w_in, g_q, g_k, g_kidx, w_pool, pool_scale, w_out,
           w_rg, b_rg, w_re, b_re, w_gate, w_up, w_down):
    b, s, d = x.shape
    d_attn = N_HEADS * HEAD_DIM
    nqb = s // Q_BLK
    nkb = s // K_BLK
    shift1, scale1, gate1, shift2, scale2, gate2 = [m[:, None, :] for m in jnp.split(mod, 6, axis=-1)]

    n_front = d_attn + 2 * HEAD_DIM + N_IDX_HEADS * IDX_DIM + IDX_DIM + N_IDX_HEADS
    pad = (-n_front) % LANES
    w_in_p = jnp.concatenate([w_in[:, :n_front], jnp.zeros((d, pad), w_in.dtype), w_in[:, n_front:]],
                             axis=1).astype(BF16)
    seg_id = jnp.arange(d_attn) // HEAD_DIM
    segsum = (seg_id[:, None] == seg_id[None, :]).astype(BF16)
    ones_half = jnp.ones((LANES - HEAD_DIM,), F32)
    gq_t = (jnp.tile(g_q, N_HEADS) * (LOG2_E * HEAD_DIM ** -0.5))[None, :]
    gk_e = jnp.concatenate([g_k, ones_half])[None, :]
    gkidx_e = jnp.concatenate([g_kidx, ones_half])[None, :]
    half = HEAD_DIM // 2
    inv_freq = ROPE_THETA ** (-jnp.arange(0, HEAD_DIM, 2, dtype=F32) / HEAD_DIM)
    invf = jnp.tile(inv_freq, LANES // half)[None, :]

    qt, kv, qit, ki, w_t, vt4, pool = _inproj(pos3, x, scale1, shift1, g_mix[None, :], w_in_p, segsum,
                                              gq_t, gk_e, gkidx_e, invf, w_pool.astype(BF16),
                                              pool_scale[None, :])
    kv4 = kv.reshape(b, s // CNT_BLK, CNT_BLK, LANES)
    ki4 = ki.reshape(b, s // CNT_BLK, CNT_BLK, LANES)
    attn = _dsa(qt, qit, w_t, kv4, ki4, vt4)

    w_out_b = w_out.astype(BF16)
    w_r = jnp.concatenate([w_re, w_rg, jnp.zeros((d, LANES - N_EXPERTS - N_GROUPS), F32)], axis=1)
    b_r = jnp.concatenate([b_re, b_rg, jnp.zeros((LANES - N_EXPERTS - N_GROUPS,), F32)])[None, :]
    x1, h2, gates = _outproj(x, attn, pool, w_out_b[:d_attn], w_out_b[d_attn:], gate1,
                             g_ffn[None, :], scale2, shift2, w_r, b_r)

    w_gu = jnp.concatenate([w_gate, w_up], axis=-1).astype(BF16)
    return _moe(x1, h2, gates, gate2, w_gu, w_down.astype(BF16))


def kernel(x, c, positions, w_ada, b_ada, g_norm_mix, g_norm_ffn, w_in, g_q, g_k, g_kidx, w_pool,
           pool_scale, w_out, w_router_group, b_router_group, w_router_expert, b_router_expert,
           w_gate, w_up, w_down):
    b, s, d = x.shape
    depth = w_ada.shape[0]
    assert s % TM_MOE == 0 and s % K_BLK == 0 and d % LANES == 0
    pos3 = positions[:, :, None]
    c_pad = jnp.concatenate([c, jnp.zeros((-b % SUBLANES, d), c.dtype)], axis=0)
    for l in range(depth):
        mod = _adaln(c_pad, w_ada[l], b_ada[l][None, :])[:b]
        x = _layer(x, mod, pos3, g_norm_mix[l], g_norm_ffn[l], w_in[l], g_q[l], g_k[l], g_kidx[l],
                   w_pool[l], pool_scale[l], w_out[l], w_router_group[l], b_router_group[l],
                   w_router_expert[l], b_router_expert[l], w_gate[l], w_up[l], w_down[l])
    return x
```

```python
import functools

import jax
import jax.numpy as jnp
from jax import lax
from jax.experimental import pallas as pl
from jax.experimental.pallas import tpu as pltpu

N_HEADS = 8
HEAD_DIM = 64
N_IDX_HEADS = 8
IDX_DIM = 64
TOPK_MAX = 256
ROPE_THETA = 10000.0
POOL_WINDOWS = (2, 4, 8, 16)
N_GROUPS = 4
EXPERTS_PER_GROUP = 8
N_EXPERTS = N_GROUPS * EXPERTS_PER_GROUP
EPS = 1e-6
N_MOD = 6
assert EXPERTS_PER_GROUP & (EXPERTS_PER_GROUP - 1) == 0

LANES = 128
SUBLANES = 8
VMEM_LIMIT_BYTES = 56 * 1024 * 1024
VMEM_LIMIT_MOE_BYTES = 60 * 1024 * 1024

Q_BLK = 512
K_BLK = 256
COL_BLK = 512
CNT_BLK = 512
CNT_ROWS = 32
SEARCH_FIRST = 15
SEARCH_PERIOD = 1
TM_PROJ = 512
TM_MOE = 1024
MOE_CH = 256
MOE_CH_TAIL = 128
RANK_BLK = 256
MOE_W_PARTS = 4
MAX_WIN = max(POOL_WINDOWS)
assert all(w == 2 ** (g + 1) for g, w in enumerate(POOL_WINDOWS))
M_INIT = -1e29
MASKED = -1e30
F32_LOWEST = -3.0e38
LOG2_E = 1.4426950408889634

BF16 = jnp.bfloat16
F32 = jnp.float32


def _cparams(sem):
    return pltpu.CompilerParams(dimension_semantics=sem, vmem_limit_bytes=VMEM_LIMIT_BYTES)


def _adaln_kernel(c_ref, w_ref, b_ref, o_ref):
    c = c_ref[...]
    c_act = c * jax.nn.sigmoid(c)
    o_ref[...] = jnp.dot(c_act, w_ref[...], preferred_element_type=F32) + b_ref[...]


def _adaln(c_pad, w_ada, b_ada):
    rows, d = c_pad.shape
    n = w_ada.shape[1]
    tn = n // N_MOD
    return pl.pallas_call(
        _adaln_kernel,
        out_shape=jax.ShapeDtypeStruct((rows, n), F32),
        grid=(n // tn,),
        in_specs=[pl.BlockSpec((rows, d), lambda j: (0, 0)),
                  pl.BlockSpec((d, tn), lambda j: (0, j)),
                  pl.BlockSpec((1, tn), lambda j: (0, j))],
        out_specs=pl.BlockSpec((rows, tn), lambda j: (0, j)),
        compiler_params=_cparams(("arbitrary",)),
        name="adaln",
    )(c_pad, w_ada, b_ada)


_PIO2_HI, _PIO2_MID, _PIO2_LO = 1.5703125, 4.837512969970703125e-4, 7.54978995489188e-8
_SIN_COEF = (-1.9515295891e-4, 8.3321608736e-3, -1.6666654611e-1)
_COS_COEF = (2.443315711809948e-5, -1.388731625493765e-3, 4.166664568298827e-2)


def _sincos(x):
    k = jnp.floor(x * (2.0 / jnp.pi) + 0.5)
    r = ((x - k * _PIO2_HI) - k * _PIO2_MID) - k * _PIO2_LO
    z = r * r
    s = r + r * z * (_SIN_COEF[2] + z * (_SIN_COEF[1] + z * _SIN_COEF[0]))
    c = 1.0 - 0.5 * z + z * z * (_COS_COEF[2] + z * (_COS_COEF[1] + z * _COS_COEF[0]))
    q = k - 4.0 * jnp.floor(k * 0.25)
    odd = jnp.logical_or(q == 1.0, q == 3.0)
    sin_b = jnp.where(odd, c, s)
    cos_b = jnp.where(odd, s, c)
    return (jnp.where(q >= 2.0, -sin_b, sin_b),
            jnp.where(jnp.logical_or(q == 1.0, q == 2.0), -cos_b, cos_b))


def _rope_chunk(y, cos, sin_signed, first_half):
    from_hi = pltpu.roll(y, LANES - HEAD_DIM // 2, 1)
    from_lo = pltpu.roll(y, HEAD_DIM // 2, 1)
    return y * cos + jnp.where(first_half, from_hi, from_lo) * sin_signed


def _inproj_kernel(pos_ref, x_ref, scale_ref, shift_ref, gmix_ref, win_ref, segsum_ref,
                   gq_ref, gk_ref, gkidx_ref, invf_ref, wpool_ref, pscale_ref,
                   qt_ref, kv_ref, qit_ref, ki_ref, wt_ref, vt_ref, pool_ref,
                   ubuf_ref, proj_a_ref, proj_b_ref):
    tm = x_ref.shape[1]
    sb = K_BLK
    step = pl.program_id(1)
    tile = jnp.maximum(step - 1, 0)

    @pl.when(step == 0)
    def _():
        proj_b_ref[...] = jnp.zeros(proj_b_ref.shape, F32)

    @pl.when(step <= 1)
    def _():
        for lvl in range(len(POOL_WINDOWS)):
            ubuf_ref[lvl, 0:MAX_WIN, lvl * LANES:] = jnp.zeros(
                (MAX_WIN, ubuf_ref.shape[2] - lvl * LANES), F32)

    @pl.when(step > 1)
    def _():
        for lvl in range(len(POOL_WINDOWS)):
            ubuf_ref[lvl, 0:MAX_WIN, lvl * LANES:] = ubuf_ref[lvl, tm:tm + MAX_WIN, lvl * LANES:]

    def run(write_ref, read_ref):
        for t in range(tm // sb):
            _inproj_post(t, read_ref[t], tile * tm + t * sb, pos_ref, segsum_ref, gq_ref, gk_ref,
                         gkidx_ref, invf_ref, wpool_ref, pscale_ref, qt_ref, kv_ref, qit_ref, ki_ref,
                         wt_ref, vt_ref, pool_ref, ubuf_ref)
        gain = gmix_ref[...] * (1.0 + scale_ref[0])
        for t in range(tm // sb):
            x = x_ref[0, t * sb:(t + 1) * sb, :]
            ms = jnp.mean(x * x, axis=-1, keepdims=True)
            h = (x * lax.rsqrt(ms + EPS) * gain + shift_ref[0]).astype(BF16)
            write_ref[t] = jnp.dot(h, win_ref[...], preferred_element_type=F32)

    @pl.when(step % 2 == 0)
    def _():
        run(proj_a_ref, proj_b_ref)

    @pl.when(step % 2 == 1)
    def _():
        run(proj_b_ref, proj_a_ref)


def _inproj_post(t, proj, t0, pos_ref, segsum_ref, gq_ref, gk_ref, gkidx_ref, invf_ref, wpool_ref,
                 pscale_ref, qt_ref, kv_ref, qit_ref, ki_ref, wt_ref, vt_ref, pool_ref, ubuf_ref):
    sb = K_BLK
    rows = slice(t * sb, (t + 1) * sb)
    tq, q0 = (t * sb) // Q_BLK, (t * sb) % Q_BLK
    d_attn = N_HEADS * HEAD_DIM
    d_qidx = N_IDX_HEADS * IDX_DIM

    def store_cols(dst_ref, chunk, j):
        ct = chunk.T
        for hh in range(2):
            col = (2 * j + hh) * Q_BLK + q0
            dst_ref[0, tq, :, col:col + sb] = ct[hh * HEAD_DIM:(hh + 1) * HEAD_DIM, :].astype(dst_ref.dtype)

    lane = lax.broadcasted_iota(jnp.int32, (sb, LANES), 1)
    first_half = (lane & (HEAD_DIM - 1)) < (HEAD_DIM // 2)
    ang = pos_ref[0, rows, :].astype(F32) * invf_ref[...]
    sin, cos = _sincos(ang)
    sin_signed = jnp.where(first_half, -sin, sin)
    rope = functools.partial(_rope_chunk, cos=cos, sin_signed=sin_signed, first_half=first_half)

    qf = proj[:, :d_attn]
    qsq = qf * qf
    qsq_hi = qsq.astype(BF16)
    qsq_lo = (qsq - qsq_hi.astype(F32)).astype(BF16)
    seg = segsum_ref[...]
    ssq = (jnp.dot(qsq_hi, seg, preferred_element_type=F32)
           + jnp.dot(qsq_lo, seg, preferred_element_type=F32))
    qn = qf * lax.rsqrt(ssq * (1.0 / HEAD_DIM) + EPS) * gq_ref[...]
    for j in range(d_attn // LANES):
        sl = slice(j * LANES, (j + 1) * LANES)
        store_cols(qt_ref, rope(qn[:, sl]), j)

    kvc = proj[:, d_attn:d_attn + LANES]
    is_k = lane < HEAD_DIM
    ksq = jnp.sum(jnp.where(is_k, kvc * kvc, 0.0), axis=-1, keepdims=True)
    kn = kvc * lax.rsqrt(ksq * (1.0 / HEAD_DIM) + EPS) * gk_ref[...]
    kv_ref[0, rows, :] = jnp.where(is_k, rope(kn), kvc).astype(BF16)
    row8 = lax.broadcasted_iota(jnp.int32, (SUBLANES, K_BLK), 0)
    vt_ref[0, t, 0:HEAD_DIM, :] = kvc.T[HEAD_DIM:, :].astype(BF16)
    vt_ref[0, t, HEAD_DIM:HEAD_DIM + SUBLANES, :] = jnp.where(row8 == 0, 1.0, 0.0).astype(BF16)

    o_qi = d_attn + LANES
    for j in range(d_qidx // LANES):
        store_cols(qit_ref, rope(proj[:, o_qi + j * LANES:o_qi + (j + 1) * LANES]), j)

    o_ki = o_qi + d_qidx
    kic = proj[:, o_ki:o_ki + LANES]
    kisq = jnp.sum(jnp.where(is_k, kic * kic, 0.0), axis=-1, keepdims=True)
    kin = kic * lax.rsqrt(kisq * (1.0 / IDX_DIM) + EPS) * gkidx_ref[...]
    ki_ref[0, rows, :] = jnp.where(is_k, rope(kin), 0.0).astype(BF16)
    wt_ref[0, tq, :, q0:q0 + sb] = kic.T[IDX_DIM:IDX_DIM + N_IDX_HEADS, :] * (
        N_IDX_HEADS ** -0.5 * IDX_DIM ** -0.5)

    o_u = o_ki + LANES
    u = proj[:, o_u:o_u + LANES * len(POOL_WINDOWS)]

    base = MAX_WIN + t * sb
    t_idx = t0 + lax.broadcasted_iota(jnp.int32, (sb, 1), 0)
    level = u
    for g, win in enumerate(POOL_WINDOWS):
        sl = slice(g * LANES, (g + 1) * LANES)
        shift = win // 2
        ubuf_ref[g, base:base + sb, g * LANES:] = level
        level = level + ubuf_ref[g, base - shift:base - shift + sb, g * LANES:]
        wsum = level[:, :LANES]
        if g + 1 < len(POOL_WINDOWS):
            level = level[:, LANES:]
        cnt = jnp.minimum(t_idx + 1, win).astype(F32)
        pooled = wsum / cnt - u[:, sl]
        mixed = jnp.dot(pooled.astype(BF16), wpool_ref[g], preferred_element_type=F32)
        pool_ref[0, rows, sl] = (mixed * pscale_ref[:, sl]).astype(BF16)


def _inproj(pos3, x, scale1, shift1, g_mix, w_in_p, segsum, gq_t, gk_e, gkidx_e, invf, w_pool, pscale):
    b, s, d = x.shape
    tm = TM_PROJ
    d_attn = N_HEADS * HEAD_DIM
    d_qidx = N_IDX_HEADS * IDX_DIM
    d_pool = LANES * len(POOL_WINDOWS)
    assert tm % Q_BLK == 0 and Q_BLK % K_BLK == 0 and HEAD_DIM == IDX_DIM and 2 * HEAD_DIM == LANES
    n_tiles = s // tm
    ahead = lambda si: jnp.minimum(si, n_tiles - 1)
    behind = lambda si: jnp.maximum(si - 1, 0)
    tok = lambda w: pl.BlockSpec((1, tm, w), lambda bi, si: (bi, behind(si), 0))
    blk = lambda n, r, c: pl.BlockSpec((1, tm // n, r, c), lambda bi, si: (bi, behind(si), 0, 0))
    per_b = pl.BlockSpec((1, 1, d), lambda bi, si: (bi, 0, 0))
    full = lambda a: pl.BlockSpec(a.shape, lambda bi, si: (0,) * a.ndim)
    nqb, nkb = s // Q_BLK, s // K_BLK
    proj_buf = pltpu.VMEM((tm // K_BLK, K_BLK, w_in_p.shape[1]), F32)
    return pl.pallas_call(
        _inproj_kernel,
        out_shape=(jax.ShapeDtypeStruct((b, nqb, HEAD_DIM, N_HEADS * Q_BLK), BF16),
                   jax.ShapeDtypeStruct((b, s, LANES), BF16),
                   jax.ShapeDtypeStruct((b, nqb, IDX_DIM, N_IDX_HEADS * Q_BLK), BF16),
                   jax.ShapeDtypeStruct((b, s, LANES), BF16),
                   jax.ShapeDtypeStruct((b, nqb, N_IDX_HEADS, Q_BLK), F32),
                   jax.ShapeDtypeStruct((b, nkb, HEAD_DIM + SUBLANES, K_BLK), BF16),
                   jax.ShapeDtypeStruct((b, s, d_pool), BF16)),
        grid=(b, n_tiles + 1),
        in_specs=[tok(1), pl.BlockSpec((1, tm, d), lambda bi, si: (bi, ahead(si), 0)), per_b, per_b,
                  full(g_mix), full(w_in_p), full(segsum),
                  full(gq_t), full(gk_e), full(gkidx_e), full(invf), full(w_pool), full(pscale)],
        out_specs=(blk(Q_BLK, HEAD_DIM, N_HEADS * Q_BLK), tok(LANES),
                   blk(Q_BLK, IDX_DIM, N_IDX_HEADS * Q_BLK), tok(LANES),
                   blk(Q_BLK, N_IDX_HEADS, Q_BLK), blk(K_BLK, HEAD_DIM + SUBLANES, K_BLK), tok(d_pool)),
        scratch_shapes=[pltpu.VMEM((len(POOL_WINDOWS), tm + MAX_WIN, d_pool), F32),
                        proj_buf, proj_buf],
        compiler_params=_cparams(("arbitrary", "arbitrary")),
        name="inproj",
    )(pos3, x, scale1, shift1, g_mix, w_in_p, segsum, gq_t, gk_e, gkidx_e, invf, w_pool, pscale)


def _dsa_kernel(qt_ref, qit_ref, w_ref, kv_ref, ki_ref, vt_ref, o_ref,
                sc_ref, qe_ref, qie_ref, m_ref, mx_ref, st_ref, acc_ref, lg_ref, p_ref):
    topk = float(min(TOPK_MAX, (sc_ref.shape[0] * CNT_BLK) // 4))
    qb = pl.program_id(1)
    n_cols = qt_ref.shape[3]
    n_chunks = n_cols // COL_BLK
    sub = CNT_BLK // K_BLK
    nch = ((qb + 1) * Q_BLK + CNT_BLK - 1) // CNT_BLK
    kgrp = K_BLK // SUBLANES
    sub_rows = [slice(j * K_BLK, (j + 1) * K_BLK) for j in range(sub)]

    zeros_half = jnp.zeros((LANES - HEAD_DIM, n_cols), BF16)
    qe_ref[0:HEAD_DIM, :] = qt_ref[0, 0]
    qe_ref[HEAD_DIM:LANES, :] = zeros_half
    qie_ref[0:IDX_DIM, :] = qit_ref[0, 0]
    qie_ref[IDX_DIM:LANES, :] = zeros_half

    q_pos = qb * Q_BLK + lax.broadcasted_iota(jnp.int32, (K_BLK, Q_BLK), 1)
    key_off = lax.broadcasted_iota(jnp.int32, (K_BLK, Q_BLK), 0)

    def score_step(ch, carry):
        rmax, rmin = carry
        for j in range(sub):
            ki_blk = ki_ref[0, ch, sub_rows[j], :]
            score = None
            for cc in range(n_chunks):
                cs = slice(cc * COL_BLK, (cc + 1) * COL_BLK)
                s_h = jnp.dot(ki_blk, qie_ref[:, cs], preferred_element_type=F32)
                s_h = jnp.maximum(s_h, 0.0)
                for hh in range(COL_BLK // Q_BLK):
                    head = cc * (COL_BLK // Q_BLK) + hh
                    part = s_h[:, hh * Q_BLK:(hh + 1) * Q_BLK] * w_ref[0, 0, head:head + 1, :]
                    score = part if score is None else score + part
            causal = (ch * CNT_BLK + j * K_BLK + key_off) <= q_pos
            masked = jnp.where(causal, score, -jnp.inf)
            sc_ref[ch, sub_rows[j], :] = masked
            hi_part = masked.reshape(kgrp, SUBLANES, Q_BLK).max(axis=0)
            lo_part = jnp.where(causal, score, jnp.inf).reshape(kgrp, SUBLANES, Q_BLK).min(axis=0)
            rmax, rmin = jnp.maximum(rmax, hi_part), jnp.minimum(rmin, lo_part)
        return rmax, rmin

    def score_body(i, carry):
        return score_step(2 * i + 1, score_step(2 * i, carry))

    stats = lax.fori_loop(
        0, nch // 2, score_body,
        (jnp.full((SUBLANES, Q_BLK), -jnp.inf, F32), jnp.full((SUBLANES, Q_BLK), jnp.inf, F32)))
    rmax8, rmin8 = lax.cond(nch % 2 == 1, lambda c: score_step(nch - 1, c), lambda c: c, stats)
    rowmax = jnp.max(rmax8, axis=0, keepdims=True)
    rowmin = jnp.min(rmin8, axis=0, keepdims=True)

    n_causal = (qb * Q_BLK + 1 + lax.broadcasted_iota(jnp.int32, (1, Q_BLK), 1)).astype(F32)
    kt = jnp.minimum(n_causal, topk)

    cgrp = CNT_BLK // CNT_ROWS

    def count_ge(t):
        def body(ch, acc):
            for r in range(cgrp):
                rows = sc_ref[ch, r * CNT_ROWS:(r + 1) * CNT_ROWS, :]
                acc = acc + jnp.where(rows >= t, 1.0, 0.0)
            return acc
        acc = lax.fori_loop(0, nch, body, jnp.zeros((CNT_ROWS, Q_BLK), F32))
        return jnp.sum(acc, axis=0, keepdims=True)

    def bisect_pass(state):
        lo, hi, top, c_lo, c_hi, thr, done = state
        cap = jnp.minimum(hi, top)
        mid = lo + 0.5 * (cap - lo)
        mid = jnp.where(mid <= lo, cap, mid)
        c = count_ge(mid)
        hit = jnp.logical_and(done == 0.0, c == kt)
        thr = jnp.where(hit, mid, thr)
        done = jnp.where(hit, 1.0, done)
        active = done == 0.0
        up = jnp.logical_and(active, c >= kt)
        down = jnp.logical_and(active, c < kt)
        return (jnp.where(up, mid, lo), jnp.where(down, mid, hi), jnp.where(down, jnp.inf, top),
                jnp.where(up, c, c_lo), jnp.where(down, c, c_hi), thr, done)

    def snap_pass(state):
        lo, hi, top, c_lo, c_hi, thr, done = state

        def body(ch, carry):
            a8, b8 = carry
            for r in range(cgrp):
                s = sc_ref[ch, r * CNT_ROWS:(r + 1) * CNT_ROWS, :]
                a8 = jnp.minimum(a8, jnp.where(s >= lo, s, jnp.inf))
                b8 = jnp.maximum(b8, jnp.where(s < hi, s, -jnp.inf))
            return a8, b8

        a8, b8 = lax.fori_loop(
            0, nch, body,
            (jnp.full((CNT_ROWS, Q_BLK), jnp.inf, F32), jnp.full((CNT_ROWS, Q_BLK), -jnp.inf, F32)))
        a = jnp.min(a8, axis=0, keepdims=True)
        b = jnp.max(b8, axis=0, keepdims=True)
        active = done == 0.0
        hit = jnp.logical_and(active, jnp.logical_or(a == b, kt - c_hi == 1.0))
        thr = jnp.where(hit, b, thr)
        done = jnp.where(hit, 2.0, done)
        c_lo = jnp.where(jnp.logical_and(hit, a != b), kt + 1.0, c_lo)
        return jnp.where(active, a, lo), hi, jnp.where(active, b, top), c_lo, c_hi, thr, done

    few = n_causal <= topk
    state0 = (rowmin, jnp.full((1, Q_BLK), jnp.inf, F32), rowmax, n_causal,
              jnp.zeros((1, Q_BLK), F32), jnp.where(few, F32_LOWEST, 0.0), jnp.where(few, 1.0, 0.0))

    def outer_cond(carry):
        return carry[1] > 0.0

    def outer_body(carry):
        state, _ = carry
        state = lax.fori_loop(0, SEARCH_PERIOD, lambda i, st: bisect_pass(st), state)
        state = snap_pass(state)
        pending = jnp.max(jnp.where(state[6] == 0.0, 1.0, 0.0))
        return state, pending

    state1 = lax.fori_loop(0, SEARCH_FIRST, lambda i, st: bisect_pass(st), state0)
    state1 = snap_pass(state1)
    pending1 = jnp.max(jnp.where(state1[6] == 0.0, 1.0, 0.0))
    (lo, hi, _, c_lo, c_hi, thr, done), _ = lax.while_loop(outer_cond, outer_body, (state1, pending1))

    excess = jnp.where(done == 2.0, c_lo - kt, 0.0)
    need = kt - c_hi

    @pl.when(jnp.max(excess) > 0.0)
    def _():
        tri = (lax.broadcasted_iota(jnp.int32, (K_BLK, K_BLK), 0)
               >= lax.broadcasted_iota(jnp.int32, (K_BLK, K_BLK), 1)).astype(BF16)
        has_excess = excess > 0.0

        def drop_step(ch, run):
            for j in range(sub):
                s = sc_ref[ch, sub_rows[j], :]
                tied = jnp.logical_and(s == thr, has_excess)
                prefix = jnp.dot(tri, jnp.where(tied, 1.0, 0.0).astype(BF16), preferred_element_type=F32)
                drop = jnp.logical_and(tied, run + prefix > need)
                sc_ref[ch, sub_rows[j], :] = jnp.where(drop, -jnp.inf, s)
                run = run + jnp.max(prefix, axis=0, keepdims=True)
            return run

        run = lax.fori_loop(0, nch // 2, lambda i, r: drop_step(2 * i + 1, drop_step(2 * i, r)),
                            jnp.zeros((1, Q_BLK), F32))

        @pl.when(nch % 2 == 1)
        def _():
            drop_step(nch - 1, run)

    m_ref[...] = jnp.full(m_ref.shape, M_INIT, F32)
    acc_ref[...] = jnp.zeros(acc_ref.shape, F32)
    row_m = lambda j: slice(j, j + 1)
    row_a = lambda j: slice(sub + j, sub + j + 1)

    def logits_stage(ch, j):
        kv_blk = kv_ref[0, ch, sub_rows[j], :]
        bias = jnp.where(sc_ref[ch, sub_rows[j], :] >= thr, 0.0, MASKED)
        for cc in range(n_chunks):
            logits = jnp.dot(kv_blk, qe_ref[:, cc * COL_BLK:(cc + 1) * COL_BLK],
                             preferred_element_type=F32)
            for hh in range(COL_BLK // Q_BLK):
                cs = slice(cc * COL_BLK + hh * Q_BLK, cc * COL_BLK + (hh + 1) * Q_BLK)
                lg = logits[:, hh * Q_BLK:(hh + 1) * Q_BLK] + bias
                lg_ref[j, :, cs] = lg
                mx_ref[:, cs] = lg.reshape(kgrp, SUBLANES, Q_BLK).max(axis=0)
        m_old = m_ref[...]
        m_new = jnp.maximum(m_old, jnp.max(mx_ref[...], axis=0, keepdims=True))
        st_ref[row_m(j), :] = m_new
        st_ref[row_a(j), :] = jnp.exp2(m_old - m_new)
        m_ref[...] = m_new

    def probs_stage(j):
        p_ref[j] = jnp.exp2(lg_ref[j] - st_ref[row_m(j), :]).astype(BF16)

    def value_stage(kb, j, alpha):
        acc_ref[...] = acc_ref[...] * alpha + jnp.dot(
            vt_ref[0, kb], p_ref[j], preferred_element_type=F32)

    p_ref[sub - 1] = jnp.zeros(p_ref.shape[1:], BF16)
    st_ref[row_a(sub - 1), :] = jnp.ones((1, n_cols), F32)
    logits_stage(0, 0)

    def attn_body(ch, _):
        alpha_prev = st_ref[row_a(1), :]
        logits_stage(ch, 1)
        value_stage(jnp.maximum(ch * sub - 1, 0), 1, alpha_prev)
        probs_stage(0)
        alpha_cur = st_ref[row_a(0), :]
        logits_stage(jnp.minimum(ch + 1, nch - 1), 0)
        value_stage(ch * sub, 0, alpha_cur)
        probs_stage(1)
        return 0

    lax.fori_loop(0, nch, attn_body, 0)
    value_stage(nch * sub - 1, 1, st_ref[row_a(1), :])
    dh = qt_ref.shape[2]
    inv_l = 1.0 / acc_ref[dh:dh + 1, :]
    for j in range(n_cols // Q_BLK // 2):
        pair = [acc_ref[0:dh, (2 * j + hh) * Q_BLK:(2 * j + hh + 1) * Q_BLK]
                * inv_l[:, (2 * j + hh) * Q_BLK:(2 * j + hh + 1) * Q_BLK] for hh in range(2)]
        o_ref[0, :, j * 2 * dh:(j + 1) * 2 * dh] = jnp.concatenate(pair, axis=0).T.astype(o_ref.dtype)


def _dsa(qt, qit, w_t, kv4, ki4, vt4):
    b, nqb, dh, n_cols = qt.shape
    n_steps = kv4.shape[1]
    assert kv4.shape[2] == CNT_BLK and CNT_BLK == 2 * K_BLK and n_cols % COL_BLK == 0
    assert vt4.shape[1] * K_BLK == n_steps * CNT_BLK and vt4.shape[2] == dh + SUBLANES
    per_q = lambda a: pl.BlockSpec((1, 1) + a.shape[2:], lambda bi, qi: (bi, qi, 0, 0))
    per_b = lambda a: pl.BlockSpec((1,) + a.shape[1:], lambda bi, qi: (bi, 0, 0, 0))
    return pl.pallas_call(
        _dsa_kernel,
        out_shape=jax.ShapeDtypeStruct((b, nqb * Q_BLK, (n_cols // Q_BLK) * dh), BF16),
        grid=(b, nqb),
        in_specs=[per_q(qt), per_q(qit), per_q(w_t), per_b(kv4), per_b(ki4), per_b(vt4)],
        out_specs=pl.BlockSpec((1, Q_BLK, (n_cols // Q_BLK) * dh), lambda bi, qi: (bi, qi, 0)),
        scratch_shapes=[pltpu.VMEM((n_steps, CNT_BLK, Q_BLK), F32),
                        pltpu.VMEM((LANES, n_cols), BF16),
                        pltpu.VMEM((LANES, n_cols), BF16),
                        pltpu.VMEM((1, n_cols), F32),
                        pltpu.VMEM((SUBLANES, n_cols), F32),
                        pltpu.VMEM((SUBLANES, n_cols), F32),
                        pltpu.VMEM((dh + SUBLANES, n_cols), F32),
                        pltpu.VMEM((CNT_BLK // K_BLK, K_BLK, n_cols), F32),
                        pltpu.VMEM((CNT_BLK // K_BLK, K_BLK, n_cols), BF16)],
        compiler_params=_cparams(("arbitrary", "arbitrary")),
        name="dsa",
    )(qt, qit, w_t, kv4, ki4, vt4)


def _outproj_kernel(x_ref, attn_ref, pool_ref, woa_ref, wop_ref, gate1_ref, gffn_ref,
                    scale2_ref, shift2_ref, wr_ref, br_ref, x1_ref, h2_ref, gates_ref):
    tm = x_ref.shape[1]
    mix = (jnp.dot(attn_ref[0], woa_ref[...], preferred_element_type=F32)
           + jnp.dot(pool_ref[0], wop_ref[...], preferred_element_type=F32))
    x1 = x_ref[0] + gate1_ref[0] * mix
    x1_ref[0] = x1
    ms = jnp.mean(x1 * x1, axis=-1, keepdims=True)
    h2 = (x1 * lax.rsqrt(ms + EPS) * gffn_ref[...]) * (1.0 + scale2_ref[0]) + shift2_ref[0]
    h2_hi = h2.astype(BF16)
    h2_ref[0] = h2_hi

    h2_lo = (h2 - h2_hi.astype(F32)).astype(BF16)
    wr = wr_ref[...]
    wr_hi = wr.astype(BF16)
    wr_lo = (wr - wr_hi.astype(F32)).astype(BF16)
    logits = (jnp.dot(h2_hi, wr_hi, preferred_element_type=F32)
              + jnp.dot(h2_lo, wr_hi, preferred_element_type=F32)
              + jnp.dot(h2_hi, wr_lo, preferred_element_type=F32)) + br_ref[...]

    lt = logits.T
    n_e = EXPERTS_PER_GROUP
    sub_id = lax.broadcasted_iota(jnp.int32, (SUBLANES, tm), 0)
    big = jnp.int32(LANES)
    glog = jnp.where(sub_id < N_GROUPS, lt[N_EXPERTS:N_EXPERTS + SUBLANES, :], -jnp.inf)
    gmax = jnp.max(glog, axis=0, keepdims=True)
    gsum = jnp.sum(jnp.exp(glog - gmax), axis=0, keepdims=True)
    p_g = 1.0 / gsum
    g_sel = jnp.min(jnp.where(glog == gmax, sub_id, big), axis=0, keepdims=True)
    elog = lt[0:n_e, :]
    for gi in range(1, N_GROUPS):
        elog = jnp.where(g_sel == gi, lt[gi * n_e:(gi + 1) * n_e, :], elog)
    emax = jnp.max(elog, axis=0, keepdims=True)
    eexp = jnp.exp(elog - emax)
    p_e = eexp / jnp.sum(eexp, axis=0, keepdims=True)
    p1 = jnp.max(p_e, axis=0, keepdims=True)
    i1 = jnp.min(jnp.where(p_e == p1, sub_id, big), axis=0, keepdims=True)
    p_e2 = jnp.where(sub_id == i1, -1.0, p_e)
    p2 = jnp.max(p_e2, axis=0, keepdims=True)
    i2 = jnp.min(jnp.where(p_e2 == p2, sub_id, big), axis=0, keepdims=True)
    tot = p1 + p2
    in_grp = (jnp.where(sub_id == i1, p_g * (p1 / tot), 0.0)
              + jnp.where(sub_id == i2, p_g * (p2 / tot), 0.0))
    rows = [jnp.where(g_sel == gi, in_grp, 0.0) for gi in range(N_GROUPS)]
    rows.append(jnp.where(sub_id == 0, g_sel.astype(F32), 0.0))
    rows.append(jnp.zeros((LANES - N_EXPERTS - SUBLANES, tm), F32))
    gates_ref[0] = jnp.concatenate(rows, axis=0).T


def _outproj(x, attn, pool, wo_a, wo_p, gate1, g_ffn, scale2, shift2, w_r, b_r):
    b, s, d = x.shape
    tm = TM_PROJ
    tok = lambda w: pl.BlockSpec((1, tm, w), lambda bi, si: (bi, si, 0))
    per_b = pl.BlockSpec((1, 1, d), lambda bi, si: (bi, 0, 0))
    full = lambda a: pl.BlockSpec(a.shape, lambda bi, si: (0,) * a.ndim)
    return pl.pallas_call(
        _outproj_kernel,
        out_shape=(jax.ShapeDtypeStruct((b, s, d), F32),
                   jax.ShapeDtypeStruct((b, s, d), BF16),
                   jax.ShapeDtypeStruct((b, s, LANES), F32)),
        grid=(b, s // tm),
        in_specs=[tok(d), tok(attn.shape[2]), tok(pool.shape[2]), full(wo_a), full(wo_p), per_b,
                  full(g_ffn), per_b, per_b, full(w_r), full(b_r)],
        out_specs=(tok(d), tok(d), tok(LANES)),
        compiler_params=_cparams(("arbitrary", "arbitrary")),
        name="outproj",
    )(x, attn, pool, wo_a, wo_p, gate1, g_ffn, scale2, shift2, w_r, b_r)


def _moe_kernel(x1_ref, h2_ref, gates_ref, gate2_ref, *refs):
    wgu_parts, wd_parts = refs[:MOE_W_PARTS], refs[MOE_W_PARTS:2 * MOE_W_PARTS]
    o_ref, xe_ref, rank_ref, rank_t_ref, own_ref, grpb_ref = refs[2 * MOE_W_PARTS:]
    g = pl.program_id(2)
    tm, d = h2_ref.shape[1], h2_ref.shape[2]
    per_part = wd_parts[0].shape[0]
    n_e, d_exp = per_part * MOE_W_PARTS, wd_parts[0].shape[1]
    gf = g.astype(F32)

    @pl.when(g == 0)
    def _():
        tri = (lax.broadcasted_iota(jnp.int32, (RANK_BLK, RANK_BLK), 0)
               >= lax.broadcasted_iota(jnp.int32, (RANK_BLK, RANK_BLK), 1)).astype(BF16)
        lane_b = lax.broadcasted_iota(jnp.int32, (RANK_BLK, LANES), 1)
        run = jnp.zeros((1, LANES), F32)
        for sb in range(tm // RANK_BLK):
            rows = slice(sb * RANK_BLK, (sb + 1) * RANK_BLK)
            gts_b = gates_ref[0, rows, :]
            grp = jnp.sum(jnp.where(lane_b == N_EXPERTS, gts_b, 0.0), axis=-1, keepdims=True)
            member = jnp.where(jnp.logical_and(lane_b < N_GROUPS, lane_b.astype(F32) == grp), 1.0, 0.0)
            pre = jnp.dot(tri, member.astype(BF16), preferred_element_type=F32) + run
            rank_ref[rows, :] = jnp.where(lane_b == N_GROUPS, grp, pre)
            own = jnp.sum(member * pre, axis=-1, keepdims=True) - 1.0
            own_ref[rows, :] = jnp.broadcast_to(own, (RANK_BLK, LANES))
            grpb_ref[rows, :] = jnp.broadcast_to(grp, (RANK_BLK, LANES))
            run = jnp.max(pre, axis=0, keepdims=True)
        rank_t_ref[...] = rank_ref[...].T
        gts = gates_ref[0]
        g_hi = gts.astype(BF16)
        xe_ref[:, :d] = h2_ref[0]
        xe_ref[:, d:d + LANES] = g_hi
        xe_ref[:, d + LANES:d + 2 * LANES] = (gts - g_hi.astype(F32)).astype(BF16)
        o_ref[0] = jnp.zeros((tm, d), F32)

    rank_row = rank_t_ref[pl.ds(g, 1), :]
    pos_row = jnp.where(rank_t_ref[N_GROUPS:N_GROUPS + 1, :] == gf, rank_row - 1.0, -1.0)
    pos_col = jnp.where(grpb_ref[...] == gf, own_ref[...], -1.0)
    n_rows = jnp.max(rank_row).astype(jnp.int32)

    def expert_pass(first_row, n_ch):
        row_id = lax.broadcasted_iota(jnp.int32, (n_ch, tm), 0).astype(F32)
        col_id = lax.broadcasted_iota(jnp.int32, (tm, n_ch), 1).astype(F32)
        lane_c = lax.broadcasted_iota(jnp.int32, (n_ch, LANES), 1)
        r0 = first_row.astype(F32)
        gather = jnp.where(pos_row - r0 == row_id, 1.0, 0.0).astype(BF16)
        xg = jnp.dot(gather, xe_ref[...], preferred_element_type=F32)
        xb = xg[:, :d].astype(BF16)
        gates_c = xg[:, d:d + LANES] + xg[:, d + LANES:d + 2 * LANES]
        ya = None
        for e in range(n_e):
            w_gu_e = wgu_parts[e // per_part][e % per_part]
            w_d_e = wd_parts[e // per_part][e % per_part]
            gu = jnp.dot(xb, w_gu_e, preferred_element_type=F32)
            gt = gu[:, :d_exp]
            a = (gt * jax.nn.sigmoid(gt)) * gu[:, d_exp:]
            gate_e = jnp.sum(jnp.where(lane_c == g * n_e + e, gates_c, 0.0), axis=-1, keepdims=True)
            y = jnp.dot((a * gate_e).astype(BF16), w_d_e, preferred_element_type=F32)
            ya = y if ya is None else ya + y
        pos_wide = jnp.concatenate([pos_col] * (n_ch // LANES), axis=1)
        scatter = jnp.where(pos_wide - r0 == col_id, 1.0, 0.0).astype(BF16)
        o_ref[0] += jnp.dot(scatter, ya.astype(BF16), preferred_element_type=F32)

    n_full = (n_rows + MOE_CH - MOE_CH_TAIL - 1) // MOE_CH

    def full_pass(c, _):
        expert_pass(c * MOE_CH, MOE_CH)
        return 0

    lax.fori_loop(0, n_full, full_pass, 0)

    @pl.when(n_rows > n_full * MOE_CH)
    def _():
        expert_pass(n_full * MOE_CH, MOE_CH_TAIL)

    @pl.when(g == pl.num_programs(2) - 1)
    def _():
        o_ref[0] = x1_ref[0] + gate2_ref[0] * o_ref[0]


def _moe(x1, h2, gates, gate2, w_gu, w_d):
    b, s, d = x1.shape
    tm = TM_MOE
    n_e = EXPERTS_PER_GROUP
    assert w_gu.shape[0] == N_GROUPS * n_e and tm % RANK_BLK == 0 and n_e % MOE_W_PARTS == 0
    assert MOE_CH % LANES == 0 and MOE_CH_TAIL % LANES == 0 and MOE_CH_TAIL <= MOE_CH
    per_part = n_e // MOE_W_PARTS
    tok = lambda w: pl.BlockSpec((1, tm, w), lambda bi, si, g: (bi, si, 0))

    def slab(w, k):
        return pl.BlockSpec((per_part,) + w.shape[1:], lambda bi, si, g: (g * MOE_W_PARTS + k, 0, 0))

    return pl.pallas_call(
        _moe_kernel,
        out_shape=jax.ShapeDtypeStruct((b, s, d), F32),
        grid=(b, s // tm, N_GROUPS),
        in_specs=([tok(d), tok(d), tok(LANES), pl.BlockSpec((1, 1, d), lambda bi, si, g: (bi, 0, 0))]
                  + [slab(w_gu, k) for k in range(MOE_W_PARTS)]
                  + [slab(w_d, k) for k in range(MOE_W_PARTS)]),
        out_specs=tok(d),
        scratch_shapes=[pltpu.VMEM((tm, d + 2 * LANES), BF16),
                        pltpu.VMEM((tm, LANES), F32),
                        pltpu.VMEM((LANES, tm), F32),
                        pltpu.VMEM((tm, LANES), F32),
                        pltpu.VMEM((tm, LANES), F32)],
        compiler_params=pltpu.CompilerParams(
            dimension_semantics=("arbitrary", "arbitrary", "arbitrary"),
            vmem_limit_bytes=VMEM_LIMIT_MOE_BYTES),
        name="moe",
    )(x1, h2, gates, gate2, *([w_gu] * MOE_W_PARTS), *([w_d] * MOE_W_PARTS))


def _layer(x, mod, pos3, g_mix, g_ffn, w_in, g_q, g_k, g_kidx, w_pool, pool_scale, w_out,
           w_rg, b_rg, w_re, b_re, w_gate, w_up, w_down):
    b, s, d = x.shape
    d_attn = N_HEADS * HEAD_DIM
    nqb = s // Q_BLK
    nkb = s // K_BLK
    shift1, scale1, gate1, shift2, scale2, gate2 = [m[:, None, :] for m in jnp.split(mod, 6, axis=-1)]

    n_front = d_attn + 2 * HEAD_DIM + N_IDX_HEADS * IDX_DIM + IDX_DIM + N_IDX_HEADS
    pad = (-n_front) % LANES
    w_in_p = jnp.concatenate([w_in[:, :n_front], jnp.zeros((d, pad), w_in.dtype), w_in[:, n_front:]],
                             axis=1).astype(BF16)
    seg_id = jnp.arange(d_attn) // HEAD_DIM
    segsum = (seg_id[:, None] == seg_id[None, :]).astype(BF16)
    ones_half = jnp.ones((LANES - HEAD_DIM,), F32)
    gq_t = (jnp.tile(g_q, N_HEADS) * (LOG2_E * HEAD_DIM ** -0.5))[None, :]
    gk_e = jnp.concatenate([g_k, ones_half])[None, :]
    gkidx_e = jnp.concatenate([g_kidx, ones_half])[None, :]
    half = HEAD_DIM // 2
    inv_freq = ROPE_THETA ** (-jnp.arange(0, HEAD_DIM, 2, dtype=F32) / HEAD_DIM)
    invf = jnp.tile(inv_freq, LANES // half)[None, :]

    qt, kv, qit, ki, w_t, vt4, pool = _inproj(pos3, x, scale1, shift1, g_mix[None, :], w_in_p, segsum,
                                              gq_t, gk_e, gkidx_e, invf, w_pool.astype(BF16),
                                              pool_scale[None, :])
    kv4 = kv.reshape(b, s // CNT_BLK, CNT_BLK, LANES)
    ki4 = ki.reshape(b, s // CNT_BLK, CNT_BLK, LANES)
    attn = _dsa(qt, qit, w_t, kv4, ki4, vt4)

    w_out_b = w_out.astype(BF16)
    w_r = jnp.concatenate([w_re, w_rg, jnp.zeros((d, LANES - N_EXPERTS - N_GROUPS), F32)], axis=1)
    b_r = jnp.concatenate([b_re, b_rg, jnp.zeros((LANES - N_EXPERTS - N_GROUPS,), F32)])[None, :]
    x1, h2, gates = _outproj(x, attn, pool, w_out_b[:d_attn], w_out_b[d_attn:], gate1,
                             g_ffn[None, :], scale2, shift2, w_r, b_r)

    w_gu = jnp.concatenate([w_gate, w_up], axis=-1).astype(BF16)
    return _moe(x1, h2, gates, gate2, w_gu, w_down.astype(BF16))


def kernel(x, c, positions, w_ada, b_ada, g_norm_mix, g_norm_ffn, w_in, g_q, g_k, g_kidx, w_pool,
           pool_scale, w_out, w_router_group, b_router_group, w_router_expert, b_router_expert,
           w_gate, w_up, w_down):
    b, s, d = x.shape
    depth = w_ada.shape[0]
    assert s % TM_MOE == 0 and s % K_BLK == 0 and d % LANES == 0
    pos3 = positions[:, :, None]
    c_pad = jnp.concatenate([c, jnp.zeros((-b % SUBLANES, d), c.dtype)], axis=0)
    for l in range(depth):
        mod = _adaln(c_pad, w_ada[l], b_ada[l][None, :])[:b]
        x = _layer(x, mod, pos3, g_norm_mix[l], g_norm_ffn[l], w_in[l], g_q[l], g_k[l], g_kidx[l],
                   w_pool[l], pool_scale[l], w_out[l], w_router_group[l], b_router_group[l],
                   w_router_expert[l], b_router_expert[l], w_gate[l], w_up[l], w_down[l])
    return x
```

```python
import functools

import jax
import jax.numpy as jnp
from jax import lax
from jax.experimental import pallas as pl
from jax.experimental.pallas import tpu as pltpu

N_HEADS = 8
HEAD_DIM = 64
N_IDX_HEADS = 8
IDX_DIM = 64
TOPK_MAX = 256
ROPE_THETA = 10000.0
POOL_WINDOWS = (2, 4, 8, 16)
N_GROUPS = 4
EXPERTS_PER_GROUP = 8
N_EXPERTS = N_GROUPS * EXPERTS_PER_GROUP
EPS = 1e-6
N_MOD = 6
assert EXPERTS_PER_GROUP & (EXPERTS_PER_GROUP - 1) == 0

LANES = 128
SUBLANES = 8
VMEM_LIMIT_BYTES = 56 * 1024 * 1024
VMEM_LIMIT_MOE_BYTES = 60 * 1024 * 1024

Q_BLK = 512
K_BLK = 256
COL_BLK = 512
CNT_BLK = 512
CNT_ROWS = 32
SEARCH_FIRST = 17
SEARCH_PERIOD = 1
TM_PROJ = 512
TM_MOE = 1024
MOE_CH = 256
MOE_CH_TAIL = 128
RANK_BLK = 256
MOE_W_PARTS = 4
MAX_WIN = max(POOL_WINDOWS)
assert all(w == 2 ** (g + 1) for g, w in enumerate(POOL_WINDOWS))
M_INIT = -1e29
MASKED = -1e30
F32_LOWEST = -3.0e38
LOG2_E = 1.4426950408889634

BF16 = jnp.bfloat16
F32 = jnp.float32


def _cparams(sem):
    return pltpu.CompilerParams(dimension_semantics=sem, vmem_limit_bytes=VMEM_LIMIT_BYTES)


def _adaln_kernel(c_ref, w_ref, b_ref, o_ref):
    c = c_ref[...]
    c_act = c * jax.nn.sigmoid(c)
    o_ref[...] = jnp.dot(c_act, w_ref[...], preferred_element_type=F32) + b_ref[...]


def _adaln(c_pad, w_ada, b_ada):
    rows, d = c_pad.shape
    n = w_ada.shape[1]
    tn = n // N_MOD
    return pl.pallas_call(
        _adaln_kernel,
        out_shape=jax.ShapeDtypeStruct((rows, n), F32),
        grid=(n // tn,),
        in_specs=[pl.BlockSpec((rows, d), lambda j: (0, 0)),
                  pl.BlockSpec((d, tn), lambda j: (0, j)),
                  pl.BlockSpec((1, tn), lambda j: (0, j))],
        out_specs=pl.BlockSpec((rows, tn), lambda j: (0, j)),
        compiler_params=_cparams(("arbitrary",)),
        name="adaln",
    )(c_pad, w_ada, b_ada)


_PIO2_HI, _PIO2_MID, _PIO2_LO = 1.5703125, 4.837512969970703125e-4, 7.54978995489188e-8
_SIN_COEF = (-1.9515295891e-4, 8.3321608736e-3, -1.6666654611e-1)
_COS_COEF = (2.443315711809948e-5, -1.388731625493765e-3, 4.166664568298827e-2)


def _sincos(x):
    k = jnp.floor(x * (2.0 / jnp.pi) + 0.5)
    r = ((x - k * _PIO2_HI) - k * _PIO2_MID) - k * _PIO2_LO
    z = r * r
    s = r + r * z * (_SIN_COEF[2] + z * (_SIN_COEF[1] + z * _SIN_COEF[0]))
    c = 1.0 - 0.5 * z + z * z * (_COS_COEF[2] + z * (_COS_COEF[1] + z * _COS_COEF[0]))
    q = k - 4.0 * jnp.floor(k * 0.25)
    odd = jnp.logical_or(q == 1.0, q == 3.0)
    sin_b = jnp.where(odd, c, s)
    cos_b = jnp.where(odd, s, c)
    return (jnp.where(q >= 2.0, -sin_b, sin_b),
            jnp.where(jnp.logical_or(q == 1.0, q == 2.0), -cos_b, cos_b))


def _rope_chunk(y, cos, sin_signed, first_half):
    from_hi = pltpu.roll(y, LANES - HEAD_DIM // 2, 1)
    from_lo = pltpu.roll(y, HEAD_DIM // 2, 1)
    return y * cos + jnp.where(first_half, from_hi, from_lo) * sin_signed


def _inproj_kernel(pos_ref, x_ref, scale_ref, shift_ref, gmix_ref, win_ref, segsum_ref,
                   gq_ref, gk_ref, gkidx_ref, invf_ref, wpool_ref, pscale_ref,
                   qt_ref, kv_ref, qit_ref, ki_ref, wt_ref, vt_ref, pool_ref,
                   ubuf_ref, proj_a_ref, proj_b_ref):
    tm = x_ref.shape[1]
    sb = K_BLK
    step = pl.program_id(1)
    tile = jnp.maximum(step - 1, 0)

    @pl.when(step == 0)
    def _():
        proj_b_ref[...] = jnp.zeros(proj_b_ref.shape, F32)

    @pl.when(step <= 1)
    def _():
        for lvl in range(len(POOL_WINDOWS)):
            ubuf_ref[lvl, 0:MAX_WIN, lvl * LANES:] = jnp.zeros(
                (MAX_WIN, ubuf_ref.shape[2] - lvl * LANES), F32)

    @pl.when(step > 1)
    def _():
        for lvl in range(len(POOL_WINDOWS)):
            ubuf_ref[lvl, 0:MAX_WIN, lvl * LANES:] = ubuf_ref[lvl, tm:tm + MAX_WIN, lvl * LANES:]

    def run(write_ref, read_ref):
        for t in range(tm // sb):
            _inproj_post(t, read_ref[t], tile * tm + t * sb, pos_ref, segsum_ref, gq_ref, gk_ref,
                         gkidx_ref, invf_ref, wpool_ref, pscale_ref, qt_ref, kv_ref, qit_ref, ki_ref,
                         wt_ref, vt_ref, pool_ref, ubuf_ref)
        gain = gmix_ref[...] * (1.0 + scale_ref[0])
        for t in range(tm // sb):
            x = x_ref[0, t * sb:(t + 1) * sb, :]
            ms = jnp.mean(x * x, axis=-1, keepdims=True)
            h = (x * lax.rsqrt(ms + EPS) * gain + shift_ref[0]).astype(BF16)
            write_ref[t] = jnp.dot(h, win_ref[...], preferred_element_type=F32)

    @pl.when(step % 2 == 0)
    def _():
        run(proj_a_ref, proj_b_ref)

    @pl.when(step % 2 == 1)
    def _():
        run(proj_b_ref, proj_a_ref)


def _inproj_post(t, proj, t0, pos_ref, segsum_ref, gq_ref, gk_ref, gkidx_ref, invf_ref, wpool_ref,
                 pscale_ref, qt_ref, kv_ref, qit_ref, ki_ref, wt_ref, vt_ref, pool_ref, ubuf_ref):
    sb = K_BLK
    rows = slice(t * sb, (t + 1) * sb)
    tq, q0 = (t * sb) // Q_BLK, (t * sb) % Q_BLK
    d_attn = N_HEADS * HEAD_DIM
    d_qidx = N_IDX_HEADS * IDX_DIM

    def store_cols(dst_ref, chunk, j):
        ct = chunk.T
        for hh in range(2):
            col = (2 * j + hh) * Q_BLK + q0
            dst_ref[0, tq, :, col:col + sb] = ct[hh * HEAD_DIM:(hh + 1) * HEAD_DIM, :].astype(dst_ref.dtype)

    lane = lax.broadcasted_iota(jnp.int32, (sb, LANES), 1)
    first_half = (lane & (HEAD_DIM - 1)) < (HEAD_DIM // 2)
    ang = pos_ref[0, rows, :].astype(F32) * invf_ref[...]
    sin, cos = _sincos(ang)
    sin_signed = jnp.where(first_half, -sin, sin)
    rope = functools.partial(_rope_chunk, cos=cos, sin_signed=sin_signed, first_half=first_half)

    qf = proj[:, :d_attn]
    qsq = qf * qf
    qsq_hi = qsq.astype(BF16)
    qsq_lo = (qsq - qsq_hi.astype(F32)).astype(BF16)
    seg = segsum_ref[...]
    ssq = (jnp.dot(qsq_hi, seg, preferred_element_type=F32)
           + jnp.dot(qsq_lo, seg, preferred_element_type=F32))
    qn = qf * lax.rsqrt(ssq * (1.0 / HEAD_DIM) + EPS) * gq_ref[...]
    for j in range(d_attn // LANES):
        sl = slice(j * LANES, (j + 1) * LANES)
        store_cols(qt_ref, rope(qn[:, sl]), j)

    kvc = proj[:, d_attn:d_attn + LANES]
    is_k = lane < HEAD_DIM
    ksq = jnp.sum(jnp.where(is_k, kvc * kvc, 0.0), axis=-1, keepdims=True)
    kn = kvc * lax.rsqrt(ksq * (1.0 / HEAD_DIM) + EPS) * gk_ref[...]
    kv_ref[0, rows, :] = jnp.where(is_k, rope(kn), kvc).astype(BF16)
    row8 = lax.broadcasted_iota(jnp.int32, (SUBLANES, K_BLK), 0)
    vt_ref[0, t, 0:HEAD_DIM, :] = kvc.T[HEAD_DIM:, :].astype(BF16)
    vt_ref[0, t, HEAD_DIM:HEAD_DIM + SUBLANES, :] = jnp.where(row8 == 0, 1.0, 0.0).astype(BF16)

    o_qi = d_attn + LANES
    for j in range(d_qidx // LANES):
        store_cols(qit_ref, rope(proj[:, o_qi + j * LANES:o_qi + (j + 1) * LANES]), j)

    o_ki = o_qi + d_qidx
    kic = proj[:, o_ki:o_ki + LANES]
    kisq = jnp.sum(jnp.where(is_k, kic * kic, 0.0), axis=-1, keepdims=True)
    kin = kic * lax.rsqrt(kisq * (1.0 / IDX_DIM) + EPS) * gkidx_ref[...]
    ki_ref[0, rows, :] = jnp.where(is_k, rope(kin), 0.0).astype(BF16)
    wt_ref[0, tq, :, q0:q0 + sb] = kic.T[IDX_DIM:IDX_DIM + N_IDX_HEADS, :] * (
        N_IDX_HEADS ** -0.5 * IDX_DIM ** -0.5)

    o_u = o_ki + LANES
    u = proj[:, o_u:o_u + LANES * len(POOL_WINDOWS)]

    base = MAX_WIN + t * sb
    t_idx = t0 + lax.broadcasted_iota(jnp.int32, (sb, 1), 0)
    level = u
    for g, win in enumerate(POOL_WINDOWS):
        sl = slice(g * LANES, (g + 1) * LANES)
        shift = win // 2
        ubuf_ref[g, base:base + sb, g * LANES:] = level
        level = level + ubuf_ref[g, base - shift:base - shift + sb, g * LANES:]
        wsum = level[:, :LANES]
        if g + 1 < len(POOL_WINDOWS):
            level = level[:, LANES:]
        cnt = jnp.minimum(t_idx + 1, win).astype(F32)
        pooled = wsum / cnt - u[:, sl]
        mixed = jnp.dot(pooled.astype(BF16), wpool_ref[g], preferred_element_type=F32)
        pool_ref[0, rows, sl] = (mixed * pscale_ref[:, sl]).astype(BF16)


def _inproj(pos3, x, scale1, shift1, g_mix, w_in_p, segsum, gq_t, gk_e, gkidx_e, invf, w_pool, pscale):
    b, s, d = x.shape
    tm = TM_PROJ
    d_attn = N_HEADS * HEAD_DIM
    d_qidx = N_IDX_HEADS * IDX_DIM
    d_pool = LANES * len(POOL_WINDOWS)
    assert tm % Q_BLK == 0 and Q_BLK % K_BLK == 0 and HEAD_DIM == IDX_DIM and 2 * HEAD_DIM == LANES
    n_tiles = s // tm
    ahead = lambda si: jnp.minimum(si, n_tiles - 1)
    behind = lambda si: jnp.maximum(si - 1, 0)
    tok = lambda w: pl.BlockSpec((1, tm, w), lambda bi, si: (bi, behind(si), 0))
    blk = lambda n, r, c: pl.BlockSpec((1, tm // n, r, c), lambda bi, si: (bi, behind(si), 0, 0))
    per_b = pl.BlockSpec((1, 1, d), lambda bi, si: (bi, 0, 0))
    full = lambda a: pl.BlockSpec(a.shape, lambda bi, si: (0,) * a.ndim)
    nqb, nkb = s // Q_BLK, s // K_BLK
    proj_buf = pltpu.VMEM((tm // K_BLK, K_BLK, w_in_p.shape[1]), F32)
    return pl.pallas_call(
        _inproj_kernel,
        out_shape=(jax.ShapeDtypeStruct((b, nqb, HEAD_DIM, N_HEADS * Q_BLK), BF16),
                   jax.ShapeDtypeStruct((b, s, LANES), BF16),
                   jax.ShapeDtypeStruct((b, nqb, IDX_DIM, N_IDX_HEADS * Q_BLK), BF16),
                   jax.ShapeDtypeStruct((b, s, LANES), BF16),
                   jax.ShapeDtypeStruct((b, nqb, N_IDX_HEADS, Q_BLK), F32),
                   jax.ShapeDtypeStruct((b, nkb, HEAD_DIM + SUBLANES, K_BLK), BF16),
                   jax.ShapeDtypeStruct((b, s, d_pool), BF16)),
        grid=(b, n_tiles + 1),
        in_specs=[tok(1), pl.BlockSpec((1, tm, d), lambda bi, si: (bi, ahead(si), 0)), per_b, per_b,
                  full(g_mix), full(w_in_p), full(segsum),
                  full(gq_t), full(gk_e), full(gkidx_e), full(invf), full(w_pool), full(pscale)],
        out_specs=(blk(Q_BLK, HEAD_DIM, N_HEADS * Q_BLK), tok(LANES),
                   blk(Q_BLK, IDX_DIM, N_IDX_HEADS * Q_BLK), tok(LANES),
                   blk(Q_BLK, N_IDX_HEADS, Q_BLK), blk(K_BLK, HEAD_DIM + SUBLANES, K_BLK), tok(d_pool)),
        scratch_shapes=[pltpu.VMEM((len(POOL_WINDOWS), tm + MAX_WIN, d_pool), F32),
                        proj_buf, proj_buf],
        compiler_params=_cparams(("arbitrary", "arbitrary")),
        name="inproj",
    )(pos3, x, scale1, shift1, g_mix, w_in_p, segsum, gq_t, gk_e, gkidx_e, invf, w_pool, pscale)


def _dsa_kernel(qt_ref, qit_ref, w_ref, kv_ref, ki_ref, vt_ref, o_ref,
                sc_ref, qe_ref, qie_ref, m_ref, mx_ref, st_ref, acc_ref, lg_ref, p_ref):
    topk = float(min(TOPK_MAX, (sc_ref.shape[0] * CNT_BLK) // 4))
    qb = pl.program_id(1)
    n_cols = qt_ref.shape[3]
    n_chunks = n_cols // COL_BLK
    sub = CNT_BLK // K_BLK
    nch = ((qb + 1) * Q_BLK + CNT_BLK - 1) // CNT_BLK
    kgrp = K_BLK // SUBLANES
    sub_rows = [slice(j * K_BLK, (j + 1) * K_BLK) for j in range(sub)]

    zeros_half = jnp.zeros((LANES - HEAD_DIM, n_cols), BF16)
    qe_ref[0:HEAD_DIM, :] = qt_ref[0, 0]
    qe_ref[HEAD_DIM:LANES, :] = zeros_half
    qie_ref[0:IDX_DIM, :] = qit_ref[0, 0]
    qie_ref[IDX_DIM:LANES, :] = zeros_half

    q_pos = qb * Q_BLK + lax.broadcasted_iota(jnp.int32, (K_BLK, Q_BLK), 1)
    key_off = lax.broadcasted_iota(jnp.int32, (K_BLK, Q_BLK), 0)

    def score_step(ch, carry):
        rmax, rmin = carry
        for j in range(sub):
            ki_blk = ki_ref[0, ch, sub_rows[j], :]
            score = None
            for cc in range(n_chunks):
                cs = slice(cc * COL_BLK, (cc + 1) * COL_BLK)
                s_h = jnp.dot(ki_blk, qie_ref[:, cs], preferred_element_type=F32)
                s_h = jnp.maximum(s_h, 0.0)
                for hh in range(COL_BLK // Q_BLK):
                    head = cc * (COL_BLK // Q_BLK) + hh
                    part = s_h[:, hh * Q_BLK:(hh + 1) * Q_BLK] * w_ref[0, 0, head:head + 1, :]
                    score = part if score is None else score + part
            causal = (ch * CNT_BLK + j * K_BLK + key_off) <= q_pos
            masked = jnp.where(causal, score, -jnp.inf)
            sc_ref[ch, sub_rows[j], :] = masked
            hi_part = masked.reshape(kgrp, SUBLANES, Q_BLK).max(axis=0)
            lo_part = jnp.where(causal, score, jnp.inf).reshape(kgrp, SUBLANES, Q_BLK).min(axis=0)
            rmax, rmin = jnp.maximum(rmax, hi_part), jnp.minimum(rmin, lo_part)
        return rmax, rmin

    def score_body(i, carry):
        return score_step(2 * i + 1, score_step(2 * i, carry))

    stats = lax.fori_loop(
        0, nch // 2, score_body,
        (jnp.full((SUBLANES, Q_BLK), -jnp.inf, F32), jnp.full((SUBLANES, Q_BLK), jnp.inf, F32)))
    rmax8, rmin8 = lax.cond(nch % 2 == 1, lambda c: score_step(nch - 1, c), lambda c: c, stats)
    rowmax = jnp.max(rmax8, axis=0, keepdims=True)
    rowmin = jnp.min(rmin8, axis=0, keepdims=True)

    n_causal = (qb * Q_BLK + 1 + lax.broadcasted_iota(jnp.int32, (1, Q_BLK), 1)).astype(F32)
    kt = jnp.minimum(n_causal, topk)

    cgrp = CNT_BLK // CNT_ROWS

    def count_ge(t):
        def body(ch, acc):
            for r in range(cgrp):
                rows = sc_ref[ch, r * CNT_ROWS:(r + 1) * CNT_ROWS, :]
                acc = acc + jnp.where(rows >= t, 1.0, 0.0)
            return acc
        acc = lax.fori_loop(0, nch, body, jnp.zeros((CNT_ROWS, Q_BLK), F32))
        return jnp.sum(acc, axis=0, keepdims=True)

    def bisect_pass(state):
        lo, hi, top, c_lo, c_hi, thr, done = state
        cap = jnp.minimum(hi, top)
        mid = lo + 0.5 * (cap - lo)
        mid = jnp.where(mid <= lo, cap, mid)
        c = count_ge(mid)
        hit = jnp.logical_and(done == 0.0, c == kt)
        thr = jnp.where(hit, mid, thr)
        done = jnp.where(hit, 1.0, done)
        active = done == 0.0
        up = jnp.logical_and(active, c >= kt)
        down = jnp.logical_and(active, c < kt)
        return (jnp.where(up, mid, lo), jnp.where(down, mid, hi), jnp.where(down, jnp.inf, top),
                jnp.where(up, c, c_lo), jnp.where(down, c, c_hi), thr, done)

    def snap_pass(state):
        lo, hi, top, c_lo, c_hi, thr, done = state

        def body(ch, carry):
            a8, b8 = carry
            for r in range(cgrp):
                s = sc_ref[ch, r * CNT_ROWS:(r + 1) * CNT_ROWS, :]
                a8 = jnp.minimum(a8, jnp.where(s >= lo, s, jnp.inf))
                b8 = jnp.maximum(b8, jnp.where(s < hi, s, -jnp.inf))
            return a8, b8

        a8, b8 = lax.fori_loop(
            0, nch, body,
            (jnp.full((CNT_ROWS, Q_BLK), jnp.inf, F32), jnp.full((CNT_ROWS, Q_BLK), -jnp.inf, F32)))
        a = jnp.min(a8, axis=0, keepdims=True)
        b = jnp.max(b8, axis=0, keepdims=True)
        active = done == 0.0
        hit = jnp.logical_and(active, jnp.logical_or(a == b, kt - c_hi == 1.0))
        thr = jnp.where(hit, b, thr)
        done = jnp.where(hit, 2.0, done)
        c_lo = jnp.where(jnp.logical_and(hit, a != b), kt + 1.0, c_lo)
        return jnp.where(active, a, lo), hi, jnp.where(active, b, top), c_lo, c_hi, thr, done

    few = n_causal <= topk
    state0 = (rowmin, jnp.full((1, Q_BLK), jnp.inf, F32), rowmax, n_causal,
              jnp.zeros((1, Q_BLK), F32), jnp.where(few, F32_LOWEST, 0.0), jnp.where(few, 1.0, 0.0))

    def outer_cond(carry):
        return carry[1] > 0.0

    def outer_body(carry):
        state, _ = carry
        state = lax.fori_loop(0, SEARCH_PERIOD, lambda i, st: bisect_pass(st), state)
        state = snap_pass(state)
        pending = jnp.max(jnp.where(state[6] == 0.0, 1.0, 0.0))
        return state, pending

    state1 = lax.fori_loop(0, SEARCH_FIRST, lambda i, st: bisect_pass(st), state0)
    state1 = snap_pass(state1)
    pending1 = jnp.max(jnp.where(state1[6] == 0.0, 1.0, 0.0))
    (lo, hi, _, c_lo, c_hi, thr, done), _ = lax.while_loop(outer_cond, outer_body, (state1, pending1))

    excess = jnp.where(done == 2.0, c_lo - kt, 0.0)
    need = kt - c_hi

    @pl.when(jnp.max(excess) > 0.0)
    def _():
        tri = (lax.broadcasted_iota(jnp.int32, (K_BLK, K_BLK), 0)
               >= lax.broadcasted_iota(jnp.int32, (K_BLK, K_BLK), 1)).astype(BF16)
        has_excess = excess > 0.0

        def drop_step(ch, run):
            for j in range(sub):
                s = sc_ref[ch, sub_rows[j], :]
                tied = jnp.logical_and(s == thr, has_excess)
                prefix = jnp.dot(tri, jnp.where(tied, 1.0, 0.0).astype(BF16), preferred_element_type=F32)
                drop = jnp.logical_and(tied, run + prefix > need)
                sc_ref[ch, sub_rows[j], :] = jnp.where(drop, -jnp.inf, s)
                run = run + jnp.max(prefix, axis=0, keepdims=True)
            return run

        run = lax.fori_loop(0, nch // 2, lambda i, r: drop_step(2 * i + 1, drop_step(2 * i, r)),
                            jnp.zeros((1, Q_BLK), F32))

        @pl.when(nch % 2 == 1)
        def _():
            drop_step(nch - 1, run)

    m_ref[...] = jnp.full(m_ref.shape, M_INIT, F32)
    acc_ref[...] = jnp.zeros(acc_ref.shape, F32)
    row_m = lambda j: slice(j, j + 1)
    row_a = lambda j: slice(sub + j, sub + j + 1)

    def logits_stage(ch, j):
        kv_blk = kv_ref[0, ch, sub_rows[j], :]
        bias = jnp.where(sc_ref[ch, sub_rows[j], :] >= thr, 0.0, MASKED)
        for cc in range(n_chunks):
            logits = jnp.dot(kv_blk, qe_ref[:, cc * COL_BLK:(cc + 1) * COL_BLK],
                             preferred_element_type=F32)
            for hh in range(COL_BLK // Q_BLK):
                cs = slice(cc * COL_BLK + hh * Q_BLK, cc * COL_BLK + (hh + 1) * Q_BLK)
                lg = logits[:, hh * Q_BLK:(hh + 1) * Q_BLK] + bias
                lg_ref[j, :, cs] = lg
                mx_ref[:, cs] = lg.reshape(kgrp, SUBLANES, Q_BLK).max(axis=0)
        m_old = m_ref[...]
        m_new = jnp.maximum(m_old, jnp.max(mx_ref[...], axis=0, keepdims=True))
        st_ref[row_m(j), :] = m_new
        st_ref[row_a(j), :] = jnp.exp2(m_old - m_new)
        m_ref[...] = m_new

    def probs_stage(j):
        p_ref[j] = jnp.exp2(lg_ref[j] - st_ref[row_m(j), :]).astype(BF16)

    def value_stage(kb, j, alpha):
        acc_ref[...] = acc_ref[...] * alpha + jnp.dot(
            vt_ref[0, kb], p_ref[j], preferred_element_type=F32)

    p_ref[sub - 1] = jnp.zeros(p_ref.shape[1:], BF16)
    st_ref[row_a(sub - 1), :] = jnp.ones((1, n_cols), F32)
    logits_stage(0, 0)

    def attn_body(ch, _):
        alpha_prev = st_ref[row_a(1), :]
        logits_stage(ch, 1)
        value_stage(jnp.maximum(ch * sub - 1, 0), 1, alpha_prev)
        probs_stage(0)
        alpha_cur = st_ref[row_a(0), :]
        logits_stage(jnp.minimum(ch + 1, nch - 1), 0)
        value_stage(ch * sub, 0, alpha_cur)
        probs_stage(1)
        return 0

    lax.fori_loop(0, nch, attn_body, 0)
    value_stage(nch * sub - 1, 1, st_ref[row_a(1), :])
    dh = qt_ref.shape[2]
    inv_l = 1.0 / acc_ref[dh:dh + 1, :]
    for j in range(n_cols // Q_BLK // 2):
        pair = [acc_ref[0:dh, (2 * j + hh) * Q_BLK:(2 * j + hh + 1) * Q_BLK]
                * inv_l[:, (2 * j + hh) * Q_BLK:(2 * j + hh + 1) * Q_BLK] for hh in range(2)]
        o_ref[0, :, j * 2 * dh:(j + 1) * 2 * dh] = jnp.concatenate(pair, axis=0).T.astype(o_ref.dtype)


def _dsa(qt, qit, w_t, kv4, ki4, vt4):
    b, nqb, dh, n_cols = qt.shape
    n_steps = kv4.shape[1]
    assert kv4.shape[2] == CNT_BLK and CNT_BLK == 2 * K_BLK and n_cols % COL_BLK == 0
    assert vt4.shape[1] * K_BLK == n_steps * CNT_BLK and vt4.shape[2] == dh + SUBLANES
    per_q = lambda a: pl.BlockSpec((1, 1) + a.shape[2:], lambda bi, qi: (bi, qi, 0, 0))
    per_b = lambda a: pl.BlockSpec((1,) + a.shape[1:], lambda bi, qi: (bi, 0, 0, 0))
    return pl.pallas_call(
        _dsa_kernel,
        out_shape=jax.ShapeDtypeStruct((b, nqb * Q_BLK, (n_cols // Q_BLK) * dh), BF16),
        grid=(b, nqb),
        in_specs=[per_q(qt), per_q(qit), per_q(w_t), per_b(kv4), per_b(ki4), per_b(vt4)],
        out_specs=pl.BlockSpec((1, Q_BLK, (n_cols // Q_BLK) * dh), lambda bi, qi: (bi, qi, 0)),
        scratch_shapes=[pltpu.VMEM((n_steps, CNT_BLK, Q_BLK), F32),
                        pltpu.VMEM((LANES, n_cols), BF16),
                        pltpu.VMEM((LANES, n_cols), BF16),
                        pltpu.VMEM((1, n_cols), F32),
                        pltpu.VMEM((SUBLANES, n_cols), F32),
                        pltpu.VMEM((SUBLANES, n_cols), F32),
                        pltpu.VMEM((dh + SUBLANES, n_cols), F32),
                        pltpu.VMEM((CNT_BLK // K_BLK, K_BLK, n_cols), F32),
                        pltpu.VMEM((CNT_BLK // K_BLK, K_BLK, n_cols), BF16)],
        compiler_params=_cparams(("arbitrary", "arbitrary")),
        name="dsa",
    )(qt, qit, w_t, kv4, ki4, vt4)


def _outproj_kernel(x_ref, attn_ref, pool_ref, woa_ref, wop_ref, gate1_ref, gffn_ref,
                    scale2_ref, shift2_ref, wr_ref, br_ref, x1_ref, h2_ref, gates_ref):
    tm = x_ref.shape[1]
    mix = (jnp.dot(attn_ref[0], woa_ref[...], preferred_element_type=F32)
           + jnp.dot(pool_ref[0], wop_ref[...], preferred_element_type=F32))
    x1 = x_ref[0] + gate1_ref[0] * mix
    x1_ref[0] = x1
    ms = jnp.mean(x1 * x1, axis=-1, keepdims=True)
    h2 = (x1 * lax.rsqrt(ms + EPS) * gffn_ref[...]) * (1.0 + scale2_ref[0]) + shift2_ref[0]
    h2_hi = h2.astype(BF16)
    h2_ref[0] = h2_hi

    h2_lo = (h2 - h2_hi.astype(F32)).astype(BF16)
    wr = wr_ref[...]
    wr_hi = wr.astype(BF16)
    wr_lo = (wr - wr_hi.astype(F32)).astype(BF16)
    logits = (jnp.dot(h2_hi, wr_hi, preferred_element_type=F32)
              + jnp.dot(h2_lo, wr_hi, preferred_element_type=F32)
              + jnp.dot(h2_hi, wr_lo, preferred_element_type=F32)) + br_ref[...]

    lt = logits.T
    n_e = EXPERTS_PER_GROUP
    sub_id = lax.broadcasted_iota(jnp.int32, (SUBLANES, tm), 0)
    big = jnp.int32(LANES)
    glog = jnp.where(sub_id < N_GROUPS, lt[N_EXPERTS:N_EXPERTS + SUBLANES, :], -jnp.inf)
    gmax = jnp.max(glog, axis=0, keepdims=True)
    gsum = jnp.sum(jnp.exp(glog - gmax), axis=0, keepdims=True)
    p_g = 1.0 / gsum
    g_sel = jnp.min(jnp.where(glog == gmax, sub_id, big), axis=0, keepdims=True)
    elog = lt[0:n_e, :]
    for gi in range(1, N_GROUPS):
        elog = jnp.where(g_sel == gi, lt[gi * n_e:(gi + 1) * n_e, :], elog)
    emax = jnp.max(elog, axis=0, keepdims=True)
    eexp = jnp.exp(elog - emax)
    p_e = eexp / jnp.sum(eexp, axis=0, keepdims=True)
    p1 = jnp.max(p_e, axis=0, keepdims=True)
    i1 = jnp.min(jnp.where(p_e == p1, sub_id, big), axis=0, keepdims=True)
    p_e2 = jnp.where(sub_id == i1, -1.0, p_e)
    p2 = jnp.max(p_e2, axis=0, keepdims=True)
    i2 = jnp.min(jnp.where(p_e2 == p2, sub_id, big), axis=0, keepdims=True)
    tot = p1 + p2
    in_grp = (jnp.where(sub_id == i1, p_g * (p1 / tot), 0.0)
              + jnp.where(sub_id == i2, p_g * (p2 / tot), 0.0))
    rows = [jnp.where(g_sel == gi, in_grp, 0.0) for gi in range(N_GROUPS)]
    rows.append(jnp.where(sub_id == 0, g_sel.astype(F32), 0.0))
    rows.append(jnp.zeros((LANES - N_EXPERTS - SUBLANES, tm), F32))
    gates_ref[0] = jnp.concatenate(rows, axis=0).T


def _outproj(x, attn, pool, wo_a, wo_p, gate1, g_ffn, scale2, shift2, w_r, b_r):
    b, s, d = x.shape
    tm = TM_PROJ
    tok = lambda w: pl.BlockSpec((1, tm, w), lambda bi, si: (bi, si, 0))
    per_b = pl.BlockSpec((1, 1, d), lambda bi, si: (bi, 0, 0))
    full = lambda a: pl.BlockSpec(a.shape, lambda bi, si: (0,) * a.ndim)
    return pl.pallas_call(
        _outproj_kernel,
        out_shape=(jax.ShapeDtypeStruct((b, s, d), F32),
                   jax.ShapeDtypeStruct((b, s, d), BF16),
                   jax.ShapeDtypeStruct((b, s, LANES), F32)),
        grid=(b, s // tm),
        in_specs=[tok(d), tok(attn.shape[2]), tok(pool.shape[2]), full(wo_a), full(wo_p), per_b,
                  full(g_ffn), per_b, per_b, full(w_r), full(b_r)],
        out_specs=(tok(d), tok(d), tok(LANES)),
        compiler_params=_cparams(("arbitrary", "arbitrary")),
        name="outproj",
    )(x, attn, pool, wo_a, wo_p, gate1, g_ffn, scale2, shift2, w_r, b_r)


def _moe_kernel(x1_ref, h2_ref, gates_ref, gate2_ref, *refs):
    wgu_parts, wd_parts = refs[:MOE_W_PARTS], refs[MOE_W_PARTS:2 * MOE_W_PARTS]
    o_ref, xe_ref, rank_ref, rank_t_ref, own_ref, grpb_ref = refs[2 * MOE_W_PARTS:]
    g = pl.program_id(2)
    tm, d = h2_ref.shape[1], h2_ref.shape[2]
    per_part = wd_parts[0].shape[0]
    n_e, d_exp = per_part * MOE_W_PARTS, wd_parts[0].shape[1]
    gf = g.astype(F32)

    @pl.when(g == 0)
    def _():
        tri = (lax.broadcasted_iota(jnp.int32, (RANK_BLK, RANK_BLK), 0)
               >= lax.broadcasted_iota(jnp.int32, (RANK_BLK, RANK_BLK), 1)).astype(BF16)
        lane_b = lax.broadcasted_iota(jnp.int32, (RANK_BLK, LANES), 1)
        run = jnp.zeros((1, LANES), F32)
        for sb in range(tm // RANK_BLK):
            rows = slice(sb * RANK_BLK, (sb + 1) * RANK_BLK)
            gts_b = gates_ref[0, rows, :]
            grp = jnp.sum(jnp.where(lane_b == N_EXPERTS, gts_b, 0.0), axis=-1, keepdims=True)
            member = jnp.where(jnp.logical_and(lane_b < N_GROUPS, lane_b.astype(F32) == grp), 1.0, 0.0)
            pre = jnp.dot(tri, member.astype(BF16), preferred_element_type=F32) + run
            rank_ref[rows, :] = jnp.where(lane_b == N_GROUPS, grp, pre)
            own = jnp.sum(member * pre, axis=-1, keepdims=True) - 1.0
            own_ref[rows, :] = jnp.broadcast_to(own, (RANK_BLK, LANES))
            grpb_ref[rows, :] = jnp.broadcast_to(grp, (RANK_BLK, LANES))
            run = jnp.max(pre, axis=0, keepdims=True)
        rank_t_ref[...] = rank_ref[...].T
        gts = gates_ref[0]
        g_hi = gts.astype(BF16)
        xe_ref[:, :d] = h2_ref[0]
        xe_ref[:, d:d + LANES] = g_hi
        xe_ref[:, d + LANES:d + 2 * LANES] = (gts - g_hi.astype(F32)).astype(BF16)
        o_ref[0] = jnp.zeros((tm, d), F32)

    rank_row = rank_t_ref[pl.ds(g, 1), :]
    pos_row = jnp.where(rank_t_ref[N_GROUPS:N_GROUPS + 1, :] == gf, rank_row - 1.0, -1.0)
    pos_col = jnp.where(grpb_ref[...] == gf, own_ref[...], -1.0)
    n_rows = jnp.max(rank_row).astype(jnp.int32)

    def expert_pass(first_row, n_ch):
        row_id = lax.broadcasted_iota(jnp.int32, (n_ch, tm), 0).astype(F32)
        col_id = lax.broadcasted_iota(jnp.int32, (tm, n_ch), 1).astype(F32)
        lane_c = lax.broadcasted_iota(jnp.int32, (n_ch, LANES), 1)
        r0 = first_row.astype(F32)
        gather = jnp.where(pos_row - r0 == row_id, 1.0, 0.0).astype(BF16)
        xg = jnp.dot(gather, xe_ref[...], preferred_element_type=F32)
        xb = xg[:, :d].astype(BF16)
        gates_c = xg[:, d:d + LANES] + xg[:, d + LANES:d + 2 * LANES]
        ya = None
        for e in range(n_e):
            w_gu_e = wgu_parts[e // per_part][e % per_part]
            w_d_e = wd_parts[e // per_part][e % per_part]
            gu = jnp.dot(xb, w_gu_e, preferred_element_type=F32)
            gt = gu[:, :d_exp]
            a = (gt * jax.nn.sigmoid(gt)) * gu[:, d_exp:]
            gate_e = jnp.sum(jnp.where(lane_c == g * n_e + e, gates_c, 0.0), axis=-1, keepdims=True)
            y = jnp.dot((a * gate_e).astype(BF16), w_d_e, preferred_element_type=F32)
            ya = y if ya is None else ya + y
        pos_wide = jnp.concatenate([pos_col] * (n_ch // LANES), axis=1)
        scatter = jnp.where(pos_wide - r0 == col_id, 1.0, 0.0).astype(BF16)
        o_ref[0] += jnp.dot(scatter, ya.astype(BF16), preferred_element_type=F32)

    n_full = (n_rows + MOE_CH - MOE_CH_TAIL - 1) // MOE_CH

    def full_pass(c, _):
        expert_pass(c * MOE_CH, MOE_CH)
        return 0

    lax.fori_loop(0, n_full, full_pass, 0)

    @pl.when(n_rows > n_full * MOE_CH)
    def _():
        expert_pass(n_full * MOE_CH, MOE_CH_TAIL)

    @pl.when(g == pl.num_programs(2) - 1)
    def _():
        o_ref[0] = x1_ref[0] + gate2_ref[0] * o_ref[0]


def _moe(x1, h2, gates, gate2, w_gu, w_d):
    b, s, d = x1.shape
    tm = TM_MOE
    n_e = EXPERTS_PER_GROUP
    assert w_gu.shape[0] == N_GROUPS * n_e and tm % RANK_BLK == 0 and n_e % MOE_W_PARTS == 0
    assert MOE_CH % LANES == 0 and MOE_CH_TAIL % LANES == 0 and MOE_CH_TAIL <= MOE_CH
    per_part = n_e // MOE_W_PARTS
    tok = lambda w: pl.BlockSpec((1, tm, w), lambda bi, si, g: (bi, si, 0))

    def slab(w, k):
        return pl.BlockSpec((per_part,) + w.shape[1:], lambda bi, si, g: (g * MOE_W_PARTS + k, 0, 0))

    return pl.pallas_call(
        _moe_kernel,
        out_shape=jax.ShapeDtypeStruct((b, s, d), F32),
        grid=(b, s // tm, N_GROUPS),
        in_specs=([tok(d), tok(d), tok(LANES), pl.BlockSpec((1, 1, d), lambda bi, si, g: (bi, 0, 0))]
                  + [slab(w_gu, k) for k in range(MOE_W_PARTS)]
                  + [slab(w_d, k) for k in range(MOE_W_PARTS)]),
        out_specs=tok(d),
        scratch_shapes=[pltpu.VMEM((tm, d + 2 * LANES), BF16),
                        pltpu.VMEM((tm, LANES), F32),
                        pltpu.VMEM((LANES, tm), F32),
                        pltpu.VMEM((tm, LANES), F32),
                        pltpu.VMEM((tm, LANES), F32)],
        compiler_params=pltpu.CompilerParams(
            dimension_semantics=("arbitrary", "arbitrary", "arbitrary"),
            vmem_limit_bytes=VMEM_LIMIT_MOE_BYTES),
        name="moe",
    )(x1, h2, gates, gate2, *([w_gu] * MOE_W_PARTS), *([w_d] * MOE_W_PARTS))


def _layer(x, mod, pos3, g_mix, g_ffn, w_in, g_q, g_k, g_kidx, w_pool, pool_scale, w_out,
           w_rg, b_rg, w_re, b_re, w_gate, w_up, w_down):
    b, s, d = x.shape
    d_attn = N_HEADS * HEAD_DIM
    nqb = s // Q_BLK
    nkb = s // K_BLK
    shift1, scale1, gate1, shift2, scale2, gate2 = [m[:, None, :] for m in jnp.split(mod, 6, axis=-1)]

    n_front = d_attn + 2 * HEAD_DIM + N_IDX_HEADS * IDX_DIM + IDX_DIM + N_IDX_HEADS
    pad = (-n_front) % LANES
    w_in_p = jnp.concatenate([w_in[:, :n_front], jnp.zeros((d, pad), w_in.dtype), w_in[:, n_front:]],
                             axis=1).astype(BF16)
    seg_id = jnp.arange(d_attn) // HEAD_DIM
    segsum = (seg_id[:, None] == seg_id[None, :]).astype(BF16)
    ones_half = jnp.ones((LANES - HEAD_DIM,), F32)
    gq_t = (jnp.tile(g_q, N_HEADS) * (LOG2_E * HEAD_DIM ** -0.5))[None, :]
    gk_e = jnp.concatenate([g_k, ones_half])[None, :]
    gkidx_e = jnp.concatenate([g_kidx, ones_half])[None, :]
    half = HEAD_DIM // 2
    inv_freq = ROPE_THETA ** (-jnp.arange(0, HEAD_DIM, 2, dtype=F32) / HEAD_DIM)
    invf = jnp.tile(inv_freq, LANES // half)[None, :]

    qt, kv, qit, ki, w_t, vt4, pool = _inproj(pos3, x, scale1, shift1, g_mix[None, :], w_in_p, segsum,
                                              gq_t, gk_e, gkidx_e, invf, w_pool.astype(BF16),
                                              pool_scale[None, :])
    kv4 = kv.reshape(b, s // CNT_BLK, CNT_BLK, LANES)
    ki4 = ki.reshape(b, s // CNT_BLK, CNT_BLK, LANES)
    attn = _dsa(qt, qit, w_t, kv4, ki4, vt4)

    w_out_b = w_out.astype(BF16)
    w_r = jnp.concatenate([w_re, w_rg, jnp.zeros((d, LANES - N_EXPERTS - N_GROUPS), F32)], axis=1)
    b_r = jnp.concatenate([b_re, b_rg, jnp.zeros((LANES - N_EXPERTS - N_GROUPS,), F32)])[None, :]
    x1, h2, gates = _outproj(x, attn, pool, w_out_b[:d_attn], w_out_b[d_attn:], gate1,
                             g_ffn[None, :], scale2, shift2, w_r, b_r)

    w_gu = jnp.concatenate([w_gate, w_up], axis=-1).astype(BF16)
    return _moe(x1, h2, gates, gate2, w_gu, w_down.astype(BF16))


def kernel(x, c, positions, w_ada, b_ada, g_norm_mix, g_norm_ffn, w_in, g_q, g_k, g_kidx, w_pool,
           pool_scale, w_out, w_router_group, b_router_group, w_router_expert, b_router_expert,
           w_gate, w_up, w_down):
    b, s, d = x.shape
    depth = w_ada.shape[0]
    assert s % TM_MOE == 0 and s % K_BLK == 0 and d % LANES == 0
    pos3 = positions[:, :, None]
    c_pad = jnp.concatenate([c, jnp.zeros((-b % SUBLANES, d), c.dtype)], axis=0)
    for l in range(depth):
        mod = _adaln(c_pad, w_ada[l], b_ada[l][None, :])[:b]
        x = _layer(x, mod, pos3, g_norm_mix[l], g_norm_ffn[l], w_in[l], g_q[l], g_k[l], g_kidx[l],
                   w_pool[l], pool_scale[l], w_out[l], w_router_group[l], b_router_group[l],
                   w_router_expert[l], b_router_expert[l], w_gate[l], w_up[l], w_down[l])
    return x
```

```python
import functools

import jax
import jax.numpy as jnp
from jax import lax
from jax.experimental import pallas as pl
from jax.experimental.pallas import tpu as pltpu

N_HEADS = 8
HEAD_DIM = 64
N_IDX_HEADS = 8
IDX_DIM = 64
TOPK_MAX = 256
ROPE_THETA = 10000.0
POOL_WINDOWS = (2, 4, 8, 16)
N_GROUPS = 4
EXPERTS_PER_GROUP = 8
N_EXPERTS = N_GROUPS * EXPERTS_PER_GROUP
EPS = 1e-6
N_MOD = 6
assert EXPERTS_PER_GROUP & (EXPERTS_PER_GROUP - 1) == 0

LANES = 128
SUBLANES = 8
assert EXPERTS_PER_GROUP == SUBLANES and N_GROUPS <= SUBLANES
VMEM_LIMIT_BYTES = 56 * 1024 * 1024
VMEM_LIMIT_MOE_BYTES = 60 * 1024 * 1024

Q_BLK = 512
K_BLK = 256
COL_BLK = 512
CNT_BLK = 512
CNT_ROWS = 32
SEARCH_FIRST = 15
SEARCH_PERIOD = 1
TM_PROJ = 512
TM_MOE = 1024
MOE_CH = 256
MOE_CH_TAIL = 128
RANK_BLK = 256
MOE_W_PARTS = 4
MAX_WIN = max(POOL_WINDOWS)
assert all(w == 2 ** (g + 1) for g, w in enumerate(POOL_WINDOWS))
M_INIT = -1e29
MASKED = -1e30
F32_LOWEST = -3.0e38
LOG2_E = 1.4426950408889634

BF16 = jnp.bfloat16
F32 = jnp.float32


def _cparams(sem):
    return pltpu.CompilerParams(dimension_semantics=sem, vmem_limit_bytes=VMEM_LIMIT_BYTES)


def _adaln_kernel(c_ref, w_ref, b_ref, o_ref):
    c = c_ref[...]
    c_act = c * jax.nn.sigmoid(c)
    o_ref[...] = jnp.dot(c_act, w_ref[...], preferred_element_type=F32) + b_ref[...]


def _adaln(c_pad, w_ada, b_ada):
    rows, d = c_pad.shape
    n = w_ada.shape[1]
    tn = n // N_MOD
    return pl.pallas_call(
        _adaln_kernel,
        out_shape=jax.ShapeDtypeStruct((rows, n), F32),
        grid=(n // tn,),
        in_specs=[pl.BlockSpec((rows, d), lambda j: (0, 0)),
                  pl.BlockSpec((d, tn), lambda j: (0, j)),
                  pl.BlockSpec((1, tn), lambda j: (0, j))],
        out_specs=pl.BlockSpec((rows, tn), lambda j: (0, j)),
        compiler_params=_cparams(("arbitrary",)),
        name="adaln",
    )(c_pad, w_ada, b_ada)


_PIO2_HI, _PIO2_MID, _PIO2_LO = 1.5703125, 4.837512969970703125e-4, 7.54978995489188e-8
_SIN_COEF = (-1.9515295891e-4, 8.3321608736e-3, -1.6666654611e-1)
_COS_COEF = (2.443315711809948e-5, -1.388731625493765e-3, 4.166664568298827e-2)


def _sincos(x):
    k = jnp.floor(x * (2.0 / jnp.pi) + 0.5)
    r = ((x - k * _PIO2_HI) - k * _PIO2_MID) - k * _PIO2_LO
    z = r * r
    s = r + r * z * (_SIN_COEF[2] + z * (_SIN_COEF[1] + z * _SIN_COEF[0]))
    c = 1.0 - 0.5 * z + z * z * (_COS_COEF[2] + z * (_COS_COEF[1] + z * _COS_COEF[0]))
    q = k - 4.0 * jnp.floor(k * 0.25)
    odd = jnp.logical_or(q == 1.0, q == 3.0)
    sin_b = jnp.where(odd, c, s)
    cos_b = jnp.where(odd, s, c)
    return (jnp.where(q >= 2.0, -sin_b, sin_b),
            jnp.where(jnp.logical_or(q == 1.0, q == 2.0), -cos_b, cos_b))


def _rope_chunk(y, cos, sin_signed, first_half):
    from_hi = pltpu.roll(y, LANES - HEAD_DIM // 2, 1)
    from_lo = pltpu.roll(y, HEAD_DIM // 2, 1)
    return y * cos + jnp.where(first_half, from_hi, from_lo) * sin_signed


def _inproj_kernel(pos_ref, x_ref, scale_ref, shift_ref, gmix_ref, win_ref, segsum_ref,
                   gq_ref, gk_ref, gkidx_ref, invf_ref, wpool_ref, pscale_ref,
                   qt_ref, kv_ref, qit_ref, ki_ref, wt_ref, vt_ref, pool_ref,
                   ubuf_ref, proj_a_ref, proj_b_ref):
    tm = x_ref.shape[1]
    sb = K_BLK
    step = pl.program_id(1)
    tile = jnp.maximum(step - 1, 0)

    @pl.when(step == 0)
    def _():
        proj_b_ref[...] = jnp.zeros(proj_b_ref.shape, F32)

    @pl.when(step <= 1)
    def _():
        for lvl in range(len(POOL_WINDOWS)):
            ubuf_ref[lvl, 0:MAX_WIN, lvl * LANES:] = jnp.zeros(
                (MAX_WIN, ubuf_ref.shape[2] - lvl * LANES), F32)

    @pl.when(step > 1)
    def _():
        for lvl in range(len(POOL_WINDOWS)):
            ubuf_ref[lvl, 0:MAX_WIN, lvl * LANES:] = ubuf_ref[lvl, tm:tm + MAX_WIN, lvl * LANES:]

    def run(write_ref, read_ref):
        for t in range(tm // sb):
            _inproj_post(t, read_ref[t], tile * tm + t * sb, pos_ref, segsum_ref, gq_ref, gk_ref,
                         gkidx_ref, invf_ref, wpool_ref, pscale_ref, qt_ref, kv_ref, qit_ref, ki_ref,
                         wt_ref, vt_ref, pool_ref, ubuf_ref)
        gain = gmix_ref[...] * (1.0 + scale_ref[0])
        for t in range(tm // sb):
            x = x_ref[0, t * sb:(t + 1) * sb, :]
            ms = jnp.mean(x * x, axis=-1, keepdims=True)
            h = (x * lax.rsqrt(ms + EPS) * gain + shift_ref[0]).astype(BF16)
            write_ref[t] = jnp.dot(h, win_ref[...], preferred_element_type=F32)

    @pl.when(step % 2 == 0)
    def _():
        run(proj_a_ref, proj_b_ref)

    @pl.when(step % 2 == 1)
    def _():
        run(proj_b_ref, proj_a_ref)


def _inproj_post(t, proj, t0, pos_ref, segsum_ref, gq_ref, gk_ref, gkidx_ref, invf_ref, wpool_ref,
                 pscale_ref, qt_ref, kv_ref, qit_ref, ki_ref, wt_ref, vt_ref, pool_ref, ubuf_ref):
    sb = K_BLK
    rows = slice(t * sb, (t + 1) * sb)
    tq, q0 = (t * sb) // Q_BLK, (t * sb) % Q_BLK
    d_attn = N_HEADS * HEAD_DIM
    d_qidx = N_IDX_HEADS * IDX_DIM

    def store_cols(dst_ref, chunk, j):
        ct = chunk.T
        for hh in range(2):
            col = (2 * j + hh) * Q_BLK + q0
            dst_ref[0, tq, :, col:col + sb] = ct[hh * HEAD_DIM:(hh + 1) * HEAD_DIM, :].astype(dst_ref.dtype)

    lane = lax.broadcasted_iota(jnp.int32, (sb, LANES), 1)
    first_half = (lane & (HEAD_DIM - 1)) < (HEAD_DIM // 2)
    ang = pos_ref[0, rows, :].astype(F32) * invf_ref[...]
    sin, cos = _sincos(ang)
    sin_signed = jnp.where(first_half, -sin, sin)
    rope = functools.partial(_rope_chunk, cos=cos, sin_signed=sin_signed, first_half=first_half)

    qf = proj[:, :d_attn]
    qsq = qf * qf
    qsq_hi = qsq.astype(BF16)
    qsq_lo = (qsq - qsq_hi.astype(F32)).astype(BF16)
    seg = segsum_ref[...]
    ssq = (jnp.dot(qsq_hi, seg, preferred_element_type=F32)
           + jnp.dot(qsq_lo, seg, preferred_element_type=F32))
    qn = qf * lax.rsqrt(ssq * (1.0 / HEAD_DIM) + EPS) * gq_ref[...]
    for j in range(d_attn // LANES):
        sl = slice(j * LANES, (j + 1) * LANES)
        store_cols(qt_ref, rope(qn[:, sl]), j)

    kvc = proj[:, d_attn:d_attn + LANES]
    is_k = lane < HEAD_DIM
    ksq = jnp.sum(jnp.where(is_k, kvc * kvc, 0.0), axis=-1, keepdims=True)
    kn = kvc * lax.rsqrt(ksq * (1.0 / HEAD_DIM) + EPS) * gk_ref[...]
    kv_ref[0, rows, :] = jnp.where(is_k, rope(kn), kvc).astype(BF16)
    row8 = lax.broadcasted_iota(jnp.int32, (SUBLANES, K_BLK), 0)
    vt_ref[0, t, 0:HEAD_DIM, :] = kvc.T[HEAD_DIM:, :].astype(BF16)
    vt_ref[0, t, HEAD_DIM:HEAD_DIM + SUBLANES, :] = jnp.where(row8 == 0, 1.0, 0.0).astype(BF16)

    o_qi = d_attn + LANES
    for j in range(d_qidx // LANES):
        store_cols(qit_ref, rope(proj[:, o_qi + j * LANES:o_qi + (j + 1) * LANES]), j)

    o_ki = o_qi + d_qidx
    kic = proj[:, o_ki:o_ki + LANES]
    kisq = jnp.sum(jnp.where(is_k, kic * kic, 0.0), axis=-1, keepdims=True)
    kin = kic * lax.rsqrt(kisq * (1.0 / IDX_DIM) + EPS) * gkidx_ref[...]
    ki_ref[0, rows, :] = jnp.where(is_k, rope(kin), 0.0).astype(BF16)
    wt_ref[0, tq, :, q0:q0 + sb] = kic.T[IDX_DIM:IDX_DIM + N_IDX_HEADS, :] * (
        N_IDX_HEADS ** -0.5 * IDX_DIM ** -0.5)

    o_u = o_ki + LANES
    u = proj[:, o_u:o_u + LANES * len(POOL_WINDOWS)]

    base = MAX_WIN + t * sb
    t_idx = t0 + lax.broadcasted_iota(jnp.int32, (sb, 1), 0)
    level = u
    for g, win in enumerate(POOL_WINDOWS):
        sl = slice(g * LANES, (g + 1) * LANES)
        shift = win // 2
        ubuf_ref[g, base:base + sb, g * LANES:] = level
        level = level + ubuf_ref[g, base - shift:base - shift + sb, g * LANES:]
        wsum = level[:, :LANES]
        if g + 1 < len(POOL_WINDOWS):
            level = level[:, LANES:]
        cnt = jnp.minimum(t_idx + 1, win).astype(F32)
        pooled = wsum / cnt - u[:, sl]
        mixed = jnp.dot(pooled.astype(BF16), wpool_ref[g], preferred_element_type=F32)
        pool_ref[0, rows, sl] = (mixed * pscale_ref[:, sl]).astype(BF16)


def _inproj(pos3, x, scale1, shift1, g_mix, w_in_p, segsum, gq_t, gk_e, gkidx_e, invf, w_pool, pscale):
    b, s, d = x.shape
    tm = TM_PROJ
    d_attn = N_HEADS * HEAD_DIM
    d_qidx = N_IDX_HEADS * IDX_DIM
    d_pool = LANES * len(POOL_WINDOWS)
    assert tm % Q_BLK == 0 and Q_BLK % K_BLK == 0 and HEAD_DIM == IDX_DIM and 2 * HEAD_DIM == LANES
    n_tiles = s // tm
    ahead = lambda si: jnp.minimum(si, n_tiles - 1)
    behind = lambda si: jnp.maximum(si - 1, 0)
    tok = lambda w: pl.BlockSpec((1, tm, w), lambda bi, si: (bi, behind(si), 0))
    blk = lambda n, r, c: pl.BlockSpec((1, tm // n, r, c), lambda bi, si: (bi, behind(si), 0, 0))
    per_b = pl.BlockSpec((1, 1, d), lambda bi, si: (bi, 0, 0))
    full = lambda a: pl.BlockSpec(a.shape, lambda bi, si: (0,) * a.ndim)
    nqb, nkb = s // Q_BLK, s // K_BLK
    proj_buf = pltpu.VMEM((tm // K_BLK, K_BLK, w_in_p.shape[1]), F32)
    return pl.pallas_call(
        _inproj_kernel,
        out_shape=(jax.ShapeDtypeStruct((b, nqb, HEAD_DIM, N_HEADS * Q_BLK), BF16),
                   jax.ShapeDtypeStruct((b, s, LANES), BF16),
                   jax.ShapeDtypeStruct((b, nqb, IDX_DIM, N_IDX_HEADS * Q_BLK), BF16),
                   jax.ShapeDtypeStruct((b, s, LANES), BF16),
                   jax.ShapeDtypeStruct((b, nqb, N_IDX_HEADS, Q_BLK), F32),
                   jax.ShapeDtypeStruct((b, nkb, HEAD_DIM + SUBLANES, K_BLK), BF16),
                   jax.ShapeDtypeStruct((b, s, d_pool), BF16)),
        grid=(b, n_tiles + 1),
        in_specs=[tok(1), pl.BlockSpec((1, tm, d), lambda bi, si: (bi, ahead(si), 0)), per_b, per_b,
                  full(g_mix), full(w_in_p), full(segsum),
                  full(gq_t), full(gk_e), full(gkidx_e), full(invf), full(w_pool), full(pscale)],
        out_specs=(blk(Q_BLK, HEAD_DIM, N_HEADS * Q_BLK), tok(LANES),
                   blk(Q_BLK, IDX_DIM, N_IDX_HEADS * Q_BLK), tok(LANES),
                   blk(Q_BLK, N_IDX_HEADS, Q_BLK), blk(K_BLK, HEAD_DIM + SUBLANES, K_BLK), tok(d_pool)),
        scratch_shapes=[pltpu.VMEM((len(POOL_WINDOWS), tm + MAX_WIN, d_pool), F32),
                        proj_buf, proj_buf],
        compiler_params=_cparams(("arbitrary", "arbitrary")),
        name="inproj",
    )(pos3, x, scale1, shift1, g_mix, w_in_p, segsum, gq_t, gk_e, gkidx_e, invf, w_pool, pscale)


def _dsa_kernel(qt_ref, qit_ref, w_ref, kv_ref, ki_ref, vt_ref, o_ref,
                sc_ref, qe_ref, qie_ref, m_ref, mx_ref, st_ref, acc_ref, lg_ref, p_ref):
    topk = float(min(TOPK_MAX, (sc_ref.shape[0] * CNT_BLK) // 4))
    qb = pl.program_id(1)
    n_cols = qt_ref.shape[3]
    n_chunks = n_cols // COL_BLK
    sub = CNT_BLK // K_BLK
    nch = ((qb + 1) * Q_BLK + CNT_BLK - 1) // CNT_BLK
    kgrp = K_BLK // SUBLANES
    sub_rows = [slice(j * K_BLK, (j + 1) * K_BLK) for j in range(sub)]

    zeros_half = jnp.zeros((LANES - HEAD_DIM, n_cols), BF16)
    qe_ref[0:HEAD_DIM, :] = qt_ref[0, 0]
    qe_ref[HEAD_DIM:LANES, :] = zeros_half
    qie_ref[0:IDX_DIM, :] = qit_ref[0, 0]
    qie_ref[IDX_DIM:LANES, :] = zeros_half

    q_pos = qb * Q_BLK + lax.broadcasted_iota(jnp.int32, (K_BLK, Q_BLK), 1)
    key_off = lax.broadcasted_iota(jnp.int32, (K_BLK, Q_BLK), 0)

    def score_step(ch, carry):
        rmax, rmin = carry
        for j in range(sub):
            ki_blk = ki_ref[0, ch, sub_rows[j], :]
            score = None
            for cc in range(n_chunks):
                cs = slice(cc * COL_BLK, (cc + 1) * COL_BLK)
                s_h = jnp.dot(ki_blk, qie_ref[:, cs], preferred_element_type=F32)
                s_h = jnp.maximum(s_h, 0.0)
                for hh in range(COL_BLK // Q_BLK):
                    head = cc * (COL_BLK // Q_BLK) + hh
                    part = s_h[:, hh * Q_BLK:(hh + 1) * Q_BLK] * w_ref[0, 0, head:head + 1, :]
                    score = part if score is None else score + part
            causal = (ch * CNT_BLK + j * K_BLK + key_off) <= q_pos
            masked = jnp.where(causal, score, -jnp.inf)
            sc_ref[ch, sub_rows[j], :] = masked
            hi_part = masked.reshape(kgrp, SUBLANES, Q_BLK).max(axis=0)
            lo_part = jnp.where(causal, score, jnp.inf).reshape(kgrp, SUBLANES, Q_BLK).min(axis=0)
            rmax, rmin = jnp.maximum(rmax, hi_part), jnp.minimum(rmin, lo_part)
        return rmax, rmin

    def score_body(i, carry):
        return score_step(2 * i + 1, score_step(2 * i, carry))

    stats = lax.fori_loop(
        0, nch // 2, score_body,
        (jnp.full((SUBLANES, Q_BLK), -jnp.inf, F32), jnp.full((SUBLANES, Q_BLK), jnp.inf, F32)))
    rmax8, rmin8 = lax.cond(nch % 2 == 1, lambda c: score_step(nch - 1, c), lambda c: c, stats)
    rowmax = jnp.max(rmax8, axis=0, keepdims=True)
    rowmin = jnp.min(rmin8, axis=0, keepdims=True)

    n_causal = (qb * Q_BLK + 1 + lax.broadcasted_iota(jnp.int32, (1, Q_BLK), 1)).astype(F32)
    kt = jnp.minimum(n_causal, topk)

    cgrp = CNT_BLK // CNT_ROWS

    def count_ge(t):
        def body(ch, acc):
            for r in range(cgrp):
                rows = sc_ref[ch, r * CNT_ROWS:(r + 1) * CNT_ROWS, :]
                acc = acc + jnp.where(rows >= t, 1.0, 0.0)
            return acc
        acc = lax.fori_loop(0, nch, body, jnp.zeros((CNT_ROWS, Q_BLK), F32))
        return jnp.sum(acc, axis=0, keepdims=True)

    def bisect_pass(state):
        lo, hi, top, c_lo, c_hi, thr, done = state
        cap = jnp.minimum(hi, top)
        mid = lo + 0.5 * (cap - lo)
        mid = jnp.where(mid <= lo, cap, mid)
        c = count_ge(mid)
        hit = jnp.logical_and(done == 0.0, c == kt)
        thr = jnp.where(hit, mid, thr)
        done = jnp.where(hit, 1.0, done)
        active = done == 0.0
        up = jnp.logical_and(active, c >= kt)
        down = jnp.logical_and(active, c < kt)
        return (jnp.where(up, mid, lo), jnp.where(down, mid, hi), jnp.where(down, jnp.inf, top),
                jnp.where(up, c, c_lo), jnp.where(down, c, c_hi), thr, done)

    def snap_pass(state):
        lo, hi, top, c_lo, c_hi, thr, done = state

        def body(ch, carry):
            a8, b8 = carry
            for r in range(cgrp):
                s = sc_ref[ch, r * CNT_ROWS:(r + 1) * CNT_ROWS, :]
                a8 = jnp.minimum(a8, jnp.where(s >= lo, s, jnp.inf))
                b8 = jnp.maximum(b8, jnp.where(s < hi, s, -jnp.inf))
            return a8, b8

        a8, b8 = lax.fori_loop(
            0, nch, body,
            (jnp.full((CNT_ROWS, Q_BLK), jnp.inf, F32), jnp.full((CNT_ROWS, Q_BLK), -jnp.inf, F32)))
        a = jnp.min(a8, axis=0, keepdims=True)
        b = jnp.max(b8, axis=0, keepdims=True)
        active = done == 0.0
        hit = jnp.logical_and(active, jnp.logical_or(a == b, kt - c_hi == 1.0))
        thr = jnp.where(hit, b, thr)
        done = jnp.where(hit, 2.0, done)
        c_lo = jnp.where(jnp.logical_and(hit, a != b), kt + 1.0, c_lo)
        return jnp.where(active, a, lo), hi, jnp.where(active, b, top), c_lo, c_hi, thr, done

    few = n_causal <= topk
    state0 = (rowmin, jnp.full((1, Q_BLK), jnp.inf, F32), rowmax, n_causal,
              jnp.zeros((1, Q_BLK), F32), jnp.where(few, F32_LOWEST, 0.0), jnp.where(few, 1.0, 0.0))

    def outer_cond(carry):
        return carry[1] > 0.0

    def outer_body(carry):
        state, _ = carry
        state = lax.fori_loop(0, SEARCH_PERIOD, lambda i, st: bisect_pass(st), state)
        state = snap_pass(state)
        pending = jnp.max(jnp.where(state[6] == 0.0, 1.0, 0.0))
        return state, pending

    state1 = lax.fori_loop(0, SEARCH_FIRST, lambda i, st: bisect_pass(st), state0)
    state1 = snap_pass(state1)
    pending1 = jnp.max(jnp.where(state1[6] == 0.0, 1.0, 0.0))
    (lo, hi, _, c_lo, c_hi, thr, done), _ = lax.while_loop(outer_cond, outer_body, (state1, pending1))

    excess = jnp.where(done == 2.0, c_lo - kt, 0.0)
    need = kt - c_hi

    @pl.when(jnp.max(excess) > 0.0)
    def _():
        tri = (lax.broadcasted_iota(jnp.int32, (K_BLK, K_BLK), 0)
               >= lax.broadcasted_iota(jnp.int32, (K_BLK, K_BLK), 1)).astype(BF16)
        has_excess = excess > 0.0

        def drop_step(ch, run):
            for j in range(sub):
                s = sc_ref[ch, sub_rows[j], :]
                tied = jnp.logical_and(s == thr, has_excess)
                prefix = jnp.dot(tri, jnp.where(tied, 1.0, 0.0).astype(BF16), preferred_element_type=F32)
                drop = jnp.logical_and(tied, run + prefix > need)
                sc_ref[ch, sub_rows[j], :] = jnp.where(drop, -jnp.inf, s)
                run = run + jnp.max(prefix, axis=0, keepdims=True)
            return run

        run = lax.fori_loop(0, nch // 2, lambda i, r: drop_step(2 * i + 1, drop_step(2 * i, r)),
                            jnp.zeros((1, Q_BLK), F32))

        @pl.when(nch % 2 == 1)
        def _():
            drop_step(nch - 1, run)

    m_ref[...] = jnp.full(m_ref.shape, M_INIT, F32)
    acc_ref[...] = jnp.zeros(acc_ref.shape, F32)
    row_m = lambda j: slice(j, j + 1)
    row_a = lambda j: slice(sub + j, sub + j + 1)

    def logits_stage(ch, j):
        kv_blk = kv_ref[0, ch, sub_rows[j], :]
        bias = jnp.where(sc_ref[ch, sub_rows[j], :] >= thr, 0.0, MASKED)
        for cc in range(n_chunks):
            logits = jnp.dot(kv_blk, qe_ref[:, cc * COL_BLK:(cc + 1) * COL_BLK],
                             preferred_element_type=F32)
            for hh in range(COL_BLK // Q_BLK):
                cs = slice(cc * COL_BLK + hh * Q_BLK, cc * COL_BLK + (hh + 1) * Q_BLK)
                lg = logits[:, hh * Q_BLK:(hh + 1) * Q_BLK] + bias
                lg_ref[j, :, cs] = lg
                mx_ref[:, cs] = lg.reshape(kgrp, SUBLANES, Q_BLK).max(axis=0)
        m_old = m_ref[...]
        m_new = jnp.maximum(m_old, jnp.max(mx_ref[...], axis=0, keepdims=True))
        st_ref[row_m(j), :] = m_new
        st_ref[row_a(j), :] = jnp.exp2(m_old - m_new)
        m_ref[...] = m_new

    def probs_stage(j):
        p_ref[j] = jnp.exp2(lg_ref[j] - st_ref[row_m(j), :]).astype(BF16)

    def value_stage(kb, j, alpha):
        acc_ref[...] = acc_ref[...] * alpha + jnp.dot(
            vt_ref[0, kb], p_ref[j], preferred_element_type=F32)

    p_ref[sub - 1] = jnp.zeros(p_ref.shape[1:], BF16)
    st_ref[row_a(sub - 1), :] = jnp.ones((1, n_cols), F32)
    logits_stage(0, 0)

    def attn_body(ch, _):
        alpha_prev = st_ref[row_a(1), :]
        logits_stage(ch, 1)
        value_stage(jnp.maximum(ch * sub - 1, 0), 1, alpha_prev)
        probs_stage(0)
        alpha_cur = st_ref[row_a(0), :]
        logits_stage(jnp.minimum(ch + 1, nch - 1), 0)
        value_stage(ch * sub, 0, alpha_cur)
        probs_stage(1)
        return 0

    lax.fori_loop(0, nch, attn_body, 0)
    value_stage(nch * sub - 1, 1, st_ref[row_a(1), :])
    dh = qt_ref.shape[2]
    inv_l = 1.0 / acc_ref[dh:dh + 1, :]
    for j in range(n_cols // Q_BLK // 2):
        pair = [acc_ref[0:dh, (2 * j + hh) * Q_BLK:(2 * j + hh + 1) * Q_BLK]
                * inv_l[:, (2 * j + hh) * Q_BLK:(2 * j + hh + 1) * Q_BLK] for hh in range(2)]
        o_ref[0, :, j * 2 * dh:(j + 1) * 2 * dh] = jnp.concatenate(pair, axis=0).T.astype(o_ref.dtype)


def _dsa(qt, qit, w_t, kv4, ki4, vt4):
    b, nqb, dh, n_cols = qt.shape
    n_steps = kv4.shape[1]
    assert kv4.shape[2] == CNT_BLK and CNT_BLK == 2 * K_BLK and n_cols % COL_BLK == 0
    assert vt4.shape[1] * K_BLK == n_steps * CNT_BLK and vt4.shape[2] == dh + SUBLANES
    per_q = lambda a: pl.BlockSpec((1, 1) + a.shape[2:], lambda bi, qi: (bi, qi, 0, 0))
    per_b = lambda a: pl.BlockSpec((1,) + a.shape[1:], lambda bi, qi: (bi, 0, 0, 0))
    return pl.pallas_call(
        _dsa_kernel,
        out_shape=jax.ShapeDtypeStruct((b, nqb * Q_BLK, (n_cols // Q_BLK) * dh), BF16),
        grid=(b, nqb),
        in_specs=[per_q(qt), per_q(qit), per_q(w_t), per_b(kv4), per_b(ki4), per_b(vt4)],
        out_specs=pl.BlockSpec((1, Q_BLK, (n_cols // Q_BLK) * dh), lambda bi, qi: (bi, qi, 0)),
        scratch_shapes=[pltpu.VMEM((n_steps, CNT_BLK, Q_BLK), F32),
                        pltpu.VMEM((LANES, n_cols), BF16),
                        pltpu.VMEM((LANES, n_cols), BF16),
                        pltpu.VMEM((1, n_cols), F32),
                        pltpu.VMEM((SUBLANES, n_cols), F32),
                        pltpu.VMEM((SUBLANES, n_cols), F32),
                        pltpu.VMEM((dh + SUBLANES, n_cols), F32),
                        pltpu.VMEM((CNT_BLK // K_BLK, K_BLK, n_cols), F32),
                        pltpu.VMEM((CNT_BLK // K_BLK, K_BLK, n_cols), BF16)],
        compiler_params=_cparams(("arbitrary", "arbitrary")),
        name="dsa",
    )(qt, qit, w_t, kv4, ki4, vt4)


def _outproj_kernel(x_ref, attn_ref, pool_ref, woa_ref, wop_ref, gate1_ref, gffn_ref,
                    scale2_ref, shift2_ref, wr_ref, br_ref, x1_ref, h2_ref, gates_ref):
    tm = x_ref.shape[1]
    mix = (jnp.dot(attn_ref[0], woa_ref[...], preferred_element_type=F32)
           + jnp.dot(pool_ref[0], wop_ref[...], preferred_element_type=F32))
    x1 = x_ref[0] + gate1_ref[0] * mix
    x1_ref[0] = x1
    ms = jnp.mean(x1 * x1, axis=-1, keepdims=True)
    h2 = (x1 * lax.rsqrt(ms + EPS) * gffn_ref[...]) * (1.0 + scale2_ref[0]) + shift2_ref[0]
    h2_hi = h2.astype(BF16)
    h2_ref[0] = h2_hi

    h2_lo = (h2 - h2_hi.astype(F32)).astype(BF16)
    wr = wr_ref[...]
    wr_hi = wr.astype(BF16)
    wr_lo = (wr - wr_hi.astype(F32)).astype(BF16)
    logits = (jnp.dot(h2_hi, wr_hi, preferred_element_type=F32)
              + jnp.dot(h2_lo, wr_hi, preferred_element_type=F32)
              + jnp.dot(h2_hi, wr_lo, preferred_element_type=F32)) + br_ref[...]

    lt = logits.T
    n_e = EXPERTS_PER_GROUP
    sub_id = lax.broadcasted_iota(jnp.int32, (SUBLANES, tm), 0)
    big = jnp.int32(LANES)
    glog = jnp.where(sub_id < N_GROUPS, lt[N_EXPERTS:N_EXPERTS + SUBLANES, :], -jnp.inf)
    gmax = jnp.max(glog, axis=0, keepdims=True)
    gsum = jnp.sum(jnp.exp(glog - gmax), axis=0, keepdims=True)
    p_g = 1.0 / gsum
    g_sel = jnp.min(jnp.where(glog == gmax, sub_id, big), axis=0, keepdims=True)
    elog = lt[0:n_e, :]
    for gi in range(1, N_GROUPS):
        elog = jnp.where(g_sel == gi, lt[gi * n_e:(gi + 1) * n_e, :], elog)
    emax = jnp.max(elog, axis=0, keepdims=True)
    eexp = jnp.exp(elog - emax)
    p_e = eexp / jnp.sum(eexp, axis=0, keepdims=True)
    p1 = jnp.max(p_e, axis=0, keepdims=True)
    i1 = jnp.min(jnp.where(p_e == p1, sub_id, big), axis=0, keepdims=True)
    p_e2 = jnp.where(sub_id == i1, -1.0, p_e)
    p2 = jnp.max(p_e2, axis=0, keepdims=True)
    i2 = jnp.min(jnp.where(p_e2 == p2, sub_id, big), axis=0, keepdims=True)
    tot = p1 + p2
    in_grp = (jnp.where(sub_id == i1, p_g * (p1 / tot), 0.0)
              + jnp.where(sub_id == i2, p_g * (p2 / tot), 0.0))
    rows = [jnp.where(g_sel == gi, in_grp, 0.0) for gi in range(N_GROUPS)]
    rows.append(jnp.where(sub_id == 0, g_sel.astype(F32), 0.0))
    rows.append(jnp.zeros((LANES - N_EXPERTS - SUBLANES, tm), F32))
    gates_ref[0] = jnp.concatenate(rows, axis=0).T


def _outproj(x, attn, pool, wo_a, wo_p, gate1, g_ffn, scale2, shift2, w_r, b_r):
    b, s, d = x.shape
    tm = TM_PROJ
    tok = lambda w: pl.BlockSpec((1, tm, w), lambda bi, si: (bi, si, 0))
    per_b = pl.BlockSpec((1, 1, d), lambda bi, si: (bi, 0, 0))
    full = lambda a: pl.BlockSpec(a.shape, lambda bi, si: (0,) * a.ndim)
    return pl.pallas_call(
        _outproj_kernel,
        out_shape=(jax.ShapeDtypeStruct((b, s, d), F32),
                   jax.ShapeDtypeStruct((b, s, d), BF16),
                   jax.ShapeDtypeStruct((b, s, LANES), F32)),
        grid=(b, s // tm),
        in_specs=[tok(d), tok(attn.shape[2]), tok(pool.shape[2]), full(wo_a), full(wo_p), per_b,
                  full(g_ffn), per_b, per_b, full(w_r), full(b_r)],
        out_specs=(tok(d), tok(d), tok(LANES)),
        compiler_params=_cparams(("arbitrary", "arbitrary")),
        name="outproj",
    )(x, attn, pool, wo_a, wo_p, gate1, g_ffn, scale2, shift2, w_r, b_r)


def _moe_kernel(x1_ref, h2_ref, gates_ref, gate2_ref, *refs):
    wgu_parts, wd_parts = refs[:MOE_W_PARTS], refs[MOE_W_PARTS:2 * MOE_W_PARTS]
    o_ref, xe_ref, rank_ref, rank_t_ref, own_ref, grpb_ref = refs[2 * MOE_W_PARTS:]
    g = pl.program_id(2)
    tm, d = h2_ref.shape[1], h2_ref.shape[2]
    per_part = wd_parts[0].shape[0]
    n_e, d_exp = per_part * MOE_W_PARTS, wd_parts[0].shape[1]
    gf = g.astype(F32)

    @pl.when(g == 0)
    def _():
        tri = (lax.broadcasted_iota(jnp.int32, (RANK_BLK, RANK_BLK), 0)
               >= lax.broadcasted_iota(jnp.int32, (RANK_BLK, RANK_BLK), 1)).astype(BF16)
        lane_b = lax.broadcasted_iota(jnp.int32, (RANK_BLK, LANES), 1)
        run = jnp.zeros((1, LANES), F32)
        for sb in range(tm // RANK_BLK):
            rows = slice(sb * RANK_BLK, (sb + 1) * RANK_BLK)
            gts_b = gates_ref[0, rows, :]
            grp = jnp.sum(jnp.where(lane_b == N_EXPERTS, gts_b, 0.0), axis=-1, keepdims=True)
            member = jnp.where(jnp.logical_and(lane_b < N_GROUPS, lane_b.astype(F32) == grp), 1.0, 0.0)
            pre = jnp.dot(tri, member.astype(BF16), preferred_element_type=F32) + run
            rank_ref[rows, :] = jnp.where(lane_b == N_GROUPS, grp, pre)
            own = jnp.sum(member * pre, axis=-1, keepdims=True) - 1.0
            own_ref[rows, :] = jnp.broadcast_to(own, (RANK_BLK, LANES))
            grpb_ref[rows, :] = jnp.broadcast_to(grp, (RANK_BLK, LANES))
            run = jnp.max(pre, axis=0, keepdims=True)
        rank_t_ref[...] = rank_ref[...].T
        gts = gates_ref[0]
        g_hi = gts.astype(BF16)
        xe_ref[:, :d] = h2_ref[0]
        xe_ref[:, d:d + LANES] = g_hi
        xe_ref[:, d + LANES:d + 2 * LANES] = (gts - g_hi.astype(F32)).astype(BF16)
        o_ref[0] = jnp.zeros((tm, d), F32)

    rank_row = rank_t_ref[pl.ds(g, 1), :]
    pos_row = jnp.where(rank_t_ref[N_GROUPS:N_GROUPS + 1, :] == gf, rank_row - 1.0, -1.0)
    pos_col = jnp.where(grpb_ref[...] == gf, own_ref[...], -1.0)
    n_rows = jnp.max(rank_row).astype(jnp.int32)

    def expert_pass(first_row, n_ch):
        row_id = lax.broadcasted_iota(jnp.int32, (n_ch, tm), 0).astype(F32)
        col_id = lax.broadcasted_iota(jnp.int32, (tm, n_ch), 1).astype(F32)
        lane_c = lax.broadcasted_iota(jnp.int32, (n_ch, LANES), 1)
        r0 = first_row.astype(F32)
        gather = jnp.where(pos_row - r0 == row_id, 1.0, 0.0).astype(BF16)
        xg = jnp.dot(gather, xe_ref[...], preferred_element_type=F32)
        xb = xg[:, :d].astype(BF16)
        gates_c = xg[:, d:d + LANES] + xg[:, d + LANES:d + 2 * LANES]
        ya = None
        for e in range(n_e):
            w_gu_e = wgu_parts[e // per_part][e % per_part]
            w_d_e = wd_parts[e // per_part][e % per_part]
            gu = jnp.dot(xb, w_gu_e, preferred_element_type=F32)
            gt = gu[:, :d_exp]
            a = (gt * jax.nn.sigmoid(gt)) * gu[:, d_exp:]
            gate_e = jnp.sum(jnp.where(lane_c == g * n_e + e, gates_c, 0.0), axis=-1, keepdims=True)
            y = jnp.dot((a * gate_e).astype(BF16), w_d_e, preferred_element_type=F32)
            ya = y if ya is None else ya + y
        pos_wide = jnp.concatenate([pos_col] * (n_ch // LANES), axis=1)
        scatter = jnp.where(pos_wide - r0 == col_id, 1.0, 0.0).astype(BF16)
        o_ref[0] += jnp.dot(scatter, ya.astype(BF16), preferred_element_type=F32)

    n_full = (n_rows + MOE_CH - MOE_CH_TAIL - 1) // MOE_CH

    def full_pass(c, _):
        expert_pass(c * MOE_CH, MOE_CH)
        return 0

    lax.fori_loop(0, n_full, full_pass, 0)

    @pl.when(n_rows > n_full * MOE_CH)
    def _():
        expert_pass(n_full * MOE_CH, MOE_CH_TAIL)

    @pl.when(g == pl.num_programs(2) - 1)
    def _():
        o_ref[0] = x1_ref[0] + gate2_ref[0] * o_ref[0]


def _moe(x1, h2, gates, gate2, w_gu, w_d):
    b, s, d = x1.shape
    tm = TM_MOE
    n_e = EXPERTS_PER_GROUP
    assert w_gu.shape[0] == N_GROUPS * n_e and tm % RANK_BLK == 0 and n_e % MOE_W_PARTS == 0
    assert MOE_CH % LANES == 0 and MOE_CH_TAIL % LANES == 0 and MOE_CH_TAIL <= MOE_CH
    per_part = n_e // MOE_W_PARTS
    tok = lambda w: pl.BlockSpec((1, tm, w), lambda bi, si, g: (bi, si, 0))

    def slab(w, k):
        return pl.BlockSpec((per_part,) + w.shape[1:], lambda bi, si, g: (g * MOE_W_PARTS + k, 0, 0))

    return pl.pallas_call(
        _moe_kernel,
        out_shape=jax.ShapeDtypeStruct((b, s, d), F32),
        grid=(b, s // tm, N_GROUPS),
        in_specs=([tok(d), tok(d), tok(LANES), pl.BlockSpec((1, 1, d), lambda bi, si, g: (bi, 0, 0))]
                  + [slab(w_gu, k) for k in range(MOE_W_PARTS)]
                  + [slab(w_d, k) for k in range(MOE_W_PARTS)]),
        out_specs=tok(d),
        scratch_shapes=[pltpu.VMEM((tm, d + 2 * LANES), BF16),
                        pltpu.VMEM((tm, LANES), F32),
                        pltpu.VMEM((LANES, tm), F32),
                        pltpu.VMEM((tm, LANES), F32),
                        pltpu.VMEM((tm, LANES), F32)],
        compiler_params=pltpu.CompilerParams(
            dimension_semantics=("arbitrary", "arbitrary", "arbitrary"),
            vmem_limit_bytes=VMEM_LIMIT_MOE_BYTES),
        name="moe",
    )(x1, h2, gates, gate2, *([w_gu] * MOE_W_PARTS), *([w_d] * MOE_W_PARTS))


def _layer(x, mod, pos3, g_mix, g_ffn, w_in, g_q, g_k, g_kidx, w_pool, pool_scale, w_out,
           w_rg, b_rg, w_re, b_re, w_gate, w_up, w_down):
    b, s, d = x.shape
    d_attn = N_HEADS * HEAD_DIM
    nqb = s // Q_BLK
    nkb = s // K_BLK
    shift1, scale1, gate1, shift2, scale2, gate2 = [m[:, None, :] for m in jnp.split(mod, 6, axis=-1)]

    n_front = d_attn + 2 * HEAD_DIM + N_IDX_HEADS * IDX_DIM + IDX_DIM + N_IDX_HEADS
    pad = (-n_front) % LANES
    w_in_p = jnp.concatenate([w_in[:, :n_front], jnp.zeros((d, pad), w_in.dtype), w_in[:, n_front:]],
                             axis=1).astype(BF16)
    seg_id = jnp.arange(d_attn) // HEAD_DIM
    segsum = (seg_id[:, None] == seg_id[None, :]).astype(BF16)
    ones_half = jnp.ones((LANES - HEAD_DIM,), F32)
    gq_t = (jnp.tile(g_q, N_HEADS) * (LOG2_E * HEAD_DIM ** -0.5))[None, :]
    gk_e = jnp.concatenate([g_k, ones_half])[None, :]
    gkidx_e = jnp.concatenate([g_kidx, ones_half])[None, :]
    half = HEAD_DIM // 2
    inv_freq = ROPE_THETA ** (-jnp.arange(0, HEAD_DIM, 2, dtype=F32) / HEAD_DIM)
    invf = jnp.tile(inv_freq, LANES // half)[None, :]

    qt, kv, qit, ki, w_t, vt4, pool = _inproj(pos3, x, scale1, shift1, g_mix[None, :], w_in_p, segsum,
                                              gq_t, gk_e, gkidx_e, invf, w_pool.astype(BF16),
                                              pool_scale[None, :])
    kv4 = kv.reshape(b, s // CNT_BLK, CNT_BLK, LANES)
    ki4 = ki.reshape(b, s // CNT_BLK, CNT_BLK, LANES)
    attn = _dsa(qt, qit, w_t, kv4, ki4, vt4)

    w_out_b = w_out.astype(BF16)
    w_r = jnp.concatenate([w_re, w_rg, jnp.zeros((d, LANES - N_EXPERTS - N_GROUPS), F32)], axis=1)
    b_r = jnp.concatenate([b_re, b_rg, jnp.zeros((LANES - N_EXPERTS - N_GROUPS,), F32)])[None, :]
    x1, h2, gates = _outproj(x, attn, pool, w_out_b[:d_attn], w_out_b[d_attn:], gate1,
                             g_ffn[None, :], scale2, shift2, w_r, b_r)

    w_gu = jnp.concatenate([w_gate, w_up], axis=-1).astype(BF16)
    return _moe(x1, h2, gates, gate2, w_gu, w_down.astype(BF16))


def kernel(x, c, positions, w_ada, b_ada, g_norm_mix, g_norm_ffn, w_in, g_q, g_k, g_kidx, w_pool,
           pool_scale, w_out, w_router_group, b_router_group, w_router_expert, b_router_expert,
           w_gate, w_up, w_down):
    b, s, d = x.shape
    depth = w_ada.shape[0]
    assert s % TM_MOE == 0 and s % K_BLK == 0 and d % LANES == 0
    pos3 = positions[:, :, None]
    c_pad = jnp.concatenate([c, jnp.zeros((-b % SUBLANES, d), c.dtype)], axis=0)
    for l in range(depth):
        mod = _adaln(c_pad, w_ada[l], b_ada[l][None, :])[:b]
        x = _layer(x, mod, pos3, g_norm_mix[l], g_norm_ffn[l], w_in[l], g_q[l], g_k[l], g_kidx[l],
                   w_pool[l], pool_scale[l], w_out[l], w_router_group[l], b_router_group[l],
                   w_router_expert[l], b_router_expert[l], w_gate[l], w_up[l], w_down[l])
    return x
```

```python
import functools

import jax
import jax.numpy as jnp
from jax import lax
from jax.experimental import pallas as pl
from jax.experimental.pallas import tpu as pltpu

N_HEADS = 8
HEAD_DIM = 64
N_IDX_HEADS = 8
IDX_DIM = 64
TOPK_MAX = 256
ROPE_THETA = 10000.0
POOL_WINDOWS = (2, 4, 8, 16)
N_GROUPS = 4
EXPERTS_PER_GROUP = 8
N_EXPERTS = N_GROUPS * EXPERTS_PER_GROUP
EPS = 1e-6
N_MOD = 6
assert EXPERTS_PER_GROUP & (EXPERTS_PER_GROUP - 1) == 0

LANES = 128
SUBLANES = 8
assert EXPERTS_PER_GROUP == SUBLANES and N_GROUPS <= SUBLANES
VMEM_LIMIT_BYTES = 56 * 1024 * 1024
VMEM_LIMIT_MOE_BYTES = 60 * 1024 * 1024

Q_BLK = 512
K_BLK = 256
COL_BLK = 512
CNT_BLK = 512
CNT_ROWS = 32
SEARCH_FIRST = 15
SEARCH_PERIOD = 1
TM_PROJ = 512
TM_OUT = 1024
TM_MOE = 1024
MOE_CH = 256
MOE_CH_TAIL = 128
RANK_BLK = 256
MOE_W_PARTS = 4
MAX_WIN = max(POOL_WINDOWS)
assert all(w == 2 ** (g + 1) for g, w in enumerate(POOL_WINDOWS))
M_INIT = -1e29
MASKED = -1e30
F32_LOWEST = -3.0e38
LOG2_E = 1.4426950408889634

BF16 = jnp.bfloat16
F32 = jnp.float32


def _cparams(sem):
    return pltpu.CompilerParams(dimension_semantics=sem, vmem_limit_bytes=VMEM_LIMIT_BYTES)


def _adaln_kernel(c_ref, w_ref, b_ref, o_ref):
    c = c_ref[...]
    c_act = c * jax.nn.sigmoid(c)
    o_ref[...] = jnp.dot(c_act, w_ref[...], preferred_element_type=F32) + b_ref[...]


def _adaln(c_pad, w_ada, b_ada):
    rows, d = c_pad.shape
    n = w_ada.shape[1]
    tn = n // N_MOD
    return pl.pallas_call(
        _adaln_kernel,
        out_shape=jax.ShapeDtypeStruct((rows, n), F32),
        grid=(n // tn,),
        in_specs=[pl.BlockSpec((rows, d), lambda j: (0, 0)),
                  pl.BlockSpec((d, tn), lambda j: (0, j)),
                  pl.BlockSpec((1, tn), lambda j: (0, j))],
        out_specs=pl.BlockSpec((rows, tn), lambda j: (0, j)),
        compiler_params=_cparams(("arbitrary",)),
        name="adaln",
    )(c_pad, w_ada, b_ada)


_PIO2_HI, _PIO2_MID, _PIO2_LO = 1.5703125, 4.837512969970703125e-4, 7.54978995489188e-8
_SIN_COEF = (-1.9515295891e-4, 8.3321608736e-3, -1.6666654611e-1)
_COS_COEF = (2.443315711809948e-5, -1.388731625493765e-3, 4.166664568298827e-2)


def _sincos(x):
    k = jnp.floor(x * (2.0 / jnp.pi) + 0.5)
    r = ((x - k * _PIO2_HI) - k * _PIO2_MID) - k * _PIO2_LO
    z = r * r
    s = r + r * z * (_SIN_COEF[2] + z * (_SIN_COEF[1] + z * _SIN_COEF[0]))
    c = 1.0 - 0.5 * z + z * z * (_COS_COEF[2] + z * (_COS_COEF[1] + z * _COS_COEF[0]))
    q = k - 4.0 * jnp.floor(k * 0.25)
    odd = jnp.logical_or(q == 1.0, q == 3.0)
    sin_b = jnp.where(odd, c, s)
    cos_b = jnp.where(odd, s, c)
    return (jnp.where(q >= 2.0, -sin_b, sin_b),
            jnp.where(jnp.logical_or(q == 1.0, q == 2.0), -cos_b, cos_b))


def _rope_chunk(y, cos, sin_signed, first_half):
    from_hi = pltpu.roll(y, LANES - HEAD_DIM // 2, 1)
    from_lo = pltpu.roll(y, HEAD_DIM // 2, 1)
    return y * cos + jnp.where(first_half, from_hi, from_lo) * sin_signed


def _inproj_kernel(pos_ref, x_ref, scale_ref, shift_ref, gmix_ref, win_ref, segsum_ref,
                   gq_ref, gk_ref, gkidx_ref, invf_ref, wpool_ref, pscale_ref,
                   qt_ref, kv_ref, qit_ref, ki_ref, wt_ref, vt_ref, pool_ref,
                   ubuf_ref, proj_a_ref, proj_b_ref):
    tm = x_ref.shape[1]
    sb = K_BLK
    step = pl.program_id(1)
    tile = jnp.maximum(step - 1, 0)

    @pl.when(step == 0)
    def _():
        proj_b_ref[...] = jnp.zeros(proj_b_ref.shape, F32)

    @pl.when(step <= 1)
    def _():
        for lvl in range(len(POOL_WINDOWS)):
            ubuf_ref[lvl, 0:MAX_WIN, lvl * LANES:] = jnp.zeros(
                (MAX_WIN, ubuf_ref.shape[2] - lvl * LANES), F32)

    @pl.when(step > 1)
    def _():
        for lvl in range(len(POOL_WINDOWS)):
            ubuf_ref[lvl, 0:MAX_WIN, lvl * LANES:] = ubuf_ref[lvl, tm:tm + MAX_WIN, lvl * LANES:]

    def run(write_ref, read_ref):
        for t in range(tm // sb):
            _inproj_post(t, read_ref[t], tile * tm + t * sb, pos_ref, segsum_ref, gq_ref, gk_ref,
                         gkidx_ref, invf_ref, wpool_ref, pscale_ref, qt_ref, kv_ref, qit_ref, ki_ref,
                         wt_ref, vt_ref, pool_ref, ubuf_ref)
        gain = gmix_ref[...] * (1.0 + scale_ref[0])
        for t in range(tm // sb):
            x = x_ref[0, t * sb:(t + 1) * sb, :]
            ms = jnp.mean(x * x, axis=-1, keepdims=True)
            h = (x * lax.rsqrt(ms + EPS) * gain + shift_ref[0]).astype(BF16)
            write_ref[t] = jnp.dot(h, win_ref[...], preferred_element_type=F32)

    @pl.when(step % 2 == 0)
    def _():
        run(proj_a_ref, proj_b_ref)

    @pl.when(step % 2 == 1)
    def _():
        run(proj_b_ref, proj_a_ref)


def _inproj_post(t, proj, t0, pos_ref, segsum_ref, gq_ref, gk_ref, gkidx_ref, invf_ref, wpool_ref,
                 pscale_ref, qt_ref, kv_ref, qit_ref, ki_ref, wt_ref, vt_ref, pool_ref, ubuf_ref):
    sb = K_BLK
    rows = slice(t * sb, (t + 1) * sb)
    tq, q0 = (t * sb) // Q_BLK, (t * sb) % Q_BLK
    d_attn = N_HEADS * HEAD_DIM
    d_qidx = N_IDX_HEADS * IDX_DIM

    def store_cols(dst_ref, chunk, j):
        ct = chunk.T
        for hh in range(2):
            col = (2 * j + hh) * Q_BLK + q0
            dst_ref[0, tq, :, col:col + sb] = ct[hh * HEAD_DIM:(hh + 1) * HEAD_DIM, :].astype(dst_ref.dtype)

    lane = lax.broadcasted_iota(jnp.int32, (sb, LANES), 1)
    first_half = (lane & (HEAD_DIM - 1)) < (HEAD_DIM // 2)
    ang = pos_ref[0, rows, :].astype(F32) * invf_ref[...]
    sin, cos = _sincos(ang)
    sin_signed = jnp.where(first_half, -sin, sin)
    rope = functools.partial(_rope_chunk, cos=cos, sin_signed=sin_signed, first_half=first_half)

    qf = proj[:, :d_attn]
    qsq = qf * qf
    qsq_hi = qsq.astype(BF16)
    qsq_lo = (qsq - qsq_hi.astype(F32)).astype(BF16)
    seg = segsum_ref[...]
    ssq = (jnp.dot(qsq_hi, seg, preferred_element_type=F32)
           + jnp.dot(qsq_lo, seg, preferred_element_type=F32))
    qn = qf * lax.rsqrt(ssq * (1.0 / HEAD_DIM) + EPS) * gq_ref[...]
    for j in range(d_attn // LANES):
        sl = slice(j * LANES, (j + 1) * LANES)
        store_cols(qt_ref, rope(qn[:, sl]), j)

    kvc = proj[:, d_attn:d_attn + LANES]
    is_k = lane < HEAD_DIM
    ksq = jnp.sum(jnp.where(is_k, kvc * kvc, 0.0), axis=-1, keepdims=True)
    kn = kvc * lax.rsqrt(ksq * (1.0 / HEAD_DIM) + EPS) * gk_ref[...]
    kv_ref[0, rows, :] = jnp.where(is_k, rope(kn), kvc).astype(BF16)
    row8 = lax.broadcasted_iota(jnp.int32, (SUBLANES, K_BLK), 0)
    vt_ref[0, t, 0:HEAD_DIM, :] = kvc.T[HEAD_DIM:, :].astype(BF16)
    vt_ref[0, t, HEAD_DIM:HEAD_DIM + SUBLANES, :] = jnp.where(row8 == 0, 1.0, 0.0).astype(BF16)

    o_qi = d_attn + LANES
    for j in range(d_qidx // LANES):
        store_cols(qit_ref, rope(proj[:, o_qi + j * LANES:o_qi + (j + 1) * LANES]), j)

    o_ki = o_qi + d_qidx
    kic = proj[:, o_ki:o_ki + LANES]
    kisq = jnp.sum(jnp.where(is_k, kic * kic, 0.0), axis=-1, keepdims=True)
    kin = kic * lax.rsqrt(kisq * (1.0 / IDX_DIM) + EPS) * gkidx_ref[...]
    ki_ref[0, rows, :] = jnp.where(is_k, rope(kin), 0.0).astype(BF16)
    wt_ref[0, tq, :, q0:q0 + sb] = kic.T[IDX_DIM:IDX_DIM + N_IDX_HEADS, :] * (
        N_IDX_HEADS ** -0.5 * IDX_DIM ** -0.5)

    o_u = o_ki + LANES
    u = proj[:, o_u:o_u + LANES * len(POOL_WINDOWS)]

    base = MAX_WIN + t * sb
    t_idx = t0 + lax.broadcasted_iota(jnp.int32, (sb, 1), 0)
    level = u
    for g, win in enumerate(POOL_WINDOWS):
        sl = slice(g * LANES, (g + 1) * LANES)
        shift = win // 2
        ubuf_ref[g, base:base + sb, g * LANES:] = level
        level = level + ubuf_ref[g, base - shift:base - shift + sb, g * LANES:]
        wsum = level[:, :LANES]
        if g + 1 < len(POOL_WINDOWS):
            level = level[:, LANES:]
        cnt = jnp.minimum(t_idx + 1, win).astype(F32)
        pooled = wsum / cnt - u[:, sl]
        mixed = jnp.dot(pooled.astype(BF16), wpool_ref[g], preferred_element_type=F32)
        pool_ref[0, rows, sl] = (mixed * pscale_ref[:, sl]).astype(BF16)


def _inproj(pos3, x, scale1, shift1, g_mix, w_in_p, segsum, gq_t, gk_e, gkidx_e, invf, w_pool, pscale):
    b, s, d = x.shape
    tm = TM_PROJ
    d_attn = N_HEADS * HEAD_DIM
    d_qidx = N_IDX_HEADS * IDX_DIM
    d_pool = LANES * len(POOL_WINDOWS)
    assert tm % Q_BLK == 0 and Q_BLK % K_BLK == 0 and HEAD_DIM == IDX_DIM and 2 * HEAD_DIM == LANES
    n_tiles = s // tm
    ahead = lambda si: jnp.minimum(si, n_tiles - 1)
    behind = lambda si: jnp.maximum(si - 1, 0)
    tok = lambda w: pl.BlockSpec((1, tm, w), lambda bi, si: (bi, behind(si), 0))
    blk = lambda n, r, c: pl.BlockSpec((1, tm // n, r, c), lambda bi, si: (bi, behind(si), 0, 0))
    per_b = pl.BlockSpec((1, 1, d), lambda bi, si: (bi, 0, 0))
    full = lambda a: pl.BlockSpec(a.shape, lambda bi, si: (0,) * a.ndim)
    nqb, nkb = s // Q_BLK, s // K_BLK
    proj_buf = pltpu.VMEM((tm // K_BLK, K_BLK, w_in_p.shape[1]), F32)
    return pl.pallas_call(
        _inproj_kernel,
        out_shape=(jax.ShapeDtypeStruct((b, nqb, HEAD_DIM, N_HEADS * Q_BLK), BF16),
                   jax.ShapeDtypeStruct((b, s, LANES), BF16),
                   jax.ShapeDtypeStruct((b, nqb, IDX_DIM, N_IDX_HEADS * Q_BLK), BF16),
                   jax.ShapeDtypeStruct((b, s, LANES), BF16),
                   jax.ShapeDtypeStruct((b, nqb, N_IDX_HEADS, Q_BLK), F32),
                   jax.ShapeDtypeStruct((b, nkb, HEAD_DIM + SUBLANES, K_BLK), BF16),
                   jax.ShapeDtypeStruct((b, s, d_pool), BF16)),
        grid=(b, n_tiles + 1),
        in_specs=[tok(1), pl.BlockSpec((1, tm, d), lambda bi, si: (bi, ahead(si), 0)), per_b, per_b,
                  full(g_mix), full(w_in_p), full(segsum),
                  full(gq_t), full(gk_e), full(gkidx_e), full(invf), full(w_pool), full(pscale)],
        out_specs=(blk(Q_BLK, HEAD_DIM, N_HEADS * Q_BLK), tok(LANES),
                   blk(Q_BLK, IDX_DIM, N_IDX_HEADS * Q_BLK), tok(LANES),
                   blk(Q_BLK, N_IDX_HEADS, Q_BLK), blk(K_BLK, HEAD_DIM + SUBLANES, K_BLK), tok(d_pool)),
        scratch_shapes=[pltpu.VMEM((len(POOL_WINDOWS), tm + MAX_WIN, d_pool), F32),
                        proj_buf, proj_buf],
        compiler_params=_cparams(("arbitrary", "arbitrary")),
        name="inproj",
    )(pos3, x, scale1, shift1, g_mix, w_in_p, segsum, gq_t, gk_e, gkidx_e, invf, w_pool, pscale)


def _dsa_kernel(qt_ref, qit_ref, w_ref, kv_ref, ki_ref, vt_ref, o_ref,
                sc_ref, qe_ref, qie_ref, m_ref, mx_ref, st_ref, acc_ref, lg_ref, p_ref):
    topk = float(min(TOPK_MAX, (sc_ref.shape[0] * CNT_BLK) // 4))
    qb = pl.program_id(1)
    n_cols = qt_ref.shape[3]
    n_chunks = n_cols // COL_BLK
    sub = CNT_BLK // K_BLK
    nch = ((qb + 1) * Q_BLK + CNT_BLK - 1) // CNT_BLK
    kgrp = K_BLK // SUBLANES
    sub_rows = [slice(j * K_BLK, (j + 1) * K_BLK) for j in range(sub)]

    zeros_half = jnp.zeros((LANES - HEAD_DIM, n_cols), BF16)
    qe_ref[0:HEAD_DIM, :] = qt_ref[0, 0]
    qe_ref[HEAD_DIM:LANES, :] = zeros_half
    qie_ref[0:IDX_DIM, :] = qit_ref[0, 0]
    qie_ref[IDX_DIM:LANES, :] = zeros_half

    q_pos = qb * Q_BLK + lax.broadcasted_iota(jnp.int32, (K_BLK, Q_BLK), 1)
    key_off = lax.broadcasted_iota(jnp.int32, (K_BLK, Q_BLK), 0)

    def score_step(ch, carry):
        rmax, rmin = carry
        for j in range(sub):
            ki_blk = ki_ref[0, ch, sub_rows[j], :]
            score = None
            for cc in range(n_chunks):
                cs = slice(cc * COL_BLK, (cc + 1) * COL_BLK)
                s_h = jnp.dot(ki_blk, qie_ref[:, cs], preferred_element_type=F32)
                s_h = jnp.maximum(s_h, 0.0)
                for hh in range(COL_BLK // Q_BLK):
                    head = cc * (COL_BLK // Q_BLK) + hh
                    part = s_h[:, hh * Q_BLK:(hh + 1) * Q_BLK] * w_ref[0, 0, head:head + 1, :]
                    score = part if score is None else score + part
            causal = (ch * CNT_BLK + j * K_BLK + key_off) <= q_pos
            masked = jnp.where(causal, score, -jnp.inf)
            sc_ref[ch, sub_rows[j], :] = masked
            hi_part = masked.reshape(kgrp, SUBLANES, Q_BLK).max(axis=0)
            lo_part = jnp.where(causal, score, jnp.inf).reshape(kgrp, SUBLANES, Q_BLK).min(axis=0)
            rmax, rmin = jnp.maximum(rmax, hi_part), jnp.minimum(rmin, lo_part)
        return rmax, rmin

    def score_body(i, carry):
        return score_step(2 * i + 1, score_step(2 * i, carry))

    stats = lax.fori_loop(
        0, nch // 2, score_body,
        (jnp.full((SUBLANES, Q_BLK), -jnp.inf, F32), jnp.full((SUBLANES, Q_BLK), jnp.inf, F32)))
    rmax8, rmin8 = lax.cond(nch % 2 == 1, lambda c: score_step(nch - 1, c), lambda c: c, stats)
    rowmax = jnp.max(rmax8, axis=0, keepdims=True)
    rowmin = jnp.min(rmin8, axis=0, keepdims=True)

    n_causal = (qb * Q_BLK + 1 + lax.broadcasted_iota(jnp.int32, (1, Q_BLK), 1)).astype(F32)
    kt = jnp.minimum(n_causal, topk)

    cgrp = CNT_BLK // CNT_ROWS

    def count_ge(t):
        def body(ch, acc):
            for r in range(cgrp):
                rows = sc_ref[ch, r * CNT_ROWS:(r + 1) * CNT_ROWS, :]
                acc = acc + jnp.where(rows >= t, 1.0, 0.0)
            return acc
        acc = lax.fori_loop(0, nch, body, jnp.zeros((CNT_ROWS, Q_BLK), F32))
        return jnp.sum(acc, axis=0, keepdims=True)

    def bisect_pass(state):
        lo, hi, top, c_lo, c_hi, thr, done = state
        cap = jnp.minimum(hi, top)
        mid = lo + 0.5 * (cap - lo)
        mid = jnp.where(mid <= lo, cap, mid)
        c = count_ge(mid)
        hit = jnp.logical_and(done == 0.0, c == kt)
        thr = jnp.where(hit, mid, thr)
        done = jnp.where(hit, 1.0, done)
        active = done == 0.0
        up = jnp.logical_and(active, c >= kt)
        down = jnp.logical_and(active, c < kt)
        return (jnp.where(up, mid, lo), jnp.where(down, mid, hi), jnp.where(down, jnp.inf, top),
                jnp.where(up, c, c_lo), jnp.where(down, c, c_hi), thr, done)

    def snap_pass(state):
        lo, hi, top, c_lo, c_hi, thr, done = state

        def body(ch, carry):
            a8, b8 = carry
            for r in range(cgrp):
                s = sc_ref[ch, r * CNT_ROWS:(r + 1) * CNT_ROWS, :]
                a8 = jnp.minimum(a8, jnp.where(s >= lo, s, jnp.inf))
                b8 = jnp.maximum(b8, jnp.where(s < hi, s, -jnp.inf))
            return a8, b8

        a8, b8 = lax.fori_loop(
            0, nch, body,
            (jnp.full((CNT_ROWS, Q_BLK), jnp.inf, F32), jnp.full((CNT_ROWS, Q_BLK), -jnp.inf, F32)))
        a = jnp.min(a8, axis=0, keepdims=True)
        b = jnp.max(b8, axis=0, keepdims=True)
        active = done == 0.0
        hit = jnp.logical_and(active, jnp.logical_or(a == b, kt - c_hi == 1.0))
        thr = jnp.where(hit, b, thr)
        done = jnp.where(hit, 2.0, done)
        c_lo = jnp.where(jnp.logical_and(hit, a != b), kt + 1.0, c_lo)
        return jnp.where(active, a, lo), hi, jnp.where(active, b, top), c_lo, c_hi, thr, done

    few = n_causal <= topk
    state0 = (rowmin, jnp.full((1, Q_BLK), jnp.inf, F32), rowmax, n_causal,
              jnp.zeros((1, Q_BLK), F32), jnp.where(few, F32_LOWEST, 0.0), jnp.where(few, 1.0, 0.0))

    def outer_cond(carry):
        return carry[1] > 0.0

    def outer_body(carry):
        state, _ = carry
        state = lax.fori_loop(0, SEARCH_PERIOD, lambda i, st: bisect_pass(st), state)
        state = snap_pass(state)
        pending = jnp.max(jnp.where(state[6] == 0.0, 1.0, 0.0))
        return state, pending

    state1 = lax.fori_loop(0, SEARCH_FIRST, lambda i, st: bisect_pass(st), state0)
    state1 = snap_pass(state1)
    pending1 = jnp.max(jnp.where(state1[6] == 0.0, 1.0, 0.0))
    (lo, hi, _, c_lo, c_hi, thr, done), _ = lax.while_loop(outer_cond, outer_body, (state1, pending1))

    excess = jnp.where(done == 2.0, c_lo - kt, 0.0)
    need = kt - c_hi

    @pl.when(jnp.max(excess) > 0.0)
    def _():
        tri = (lax.broadcasted_iota(jnp.int32, (K_BLK, K_BLK), 0)
               >= lax.broadcasted_iota(jnp.int32, (K_BLK, K_BLK), 1)).astype(BF16)
        has_excess = excess > 0.0

        def drop_step(ch, run):
            for j in range(sub):
                s = sc_ref[ch, sub_rows[j], :]
                tied = jnp.logical_and(s == thr, has_excess)
                prefix = jnp.dot(tri, jnp.where(tied, 1.0, 0.0).astype(BF16), preferred_element_type=F32)
                drop = jnp.logical_and(tied, run + prefix > need)
                sc_ref[ch, sub_rows[j], :] = jnp.where(drop, -jnp.inf, s)
                run = run + jnp.max(prefix, axis=0, keepdims=True)
            return run

        run = lax.fori_loop(0, nch // 2, lambda i, r: drop_step(2 * i + 1, drop_step(2 * i, r)),
                            jnp.zeros((1, Q_BLK), F32))

        @pl.when(nch % 2 == 1)
        def _():
            drop_step(nch - 1, run)

    m_ref[...] = jnp.full(m_ref.shape, M_INIT, F32)
    acc_ref[...] = jnp.zeros(acc_ref.shape, F32)
    row_m = lambda j: slice(j, j + 1)
    row_a = lambda j: slice(sub + j, sub + j + 1)

    def logits_stage(ch, j):
        kv_blk = kv_ref[0, ch, sub_rows[j], :]
        bias = jnp.where(sc_ref[ch, sub_rows[j], :] >= thr, 0.0, MASKED)
        for cc in range(n_chunks):
            logits = jnp.dot(kv_blk, qe_ref[:, cc * COL_BLK:(cc + 1) * COL_BLK],
                             preferred_element_type=F32)
            for hh in range(COL_BLK // Q_BLK):
                cs = slice(cc * COL_BLK + hh * Q_BLK, cc * COL_BLK + (hh + 1) * Q_BLK)
                lg = logits[:, hh * Q_BLK:(hh + 1) * Q_BLK] + bias
                lg_ref[j, :, cs] = lg
                mx_ref[:, cs] = lg.reshape(kgrp, SUBLANES, Q_BLK).max(axis=0)
        m_old = m_ref[...]
        m_new = jnp.maximum(m_old, jnp.max(mx_ref[...], axis=0, keepdims=True))
        st_ref[row_m(j), :] = m_new
        st_ref[row_a(j), :] = jnp.exp2(m_old - m_new)
        m_ref[...] = m_new

    def probs_stage(j):
        p_ref[j] = jnp.exp2(lg_ref[j] - st_ref[row_m(j), :]).astype(BF16)

    def value_stage(kb, j, alpha):
        acc_ref[...] = acc_ref[...] * alpha + jnp.dot(
            vt_ref[0, kb], p_ref[j], preferred_element_type=F32)

    p_ref[sub - 1] = jnp.zeros(p_ref.shape[1:], BF16)
    st_ref[row_a(sub - 1), :] = jnp.ones((1, n_cols), F32)
    logits_stage(0, 0)

    def attn_body(ch, _):
        alpha_prev = st_ref[row_a(1), :]
        logits_stage(ch, 1)
        value_stage(jnp.maximum(ch * sub - 1, 0), 1, alpha_prev)
        probs_stage(0)
        alpha_cur = st_ref[row_a(0), :]
        logits_stage(jnp.minimum(ch + 1, nch - 1), 0)
        value_stage(ch * sub, 0, alpha_cur)
        probs_stage(1)
        return 0

    lax.fori_loop(0, nch, attn_body, 0)
    value_stage(nch * sub - 1, 1, st_ref[row_a(1), :])
    dh = qt_ref.shape[2]
    inv_l = 1.0 / acc_ref[dh:dh + 1, :]
    for j in range(n_cols // Q_BLK // 2):
        pair = [acc_ref[0:dh, (2 * j + hh) * Q_BLK:(2 * j + hh + 1) * Q_BLK]
                * inv_l[:, (2 * j + hh) * Q_BLK:(2 * j + hh + 1) * Q_BLK] for hh in range(2)]
        o_ref[0, :, j * 2 * dh:(j + 1) * 2 * dh] = jnp.concatenate(pair, axis=0).T.astype(o_ref.dtype)


def _dsa(qt, qit, w_t, kv4, ki4, vt4):
    b, nqb, dh, n_cols = qt.shape
    n_steps = kv4.shape[1]
    assert kv4.shape[2] == CNT_BLK and CNT_BLK == 2 * K_BLK and n_cols % COL_BLK == 0
    assert vt4.shape[1] * K_BLK == n_steps * CNT_BLK and vt4.shape[2] == dh + SUBLANES
    per_q = lambda a: pl.BlockSpec((1, 1) + a.shape[2:], lambda bi, qi: (bi, qi, 0, 0))
    per_b = lambda a: pl.BlockSpec((1,) + a.shape[1:], lambda bi, qi: (bi, 0, 0, 0))
    return pl.pallas_call(
        _dsa_kernel,
        out_shape=jax.ShapeDtypeStruct((b, nqb * Q_BLK, (n_cols // Q_BLK) * dh), BF16),
        grid=(b, nqb),
        in_specs=[per_q(qt), per_q(qit), per_q(w_t), per_b(kv4), per_b(ki4), per_b(vt4)],
        out_specs=pl.BlockSpec((1, Q_BLK, (n_cols // Q_BLK) * dh), lambda bi, qi: (bi, qi, 0)),
        scratch_shapes=[pltpu.VMEM((n_steps, CNT_BLK, Q_BLK), F32),
                        pltpu.VMEM((LANES, n_cols), BF16),
                        pltpu.VMEM((LANES, n_cols), BF16),
                        pltpu.VMEM((1, n_cols), F32),
                        pltpu.VMEM((SUBLANES, n_cols), F32),
                        pltpu.VMEM((SUBLANES, n_cols), F32),
                        pltpu.VMEM((dh + SUBLANES, n_cols), F32),
                        pltpu.VMEM((CNT_BLK // K_BLK, K_BLK, n_cols), F32),
                        pltpu.VMEM((CNT_BLK // K_BLK, K_BLK, n_cols), BF16)],
        compiler_params=_cparams(("arbitrary", "arbitrary")),
        name="dsa",
    )(qt, qit, w_t, kv4, ki4, vt4)


def _outproj_kernel(x_ref, attn_ref, pool_ref, woa_ref, wop_ref, gate1_ref, gffn_ref,
                    scale2_ref, shift2_ref, wr_ref, br_ref, x1_ref, h2_ref, gates_ref):
    tm = x_ref.shape[1]
    mix = (jnp.dot(attn_ref[0], woa_ref[...], preferred_element_type=F32)
           + jnp.dot(pool_ref[0], wop_ref[...], preferred_element_type=F32))
    x1 = x_ref[0] + gate1_ref[0] * mix
    x1_ref[0] = x1
    ms = jnp.mean(x1 * x1, axis=-1, keepdims=True)
    h2 = (x1 * lax.rsqrt(ms + EPS) * gffn_ref[...]) * (1.0 + scale2_ref[0]) + shift2_ref[0]
    h2_hi = h2.astype(BF16)
    h2_ref[0] = h2_hi

    h2_lo = (h2 - h2_hi.astype(F32)).astype(BF16)
    wr = wr_ref[...]
    wr_hi = wr.astype(BF16)
    wr_lo = (wr - wr_hi.astype(F32)).astype(BF16)
    logits = (jnp.dot(h2_hi, wr_hi, preferred_element_type=F32)
              + jnp.dot(h2_lo, wr_hi, preferred_element_type=F32)
              + jnp.dot(h2_hi, wr_lo, preferred_element_type=F32)) + br_ref[...]

    lt = logits.T
    n_e = EXPERTS_PER_GROUP
    sub_id = lax.broadcasted_iota(jnp.int32, (SUBLANES, tm), 0)
    big = jnp.int32(LANES)
    glog = jnp.where(sub_id < N_GROUPS, lt[N_EXPERTS:N_EXPERTS + SUBLANES, :], -jnp.inf)
    gmax = jnp.max(glog, axis=0, keepdims=True)
    gsum = jnp.sum(jnp.exp(glog - gmax), axis=0, keepdims=True)
    p_g = 1.0 / gsum
    g_sel = jnp.min(jnp.where(glog == gmax, sub_id, big), axis=0, keepdims=True)
    elog = lt[0:n_e, :]
    for gi in range(1, N_GROUPS):
        elog = jnp.where(g_sel == gi, lt[gi * n_e:(gi + 1) * n_e, :], elog)
    emax = jnp.max(elog, axis=0, keepdims=True)
    eexp = jnp.exp(elog - emax)
    p_e = eexp / jnp.sum(eexp, axis=0, keepdims=True)
    p1 = jnp.max(p_e, axis=0, keepdims=True)
    i1 = jnp.min(jnp.where(p_e == p1, sub_id, big), axis=0, keepdims=True)
    p_e2 = jnp.where(sub_id == i1, -1.0, p_e)
    p2 = jnp.max(p_e2, axis=0, keepdims=True)
    i2 = jnp.min(jnp.where(p_e2 == p2, sub_id, big), axis=0, keepdims=True)
    tot = p1 + p2
    in_grp = (jnp.where(sub_id == i1, p_g * (p1 / tot), 0.0)
              + jnp.where(sub_id == i2, p_g * (p2 / tot), 0.0))
    rows = [jnp.where(g_sel == gi, in_grp, 0.0) for gi in range(N_GROUPS)]
    rows.append(jnp.where(sub_id == 0, g_sel.astype(F32), 0.0))
    rows.append(jnp.zeros((LANES - N_EXPERTS - SUBLANES, tm), F32))
    gates_ref[0] = jnp.concatenate(rows, axis=0).T


def _outproj(x, attn, pool, wo_a, wo_p, gate1, g_ffn, scale2, shift2, w_r, b_r):
    b, s, d = x.shape
    tm = TM_OUT
    tok = lambda w: pl.BlockSpec((1, tm, w), lambda bi, si: (bi, si, 0))
    per_b = pl.BlockSpec((1, 1, d), lambda bi, si: (bi, 0, 0))
    full = lambda a: pl.BlockSpec(a.shape, lambda bi, si: (0,) * a.ndim)
    return pl.pallas_call(
        _outproj_kernel,
        out_shape=(jax.ShapeDtypeStruct((b, s, d), F32),
                   jax.ShapeDtypeStruct((b, s, d), BF16),
                   jax.ShapeDtypeStruct((b, s, LANES), F32)),
        grid=(b, s // tm),
        in_specs=[tok(d), tok(attn.shape[2]), tok(pool.shape[2]), full(wo_a), full(wo_p), per_b,
                  full(g_ffn), per_b, per_b, full(w_r), full(b_r)],
        out_specs=(tok(d), tok(d), tok(LANES)),
        compiler_params=_cparams(("arbitrary", "arbitrary")),
        name="outproj",
    )(x, attn, pool, wo_a, wo_p, gate1, g_ffn, scale2, shift2, w_r, b_r)


def _moe_kernel(x1_ref, h2_ref, gates_ref, gate2_ref, *refs):
    wgu_parts, wd_parts = refs[:MOE_W_PARTS], refs[MOE_W_PARTS:2 * MOE_W_PARTS]
    o_ref, xe_ref, rank_ref, rank_t_ref, own_ref, grpb_ref = refs[2 * MOE_W_PARTS:]
    g = pl.program_id(2)
    tm, d = h2_ref.shape[1], h2_ref.shape[2]
    per_part = wd_parts[0].shape[0]
    n_e, d_exp = per_part * MOE_W_PARTS, wd_parts[0].shape[1]
    gf = g.astype(F32)

    @pl.when(g == 0)
    def _():
        tri = (lax.broadcasted_iota(jnp.int32, (RANK_BLK, RANK_BLK), 0)
               >= lax.broadcasted_iota(jnp.int32, (RANK_BLK, RANK_BLK), 1)).astype(BF16)
        lane_b = lax.broadcasted_iota(jnp.int32, (RANK_BLK, LANES), 1)
        run = jnp.zeros((1, LANES), F32)
        for sb in range(tm // RANK_BLK):
            rows = slice(sb * RANK_BLK, (sb + 1) * RANK_BLK)
            gts_b = gates_ref[0, rows, :]
            grp = jnp.sum(jnp.where(lane_b == N_EXPERTS, gts_b, 0.0), axis=-1, keepdims=True)
            member = jnp.where(jnp.logical_and(lane_b < N_GROUPS, lane_b.astype(F32) == grp), 1.0, 0.0)
            pre = jnp.dot(tri, member.astype(BF16), preferred_element_type=F32) + run
            rank_ref[rows, :] = jnp.where(lane_b == N_GROUPS, grp, pre)
            own = jnp.sum(member * pre, axis=-1, keepdims=True) - 1.0
            own_ref[rows, :] = jnp.broadcast_to(own, (RANK_BLK, LANES))
            grpb_ref[rows, :] = jnp.broadcast_to(grp, (RANK_BLK, LANES))
            run = jnp.max(pre, axis=0, keepdims=True)
        rank_t_ref[...] = rank_ref[...].T
        gts = gates_ref[0]
        g_hi = gts.astype(BF16)
        xe_ref[:, :d] = h2_ref[0]
        xe_ref[:, d:d + LANES] = g_hi
        xe_ref[:, d + LANES:d + 2 * LANES] = (gts - g_hi.astype(F32)).astype(BF16)
        o_ref[0] = jnp.zeros((tm, d), F32)

    rank_row = rank_t_ref[pl.ds(g, 1), :]
    pos_row = jnp.where(rank_t_ref[N_GROUPS:N_GROUPS + 1, :] == gf, rank_row - 1.0, -1.0)
    pos_col = jnp.where(grpb_ref[...] == gf, own_ref[...], -1.0)
    n_rows = jnp.max(rank_row).astype(jnp.int32)

    def expert_pass(first_row, n_ch):
        row_id = lax.broadcasted_iota(jnp.int32, (n_ch, tm), 0).astype(F32)
        col_id = lax.broadcasted_iota(jnp.int32, (tm, n_ch), 1).astype(F32)
        lane_c = lax.broadcasted_iota(jnp.int32, (n_ch, LANES), 1)
        r0 = first_row.astype(F32)
        gather = jnp.where(pos_row - r0 == row_id, 1.0, 0.0).astype(BF16)
        xg = jnp.dot(gather, xe_ref[...], preferred_element_type=F32)
        xb = xg[:, :d].astype(BF16)
        gates_c = xg[:, d:d + LANES] + xg[:, d + LANES:d + 2 * LANES]
        ya = None
        for e in range(n_e):
            w_gu_e = wgu_parts[e // per_part][e % per_part]
            w_d_e = wd_parts[e // per_part][e % per_part]
            gu = jnp.dot(xb, w_gu_e, preferred_element_type=F32)
            gt = gu[:, :d_exp]
            a = (gt * jax.nn.sigmoid(gt)) * gu[:, d_exp:]
            gate_e = jnp.sum(jnp.where(lane_c == g * n_e + e, gates_c, 0.0), axis=-1, keepdims=True)
            y = jnp.dot((a * gate_e).astype(BF16), w_d_e, preferred_element_type=F32)
            ya = y if ya is None else ya + y
        pos_wide = jnp.concatenate([pos_col] * (n_ch // LANES), axis=1)
        scatter = jnp.where(pos_wide - r0 == col_id, 1.0, 0.0).astype(BF16)
        o_ref[0] += jnp.dot(scatter, ya.astype(BF16), preferred_element_type=F32)

    n_full = (n_rows + MOE_CH - MOE_CH_TAIL - 1) // MOE_CH

    def full_pass(c, _):
        expert_pass(c * MOE_CH, MOE_CH)
        return 0

    lax.fori_loop(0, n_full, full_pass, 0)

    @pl.when(n_rows > n_full * MOE_CH)
    def _():
        expert_pass(n_full * MOE_CH, MOE_CH_TAIL)

    @pl.when(g == pl.num_programs(2) - 1)
    def _():
        o_ref[0] = x1_ref[0] + gate2_ref[0] * o_ref[0]


def _moe(x1, h2, gates, gate2, w_gu, w_d):
    b, s, d = x1.shape
    tm = TM_MOE
    n_e = EXPERTS_PER_GROUP
    assert w_gu.shape[0] == N_GROUPS * n_e and tm % RANK_BLK == 0 and n_e % MOE_W_PARTS == 0
    assert MOE_CH % LANES == 0 and MOE_CH_TAIL % LANES == 0 and MOE_CH_TAIL <= MOE_CH
    per_part = n_e // MOE_W_PARTS
    tok = lambda w: pl.BlockSpec((1, tm, w), lambda bi, si, g: (bi, si, 0))

    def slab(w, k):
        return pl.BlockSpec((per_part,) + w.shape[1:], lambda bi, si, g: (g * MOE_W_PARTS + k, 0, 0))

    return pl.pallas_call(
        _moe_kernel,
        out_shape=jax.ShapeDtypeStruct((b, s, d), F32),
        grid=(b, s // tm, N_GROUPS),
        in_specs=([tok(d), tok(d), tok(LANES), pl.BlockSpec((1, 1, d), lambda bi, si, g: (bi, 0, 0))]
                  + [slab(w_gu, k) for k in range(MOE_W_PARTS)]
                  + [slab(w_d, k) for k in range(MOE_W_PARTS)]),
        out_specs=tok(d),
        scratch_shapes=[pltpu.VMEM((tm, d + 2 * LANES), BF16),
                        pltpu.VMEM((tm, LANES), F32),
                        pltpu.VMEM((LANES, tm), F32),
                        pltpu.VMEM((tm, LANES), F32),
                        pltpu.VMEM((tm, LANES), F32)],
        compiler_params=pltpu.CompilerParams(
            dimension_semantics=("arbitrary", "arbitrary", "arbitrary"),
            vmem_limit_bytes=VMEM_LIMIT_MOE_BYTES),
        name="moe",
    )(x1, h2, gates, gate2, *([w_gu] * MOE_W_PARTS), *([w_d] * MOE_W_PARTS))


def _layer(x, mod, pos3, g_mix, g_ffn, w_in, g_q, g_k, g_kidx, w_pool, pool_scale, w_out,
           w_rg, b_rg, w_re, b_re, w_gate, w_up, w_down):
    b, s, d = x.shape
    d_attn = N_HEADS * HEAD_DIM
    nqb = s // Q_BLK
    nkb = s // K_BLK
    shift1, scale1, gate1, shift2, scale2, gate2 = [m[:, None, :] for m in jnp.split(mod, 6, axis=-1)]

    n_front = d_attn + 2 * HEAD_DIM + N_IDX_HEADS * IDX_DIM + IDX_DIM + N_IDX_HEADS
    pad = (-n_front) % LANES
    w_in_p = jnp.concatenate([w_in[:, :n_front], jnp.zeros((d, pad), w_in.dtype), w_in[:, n_front:]],
                             axis=1).astype(BF16)
    seg_id = jnp.arange(d_attn) // HEAD_DIM
    segsum = (seg_id[:, None] == seg_id[None, :]).astype(BF16)
    ones_half = jnp.ones((LANES - HEAD_DIM,), F32)
    gq_t = (jnp.tile(g_q, N_HEADS) * (LOG2_E * HEAD_DIM ** -0.5))[None, :]
    gk_e = jnp.concatenate([g_k, ones_half])[None, :]
    gkidx_e = jnp.concatenate([g_kidx, ones_half])[None, :]
    half = HEAD_DIM // 2
    inv_freq = ROPE_THETA ** (-jnp.arange(0, HEAD_DIM, 2, dtype=F32) / HEAD_DIM)
    invf = jnp.tile(inv_freq, LANES // half)[None, :]

    qt, kv, qit, ki, w_t, vt4, pool = _inproj(pos3, x, scale1, shift1, g_mix[None, :], w_in_p, segsum,
                                              gq_t, gk_e, gkidx_e, invf, w_pool.astype(BF16),
                                              pool_scale[None, :])
    kv4 = kv.reshape(b, s // CNT_BLK, CNT_BLK, LANES)
    ki4 = ki.reshape(b, s // CNT_BLK, CNT_BLK, LANES)
    attn = _dsa(qt, qit, w_t, kv4, ki4, vt4)

    w_out_b = w_out.astype(BF16)
    w_r = jnp.concatenate([w_re, w_rg, jnp.zeros((d, LANES - N_EXPERTS - N_GROUPS), F32)], axis=1)
    b_r = jnp.concatenate([b_re, b_rg, jnp.zeros((LANES - N_EXPERTS - N_GROUPS,), F32)])[None, :]
    x1, h2, gates = _outproj(x, attn, pool, w_out_b[:d_attn], w_out_b[d_attn:], gate1,
                             g_ffn[None, :], scale2, shift2, w_r, b_r)

    w_gu = jnp.concatenate([w_gate, w_up], axis=-1).astype(BF16)
    return _moe(x1, h2, gates, gate2, w_gu, w_down.astype(BF16))


def kernel(x, c, positions, w_ada, b_ada, g_norm_mix, g_norm_ffn, w_in, g_q, g_k, g_kidx, w_pool,
           pool_scale, w_out, w_router_group, b_router_group, w_router_expert, b_router_expert,
           w_gate, w_up, w_down):
    b, s, d = x.shape
    depth = w_ada.shape[0]
    assert s % TM_MOE == 0 and s % K_BLK == 0 and d % LANES == 0
    pos3 = positions[:, :, None]
    c_pad = jnp.concatenate([c, jnp.zeros((-b % SUBLANES, d), c.dtype)], axis=0)
    for l in range(depth):
        mod = _adaln(c_pad, w_ada[l], b_ada[l][None, :])[:b]
        x = _layer(x, mod, pos3, g_norm_mix[l], g_norm_ffn[l], w_in[l], g_q[l], g_k[l], g_kidx[l],
                   w_pool[l], pool_scale[l], w_out[l], w_router_group[l], b_router_group[l],
                   w_router_expert[l], b_router_expert[l], w_gate[l], w_up[l], w_down[l])
    return x
```

```python
import functools

import jax
import jax.numpy as jnp
from jax import lax
from jax.experimental import pallas as pl
from jax.experimental.pallas import tpu as pltpu

N_HEADS = 8
HEAD_DIM = 64
N_IDX_HEADS = 8
IDX_DIM = 64
TOPK_MAX = 256
ROPE_THETA = 10000.0
POOL_WINDOWS = (2, 4, 8, 16)
N_GROUPS = 4
EXPERTS_PER_GROUP = 8
N_EXPERTS = N_GROUPS * EXPERTS_PER_GROUP
EPS = 1e-6
N_MOD = 6
assert EXPERTS_PER_GROUP & (EXPERTS_PER_GROUP - 1) == 0

LANES = 128
SUBLANES = 8
assert EXPERTS_PER_GROUP == SUBLANES and N_GROUPS <= SUBLANES
VMEM_LIMIT_BYTES = 56 * 1024 * 1024
VMEM_LIMIT_MOE_BYTES = 60 * 1024 * 1024

Q_BLK = 512
K_BLK = 256
COL_BLK = 512
CNT_BLK = 512
CNT_ROWS = 32
SEARCH_FIRST = 15
SEARCH_PERIOD = 1
TM_PROJ = 512
SB_PROJ = 512
TM_OUT = 1024
TM_MOE = 1024
MOE_CH = 256
MOE_CH_TAIL = 128
RANK_BLK = 256
MOE_W_PARTS = 4
MAX_WIN = max(POOL_WINDOWS)
assert all(w == 2 ** (g + 1) for g, w in enumerate(POOL_WINDOWS))
M_INIT = -1e29
MASKED = -1e30
F32_LOWEST = -3.0e38
LOG2_E = 1.4426950408889634

BF16 = jnp.bfloat16
F32 = jnp.float32


def _cparams(sem):
    return pltpu.CompilerParams(dimension_semantics=sem, vmem_limit_bytes=VMEM_LIMIT_BYTES)


def _adaln_kernel(c_ref, w_ref, b_ref, o_ref):
    c = c_ref[...]
    c_act = c * jax.nn.sigmoid(c)
    o_ref[...] = jnp.dot(c_act, w_ref[...], preferred_element_type=F32) + b_ref[...]


def _adaln(c_pad, w_ada, b_ada):
    rows, d = c_pad.shape
    n = w_ada.shape[1]
    tn = n // N_MOD
    return pl.pallas_call(
        _adaln_kernel,
        out_shape=jax.ShapeDtypeStruct((rows, n), F32),
        grid=(n // tn,),
        in_specs=[pl.BlockSpec((rows, d), lambda j: (0, 0)),
                  pl.BlockSpec((d, tn), lambda j: (0, j)),
                  pl.BlockSpec((1, tn), lambda j: (0, j))],
        out_specs=pl.BlockSpec((rows, tn), lambda j: (0, j)),
        compiler_params=_cparams(("arbitrary",)),
        name="adaln",
    )(c_pad, w_ada, b_ada)


_PIO2_HI, _PIO2_MID, _PIO2_LO = 1.5703125, 4.837512969970703125e-4, 7.54978995489188e-8
_SIN_COEF = (-1.9515295891e-4, 8.3321608736e-3, -1.6666654611e-1)
_COS_COEF = (2.443315711809948e-5, -1.388731625493765e-3, 4.166664568298827e-2)


def _sincos(x):
    k = jnp.floor(x * (2.0 / jnp.pi) + 0.5)
    r = ((x - k * _PIO2_HI) - k * _PIO2_MID) - k * _PIO2_LO
    z = r * r
    s = r + r * z * (_SIN_COEF[2] + z * (_SIN_COEF[1] + z * _SIN_COEF[0]))
    c = 1.0 - 0.5 * z + z * z * (_COS_COEF[2] + z * (_COS_COEF[1] + z * _COS_COEF[0]))
    q = k - 4.0 * jnp.floor(k * 0.25)
    odd = jnp.logical_or(q == 1.0, q == 3.0)
    sin_b = jnp.where(odd, c, s)
    cos_b = jnp.where(odd, s, c)
    return (jnp.where(q >= 2.0, -sin_b, sin_b),
            jnp.where(jnp.logical_or(q == 1.0, q == 2.0), -cos_b, cos_b))


def _rope_chunk(y, cos, sin_signed, first_half):
    from_hi = pltpu.roll(y, LANES - HEAD_DIM // 2, 1)
    from_lo = pltpu.roll(y, HEAD_DIM // 2, 1)
    return y * cos + jnp.where(first_half, from_hi, from_lo) * sin_signed


def _inproj_kernel(pos_ref, x_ref, scale_ref, shift_ref, gmix_ref, win_ref, segsum_ref,
                   gq_ref, gk_ref, gkidx_ref, invf_ref, wpool_ref, pscale_ref,
                   qt_ref, kv_ref, qit_ref, ki_ref, wt_ref, vt_ref, pool_ref,
                   ubuf_ref, proj_a_ref, proj_b_ref):
    tm = x_ref.shape[1]
    sb = proj_a_ref.shape[1]
    step = pl.program_id(1)
    tile = jnp.maximum(step - 1, 0)

    @pl.when(step == 0)
    def _():
        proj_b_ref[...] = jnp.zeros(proj_b_ref.shape, F32)

    @pl.when(step <= 1)
    def _():
        for lvl in range(len(POOL_WINDOWS)):
            ubuf_ref[lvl, 0:MAX_WIN, lvl * LANES:] = jnp.zeros(
                (MAX_WIN, ubuf_ref.shape[2] - lvl * LANES), F32)

    @pl.when(step > 1)
    def _():
        for lvl in range(len(POOL_WINDOWS)):
            ubuf_ref[lvl, 0:MAX_WIN, lvl * LANES:] = ubuf_ref[lvl, tm:tm + MAX_WIN, lvl * LANES:]

    def run(write_ref, read_ref):
        for t in range(tm // sb):
            _inproj_post(t, read_ref[t], tile * tm + t * sb, pos_ref, segsum_ref, gq_ref, gk_ref,
                         gkidx_ref, invf_ref, wpool_ref, pscale_ref, qt_ref, kv_ref, qit_ref, ki_ref,
                         wt_ref, vt_ref, pool_ref, ubuf_ref)
        gain = gmix_ref[...] * (1.0 + scale_ref[0])
        for t in range(tm // sb):
            x = x_ref[0, t * sb:(t + 1) * sb, :]
            ms = jnp.mean(x * x, axis=-1, keepdims=True)
            h = (x * lax.rsqrt(ms + EPS) * gain + shift_ref[0]).astype(BF16)
            write_ref[t] = jnp.dot(h, win_ref[...], preferred_element_type=F32)

    @pl.when(step % 2 == 0)
    def _():
        run(proj_a_ref, proj_b_ref)

    @pl.when(step % 2 == 1)
    def _():
        run(proj_b_ref, proj_a_ref)


def _inproj_post(t, proj, t0, pos_ref, segsum_ref, gq_ref, gk_ref, gkidx_ref, invf_ref, wpool_ref,
                 pscale_ref, qt_ref, kv_ref, qit_ref, ki_ref, wt_ref, vt_ref, pool_ref, ubuf_ref):
    sb = proj.shape[0]
    rows = slice(t * sb, (t + 1) * sb)
    tq, q0 = (t * sb) // Q_BLK, (t * sb) % Q_BLK
    d_attn = N_HEADS * HEAD_DIM
    d_qidx = N_IDX_HEADS * IDX_DIM

    def store_cols(dst_ref, chunk, j):
        ct = chunk.T
        for hh in range(2):
            col = (2 * j + hh) * Q_BLK + q0
            dst_ref[0, tq, :, col:col + sb] = ct[hh * HEAD_DIM:(hh + 1) * HEAD_DIM, :].astype(dst_ref.dtype)

    lane = lax.broadcasted_iota(jnp.int32, (sb, LANES), 1)
    first_half = (lane & (HEAD_DIM - 1)) < (HEAD_DIM // 2)
    ang = pos_ref[0, rows, :].astype(F32) * invf_ref[...]
    sin, cos = _sincos(ang)
    sin_signed = jnp.where(first_half, -sin, sin)
    rope = functools.partial(_rope_chunk, cos=cos, sin_signed=sin_signed, first_half=first_half)

    qf = proj[:, :d_attn]
    qsq = qf * qf
    qsq_hi = qsq.astype(BF16)
    qsq_lo = (qsq - qsq_hi.astype(F32)).astype(BF16)
    seg = segsum_ref[...]
    ssq = (jnp.dot(qsq_hi, seg, preferred_element_type=F32)
           + jnp.dot(qsq_lo, seg, preferred_element_type=F32))
    qn = qf * lax.rsqrt(ssq * (1.0 / HEAD_DIM) + EPS) * gq_ref[...]
    for j in range(d_attn // LANES):
        sl = slice(j * LANES, (j + 1) * LANES)
        store_cols(qt_ref, rope(qn[:, sl]), j)

    kvc = proj[:, d_attn:d_attn + LANES]
    is_k = lane < HEAD_DIM
    ksq = jnp.sum(jnp.where(is_k, kvc * kvc, 0.0), axis=-1, keepdims=True)
    kn = kvc * lax.rsqrt(ksq * (1.0 / HEAD_DIM) + EPS) * gk_ref[...]
    kv_ref[0, rows, :] = jnp.where(is_k, rope(kn), kvc).astype(BF16)
    row8 = lax.broadcasted_iota(jnp.int32, (SUBLANES, K_BLK), 0)
    for tk in range(sb // K_BLK):
        kb = t * (sb // K_BLK) + tk
        vt_ref[0, kb, 0:HEAD_DIM, :] = kvc[tk * K_BLK:(tk + 1) * K_BLK, :].T[HEAD_DIM:, :].astype(BF16)
        vt_ref[0, kb, HEAD_DIM:HEAD_DIM + SUBLANES, :] = jnp.where(row8 == 0, 1.0, 0.0).astype(BF16)

    o_qi = d_attn + LANES
    for j in range(d_qidx // LANES):
        store_cols(qit_ref, rope(proj[:, o_qi + j * LANES:o_qi + (j + 1) * LANES]), j)

    o_ki = o_qi + d_qidx
    kic = proj[:, o_ki:o_ki + LANES]
    kisq = jnp.sum(jnp.where(is_k, kic * kic, 0.0), axis=-1, keepdims=True)
    kin = kic * lax.rsqrt(kisq * (1.0 / IDX_DIM) + EPS) * gkidx_ref[...]
    ki_ref[0, rows, :] = jnp.where(is_k, rope(kin), 0.0).astype(BF16)
    wt_ref[0, tq, :, q0:q0 + sb] = kic.T[IDX_DIM:IDX_DIM + N_IDX_HEADS, :] * (
        N_IDX_HEADS ** -0.5 * IDX_DIM ** -0.5)

    o_u = o_ki + LANES
    u = proj[:, o_u:o_u + LANES * len(POOL_WINDOWS)]

    base = MAX_WIN + t * sb
    t_idx = t0 + lax.broadcasted_iota(jnp.int32, (sb, 1), 0)
    level = u
    for g, win in enumerate(POOL_WINDOWS):
        sl = slice(g * LANES, (g + 1) * LANES)
        shift = win // 2
        ubuf_ref[g, base:base + sb, g * LANES:] = level
        level = level + ubuf_ref[g, base - shift:base - shift + sb, g * LANES:]
        wsum = level[:, :LANES]
        if g + 1 < len(POOL_WINDOWS):
            level = level[:, LANES:]
        cnt = jnp.minimum(t_idx + 1, win).astype(F32)
        pooled = wsum / cnt - u[:, sl]
        mixed = jnp.dot(pooled.astype(BF16), wpool_ref[g], preferred_element_type=F32)
        pool_ref[0, rows, sl] = (mixed * pscale_ref[:, sl]).astype(BF16)


def _inproj(pos3, x, scale1, shift1, g_mix, w_in_p, segsum, gq_t, gk_e, gkidx_e, invf, w_pool, pscale):
    b, s, d = x.shape
    tm = TM_PROJ
    d_attn = N_HEADS * HEAD_DIM
    d_qidx = N_IDX_HEADS * IDX_DIM
    d_pool = LANES * len(POOL_WINDOWS)
    assert tm % Q_BLK == 0 and Q_BLK % K_BLK == 0 and HEAD_DIM == IDX_DIM and 2 * HEAD_DIM == LANES
    n_tiles = s // tm
    ahead = lambda si: jnp.minimum(si, n_tiles - 1)
    behind = lambda si: jnp.maximum(si - 1, 0)
    tok = lambda w: pl.BlockSpec((1, tm, w), lambda bi, si: (bi, behind(si), 0))
    blk = lambda n, r, c: pl.BlockSpec((1, tm // n, r, c), lambda bi, si: (bi, behind(si), 0, 0))
    per_b = pl.BlockSpec((1, 1, d), lambda bi, si: (bi, 0, 0))
    full = lambda a: pl.BlockSpec(a.shape, lambda bi, si: (0,) * a.ndim)
    nqb, nkb = s // Q_BLK, s // K_BLK
    proj_buf = pltpu.VMEM((tm // SB_PROJ, SB_PROJ, w_in_p.shape[1]), F32)
    return pl.pallas_call(
        _inproj_kernel,
        out_shape=(jax.ShapeDtypeStruct((b, nqb, HEAD_DIM, N_HEADS * Q_BLK), BF16),
                   jax.ShapeDtypeStruct((b, s, LANES), BF16),
                   jax.ShapeDtypeStruct((b, nqb, IDX_DIM, N_IDX_HEADS * Q_BLK), BF16),
                   jax.ShapeDtypeStruct((b, s, LANES), BF16),
                   jax.ShapeDtypeStruct((b, nqb, N_IDX_HEADS, Q_BLK), F32),
                   jax.ShapeDtypeStruct((b, nkb, HEAD_DIM + SUBLANES, K_BLK), BF16),
                   jax.ShapeDtypeStruct((b, s, d_pool), BF16)),
        grid=(b, n_tiles + 1),
        in_specs=[tok(1), pl.BlockSpec((1, tm, d), lambda bi, si: (bi, ahead(si), 0)), per_b, per_b,
                  full(g_mix), full(w_in_p), full(segsum),
                  full(gq_t), full(gk_e), full(gkidx_e), full(invf), full(w_pool), full(pscale)],
        out_specs=(blk(Q_BLK, HEAD_DIM, N_HEADS * Q_BLK), tok(LANES),
                   blk(Q_BLK, IDX_DIM, N_IDX_HEADS * Q_BLK), tok(LANES),
                   blk(Q_BLK, N_IDX_HEADS, Q_BLK), blk(K_BLK, HEAD_DIM + SUBLANES, K_BLK), tok(d_pool)),
        scratch_shapes=[pltpu.VMEM((len(POOL_WINDOWS), tm + MAX_WIN, d_pool), F32),
                        proj_buf, proj_buf],
        compiler_params=_cparams(("arbitrary", "arbitrary")),
        name="inproj",
    )(pos3, x, scale1, shift1, g_mix, w_in_p, segsum, gq_t, gk_e, gkidx_e, invf, w_pool, pscale)


def _dsa_kernel(qt_ref, qit_ref, w_ref, kv_ref, ki_ref, vt_ref, o_ref,
                sc_ref, qe_ref, qie_ref, m_ref, mx_ref, st_ref, acc_ref, lg_ref, p_ref):
    topk = float(min(TOPK_MAX, (sc_ref.shape[0] * CNT_BLK) // 4))
    qb = pl.program_id(1)
    n_cols = qt_ref.shape[3]
    n_chunks = n_cols // COL_BLK
    sub = CNT_BLK // K_BLK
    nch = ((qb + 1) * Q_BLK + CNT_BLK - 1) // CNT_BLK
    kgrp = K_BLK // SUBLANES
    sub_rows = [slice(j * K_BLK, (j + 1) * K_BLK) for j in range(sub)]

    zeros_half = jnp.zeros((LANES - HEAD_DIM, n_cols), BF16)
    qe_ref[0:HEAD_DIM, :] = qt_ref[0, 0]
    qe_ref[HEAD_DIM:LANES, :] = zeros_half
    qie_ref[0:IDX_DIM, :] = qit_ref[0, 0]
    qie_ref[IDX_DIM:LANES, :] = zeros_half

    q_pos = qb * Q_BLK + lax.broadcasted_iota(jnp.int32, (K_BLK, Q_BLK), 1)
    key_off = lax.broadcasted_iota(jnp.int32, (K_BLK, Q_BLK), 0)

    def score_step(ch, carry):
        rmax, rmin = carry
        for j in range(sub):
            ki_blk = ki_ref[0, ch, sub_rows[j], :]
            score = None
            for cc in range(n_chunks):
                cs = slice(cc * COL_BLK, (cc + 1) * COL_BLK)
                s_h = jnp.dot(ki_blk, qie_ref[:, cs], preferred_element_type=F32)
                s_h = jnp.maximum(s_h, 0.0)
                for hh in range(COL_BLK // Q_BLK):
                    head = cc * (COL_BLK // Q_BLK) + hh
                    part = s_h[:, hh * Q_BLK:(hh + 1) * Q_BLK] * w_ref[0, 0, head:head + 1, :]
                    score = part if score is None else score + part
            causal = (ch * CNT_BLK + j * K_BLK + key_off) <= q_pos
            masked = jnp.where(causal, score, -jnp.inf)
            sc_ref[ch, sub_rows[j], :] = masked
            hi_part = masked.reshape(kgrp, SUBLANES, Q_BLK).max(axis=0)
            lo_part = jnp.where(causal, score, jnp.inf).reshape(kgrp, SUBLANES, Q_BLK).min(axis=0)
            rmax, rmin = jnp.maximum(rmax, hi_part), jnp.minimum(rmin, lo_part)
        return rmax, rmin

    def score_body(i, carry):
        return score_step(2 * i + 1, score_step(2 * i, carry))

    stats = lax.fori_loop(
        0, nch // 2, score_body,
        (jnp.full((SUBLANES, Q_BLK), -jnp.inf, F32), jnp.full((SUBLANES, Q_BLK), jnp.inf, F32)))
    rmax8, rmin8 = lax.cond(nch % 2 == 1, lambda c: score_step(nch - 1, c), lambda c: c, stats)
    rowmax = jnp.max(rmax8, axis=0, keepdims=True)
    rowmin = jnp.min(rmin8, axis=0, keepdims=True)

    n_causal = (qb * Q_BLK + 1 + lax.broadcasted_iota(jnp.int32, (1, Q_BLK), 1)).astype(F32)
    kt = jnp.minimum(n_causal, topk)

    cgrp = CNT_BLK // CNT_ROWS

    def count_ge(t):
        def body(ch, acc):
            for r in range(cgrp):
                rows = sc_ref[ch, r * CNT_ROWS:(r + 1) * CNT_ROWS, :]
                acc = acc + jnp.where(rows >= t, 1.0, 0.0)
            return acc
        acc = lax.fori_loop(0, nch, body, jnp.zeros((CNT_ROWS, Q_BLK), F32))
        return jnp.sum(acc, axis=0, keepdims=True)

    def bisect_pass(state):
        lo, hi, top, c_lo, c_hi, thr, done = state
        cap = jnp.minimum(hi, top)
        mid = lo + 0.5 * (cap - lo)
        mid = jnp.where(mid <= lo, cap, mid)
        c = count_ge(mid)
        hit = jnp.logical_and(done == 0.0, c == kt)
        thr = jnp.where(hit, mid, thr)
        done = jnp.where(hit, 1.0, done)
        active = done == 0.0
        up = jnp.logical_and(active, c >= kt)
        down = jnp.logical_and(active, c < kt)
        return (jnp.where(up, mid, lo), jnp.where(down, mid, hi), jnp.where(down, jnp.inf, top),
                jnp.where(up, c, c_lo), jnp.where(down, c, c_hi), thr, done)

    def snap_pass(state):
        lo, hi, top, c_lo, c_hi, thr, done = state

        def body(ch, carry):
            a8, b8 = carry
            for r in range(cgrp):
                s = sc_ref[ch, r * CNT_ROWS:(r + 1) * CNT_ROWS, :]
                a8 = jnp.minimum(a8, jnp.where(s >= lo, s, jnp.inf))
                b8 = jnp.maximum(b8, jnp.where(s < hi, s, -jnp.inf))
            return a8, b8

        a8, b8 = lax.fori_loop(
            0, nch, body,
            (jnp.full((CNT_ROWS, Q_BLK), jnp.inf, F32), jnp.full((CNT_ROWS, Q_BLK), -jnp.inf, F32)))
        a = jnp.min(a8, axis=0, keepdims=True)
        b = jnp.max(b8, axis=0, keepdims=True)
        active = done == 0.0
        hit = jnp.logical_and(active, jnp.logical_or(a == b, kt - c_hi == 1.0))
        thr = jnp.where(hit, b, thr)
        done = jnp.where(hit, 2.0, done)
        c_lo = jnp.where(jnp.logical_and(hit, a != b), kt + 1.0, c_lo)
        return jnp.where(active, a, lo), hi, jnp.where(active, b, top), c_lo, c_hi, thr, done

    few = n_causal <= topk
    state0 = (rowmin, jnp.full((1, Q_BLK), jnp.inf, F32), rowmax, n_causal,
              jnp.zeros((1, Q_BLK), F32), jnp.where(few, F32_LOWEST, 0.0), jnp.where(few, 1.0, 0.0))

    def outer_cond(carry):
        return carry[1] > 0.0

    def outer_body(carry):
        state, _ = carry
        state = lax.fori_loop(0, SEARCH_PERIOD, lambda i, st: bisect_pass(st), state)
        state = snap_pass(state)
        pending = jnp.max(jnp.where(state[6] == 0.0, 1.0, 0.0))
        return state, pending

    state1 = lax.fori_loop(0, SEARCH_FIRST, lambda i, st: bisect_pass(st), state0)
    state1 = snap_pass(state1)
    pending1 = jnp.max(jnp.where(state1[6] == 0.0, 1.0, 0.0))
    (lo, hi, _, c_lo, c_hi, thr, done), _ = lax.while_loop(outer_cond, outer_body, (state1, pending1))

    excess = jnp.where(done == 2.0, c_lo - kt, 0.0)
    need = kt - c_hi

    @pl.when(jnp.max(excess) > 0.0)
    def _():
        tri = (lax.broadcasted_iota(jnp.int32, (K_BLK, K_BLK), 0)
               >= lax.broadcasted_iota(jnp.int32, (K_BLK, K_BLK), 1)).astype(BF16)
        has_excess = excess > 0.0

        def drop_step(ch, run):
            for j in range(sub):
                s = sc_ref[ch, sub_rows[j], :]
                tied = jnp.logical_and(s == thr, has_excess)
                prefix = jnp.dot(tri, jnp.where(tied, 1.0, 0.0).astype(BF16), preferred_element_type=F32)
                drop = jnp.logical_and(tied, run + prefix > need)
                sc_ref[ch, sub_rows[j], :] = jnp.where(drop, -jnp.inf, s)
                run = run + jnp.max(prefix, axis=0, keepdims=True)
            return run

        run = lax.fori_loop(0, nch // 2, lambda i, r: drop_step(2 * i + 1, drop_step(2 * i, r)),
                            jnp.zeros((1, Q_BLK), F32))

        @pl.when(nch % 2 == 1)
        def _():
            drop_step(nch - 1, run)

    m_ref[...] = jnp.full(m_ref.shape, M_INIT, F32)
    acc_ref[...] = jnp.zeros(acc_ref.shape, F32)
    row_m = lambda j: slice(j, j + 1)
    row_a = lambda j: slice(sub + j, sub + j + 1)

    def logits_stage(ch, j):
        kv_blk = kv_ref[0, ch, sub_rows[j], :]
        bias = jnp.where(sc_ref[ch, sub_rows[j], :] >= thr, 0.0, MASKED)
        for cc in range(n_chunks):
            logits = jnp.dot(kv_blk, qe_ref[:, cc * COL_BLK:(cc + 1) * COL_BLK],
                             preferred_element_type=F32)
            for hh in range(COL_BLK // Q_BLK):
                cs = slice(cc * COL_BLK + hh * Q_BLK, cc * COL_BLK + (hh + 1) * Q_BLK)
                lg = logits[:, hh * Q_BLK:(hh + 1) * Q_BLK] + bias
                lg_ref[j, :, cs] = lg
                mx_ref[:, cs] = lg.reshape(kgrp, SUBLANES, Q_BLK).max(axis=0)
        m_old = m_ref[...]
        m_new = jnp.maximum(m_old, jnp.max(mx_ref[...], axis=0, keepdims=True))
        st_ref[row_m(j), :] = m_new
        st_ref[row_a(j), :] = jnp.exp2(m_old - m_new)
        m_ref[...] = m_new

    def probs_stage(j):
        p_ref[j] = jnp.exp2(lg_ref[j] - st_ref[row_m(j), :]).astype(BF16)

    def value_stage(kb, j, alpha):
        acc_ref[...] = acc_ref[...] * alpha + jnp.dot(
            vt_ref[0, kb], p_ref[j], preferred_element_type=F32)

    p_ref[sub - 1] = jnp.zeros(p_ref.shape[1:], BF16)
    st_ref[row_a(sub - 1), :] = jnp.ones((1, n_cols), F32)
    logits_stage(0, 0)

    def attn_body(ch, _):
        alpha_prev = st_ref[row_a(1), :]
        logits_stage(ch, 1)
        value_stage(jnp.maximum(ch * sub - 1, 0), 1, alpha_prev)
        probs_stage(0)
        alpha_cur = st_ref[row_a(0), :]
        logits_stage(jnp.minimum(ch + 1, nch - 1), 0)
        value_stage(ch * sub, 0, alpha_cur)
        probs_stage(1)
        return 0

    lax.fori_loop(0, nch, attn_body, 0)
    value_stage(nch * sub - 1, 1, st_ref[row_a(1), :])
    dh = qt_ref.shape[2]
    inv_l = 1.0 / acc_ref[dh:dh + 1, :]
    for j in range(n_cols // Q_BLK // 2):
        pair = [acc_ref[0:dh, (2 * j + hh) * Q_BLK:(2 * j + hh + 1) * Q_BLK]
                * inv_l[:, (2 * j + hh) * Q_BLK:(2 * j + hh + 1) * Q_BLK] for hh in range(2)]
        o_ref[0, :, j * 2 * dh:(j + 1) * 2 * dh] = jnp.concatenate(pair, axis=0).T.astype(o_ref.dtype)


def _dsa(qt, qit, w_t, kv4, ki4, vt4):
    b, nqb, dh, n_cols = qt.shape
    n_steps = kv4.shape[1]
    assert kv4.shape[2] == CNT_BLK and CNT_BLK == 2 * K_BLK and n_cols % COL_BLK == 0
    assert vt4.shape[1] * K_BLK == n_steps * CNT_BLK and vt4.shape[2] == dh + SUBLANES
    per_q = lambda a: pl.BlockSpec((1, 1) + a.shape[2:], lambda bi, qi: (bi, qi, 0, 0))
    per_b = lambda a: pl.BlockSpec((1,) + a.shape[1:], lambda bi, qi: (bi, 0, 0, 0))
    return pl.pallas_call(
        _dsa_kernel,
        out_shape=jax.ShapeDtypeStruct((b, nqb * Q_BLK, (n_cols // Q_BLK) * dh), BF16),
        grid=(b, nqb),
        in_specs=[per_q(qt), per_q(qit), per_q(w_t), per_b(kv4), per_b(ki4), per_b(vt4)],
        out_specs=pl.BlockSpec((1, Q_BLK, (n_cols // Q_BLK) * dh), lambda bi, qi: (bi, qi, 0)),
        scratch_shapes=[pltpu.VMEM((n_steps, CNT_BLK, Q_BLK), F32),
                        pltpu.VMEM((LANES, n_cols), BF16),
                        pltpu.VMEM((LANES, n_cols), BF16),
                        pltpu.VMEM((1, n_cols), F32),
                        pltpu.VMEM((SUBLANES, n_cols), F32),
                        pltpu.VMEM((SUBLANES, n_cols), F32),
                        pltpu.VMEM((dh + SUBLANES, n_cols), F32),
                        pltpu.VMEM((CNT_BLK // K_BLK, K_BLK, n_cols), F32),
                        pltpu.VMEM((CNT_BLK // K_BLK, K_BLK, n_cols), BF16)],
        compiler_params=_cparams(("arbitrary", "arbitrary")),
        name="dsa",
    )(qt, qit, w_t, kv4, ki4, vt4)


def _outproj_kernel(x_ref, attn_ref, pool_ref, woa_ref, wop_ref, gate1_ref, gffn_ref,
                    scale2_ref, shift2_ref, wr_ref, br_ref, x1_ref, h2_ref, gates_ref):
    tm = x_ref.shape[1]
    mix = (jnp.dot(attn_ref[0], woa_ref[...], preferred_element_type=F32)
           + jnp.dot(pool_ref[0], wop_ref[...], preferred_element_type=F32))
    x1 = x_ref[0] + gate1_ref[0] * mix
    x1_ref[0] = x1
    ms = jnp.mean(x1 * x1, axis=-1, keepdims=True)
    h2 = (x1 * lax.rsqrt(ms + EPS) * gffn_ref[...]) * (1.0 + scale2_ref[0]) + shift2_ref[0]
    h2_hi = h2.astype(BF16)
    h2_ref[0] = h2_hi

    h2_lo = (h2 - h2_hi.astype(F32)).astype(BF16)
    wr = wr_ref[...]
    wr_hi = wr.astype(BF16)
    wr_lo = (wr - wr_hi.astype(F32)).astype(BF16)
    logits = (jnp.dot(h2_hi, wr_hi, preferred_element_type=F32)
              + jnp.dot(h2_lo, wr_hi, preferred_element_type=F32)
              + jnp.dot(h2_hi, wr_lo, preferred_element_type=F32)) + br_ref[...]

    lt = logits.T
    n_e = EXPERTS_PER_GROUP
    sub_id = lax.broadcasted_iota(jnp.int32, (SUBLANES, tm), 0)
    big = jnp.int32(LANES)
    glog = jnp.where(sub_id < N_GROUPS, lt[N_EXPERTS:N_EXPERTS + SUBLANES, :], -jnp.inf)
    gmax = jnp.max(glog, axis=0, keepdims=True)
    gsum = jnp.sum(jnp.exp(glog - gmax), axis=0, keepdims=True)
    p_g = 1.0 / gsum
    g_sel = jnp.min(jnp.where(glog == gmax, sub_id, big), axis=0, keepdims=True)
    elog = lt[0:n_e, :]
    for gi in range(1, N_GROUPS):
        elog = jnp.where(g_sel == gi, lt[gi * n_e:(gi + 1) * n_e, :], elog)
    emax = jnp.max(elog, axis=0, keepdims=True)
    eexp = jnp.exp(elog - emax)
    p_e = eexp / jnp.sum(eexp, axis=0, keepdims=True)
    p1 = jnp.max(p_e, axis=0, keepdims=True)
    i1 = jnp.min(jnp.where(p_e == p1, sub_id, big), axis=0, keepdims=True)
    p_e2 = jnp.where(sub_id == i1, -1.0, p_e)
    p2 = jnp.max(p_e2, axis=0, keepdims=True)
    i2 = jnp.min(jnp.where(p_e2 == p2, sub_id, big), axis=0, keepdims=True)
    tot = p1 + p2
    in_grp = (jnp.where(sub_id == i1, p_g * (p1 / tot), 0.0)
              + jnp.where(sub_id == i2, p_g * (p2 / tot), 0.0))
    rows = [jnp.where(g_sel == gi, in_grp, 0.0) for gi in range(N_GROUPS)]
    rows.append(jnp.where(sub_id == 0, g_sel.astype(F32), 0.0))
    rows.append(jnp.zeros((LANES - N_EXPERTS - SUBLANES, tm), F32))
    gates_ref[0] = jnp.concatenate(rows, axis=0).T


def _outproj(x, attn, pool, wo_a, wo_p, gate1, g_ffn, scale2, shift2, w_r, b_r):
    b, s, d = x.shape
    tm = TM_OUT
    tok = lambda w: pl.BlockSpec((1, tm, w), lambda bi, si: (bi, si, 0))
    per_b = pl.BlockSpec((1, 1, d), lambda bi, si: (bi, 0, 0))
    full = lambda a: pl.BlockSpec(a.shape, lambda bi, si: (0,) * a.ndim)
    return pl.pallas_call(
        _outproj_kernel,
        out_shape=(jax.ShapeDtypeStruct((b, s, d), F32),
                   jax.ShapeDtypeStruct((b, s, d), BF16),
                   jax.ShapeDtypeStruct((b, s, LANES), F32)),
        grid=(b, s // tm),
        in_specs=[tok(d), tok(attn.shape[2]), tok(pool.shape[2]), full(wo_a), full(wo_p), per_b,
                  full(g_ffn), per_b, per_b, full(w_r), full(b_r)],
        out_specs=(tok(d), tok(d), tok(LANES)),
        compiler_params=_cparams(("arbitrary", "arbitrary")),
        name="outproj",
    )(x, attn, pool, wo_a, wo_p, gate1, g_ffn, scale2, shift2, w_r, b_r)


def _moe_kernel(x1_ref, h2_ref, gates_ref, gate2_ref, *refs):
    wgu_parts, wd_parts = refs[:MOE_W_PARTS], refs[MOE_W_PARTS:2 * MOE_W_PARTS]
    o_ref, xe_ref, rank_ref, rank_t_ref, own_ref, grpb_ref = refs[2 * MOE_W_PARTS:]
    g = pl.program_id(2)
    tm, d = h2_ref.shape[1], h2_ref.shape[2]
    per_part = wd_parts[0].shape[0]
    n_e, d_exp = per_part * MOE_W_PARTS, wd_parts[0].shape[1]
    gf = g.astype(F32)

    @pl.when(g == 0)
    def _():
        tri = (lax.broadcasted_iota(jnp.int32, (RANK_BLK, RANK_BLK), 0)
               >= lax.broadcasted_iota(jnp.int32, (RANK_BLK, RANK_BLK), 1)).astype(BF16)
        lane_b = lax.broadcasted_iota(jnp.int32, (RANK_BLK, LANES), 1)
        run = jnp.zeros((1, LANES), F32)
        for sb in range(tm // RANK_BLK):
            rows = slice(sb * RANK_BLK, (sb + 1) * RANK_BLK)
            gts_b = gates_ref[0, rows, :]
            grp = jnp.sum(jnp.where(lane_b == N_EXPERTS, gts_b, 0.0), axis=-1, keepdims=True)
            member = jnp.where(jnp.logical_and(lane_b < N_GROUPS, lane_b.astype(F32) == grp), 1.0, 0.0)
            pre = jnp.dot(tri, member.astype(BF16), preferred_element_type=F32) + run
            rank_ref[rows, :] = jnp.where(lane_b == N_GROUPS, grp, pre)
            own = jnp.sum(member * pre, axis=-1, keepdims=True) - 1.0
            own_ref[rows, :] = jnp.broadcast_to(own, (RANK_BLK, LANES))
            grpb_ref[rows, :] = jnp.broadcast_to(grp, (RANK_BLK, LANES))
            run = jnp.max(pre, axis=0, keepdims=True)
        rank_t_ref[...] = rank_ref[...].T
        gts = gates_ref[0]
        g_hi = gts.astype(BF16)
        xe_ref[:, :d] = h2_ref[0]
        xe_ref[:, d:d + LANES] = g_hi
        xe_ref[:, d + LANES:d + 2 * LANES] = (gts - g_hi.astype(F32)).astype(BF16)
        o_ref[0] = jnp.zeros((tm, d), F32)

    rank_row = rank_t_ref[pl.ds(g, 1), :]
    pos_row = jnp.where(rank_t_ref[N_GROUPS:N_GROUPS + 1, :] == gf, rank_row - 1.0, -1.0)
    pos_col = jnp.where(grpb_ref[...] == gf, own_ref[...], -1.0)
    n_rows = jnp.max(rank_row).astype(jnp.int32)

    def expert_pass(first_row, n_ch):
        row_id = lax.broadcasted_iota(jnp.int32, (n_ch, tm), 0).astype(F32)
        col_id = lax.broadcasted_iota(jnp.int32, (tm, n_ch), 1).astype(F32)
        lane_c = lax.broadcasted_iota(jnp.int32, (n_ch, LANES), 1)
        r0 = first_row.astype(F32)
        gather = jnp.where(pos_row - r0 == row_id, 1.0, 0.0).astype(BF16)
        xg = jnp.dot(gather, xe_ref[...], preferred_element_type=F32)
        xb = xg[:, :d].astype(BF16)
        gates_c = xg[:, d:d + LANES] + xg[:, d + LANES:d + 2 * LANES]
        ya = None
        for e in range(n_e):
            w_gu_e = wgu_parts[e // per_part][e % per_part]
            w_d_e = wd_parts[e // per_part][e % per_part]
            gu = jnp.dot(xb, w_gu_e, preferred_element_type=F32)
            gt = gu[:, :d_exp]
            a = (gt * jax.nn.sigmoid(gt)) * gu[:, d_exp:]
            gate_e = jnp.sum(jnp.where(lane_c == g * n_e + e, gates_c, 0.0), axis=-1, keepdims=True)
            y = jnp.dot((a * gate_e).astype(BF16), w_d_e, preferred_element_type=F32)
            ya = y if ya is None else ya + y
        pos_wide = jnp.concatenate([pos_col] * (n_ch // LANES), axis=1)
        scatter = jnp.where(pos_wide - r0 == col_id, 1.0, 0.0).astype(BF16)
        o_ref[0] += jnp.dot(scatter, ya.astype(BF16), preferred_element_type=F32)

    n_full = (n_rows + MOE_CH - MOE_CH_TAIL - 1) // MOE_CH

    def full_pass(c, _):
        expert_pass(c * MOE_CH, MOE_CH)
        return 0

    lax.fori_loop(0, n_full, full_pass, 0)

    @pl.when(n_rows > n_full * MOE_CH)
    def _():
        expert_pass(n_full * MOE_CH, MOE_CH_TAIL)

    @pl.when(g == pl.num_programs(2) - 1)
    def _():
        o_ref[0] = x1_ref[0] + gate2_ref[0] * o_ref[0]


def _moe(x1, h2, gates, gate2, w_gu, w_d):
    b, s, d = x1.shape
    tm = TM_MOE
    n_e = EXPERTS_PER_GROUP
    assert w_gu.shape[0] == N_GROUPS * n_e and tm % RANK_BLK == 0 and n_e % MOE_W_PARTS == 0
    assert MOE_CH % LANES == 0 and MOE_CH_TAIL % LANES == 0 and MOE_CH_TAIL <= MOE_CH
    per_part = n_e // MOE_W_PARTS
    tok = lambda w: pl.BlockSpec((1, tm, w), lambda bi, si, g: (bi, si, 0))

    def slab(w, k):
        return pl.BlockSpec((per_part,) + w.shape[1:], lambda bi, si, g: (g * MOE_W_PARTS + k, 0, 0))

    return pl.pallas_call(
        _moe_kernel,
        out_shape=jax.ShapeDtypeStruct((b, s, d), F32),
        grid=(b, s // tm, N_GROUPS),
        in_specs=([tok(d), tok(d), tok(LANES), pl.BlockSpec((1, 1, d), lambda bi, si, g: (bi, 0, 0))]
                  + [slab(w_gu, k) for k in range(MOE_W_PARTS)]
                  + [slab(w_d, k) for k in range(MOE_W_PARTS)]),
        out_specs=tok(d),
        scratch_shapes=[pltpu.VMEM((tm, d + 2 * LANES), BF16),
                        pltpu.VMEM((tm, LANES), F32),
                        pltpu.VMEM((LANES, tm), F32),
                        pltpu.VMEM((tm, LANES), F32),
                        pltpu.VMEM((tm, LANES), F32)],
        compiler_params=pltpu.CompilerParams(
            dimension_semantics=("arbitrary", "arbitrary", "arbitrary"),
            vmem_limit_bytes=VMEM_LIMIT_MOE_BYTES),
        name="moe",
    )(x1, h2, gates, gate2, *([w_gu] * MOE_W_PARTS), *([w_d] * MOE_W_PARTS))


def _layer(x, mod, pos3, g_mix, g_ffn, w_in, g_q, g_k, g_kidx, w_pool, pool_scale, w_out,
           w_rg, b_rg, w_re, b_re, w_gate, w_up, w_down):
    b, s, d = x.shape
    d_attn = N_HEADS * HEAD_DIM
    nqb = s // Q_BLK
    nkb = s // K_BLK
    shift1, scale1, gate1, shift2, scale2, gate2 = [m[:, None, :] for m in jnp.split(mod, 6, axis=-1)]

    n_front = d_attn + 2 * HEAD_DIM + N_IDX_HEADS * IDX_DIM + IDX_DIM + N_IDX_HEADS
    pad = (-n_front) % LANES
    w_in_p = jnp.concatenate([w_in[:, :n_front], jnp.zeros((d, pad), w_in.dtype), w_in[:, n_front:]],
                             axis=1).astype(BF16)
    seg_id = jnp.arange(d_attn) // HEAD_DIM
    segsum = (seg_id[:, None] == seg_id[None, :]).astype(BF16)
    ones_half = jnp.ones((LANES - HEAD_DIM,), F32)
    gq_t = (jnp.tile(g_q, N_HEADS) * (LOG2_E * HEAD_DIM ** -0.5))[None, :]
    gk_e = jnp.concatenate([g_k, ones_half])[None, :]
    gkidx_e = jnp.concatenate([g_kidx, ones_half])[None, :]
    half = HEAD_DIM // 2
    inv_freq = ROPE_THETA ** (-jnp.arange(0, HEAD_DIM, 2, dtype=F32) / HEAD_DIM)
    invf = jnp.tile(inv_freq, LANES // half)[None, :]

    qt, kv, qit, ki, w_t, vt4, pool = _inproj(pos3, x, scale1, shift1, g_mix[None, :], w_in_p, segsum,
                                              gq_t, gk_e, gkidx_e, invf, w_pool.astype(BF16),
                                              pool_scale[None, :])
    kv4 = kv.reshape(b, s // CNT_BLK, CNT_BLK, LANES)
    ki4 = ki.reshape(b, s // CNT_BLK, CNT_BLK, LANES)
    attn = _dsa(qt, qit, w_t, kv4, ki4, vt4)

    w_out_b = w_out.astype(BF16)
    w_r = jnp.concatenate([w_re, w_rg, jnp.zeros((d, LANES - N_EXPERTS - N_GROUPS), F32)], axis=1)
    b_r = jnp.concatenate([b_re, b_rg, jnp.zeros((LANES - N_EXPERTS - N_GROUPS,), F32)])[None, :]
    x1, h2, gates = _outproj(x, attn, pool, w_out_b[:d_attn], w_out_b[d_attn:], gate1,
                             g_ffn[None, :], scale2, shift2, w_r, b_r)

    w_gu = jnp.concatenate([w_gate, w_up], axis=-1).astype(BF16)
    return _moe(x1, h2, gates, gate2, w_gu, w_down.astype(BF16))


def kernel(x, c, positions, w_ada, b_ada, g_norm_mix, g_norm_ffn, w_in, g_q, g_k, g_kidx, w_pool,
           pool_scale, w_out, w_router_group, b_router_group, w_router_expert, b_router_expert,
           w_gate, w_up, w_down):
    b, s, d = x.shape
    depth = w_ada.shape[0]
    assert s % TM_MOE == 0 and s % K_BLK == 0 and d % LANES == 0
    pos3 = positions[:, :, None]
    c_pad = jnp.concatenate([c, jnp.zeros((-b % SUBLANES, d), c.dtype)], axis=0)
    for l in range(depth):
        mod = _adaln(c_pad, w_ada[l], b_ada[l][None, :])[:b]
        x = _layer(x, mod, pos3, g_norm_mix[l], g_norm_ffn[l], w_in[l], g_q[l], g_k[l], g_kidx[l],
                   w_pool[l], pool_scale[l], w_out[l], w_router_group[l], b_router_group[l],
                   w_router_expert[l], b_router_expert[l], w_gate[l], w_up[l], w_down[l])
    return x
```
